```python
import math
import jax, jax.numpy as jnp
from jax import lax
import numpy as np

D_MODEL = 1024
BATCH = 32
SEQ = 256
DEPTH = 2
DEC_BATCH = 2
DEC_SEQ = 4096
PAST_LEN = 256

GRID_W = 64
N_EVEN = (DEPTH + 1) // 2
N_ODD = DEPTH // 2
EPS = 1e-6
S5_WIDTH = D_MODEL // 2
S5_CH = 16
S5_GROUPS = S5_WIDTH // S5_CH
S5_STATE = 64
LRU_WIDTH = D_MODEL // 2
LRU_HEADS = 8
LRU_BW = LRU_WIDTH // LRU_HEADS
LRU_CONV = 4
LRU_C = 8.0
HY_WIDTH = D_MODEL
HY_SHORT = 3
HY_BANDS = 16
HY_EMB = 2 * HY_BANDS + 1
HY_FF = 64
HY_SHIFT = 0.05
PEER_HEADS = 8
PEER_DK = 256
PEER_DKH = PEER_DK // 2
PEER_NKEYS = 128
PEER_EXPERTS = PEER_NKEYS ** 2
PEER_TOPK = 16
PEER_BLOCK = 128

kernel_name = 'hybrid_s5_rglru_hyena_peer_diffusion'


def rmsnorm(x, g):
    xf = x.astype(jnp.float32)
    y = xf * lax.rsqrt(jnp.mean(xf * xf, axis=-1, keepdims=True) + EPS)
    return y.astype(x.dtype) * g


def dwconv_rows(x, w, b, grid, pad_left):
    bsz, length, ch = x.shape
    rows, row_len = grid
    taps = w.shape[0]
    xr = x.reshape(bsz, rows, row_len, ch)
    xp = jnp.pad(xr, ((0, 0), (0, 0), (pad_left, taps - 1 - pad_left), (0, 0)))
    y = sum(xp[:, :, k:k + row_len, :] * w[k] for k in range(taps)) + b
    return y.reshape(bsz, length, ch)


def _complex_combine(e1, e2):
    ar1, ai1, br1, bi1 = e1
    ar2, ai2, br2, bi2 = e2
    return (ar2 * ar1 - ai2 * ai1, ar2 * ai1 + ai2 * ar1,
            ar2 * br1 - ai2 * bi1 + br2, ar2 * bi1 + ai2 * br1 + bi2)


def _real_combine(e1, e2):
    a1, b1 = e1
    a2, b2 = e2
    return (a2 * a1, a2 * b1 + b2)


def s5_scan(u, a_re, a_im, log_dt, b_re, b_im, c_re, c_im, h0, reverse):
    f32 = jnp.float32
    a_re = a_re.astype(f32)
    a_im = a_im.astype(f32)
    dt = jnp.exp(log_dt.astype(f32))[:, None]
    mag = jnp.exp(a_re * dt)
    abar_r = mag * jnp.cos(a_im * dt)
    abar_i = mag * jnp.sin(a_im * dt)
    den = a_re * a_re + a_im * a_im
    num_r = abar_r - 1.0
    coef_r = ((num_r * a_re + abar_i * a_im) / den)[..., None]
    coef_i = ((abar_i * a_re - num_r * a_im) / den)[..., None]
    bbar_r = coef_r * b_re - coef_i * b_im
    bbar_i = coef_r * b_im + coef_i * b_re
    bu_r = jnp.einsum('gpc,blgc->blgp', bbar_r, u)
    bu_i = jnp.einsum('gpc,blgc->blgp', bbar_i, u)
    first, last = (-1, 0) if reverse else (0, -1)
    if h0 is not None:
        h0_r = h0[0].astype(f32)
        h0_i = h0[1].astype(f32)
        bu_r = bu_r.at[:, first].add(abar_r * h0_r - abar_i * h0_i)
        bu_i = bu_i.at[:, first].add(abar_r * h0_i + abar_i * h0_r)
    elems = (jnp.broadcast_to(abar_r, bu_r.shape), jnp.broadcast_to(abar_i, bu_r.shape), bu_r, bu_i)
    _, _, h_r, h_i = lax.associative_scan(_complex_combine, elems, reverse=reverse, axis=1)
    y = jnp.einsum('gcp,blgp->blgc', c_re, h_r) - jnp.einsum('gcp,blgp->blgc', c_im, h_i)
    return y, h_r[:, last], h_i[:, last]


def rglru_scan(xb, w_a, b_a, w_x, b_x, lam, h0, reverse):
    bsz, length, width = xb.shape
    xh = xb.reshape(bsz, length, LRU_HEADS, LRU_BW)
    r = jax.nn.sigmoid(jnp.einsum('blhi,hij->blhj', xh, w_a).reshape(bsz, length, width) + b_a)
    i = jax.nn.sigmoid(jnp.einsum('blhi,hij->blhj', xh, w_x).reshape(bsz, length, width) + b_x)
    log_a = -LRU_C * r * jax.nn.softplus(-lam.astype(jnp.float32))
    a = jnp.exp(log_a)
    b = jnp.sqrt(-jnp.expm1(2.0 * log_a)) * (i * xb)
    first, last = (-1, 0) if reverse else (0, -1)
    if h0 is not None:
        b = b.at[:, first].add(a[:, first] * h0.astype(jnp.float32))
    _, h = lax.associative_scan(_real_combine, (a, b), reverse=reverse, axis=1)
    return h, h[:, last]


def mixer_ab(h, P, le, grid, h0):
    f32 = jnp.float32
    bsz, length, _ = h.shape
    z = h @ P['w_in_ab'][le]
    u, xr, xg = jnp.split(z, [S5_WIDTH, S5_WIDTH + LRU_WIDTH], axis=-1)
    uf = u.astype(f32).reshape(bsz, length, S5_GROUPS, S5_CH)
    y = uf * P['s5_d'][le].astype(f32).reshape(S5_GROUPS, S5_CH)
    xb = dwconv_rows(xr, P['lru_conv_w'][le], P['lru_conv_b'][le], grid, 2).astype(f32)
    hsum = jnp.zeros_like(xb)
    st_re, st_im, st_lru = [], [], []
    for d in range(2):
        rev = d == 1
        h0_s5 = None if h0 is None else (h0[0][:, d], h0[1][:, d])
        yd, hr, hi = s5_scan(uf, P['s5_a_re'][le, d], P['s5_a_im'][le, d], P['s5_log_dt'][le, d],
                             P['s5_b_re'][le, d], P['s5_b_im'][le, d], P['s5_c_re'][le, d], P['s5_c_im'][le, d],
                             h0_s5, rev)
        y = y + yd
        hd, hl = rglru_scan(xb, P['lru_w_a'][le, d], P['lru_b_a'][le, d], P['lru_w_x'][le, d], P['lru_b_x'][le, d],
                            P['lru_lambda'][le, d], None if h0 is None else h0[2][:, d], rev)
        hsum = hsum + hd
        st_re.append(hr)
        st_im.append(hi)
        st_lru.append(hl)
    zs = jax.nn.gelu(y.reshape(bsz, length, S5_WIDTH))
    s5_out = zs * jax.nn.sigmoid(zs @ P['s5_w_glu'][le] + P['s5_b_glu'][le])
    lru_out = hsum * jax.nn.gelu(xg.astype(f32))
    out = jnp.concatenate([s5_out, lru_out], axis=-1).astype(h.dtype) @ P['w_out_ab'][le]
    return out, (jnp.stack(st_re, axis=1), jnp.stack(st_im, axis=1), jnp.stack(st_lru, axis=1))


def hyena_filters(length, w1, b1, f1, w2, b2, f2, w3, decay):
    f32 = jnp.float32
    pos = jnp.arange(length, dtype=f32)
    t = (pos / length)[:, None]
    bands = jnp.linspace(1e-4, HY_BANDS - 1, HY_BANDS, dtype=f32)
    ang = (2.0 * math.pi / length) * pos[:, None] * bands[None, :]
    emb = jnp.concatenate([t, jnp.cos(ang), -jnp.sin(ang)], axis=-1)
    z = jnp.sin(f1 * (emb @ w1 + b1))
    z = jnp.sin(f2 * (z @ w2 + b2))
    filt = (z @ w3).astype(f32).reshape(length, 2, HY_WIDTH)
    filt = filt * (jnp.exp(-t[:, :, None] * jnp.abs(decay.astype(f32))) + HY_SHIFT)
    circ = jnp.concatenate([filt[:, 0], jnp.zeros((1, HY_WIDTH), f32), filt[:0:-1, 1]], axis=0)
    return circ * lax.rsqrt(jnp.sum(circ * circ, axis=0, keepdims=True) + EPS)


def bidir_long_conv(v, circ):
    length = v.shape[1]
    n = 2 * length
    vf = jnp.fft.rfft(v, n=n, axis=1)
    kf = jnp.fft.rfft(circ, n=n, axis=0)
    return jnp.fft.irfft(vf * kf[None], n=n, axis=1)[:, :length]


def mixer_c(h, P, lo, grid):
    length = h.shape[1]
    z = h @ P['w_in_c'][lo] + P['b_in_c'][lo]
    z = dwconv_rows(z, P['hy_conv_w'][lo], P['hy_conv_b'][lo], grid, 1)
    x0, x1, v = jnp.split(z, 3, axis=-1)
    v = (v * x1).astype(jnp.float32)
    circ = hyena_filters(length, P['hy_w1'][lo], P['hy_b1'][lo], P['hy_freq1'][lo], P['hy_w2'][lo],
                         P['hy_b2'][lo], P['hy_freq2'][lo], P['hy_w3'][lo], P['hy_decay'][lo])
    v = bidir_long_conv(v, circ) + P['hy_bias'][lo] * v
    y = v.astype(h.dtype) * x0
    return y @ P['w_out_c'][lo] + P['b_out_c'][lo]


def peer(h, wq, keys, u_tab, v_tab):
    bsz, length, dm = h.shape
    ntok = bsz * length
    x = h.reshape(ntok, dm)
    q = (x @ wq).reshape(ntok, PEER_HEADS, 2, PEER_DKH)
    s = jnp.einsum('thpk,hpnk->thpn', q, keys).astype(jnp.float32)
    sv, si = lax.top_k(s, PEER_TOPK)
    cand = (sv[:, :, 0, :, None] + sv[:, :, 1, None, :]).reshape(ntok, PEER_HEADS, PEER_TOPK * PEER_TOPK)
    cidx = (si[:, :, 0, :, None] * PEER_NKEYS + si[:, :, 1, None, :]).reshape(ntok, PEER_HEADS, PEER_TOPK * PEER_TOPK)
    fv, fpos = lax.top_k(cand, PEER_TOPK)
    eidx = jnp.take_along_axis(cidx, fpos, axis=-1)
    g = jax.nn.softmax(fv, axis=-1).astype(h.dtype)
    nblk = ntok // PEER_BLOCK
    eidx = eidx.reshape(nblk, PEER_BLOCK, PEER_HEADS * PEER_TOPK)
    g = g.reshape(nblk, PEER_BLOCK, PEER_HEADS * PEER_TOPK)
    xb = x.reshape(nblk, PEER_BLOCK, dm)

    def block(args):
        xb_, idx_, g_ = args
        act = jax.nn.gelu(jnp.einsum('td,ted->te', xb_, u_tab[idx_]))
        return jnp.einsum('te,ted->td', g_ * act, v_tab[idx_])

    out = lax.map(block, (xb, eidx, g))
    return out.reshape(bsz, length, dm)


def trunk(x, cond, grid, P, cache):
    collect = cache is None
    st_re, st_im, st_lru = [], [], []
    for l in range(DEPTH):
        mod = jax.nn.silu(cond) @ P['w_mod'][l] + P['b_mod'][l]
        sh1, sc1, g1, sh2, sc2, g2 = jnp.split(mod[:, None, :], 6, axis=-1)
        h = rmsnorm(x, P['norm1_g'][l]) * (1 + sc1) + sh1
        if l % 2 == 0:
            le = l // 2
            h0 = None if collect else (cache[0][:, le], cache[1][:, le], cache[2][:, le])
            out, (s_re, s_im, s_lru) = mixer_ab(h, P, le, grid, h0)
            if collect:
                st_re.append(s_re)
                st_im.append(s_im)
                st_lru.append(s_lru)
        else:
            out = mixer_c(h, P, l // 2, grid)
        x = x + g1 * out
        h = rmsnorm(x, P['norm2_g'][l]) * (1 + sc2) + sh2
        x = x + g2 * peer(h, P['peer_wq'][l], P['peer_keys'][l], P['peer_u'][l], P['peer_v'][l])
    y = rmsnorm(x, P['final_g'])
    if collect:
        return y, (jnp.stack(st_re, axis=1), jnp.stack(st_im, axis=1), jnp.stack(st_lru, axis=1))
    return y, None


def setup_inputs(seed: int = 0) -> dict:
    key = jax.random.key(seed)
    ks = iter(jax.random.split(key, 64))
    f32 = jnp.float32

    def nrm(shape, scale):
        return scale * jax.random.normal(next(ks), shape, f32)

    def uni(shape, lo, hi):
        return jax.random.uniform(next(ks), shape, f32, lo, hi)

    D = D_MODEL
    lru_a = uni((N_EVEN, 2, LRU_WIDTH), 0.9, 0.999)
    s5_shape = (N_EVEN, 2, S5_GROUPS, S5_STATE)
    return {
        'x_prompt': nrm((BATCH, SEQ, D), 1.0),
        'x_sample': nrm((DEC_BATCH, DEC_SEQ, D), 1.0),
        'state_s5_re': nrm((DEC_BATCH, N_EVEN, 2, S5_GROUPS, S5_STATE), 0.1),
        'state_s5_im': nrm((DEC_BATCH, N_EVEN, 2, S5_GROUPS, S5_STATE), 0.1),
        'state_lru': nrm((DEC_BATCH, N_EVEN, 2, LRU_WIDTH), 0.5),
        'c': nrm((DEC_BATCH, D), 1.0),
        'c_ctx': nrm((D,), 1.0),
        'norm1_g': 1.0 + nrm((DEPTH, D), 0.01),
        'norm2_g': 1.0 + nrm((DEPTH, D), 0.01),
        'w_mod': nrm((DEPTH, D, 6 * D), 0.5 * D ** -0.5),
        'b_mod': nrm((DEPTH, 6 * D), 0.01),
        'w_in_ab': nrm((N_EVEN, D, S5_WIDTH + 2 * LRU_WIDTH), D ** -0.5),
        's5_a_re': -0.5 + nrm(s5_shape, 0.01),
        's5_a_im': jnp.pi * jnp.arange(S5_STATE, dtype=f32) + nrm(s5_shape, 0.01),
        's5_log_dt': uni((N_EVEN, 2, S5_GROUPS), math.log(1e-3), math.log(1e-1)),
        's5_b_re': nrm((N_EVEN, 2, S5_GROUPS, S5_STATE, S5_CH), (2 * S5_CH) ** -0.5),
        's5_b_im': nrm((N_EVEN, 2, S5_GROUPS, S5_STATE, S5_CH), (2 * S5_CH) ** -0.5),
        's5_c_re': nrm((N_EVEN, 2, S5_GROUPS, S5_CH, S5_STATE), S5_STATE ** -0.5),
        's5_c_im': nrm((N_EVEN, 2, S5_GROUPS, S5_CH, S5_STATE), S5_STATE ** -0.5),
        's5_d': nrm((N_EVEN, S5_WIDTH), 1.0),
        's5_w_glu': nrm((N_EVEN, S5_WIDTH, S5_WIDTH), S5_WIDTH ** -0.5),
        's5_b_glu': nrm((N_EVEN, S5_WIDTH), 0.01),
        'lru_conv_w': nrm((N_EVEN, LRU_CONV, LRU_WIDTH), LRU_CONV ** -0.5),
        'lru_conv_b': nrm((N_EVEN, LRU_WIDTH), 0.01),
        'lru_w_a': nrm((N_EVEN, 2, LRU_HEADS, LRU_BW, LRU_BW), LRU_BW ** -0.5),
        'lru_b_a': nrm((N_EVEN, 2, LRU_WIDTH), 0.01),
        'lru_w_x': nrm((N_EVEN, 2, LRU_HEADS, LRU_BW, LRU_BW), LRU_BW ** -0.5),
        'lru_b_x': nrm((N_EVEN, 2, LRU_WIDTH), 0.01),
        'lru_lambda': jnp.log(lru_a) - jnp.log1p(-lru_a),
        'w_out_ab': nrm((N_EVEN, S5_WIDTH + LRU_WIDTH, D), (S5_WIDTH + LRU_WIDTH) ** -0.5),
        'w_in_c': nrm((N_ODD, D, 3 * HY_WIDTH), D ** -0.5),
        'b_in_c': nrm((N_ODD, 3 * HY_WIDTH), 0.01),
        'hy_conv_w': nrm((N_ODD, HY_SHORT, 3 * HY_WIDTH), HY_SHORT ** -0.5),
        'hy_conv_b': nrm((N_ODD, 3 * HY_WIDTH), 0.01),
        'hy_w1': nrm((N_ODD, HY_EMB, HY_FF), HY_EMB ** -0.5),
        'hy_b1': nrm((N_ODD, HY_FF), 0.1),
        'hy_freq1': 1.0 + nrm((N_ODD, HY_FF), 0.01),
        'hy_w2': nrm((N_ODD, HY_FF, HY_FF), HY_FF ** -0.5),
        'hy_b2': nrm((N_ODD, HY_FF), 0.1),
        'hy_freq2': 1.0 + nrm((N_ODD, HY_FF), 0.01),
        'hy_w3': nrm((N_ODD, HY_FF, 2 * HY_WIDTH), HY_FF ** -0.5),
        'hy_decay': uni((N_ODD, 2, HY_WIDTH), 3.0, 15.0),
        'hy_bias': nrm((N_ODD, HY_WIDTH), 1.0),
        'w_out_c': nrm((N_ODD, HY_WIDTH, D), HY_WIDTH ** -0.5),
        'b_out_c': nrm((N_ODD, D), 0.01),
        'peer_wq': nrm((DEPTH, D, PEER_HEADS * PEER_DK), D ** -0.5),
        'peer_keys': nrm((DEPTH, PEER_HEADS, 2, PEER_NKEYS, PEER_DKH), PEER_DKH ** -0.5),
        'peer_u': nrm((DEPTH, PEER_EXPERTS, D), D ** -0.5),
        'peer_v': nrm((DEPTH, PEER_EXPERTS, D), PEER_HEADS ** -0.5),
        'final_g': 1.0 + nrm((D,), 0.01),
    }


def reference(x_prompt, x_sample, state_s5_re, state_s5_im, state_lru, c, c_ctx,
              norm1_g, norm2_g, w_mod, b_mod, w_in_ab,
              s5_a_re, s5_a_im, s5_log_dt, s5_b_re, s5_b_im, s5_c_re, s5_c_im, s5_d, s5_w_glu, s5_b_glu,
              lru_conv_w, lru_conv_b, lru_w_a, lru_b_a, lru_w_x, lru_b_x, lru_lambda, w_out_ab,
              w_in_c, b_in_c, hy_conv_w, hy_conv_b, hy_w1, hy_b1, hy_freq1, hy_w2, hy_b2, hy_freq2, hy_w3,
              hy_decay, hy_bias, w_out_c, b_out_c,
              peer_wq, peer_keys, peer_u, peer_v, final_g):
    P = dict(norm1_g=norm1_g, norm2_g=norm2_g, w_mod=w_mod, b_mod=b_mod, w_in_ab=w_in_ab,
             s5_a_re=s5_a_re, s5_a_im=s5_a_im, s5_log_dt=s5_log_dt, s5_b_re=s5_b_re, s5_b_im=s5_b_im,
             s5_c_re=s5_c_re, s5_c_im=s5_c_im, s5_d=s5_d, s5_w_glu=s5_w_glu, s5_b_glu=s5_b_glu,
             lru_conv_w=lru_conv_w, lru_conv_b=lru_conv_b, lru_w_a=lru_w_a, lru_b_a=lru_b_a,
             lru_w_x=lru_w_x, lru_b_x=lru_b_x, lru_lambda=lru_lambda, w_out_ab=w_out_ab,
             w_in_c=w_in_c, b_in_c=b_in_c, hy_conv_w=hy_conv_w, hy_conv_b=hy_conv_b,
             hy_w1=hy_w1, hy_b1=hy_b1, hy_freq1=hy_freq1, hy_w2=hy_w2, hy_b2=hy_b2, hy_freq2=hy_freq2,
             hy_w3=hy_w3, hy_decay=hy_decay, hy_bias=hy_bias, w_out_c=w_out_c, b_out_c=b_out_c,
             peer_wq=peer_wq, peer_keys=peer_keys, peer_u=peer_u, peer_v=peer_v, final_g=final_g)
    ctx_grid = (1, x_prompt.shape[1])
    y_prompt, (new_s5_re, new_s5_im, new_lru) = trunk(x_prompt, c_ctx[None, :], ctx_grid, P, None)
    rows = x_sample.shape[1] // GRID_W
    y_sample, _ = trunk(x_sample, c, (rows, GRID_W), P, (state_s5_re, state_s5_im, state_lru))
    return (y_prompt, y_sample, new_s5_re, new_s5_im, new_lru)
```

```python
import functools
import math

import jax
import jax.numpy as jnp
import numpy as np
from jax import lax
from jax.experimental import pallas as pl
from jax.experimental.pallas import tpu as pltpu

F32 = jnp.float32
BF16 = jnp.bfloat16
HIGHEST = lax.Precision.HIGHEST

DEPTH = 2
GRID_W = 64
EPS = 1e-6
S5_CH = 16
S5_STATE = 64
LRU_HEADS = 8
LRU_C = 8.0
HY_BANDS = 16
HY_SHIFT = 0.05
PEER_HEADS = 8
PEER_NKEYS = 128
PEER_TOPK = 16

V7X_LANES = 128
V7X_SUBLANES = 8
V7X_VMEM_BYTES = 64 * 1024 * 1024

SCAN_SEQS = V7X_SUBLANES
SAMPLE_SEGS = 4
S5_COLBLK = 4


def _vmem_limit(est_bytes):
    return int(min(V7X_VMEM_BYTES - (8 << 20), max(32 << 20, est_bytes * 3 // 2)))


def _cparams(est_bytes, sem):
    return pltpu.CompilerParams(dimension_semantics=sem, vmem_limit_bytes=_vmem_limit(est_bytes))


def _gelu(x):
    return 0.5 * x * (1.0 + jnp.tanh(0.7978845608028654 * (x + 0.044715 * (x * x * x))))


def _norm_mod(x, g, sc, sh):
    ms = jnp.mean(x * x, axis=-1, keepdims=True)
    return (x * lax.rsqrt(ms + EPS)) * g * (1.0 + sc) + sh


def _cond_index(row0, n_ctx, dec_seq):
    return jnp.where(row0 < n_ctx, 0, 1 + (row0 - n_ctx) // dec_seq)


def _mod_kernel(c_ref, w_ref, b_ref, o_ref):
    c = c_ref[...]
    s = c * jax.nn.sigmoid(c)
    o_ref[...] = jnp.dot(s, w_ref[...], precision=HIGHEST, preferred_element_type=F32) + b_ref[...]


def _modulation(cond, w_mod, b_mod):
    depth, d, n = w_mod.shape
    rows = cond.shape[0]
    tn = 1536
    est = 2 * d * tn * 4 + 4 * rows * (d + tn) * 4
    return pl.pallas_call(
        _mod_kernel,
        grid=(depth, n // tn),
        in_specs=[pl.BlockSpec((rows, d), lambda l, j: (0, 0)),
                  pl.BlockSpec((None, d, tn), lambda l, j: (l, 0, j)),
                  pl.BlockSpec((None, 1, tn), lambda l, j: (l, 0, j))],
        out_specs=pl.BlockSpec((None, rows, tn), lambda l, j: (l, 0, j)),
        out_shape=jax.ShapeDtypeStruct((depth, rows, n), F32),
        compiler_params=_cparams(est, ("arbitrary", "arbitrary")),
        name="modulation",
    )(cond, w_mod, b_mod.reshape(depth, 1, n))


def _dwconv_rows(x, w, bias, pos, row_len, pad_left):
    tm = x.shape[0]
    acc = None
    for k in range(w.shape[0]):
        d = k - pad_left
        if d == 0:
            term = x
        else:
            shifted = pltpu.roll(x, (-d) % tm, axis=0)
            q = pos + d
            term = jnp.where(q >= 0, jnp.where(q < row_len, shifted, 0.0), 0.0)
        term = term * w[k:k + 1]
        acc = term if acc is None else acc + term
    return acc + bias


def _ab_in_kernel(x_ref, mod_ref, g_ref, w_ref, cw_ref, cb_ref, wg_ref, bg_ref, c8_ref,
                  s_ref, xg_ref, *, n_ctx, ctx_row, width):
    tm = x_ref.shape[0]
    m = mod_ref[...]
    h = _norm_mod(x_ref[...], g_ref[...], m[1:2], m[0:1])
    z = jnp.dot(h.astype(BF16), w_ref[...], preferred_element_type=F32)
    u = z[:, :width]
    xr = z[:, width:2 * width]
    xg_ref[...] = z[:, 2 * width:]
    row_len = jnp.where(pl.program_id(0) * tm < n_ctx, ctx_row, GRID_W)
    pos = lax.broadcasted_iota(jnp.int32, (tm, width), 0) & (row_len - 1)
    xb = _dwconv_rows(xr, cw_ref[...], cb_ref[...], pos, row_len, 2)
    gates = jnp.dot(xb.astype(BF16), wg_ref[...], preferred_element_type=F32) + bg_ref[...]
    s_ref[:, :width] = u
    c8 = c8_ref[...]
    for d in range(2):
        r = jax.nn.sigmoid(gates[:, (2 * d) * width:(2 * d + 1) * width])
        i = jax.nn.sigmoid(gates[:, (2 * d + 1) * width:(2 * d + 2) * width])
        a = jnp.exp(-(c8[d:d + 1] * r))
        b = jnp.sqrt(1.0 - a * a) * (i * xb)
        s_ref[:, (1 + 2 * d) * width:(2 + 2 * d) * width] = a
        s_ref[:, (2 + 2 * d) * width:(3 + 2 * d) * width] = b


def _ab_in(x, mod, g, w_in, conv_w, conv_b, w_gates, b_gates, c8, *, n_ctx, ctx_row, dec_seq, tm):
    t, d = x.shape
    width = w_in.shape[1] // 3
    kern = functools.partial(_ab_in_kernel, n_ctx=n_ctx, ctx_row=ctx_row, width=width)
    const = lambda i: (0, 0)
    est = (2 * tm * d * 4 + d * 3 * width * 2 * 2 + width * 4 * width * 2 * 2
           + 2 * tm * 6 * width * 4 + tm * 12 * width * 4)
    return pl.pallas_call(
        kern,
        grid=(t // tm,),
        in_specs=[pl.BlockSpec((tm, d), lambda i: (i, 0)),
                  pl.BlockSpec((None, 6, d), lambda i: (_cond_index(i * tm, n_ctx, dec_seq), 0, 0)),
                  pl.BlockSpec((1, d), const),
                  pl.BlockSpec(w_in.shape, const),
                  pl.BlockSpec(conv_w.shape, const),
                  pl.BlockSpec((1, width), const),
                  pl.BlockSpec(w_gates.shape, const),
                  pl.BlockSpec((1, 4 * width), const),
                  pl.BlockSpec((2, width), const)],
        out_specs=[pl.BlockSpec((tm, 5 * width), lambda i: (i, 0)),
                   pl.BlockSpec((tm, width), lambda i: (i, 0))],
        out_shape=[jax.ShapeDtypeStruct((t, 5 * width), F32),
                   jax.ShapeDtypeStruct((t, width), F32)],
        compiler_params=_cparams(est, ("parallel",)),
        name="ab_in",
    )(x, mod, g, w_in, conv_w, conv_b, w_gates, b_gates, c8)


def _scan_kernel(u_ref, a_ref, b_ref, bblk_ref, cblk_ref, ar_ref, ai_ref, i5_ref, il_ref, *rest,
                 reverse, store):
    if store:
        y_ref, hl_ref, f5_ref, fl_ref, fp_ref, buf, st5, stl, stp = rest
    else:
        f5_ref, fl_ref, fp_ref, buf, st5, stl, stp = rest
    c = pl.program_id(1)
    rows = u_ref.shape[0]
    nsteps = rows // SCAN_SEQS
    sw = st5.shape[1] // (2 * S5_COLBLK)
    cw = u_ref.shape[1] // S5_COLBLK

    @pl.when(c == 0)
    def _():
        st5[...] = i5_ref[...]
        stl[...] = il_ref[...]
        stp[...] = jnp.ones_like(stp)

    def row0(k):
        t = (nsteps - 1 - k) if reverse else k
        return pl.multiple_of(t * SCAN_SEQS, SCAN_SEQS)

    u = u_ref[...].astype(BF16)
    for j in range(S5_COLBLK):
        buf[...] = jnp.dot(u[:, j * cw:(j + 1) * cw], bblk_ref[j], preferred_element_type=F32)
        arj = jnp.broadcast_to(ar_ref[j], (SCAN_SEQS, sw))
        aij = jnp.broadcast_to(ai_ref[j], (SCAN_SEQS, sw))

        def step(k, carry, arj=arj, aij=aij):
            hr, hi = carry
            r0 = row0(k)
            xr = buf[pl.ds(r0, SCAN_SEQS), 0:sw]
            xi = buf[pl.ds(r0, SCAN_SEQS), sw:2 * sw]
            nr = arj * hr - aij * hi + xr
            ni = arj * hi + aij * hr + xi
            buf[pl.ds(r0, SCAN_SEQS), 0:sw] = nr
            buf[pl.ds(r0, SCAN_SEQS), sw:2 * sw] = ni
            return nr, ni

        base = 2 * sw * j
        hr, hi = lax.fori_loop(0, nsteps, step, (st5[:, base:base + sw], st5[:, base + sw:base + 2 * sw]))
        st5[:, base:base + sw] = hr
        st5[:, base + sw:base + 2 * sw] = hi
        if store:
            y_ref[:, j * cw:(j + 1) * cw] = jnp.dot(buf[...].astype(BF16), cblk_ref[j],
                                                    preferred_element_type=F32)

    def lstep(k, carry):
        h, p = carry
        r0 = row0(k)
        a = a_ref[pl.ds(r0, SCAN_SEQS), :]
        h = a * h + b_ref[pl.ds(r0, SCAN_SEQS), :]
        if store:
            hl_ref[pl.ds(r0, SCAN_SEQS), :] = h
        return h, p * a

    h, p = lax.fori_loop(0, nsteps, lstep, (stl[...], stp[...]))
    stl[...] = h
    stp[...] = p

    @pl.when(c == pl.num_programs(1) - 1)
    def _():
        f5_ref[...] = st5[...]
        fl_ref[...] = stl[...]
        fp_ref[...] = stp[...]


def _scan(s_arr, bblk, cblk, ar, ai, init5, initl, *, direction, store, groups, chunk_rows):
    rows_total, cols = s_arr.shape
    width = cols // 5
    rows_per_group = rows_total // groups
    nc = rows_per_group // chunk_rows
    reverse = direction == 1
    nstate = init5.shape[-1]

    def chunk(g, c):
        return g * nc + ((nc - 1 - c) if reverse else c)

    kern = functools.partial(_scan_kernel, reverse=reverse, store=store)
    const3 = lambda g, c: (0, 0, 0)
    in_specs = [pl.BlockSpec((chunk_rows, width), lambda g, c: (chunk(g, c), 0)),
                pl.BlockSpec((chunk_rows, width), lambda g, c: (chunk(g, c), 1 + 2 * direction)),
                pl.BlockSpec((chunk_rows, width), lambda g, c: (chunk(g, c), 2 + 2 * direction)),
                pl.BlockSpec(bblk.shape, const3),
                pl.BlockSpec(cblk.shape, const3),
                pl.BlockSpec(ar.shape, const3),
                pl.BlockSpec(ai.shape, const3),
                pl.BlockSpec((None, SCAN_SEQS, nstate), lambda g, c: (g, 0, 0)),
                pl.BlockSpec((None, SCAN_SEQS, width), lambda g, c: (g, 0, 0))]
    fin_specs = [pl.BlockSpec((None, SCAN_SEQS, nstate), lambda g, c: (g, 0, 0)),
                 pl.BlockSpec((None, SCAN_SEQS, width), lambda g, c: (g, 0, 0)),
                 pl.BlockSpec((None, SCAN_SEQS, width), lambda g, c: (g, 0, 0))]
    fin_shapes = [jax.ShapeDtypeStruct((groups, SCAN_SEQS, nstate), F32),
                  jax.ShapeDtypeStruct((groups, SCAN_SEQS, width), F32),
                  jax.ShapeDtypeStruct((groups, SCAN_SEQS, width), F32)]
    out_specs, out_shapes = fin_specs, fin_shapes
    if store:
        out_specs = [pl.BlockSpec((chunk_rows, width), lambda g, c: (chunk(g, c), 0))] * 2 + fin_specs
        out_shapes = [jax.ShapeDtypeStruct((rows_total, width), F32)] * 2 + fin_shapes
    bufw = 2 * nstate // (2 * S5_COLBLK)
    est = (10 * chunk_rows * width * 4 + chunk_rows * bufw * 4 * 2
           + 4 * bblk.size * 2 + 8 * SCAN_SEQS * nstate * 4)
    return pl.pallas_call(
        kern,
        grid=(groups, nc),
        in_specs=in_specs,
        out_specs=out_specs,
        out_shape=out_shapes,
        scratch_shapes=[pltpu.VMEM((chunk_rows, bufw), F32),
                        pltpu.VMEM((SCAN_SEQS, nstate), F32),
                        pltpu.VMEM((SCAN_SEQS, width), F32),
                        pltpu.VMEM((SCAN_SEQS, width), F32)],
        compiler_params=_cparams(est, ("arbitrary", "arbitrary")),
        name="scan_store" if store else "scan_ends",
    )(s_arr, s_arr, s_arr, bblk, cblk, ar, ai, init5, initl)


def _chain_kernel(e5_ref, el_ref, ep_ref, h5_ref, hl_ref, arl_ref, ail_ref, i5_ref, il_ref, *, reverse):
    nseg = SAMPLE_SEGS
    sw = arl_ref.shape[-1]
    shift = (SCAN_SEQS - 1) if reverse else 1
    order = range(nseg - 2, -1, -1) if reverse else range(1, nseg)
    seg5 = lax.broadcasted_iota(jnp.int32, (SCAN_SEQS, sw), 0) % nseg
    for j in range(S5_COLBLK):
        base = 2 * sw * j
        re, im = slice(base, base + sw), slice(base + sw, base + 2 * sw)
        ar, ai = arl_ref[j], ail_ref[j]
        er = pltpu.roll(e5_ref[:, re], shift, axis=0)
        ei = pltpu.roll(e5_ref[:, im], shift, axis=0)
        hr, hi = h5_ref[:, re], h5_ref[:, im]
        for k in order:
            pr = pltpu.roll(hr, shift, axis=0)
            pi = pltpu.roll(hi, shift, axis=0)
            hr = jnp.where(seg5 == k, ar * pr - ai * pi + er, hr)
            hi = jnp.where(seg5 == k, ar * pi + ai * pr + ei, hi)
        i5_ref[:, re] = hr
        i5_ref[:, im] = hi
    segl = lax.broadcasted_iota(jnp.int32, el_ref.shape, 0) % nseg
    el = pltpu.roll(el_ref[...], shift, axis=0)
    ep = pltpu.roll(ep_ref[...], shift, axis=0)
    h = hl_ref[...]
    for k in order:
        h = jnp.where(segl == k, ep * pltpu.roll(h, shift, axis=0) + el, h)
    il_ref[...] = h


def _chain(end5, endl, endp, h5_rows, hl_rows, arl, ail, *, direction):
    kern = functools.partial(_chain_kernel, reverse=direction == 1)
    return pl.pallas_call(
        kern,
        out_shape=[jax.ShapeDtypeStruct(end5.shape, F32), jax.ShapeDtypeStruct(endl.shape, F32)],
        name="scan_chain",
    )(end5, endl, endp, h5_rows, hl_rows, arl, ail)


def _ab_out_kernel(x_ref, mod_ref, s_ref, xg_ref, yf_ref, yb_ref, hf_ref, hb_ref, d_ref,
                   wglu_ref, bglu_ref, wout_ref, o_ref):
    width = xg_ref.shape[1]
    m = mod_ref[...]
    y = s_ref[...] * d_ref[...] + yf_ref[...] + yb_ref[...]
    zs = _gelu(y)
    gate = jnp.dot(zs.astype(BF16), wglu_ref[...], preferred_element_type=F32) + bglu_ref[...]
    s5_out = zs * jax.nn.sigmoid(gate)
    lru_out = (hf_ref[...] + hb_ref[...]) * _gelu(xg_ref[...])
    out = (jnp.dot(s5_out.astype(BF16), wout_ref[:width, :], preferred_element_type=F32)
           + jnp.dot(lru_out.astype(BF16), wout_ref[width:, :], preferred_element_type=F32))
    o_ref[...] = x_ref[...] + m[2:3] * out


def _ab_out(x, mod, s_arr, xg, yf, yb, hf, hb, s5_d, w_glu, b_glu, w_out, *, n_ctx, dec_seq, tm):
    t, d = x.shape
    width = xg.shape[1]
    const = lambda i: (0, 0)
    row = lambda i: (i, 0)
    est = 4 * tm * d * 4 + 12 * tm * width * 4 + 4 * (width * width + 2 * width * d) + 8 * tm * width * 4
    return pl.pallas_call(
        _ab_out_kernel,
        grid=(t // tm,),
        in_specs=[pl.BlockSpec((tm, d), row),
                  pl.BlockSpec((None, 6, d), lambda i: (_cond_index(i * tm, n_ctx, dec_seq), 0, 0)),
                  pl.BlockSpec((tm, width), row),
                  pl.BlockSpec((tm, width), row),
                  pl.BlockSpec((tm, width), row),
                  pl.BlockSpec((tm, width), row),
                  pl.BlockSpec((tm, width), row),
                  pl.BlockSpec((tm, width), row),
                  pl.BlockSpec((1, width), const),
                  pl.BlockSpec(w_glu.shape, const),
                  pl.BlockSpec((1, width), const),
                  pl.BlockSpec(w_out.shape, const)],
        out_specs=pl.BlockSpec((tm, d), row),
        out_shape=jax.ShapeDtypeStruct((t, d), F32),
        compiler_params=_cparams(est, ("parallel",)),
        name="ab_out",
    )(x, mod, s_arr, xg, yf, yb, hf, hb, s5_d, w_glu, b_glu, w_out)


def _c_in_kernel(x_ref, mod_ref, g_ref, w_ref, b_ref, cw_ref, cb_ref, x0_ref, vg_ref, *, n_ctx, ctx_row):
    tm = x_ref.shape[0]
    width = x0_ref.shape[1]
    m = mod_ref[...]
    h = _norm_mod(x_ref[...], g_ref[...], m[1:2], m[0:1])
    z = jnp.dot(h.astype(BF16), w_ref[...], preferred_element_type=F32) + b_ref[...]
    row_len = jnp.where(pl.program_id(0) * tm < n_ctx, ctx_row, GRID_W)
    pos = lax.broadcasted_iota(jnp.int32, (tm, width), 0) & (row_len - 1)
    cw = cw_ref[...]
    cb = cb_ref[...]
    parts = [_dwconv_rows(z[:, k * width:(k + 1) * width], cw[:, k * width:(k + 1) * width],
                          cb[:, k * width:(k + 1) * width], pos, row_len, 1) for k in range(3)]
    x0_ref[...] = parts[0]
    vg_ref[...] = parts[2] * parts[1]


def _c_in(x, mod, g, w_in, b_in, conv_w, conv_b, *, n_ctx, ctx_row, dec_seq, tm):
    t, d = x.shape
    width = w_in.shape[1] // 3
    kern = functools.partial(_c_in_kernel, n_ctx=n_ctx, ctx_row=ctx_row)
    const = lambda i: (0, 0)
    est = 2 * tm * d * 4 + 4 * d * 3 * width + 4 * tm * width * 4 + 6 * tm * 3 * width * 4
    return pl.pallas_call(
        kern,
        grid=(t // tm,),
        in_specs=[pl.BlockSpec((tm, d), lambda i: (i, 0)),
                  pl.BlockSpec((None, 6, d), lambda i: (_cond_index(i * tm, n_ctx, dec_seq), 0, 0)),
                  pl.BlockSpec((1, d), const),
                  pl.BlockSpec(w_in.shape, const),
                  pl.BlockSpec((1, 3 * width), const),
                  pl.BlockSpec(conv_w.shape, const),
                  pl.BlockSpec((1, 3 * width), const)],
        out_specs=[pl.BlockSpec((tm, width), lambda i: (i, 0))] * 2,
        out_shape=[jax.ShapeDtypeStruct((t, width), F32)] * 2,
        compiler_params=_cparams(est, ("parallel",)),
        name="c_in",
    )(x, mod, g, w_in, b_in, conv_w, conv_b)


def _c_out_kernel(x_ref, mod_ref, x0_ref, vg_ref, yc_ref, hb_ref, w_ref, b_ref, o_ref):
    m = mod_ref[...]
    vg = vg_ref[...]
    y = (yc_ref[...] + hb_ref[...] * vg) * x0_ref[...]
    out = jnp.dot(y.astype(BF16), w_ref[...], preferred_element_type=F32) + b_ref[...]
    o_ref[...] = x_ref[...] + m[2:3] * out


def _c_out(x, mod, x0, vg, yconv, hy_bias, w_out, b_out, *, n_ctx, dec_seq, tm):
    t, d = x.shape
    width = x0.shape[1]
    const = lambda i: (0, 0)
    row = lambda i: (i, 0)
    est = 4 * tm * d * 4 + 6 * tm * width * 4 + 4 * width * d + 4 * tm * width * 4
    return pl.pallas_call(
        _c_out_kernel,
        grid=(t // tm,),
        in_specs=[pl.BlockSpec((tm, d), row),
                  pl.BlockSpec((None, 6, d), lambda i: (_cond_index(i * tm, n_ctx, dec_seq), 0, 0)),
                  pl.BlockSpec((tm, width), row),
                  pl.BlockSpec((tm, width), row),
                  pl.BlockSpec((tm, width), row),
                  pl.BlockSpec((1, width), const),
                  pl.BlockSpec(w_out.shape, const),
                  pl.BlockSpec((1, d), const)],
        out_specs=pl.BlockSpec((tm, d), row),
        out_shape=jax.ShapeDtypeStruct((t, d), F32),
        compiler_params=_cparams(est, ("parallel",)),
        name="c_out",
    )(x, mod, x0, vg, yconv, hy_bias, w_out, b_out)


def _hy_filter_kernel(w1_ref, b1_ref, f1_ref, w2_ref, b2_ref, f2_ref, w3_ref, dec_ref, o_ref, *, length):
    ct = o_ref.shape[1]
    nb = HY_BANDS
    ecols = w1_ref.shape[0]

    def half(pos, side):
        posf = pos.astype(F32)
        col = lax.broadcasted_iota(jnp.int32, (length, ecols), 1)
        band_idx = jnp.where(col <= nb, col - 1, col - 1 - nb).astype(F32)
        bands = 1e-4 + band_idx * ((nb - 1 - 1e-4) / (nb - 1))
        t = posf / length
        ang = (2.0 * math.pi / length) * posf * bands
        emb = jnp.where(col == 0, t,
                        jnp.where(col <= nb, jnp.cos(ang),
                                  jnp.where(col <= 2 * nb, -jnp.sin(ang), 0.0)))
        z = jnp.sin(f1_ref[...] * (jnp.dot(emb, w1_ref[...], precision=HIGHEST,
                                           preferred_element_type=F32) + b1_ref[...]))
        z = jnp.sin(f2_ref[...] * (jnp.dot(z, w2_ref[...], precision=HIGHEST,
                                           preferred_element_type=F32) + b2_ref[...]))
        filt = jnp.dot(z, w3_ref[side], precision=HIGHEST, preferred_element_type=F32)
        return filt * (jnp.exp(-t * jnp.abs(dec_ref[side:side + 1, :])) + HY_SHIFT)

    row = lax.broadcasted_iota(jnp.int32, (length, 1), 0)
    fwd = half(row, 0)
    bwd = jnp.where(lax.broadcasted_iota(jnp.int32, (length, ct), 0) == 0, 0.0, half(length - row, 1))
    ss = jnp.sum(fwd * fwd, axis=0, keepdims=True) + jnp.sum(bwd * bwd, axis=0, keepdims=True)
    scale = lax.rsqrt(ss + EPS)
    o_ref[:length, :] = fwd * scale
    o_ref[length:, :] = bwd * scale


def _hy_filter(length, w1, b1, f1, w2, b2, f2, w3, decay, *, ct):
    emb, ff = w1.shape
    width = decay.shape[1]
    ecols = 128
    w1p = jnp.zeros((ecols, ff), F32).at[:emb].set(w1)
    w3s = w3.reshape(ff, 2, width).transpose(1, 0, 2)
    kern = functools.partial(_hy_filter_kernel, length=length)
    const = lambda j: (0, 0)
    est = 10 * length * ct * 4 + 4 * length * ecols * 4
    return pl.pallas_call(
        kern,
        grid=(width // ct,),
        in_specs=[pl.BlockSpec((ecols, ff), const),
                  pl.BlockSpec((1, ff), const),
                  pl.BlockSpec((1, ff), const),
                  pl.BlockSpec((ff, ff), const),
                  pl.BlockSpec((1, ff), const),
                  pl.BlockSpec((1, ff), const),
                  pl.BlockSpec((2, ff, ct), lambda j: (0, 0, j)),
                  pl.BlockSpec((2, ct), lambda j: (0, j))],
        out_specs=pl.BlockSpec((2 * length, ct), lambda j: (0, j)),
        out_shape=jax.ShapeDtypeStruct((2 * length, width), F32),
        compiler_params=_cparams(est, ("parallel",)),
        name="hy_filter",
    )(w1p, b1.reshape(1, ff), f1.reshape(1, ff), w2, b2.reshape(1, ff), f2.reshape(1, ff), w3s, decay)


def _fftconv_kernel(ct_ref, st_ref, x_ref, k_ref, c_ref, s_ref, twr_ref, twi_ref, o_ref, *scratch, n1):
    kfr, kfi = scratch if scratch else (None, None)
    h1 = n1 // 2
    n2 = c_ref.shape[0]
    n = n1 * n2
    cm = c_ref[...]
    sm = s_ref[...]

    def dft(ar, ai, inverse):
        ab, bb = ar.astype(BF16), ai.astype(BF16)
        ca = jnp.dot(cm, ab, preferred_element_type=F32)
        sa = jnp.dot(sm, ab, preferred_element_type=F32)
        cb = jnp.dot(cm, bb, preferred_element_type=F32)
        sb = jnp.dot(sm, bb, preferred_element_type=F32)
        if inverse:
            return ca - sb, cb + sa
        return ca + sb, cb - sa

    def slab(t1):
        return slice(t1 * n2, (t1 + 1) * n2)

    twr1, twi1 = twr_ref[...], twi_ref[...]
    tw0 = (jnp.ones_like(twr1), jnp.zeros_like(twi1))

    def next_twiddle(tw):
        return tw[0] * twr1 - tw[1] * twi1, tw[0] * twi1 + tw[1] * twr1

    def filter_dft(k1, tw):
        ar = ai = None
        for t1 in range(n1):
            kr = k_ref[slab(t1), :]
            tr, ti = ct_ref[k1, t1] * kr, -(st_ref[k1, t1] * kr)
            ar = tr if ar is None else ar + tr
            ai = ti if ai is None else ai + ti
        ar, ai = ar * (1.0 / n), ai * (1.0 / n)
        return dft(ar * tw[0] + ai * tw[1], ai * tw[0] - ar * tw[1], False)

    if kfr is not None:
        @pl.when(pl.program_id(1) == 0)
        def _():
            def fbody(k1, tw):
                kfr[k1], kfi[k1] = filter_dft(k1, tw)
                return next_twiddle(tw)
            lax.fori_loop(0, n1, fbody, tw0)

    o_ref[...] = jnp.zeros_like(o_ref)

    def body(k1, tw):
        twr, twi = tw
        ar = ai = None
        for t1 in range(h1):
            cc, ss = ct_ref[k1, t1], st_ref[k1, t1]
            xr, xi = x_ref[0, slab(t1), :], x_ref[1, slab(t1), :]
            tr, ti = cc * xr + ss * xi, cc * xi - ss * xr
            ar = tr if ar is None else ar + tr
            ai = ti if ai is None else ai + ti
        fr, fi = dft(ar * twr + ai * twi, ai * twr - ar * twi, False)
        gr, gi = (kfr[k1], kfi[k1]) if kfr is not None else filter_dft(k1, tw)
        br, bi = dft(fr * gr - fi * gi, fr * gi + fi * gr, True)
        br, bi = br * twr - bi * twi, bi * twr + br * twi
        for t1 in range(h1):
            cc, ss = ct_ref[k1, t1], st_ref[k1, t1]
            o_ref[0, slab(t1), :] += cc * br - ss * bi
            o_ref[1, slab(t1), :] += cc * bi + ss * br
        return next_twiddle(tw)

    lax.fori_loop(0, n1, body, tw0)


def _fftconv(v, circ, *, n1, ct):
    bsz, length, width = v.shape
    n = 2 * length
    n2 = n // n1
    idx = np.arange(n2)
    ang = 2.0 * np.pi * ((idx[:, None] * idx[None, :]) % n2) / n2
    cmat = jnp.asarray(np.cos(ang), BF16)
    smat = jnp.asarray(np.sin(ang), BF16)
    tang = 2.0 * np.pi * idx / n
    twr = jnp.asarray(np.broadcast_to(np.cos(tang)[:, None], (n2, ct)), F32)
    twi = jnp.asarray(np.broadcast_to(np.sin(tang)[:, None], (n2, ct)), F32)
    i1 = np.arange(n1)
    ang1 = 2.0 * np.pi * ((i1[:, None] * i1[None, :]) % n1) / n1
    ctab = jnp.asarray(np.round(np.cos(ang1), 12), F32)
    stab = jnp.asarray(np.round(np.sin(ang1), 12), F32)
    kern = functools.partial(_fftconv_kernel, n1=n1)
    const = lambda j, p: (0, 0)
    once = pl.Buffered(1)
    npairs = bsz // 2
    scratch = [pltpu.VMEM((n1, n2, ct), F32)] * 2 if npairs > 1 else []
    est = (4 * 2 * length * ct * 4 + n * ct * 4 + 2 * n2 * n2 * 2 + 2 * n2 * ct * 4
           + len(scratch) * n * ct * 4 + (8 + 2 * n1) * n2 * ct * 4)
    return pl.pallas_call(
        kern,
        grid=(width // ct, npairs),
        in_specs=[pl.BlockSpec(memory_space=pltpu.SMEM),
                  pl.BlockSpec(memory_space=pltpu.SMEM),
                  pl.BlockSpec((2, length, ct), lambda j, p: (p, 0, j)),
                  pl.BlockSpec((n, ct), lambda j, p: (0, j), pipeline_mode=once),
                  pl.BlockSpec((n2, n2), const, pipeline_mode=once),
                  pl.BlockSpec((n2, n2), const, pipeline_mode=once),
                  pl.BlockSpec((n2, ct), const, pipeline_mode=once),
                  pl.BlockSpec((n2, ct), const, pipeline_mode=once)],
        out_specs=pl.BlockSpec((2, length, ct), lambda j, p: (p, 0, j)),
        out_shape=jax.ShapeDtypeStruct((bsz, length, width), F32),
        scratch_shapes=scratch,
        compiler_params=_cparams(est, ("arbitrary", "arbitrary")),
        name="fftconv",
    )(ctab, stab, v, circ, cmat, smat, twr, twi)


def _peer_route_kernel(x_ref, mod_ref, g_ref, wq_ref, keys_ref, hb_ref, th_ref, e0_ref, s1_ref, e1_ref,
                       q_scr, sc_scr, top_scr):
    tq = x_ref.shape[0]
    nk = PEER_NKEYS
    k = PEER_TOPK
    nhp = keys_ref.shape[0]
    dkh = keys_ref.shape[2]
    neg = -jnp.inf
    m = mod_ref[...]
    h = _norm_mod(x_ref[...], g_ref[...], m[4:5], m[3:4]).astype(BF16)
    hb_ref[...] = h
    q = jnp.dot(h, wq_ref[...], preferred_element_type=F32)
    for hp in range(nhp):
        q_scr[hp] = q[:, hp * dkh:(hp + 1) * dkh].astype(BF16)

    def take_max(vals, iota):
        mx = jnp.max(vals, axis=0, keepdims=True)
        first = jnp.min(jnp.where(vals == mx, iota, vals.shape[0]), axis=0, keepdims=True)
        return mx, jnp.where(iota == first, neg, vals)

    iota_k = lax.broadcasted_iota(jnp.int32, (nk, tq), 0)

    def half_body(hp, carry):
        s = lax.dot_general(keys_ref[hp], q_scr[hp], (((1,), (1,)), ((), ())),
                            preferred_element_type=F32)
        sc_scr[hp] = s
        vals = s
        for r in range(k):
            mx, vals = take_max(vals, iota_k)
            top_scr[hp, r:r + 1, :] = mx
        return carry

    lax.fori_loop(0, nhp, half_body, 0)

    sub = V7X_SUBLANES
    assert k == 2 * sub
    ncand = k + 7 * sub + (k - sub)
    iota_c = lax.broadcasted_iota(jnp.int32, (ncand, tq), 0)
    iota_8 = lax.broadcasted_iota(jnp.int32, (sub, tq), 0)

    def head_body(hd, carry):
        a = top_scr[2 * hd]
        b = top_scr[2 * hd + 1]
        pieces = [a[0:1] + b, a[1:2] + b[0:sub]]
        for i in range(2, sub):
            pieces.append(jnp.where(iota_8 < k // (i + 1), a[i:i + 1] + b[0:sub], neg))
        pieces.append(a[sub:k] + b[0:1])
        vals = jnp.concatenate(pieces, axis=0)
        vmax = a[0:1] + b[0:1]
        z = jnp.zeros((1, tq), F32)
        mx = vmax
        for r in range(k):
            mx, vals = take_max(vals, iota_c)
            z = z + jnp.exp(mx - vmax)
        nxt = jnp.max(vals, axis=0, keepdims=True)
        tau = jnp.where(nxt == neg, mx, 0.5 * (mx + nxt))
        s0 = sc_scr[2 * hd]
        s1 = sc_scr[2 * hd + 1]
        th_ref[hd] = tau - s0
        e0_ref[hd] = jnp.exp(s0 - a[0:1]) / z
        s1_ref[hd] = s1
        e1_ref[hd] = jnp.exp(s1 - b[0:1])
        return carry

    lax.fori_loop(0, nhp // 2, head_body, 0)


def _peer_route(x, mod, g, wq, keys, *, n_ctx, dec_seq, tq):
    t, d = x.shape
    nhp, nk, dkh = keys.shape
    nh = nhp // 2
    const = lambda i: (0, 0)
    hshape = jax.ShapeDtypeStruct((nh, nk, t), F32)
    hspec = pl.BlockSpec((nh, nk, tq), lambda i: (0, 0, i))
    est = (2 * tq * d * 4 + 4 * d * nhp * dkh + 2 * tq * d * 2 + 8 * nh * nk * tq * 4
           + 2 * nhp * nk * tq * 4 + 6 * tq * nhp * dkh * 4)
    return pl.pallas_call(
        _peer_route_kernel,
        grid=(t // tq,),
        in_specs=[pl.BlockSpec((tq, d), lambda i: (i, 0)),
                  pl.BlockSpec((None, 6, d), lambda i: (_cond_index(i * tq, n_ctx, dec_seq), 0, 0)),
                  pl.BlockSpec((1, d), const),
                  pl.BlockSpec(wq.shape, const),
                  pl.BlockSpec(keys.shape, lambda i: (0, 0, 0))],
        out_specs=[pl.BlockSpec((tq, d), lambda i: (i, 0)), hspec, hspec, hspec, hspec],
        out_shape=[jax.ShapeDtypeStruct((t, d), BF16), hshape, hshape, hshape, hshape],
        scratch_shapes=[pltpu.VMEM((nhp, tq, dkh), BF16),
                        pltpu.VMEM((nhp, nk, tq), F32),
                        pltpu.VMEM((nhp, PEER_TOPK, tq), F32)],
        compiler_params=_cparams(est, ("parallel",)),
        name="peer_route",
    )(x, mod, g, wq, keys)


def _peer_expert_kernel(hb_ref, u_ref, vt_ref, th_ref, e0_ref, s1_ref, e1_ref, x_ref, mod_ref, o_ref,
                        acc, hbuf):
    c = pl.program_id(1)
    nh = th_ref.shape[0]
    nk = s1_ref.shape[1]
    slabs = th_ref.shape[1]

    @pl.when(c == 0)
    def _():
        acc[...] = jnp.zeros_like(acc)

    st = lax.dot_general(u_ref[...], hb_ref[...], (((1,), (1,)), ((), ())),
                         preferred_element_type=F32)
    for i in range(slabs):
        w = None
        for hd in range(nh):
            sel = jnp.where(s1_ref[hd] >= th_ref[hd, i:i + 1, :], e1_ref[hd], 0.0) * e0_ref[hd, i:i + 1, :]
            w = sel if w is None else w + sel
        hbuf[i * nk:(i + 1) * nk, :] = (_gelu(st[i * nk:(i + 1) * nk, :]) * w).astype(BF16)
    acc[...] += jnp.dot(vt_ref[...], hbuf[...], preferred_element_type=F32)

    @pl.when(c == pl.num_programs(1) - 1)
    def _():
        o_ref[...] = x_ref[...] + mod_ref[5:6, :] * acc[...].T


def _peer_expert(hb, u_tab, vt_tab, th, e0, s1, e1, x, mod, *, n_ctx, dec_seq, tq, ec):
    t, d = x.shape
    ne = u_tab.shape[0]
    nh, nk, _ = th.shape
    slabs = ec // nk
    est = (2 * tq * d * 2 + 4 * ec * d * 2 + 4 * nh * slabs * tq * 4 + 4 * nh * nk * tq * 4
           + 4 * tq * d * 4 + d * tq * 4 + ec * tq * 2 + 3 * ec * tq * 4)
    return pl.pallas_call(
        _peer_expert_kernel,
        grid=(t // tq, ne // ec),
        in_specs=[pl.BlockSpec((tq, d), lambda i, c: (i, 0)),
                  pl.BlockSpec((ec, d), lambda i, c: (c, 0)),
                  pl.BlockSpec((d, ec), lambda i, c: (0, c)),
                  pl.BlockSpec((nh, slabs, tq), lambda i, c: (0, c, i)),
                  pl.BlockSpec((nh, slabs, tq), lambda i, c: (0, c, i)),
                  pl.BlockSpec((nh, nk, tq), lambda i, c: (0, 0, i)),
                  pl.BlockSpec((nh, nk, tq), lambda i, c: (0, 0, i)),
                  pl.BlockSpec((tq, d), lambda i, c: (i, 0)),
                  pl.BlockSpec((None, 6, d), lambda i, c: (_cond_index(i * tq, n_ctx, dec_seq), 0, 0))],
        out_specs=pl.BlockSpec((tq, d), lambda i, c: (i, 0)),
        out_shape=jax.ShapeDtypeStruct((t, d), F32),
        scratch_shapes=[pltpu.VMEM((d, tq), F32), pltpu.VMEM((ec, tq), BF16)],
        compiler_params=_cparams(est, ("parallel", "arbitrary")),
        name="peer_expert",
    )(hb, u_tab, vt_tab, th, e0, s1, e1, x, mod)


def _final_kernel(x_ref, g_ref, o_ref):
    x = x_ref[...]
    ms = jnp.mean(x * x, axis=-1, keepdims=True)
    o_ref[...] = (x * lax.rsqrt(ms + EPS)) * g_ref[...]


def _final_norm(x, g, *, tm):
    t, d = x.shape
    return pl.pallas_call(
        _final_kernel,
        grid=(t // tm,),
        in_specs=[pl.BlockSpec((tm, d), lambda i: (i, 0)), pl.BlockSpec((1, d), lambda i: (0, 0))],
        out_specs=pl.BlockSpec((tm, d), lambda i: (i, 0)),
        out_shape=jax.ShapeDtypeStruct((t, d), F32),
        compiler_params=_cparams(6 * tm * d * 4, ("parallel",)),
        name="final_norm",
    )(x, g)


def _block_diag(w):
    h, i, j = w.shape
    return jnp.einsum('hij,hg->higj', w, jnp.eye(h, dtype=w.dtype)).reshape(h * i, h * j)


def _s5_discretize(a_re, a_im, log_dt, b_re, b_im):
    dt = jnp.exp(log_dt)[:, None]
    mag = jnp.exp(a_re * dt)
    abar_r = mag * jnp.cos(a_im * dt)
    abar_i = mag * jnp.sin(a_im * dt)
    den = a_re * a_re + a_im * a_im
    num_r = abar_r - 1.0
    coef_r = ((num_r * a_re + abar_i * a_im) / den)[..., None]
    coef_i = ((abar_i * a_re - num_r * a_im) / den)[..., None]
    return abar_r, abar_i, coef_r * b_re - coef_i * b_im, coef_r * b_im + coef_i * b_re


def _s5_blocks(abar_r, abar_i, bbar_r, bbar_i, c_re, c_im):
    g, p, ch = bbar_r.shape
    gb = g // S5_COLBLK

    def vec(v):
        return v.reshape(S5_COLBLK, 1, gb * p)

    def bmat(b):
        bt = b.transpose(0, 2, 1).reshape(S5_COLBLK, gb, ch, p)
        return jnp.stack([_block_diag(bt[j]) for j in range(S5_COLBLK)])

    def cmat(cc):
        ct = cc.transpose(0, 2, 1).reshape(S5_COLBLK, gb, p, ch)
        return jnp.stack([_block_diag(ct[j]) for j in range(S5_COLBLK)])

    bblk = jnp.concatenate([bmat(bbar_r), bmat(bbar_i)], axis=2).astype(BF16)
    cblk = jnp.concatenate([cmat(c_re), cmat(-c_im)], axis=1).astype(BF16)
    return bblk, cblk, vec(abar_r), vec(abar_i)


def _state_to_blocked(re, im):
    n, g, p = re.shape
    gb = g // S5_COLBLK
    both = jnp.stack([re.reshape(n, S5_COLBLK, gb * p), im.reshape(n, S5_COLBLK, gb * p)], axis=2)
    return both.reshape(n, -1)


def _state_from_blocked(st, g, p):
    n = st.shape[0]
    both = st.reshape(n, S5_COLBLK, 2, (g // S5_COLBLK) * p)
    return both[:, :, 0].reshape(n, g, p), both[:, :, 1].reshape(n, g, p)


def _complex_power(ar, ai, n):
    for _ in range(int(math.log2(n))):
        ar, ai = ar * ar - ai * ai, 2.0 * ar * ai
    return ar, ai


def _mixer_ab(x, mod, p, le, h0, *, bc, lc, bs, ls, tm):
    n_ctx = bc * lc
    width = p['s5_d'].shape[1]
    g_cnt, p_cnt = p['s5_a_re'].shape[2:]
    w_gates = jnp.concatenate([_block_diag(p['lru_w_a'][le, 0]), _block_diag(p['lru_w_x'][le, 0]),
                               _block_diag(p['lru_w_a'][le, 1]), _block_diag(p['lru_w_x'][le, 1])],
                              axis=1).astype(BF16)
    b_gates = jnp.concatenate([p['lru_b_a'][le, 0], p['lru_b_x'][le, 0],
                               p['lru_b_a'][le, 1], p['lru_b_x'][le, 1]]).reshape(1, 4 * width)
    c8 = LRU_C * jax.nn.softplus(-p['lru_lambda'][le])
    s_arr, xg = _ab_in(x, mod, p['norm1_g'][2 * le].reshape(1, -1), p['w_in_ab'][le].astype(BF16),
                       p['lru_conv_w'][le], p['lru_conv_b'][le].reshape(1, width), w_gates, b_gates, c8,
                       n_ctx=n_ctx, ctx_row=lc, dec_seq=ls, tm=tm)

    cols = s_arr.shape[1]
    gc = bc // SCAN_SEQS
    lseg = ls // SAMPLE_SEGS
    s_ctx = (s_arr[:n_ctx].reshape(gc, SCAN_SEQS, lc, cols).transpose(0, 2, 1, 3)
             .reshape(n_ctx, cols))
    s_smp = (s_arr[n_ctx:].reshape(bs * SAMPLE_SEGS, lseg, cols).transpose(1, 0, 2)
             .reshape(bs * ls, cols))
    chunk_rows = min(1024, lc * SCAN_SEQS, lseg * SCAN_SEQS)

    outs = {}
    finals = {}
    for d in range(2):
        abar_r, abar_i, bbar_r, bbar_i = _s5_discretize(
            p['s5_a_re'][le, d], p['s5_a_im'][le, d], p['s5_log_dt'][le, d],
            p['s5_b_re'][le, d], p['s5_b_im'][le, d])
        bblk, cblk, ar, ai = _s5_blocks(abar_r, abar_i, bbar_r, bbar_i,
                                        p['s5_c_re'][le, d], p['s5_c_im'][le, d])
        nstate = 2 * g_cnt * p_cnt
        z5 = jnp.zeros((gc, SCAN_SEQS, nstate), F32)
        zl = jnp.zeros((gc, SCAN_SEQS, width), F32)
        y, hl, f5, fl, _ = _scan(s_ctx, bblk, cblk, ar, ai, z5, zl, direction=d, store=True,
                                 groups=gc, chunk_rows=chunk_rows)
        outs[('ctx', d)] = (y, hl)
        finals[d] = (f5.reshape(bc, nstate), fl.reshape(bc, width))
        z5 = jnp.zeros((1, SCAN_SEQS, nstate), F32)
        zl = jnp.zeros((1, SCAN_SEQS, width), F32)
        e5, el, ep = _scan(s_smp, bblk, cblk, ar, ai, z5, zl, direction=d, store=False,
                           groups=1, chunk_rows=chunk_rows)
        first = (SAMPLE_SEGS - 1) if d == 1 else 0
        h5 = _state_to_blocked(h0[0][:, d], h0[1][:, d])
        h5_rows = jnp.zeros((bs, SAMPLE_SEGS, nstate), F32).at[:, first].set(h5).reshape(SCAN_SEQS, nstate)
        hl_rows = jnp.zeros((bs, SAMPLE_SEGS, width), F32).at[:, first].set(h0[2][:, d]).reshape(SCAN_SEQS, width)
        arl, ail = _complex_power(ar, ai, lseg)
        i5, il = _chain(e5[0], el[0], ep[0], h5_rows, hl_rows, arl, ail, direction=d)
        y, hl, _, _, _ = _scan(s_smp, bblk, cblk, ar, ai, i5[None], il[None], direction=d, store=True,
                               groups=1, chunk_rows=chunk_rows)
        outs[('smp', d)] = (y, hl)

    def to_tokens(ctx_arr, smp_arr):
        c = ctx_arr.reshape(gc, lc, SCAN_SEQS, width).transpose(0, 2, 1, 3).reshape(n_ctx, width)
        s = smp_arr.reshape(lseg, bs * SAMPLE_SEGS, width).transpose(1, 0, 2).reshape(bs * ls, width)
        return jnp.concatenate([c, s], axis=0)

    yf = to_tokens(outs[('ctx', 0)][0], outs[('smp', 0)][0])
    yb = to_tokens(outs[('ctx', 1)][0], outs[('smp', 1)][0])
    hf = to_tokens(outs[('ctx', 0)][1], outs[('smp', 0)][1])
    hb = to_tokens(outs[('ctx', 1)][1], outs[('smp', 1)][1])
    x1 = _ab_out(x, mod, s_arr, xg, yf, yb, hf, hb, p['s5_d'][le].reshape(1, width),
                 p['s5_w_glu'][le].astype(BF16), p['s5_b_glu'][le].reshape(1, width),
                 p['w_out_ab'][le].astype(BF16), n_ctx=n_ctx, dec_seq=ls, tm=tm)
    st = [_state_from_blocked(finals[d][0], g_cnt, p_cnt) for d in range(2)]
    new_re = jnp.stack([st[0][0], st[1][0]], axis=1)
    new_im = jnp.stack([st[0][1], st[1][1]], axis=1)
    new_lru = jnp.stack([finals[0][1], finals[1][1]], axis=1)
    return x1, (new_re, new_im, new_lru)


def _fft_split(length):
    n = 2 * length
    n1 = 2
    while n // n1 > 1024:
        n1 *= 2
    return n1


def _mixer_c(x, mod, p, lo, *, bc, lc, bs, ls, tm):
    n_ctx = bc * lc
    width = p['hy_bias'].shape[1]
    x0, vg = _c_in(x, mod, p['norm1_g'][2 * lo + 1].reshape(1, -1), p['w_in_c'][lo].astype(BF16),
                   p['b_in_c'][lo].reshape(1, -1), p['hy_conv_w'][lo], p['hy_conv_b'][lo].reshape(1, -1),
                   n_ctx=n_ctx, ctx_row=lc, dec_seq=ls, tm=tm)
    ys = []
    for (b, l, v) in ((bc, lc, vg[:n_ctx]), (bs, ls, vg[n_ctx:])):
        circ = _hy_filter(l, p['hy_w1'][lo], p['hy_b1'][lo], p['hy_freq1'][lo], p['hy_w2'][lo],
                          p['hy_b2'][lo], p['hy_freq2'][lo], p['hy_w3'][lo], p['hy_decay'][lo],
                          ct=min(width, 256 if l <= 1024 else 128))
        y = _fftconv(v.reshape(b, l, width), circ, n1=_fft_split(l), ct=128)
        ys.append(y.reshape(b * l, width))
    yconv = jnp.concatenate(ys, axis=0)
    return _c_out(x, mod, x0, vg, yconv, p['hy_bias'][lo].reshape(1, width), p['w_out_c'][lo].astype(BF16),
                  p['b_out_c'][lo].reshape(1, -1), n_ctx=n_ctx, dec_seq=ls, tm=tm)


def _peer(x, mod, p, l, *, n_ctx, dec_seq, tq, ec):
    keys = p['peer_keys'][l]
    nh, _, nk, dkh = keys.shape
    hb, th, e0, s1, e1 = _peer_route(x, mod, p['norm2_g'][l].reshape(1, -1), p['peer_wq'][l].astype(BF16),
                                     keys.reshape(nh * 2, nk, dkh).astype(BF16),
                                     n_ctx=n_ctx, dec_seq=dec_seq, tq=tq)
    return _peer_expert(hb, p['peer_u'][l].astype(BF16), p['peer_v'][l].T.astype(BF16), th, e0, s1, e1,
                        x, mod, n_ctx=n_ctx, dec_seq=dec_seq, tq=tq, ec=ec)


def _forward(x_prompt, x_sample, state_s5_re, state_s5_im, state_lru, c, c_ctx, p, *, tm, tq, ec):
    bc, lc, d = x_prompt.shape
    bs, ls, _ = x_sample.shape
    assert bc % SCAN_SEQS == 0 and bs * SAMPLE_SEGS == SCAN_SEQS
    assert tm % lc == 0 and ls % tm == 0 and (ls // SAMPLE_SEGS) % GRID_W == 0
    n_ctx = bc * lc
    x = jnp.concatenate([x_prompt.reshape(n_ctx, d), x_sample.reshape(bs * ls, d)], axis=0)
    cond = jnp.zeros((V7X_SUBLANES, d), F32).at[0].set(c_ctx).at[1:1 + bs].set(c)
    mod_all = _modulation(cond, p['w_mod'], p['b_mod'])
    states = None
    for l in range(DEPTH):
        mod = mod_all[l].reshape(V7X_SUBLANES, 6, d)
        if l % 2 == 0:
            le = l // 2
            h0 = (state_s5_re[:, le], state_s5_im[:, le], state_lru[:, le])
            x, st = _mixer_ab(x, mod, p, le, h0, bc=bc, lc=lc, bs=bs, ls=ls, tm=tm)
            states = st if states is None else states
        else:
            x = _mixer_c(x, mod, p, l // 2, bc=bc, lc=lc, bs=bs, ls=ls, tm=tm)
        x = _peer(x, mod, p, l, n_ctx=n_ctx, dec_seq=ls, tq=tq, ec=ec)
    y = _final_norm(x, p['final_g'].reshape(1, d), tm=tm)
    new_re, new_im, new_lru = states
    return (y[:n_ctx].reshape(bc, lc, d), y[n_ctx:].reshape(bs, ls, d),
            new_re[:, None], new_im[:, None], new_lru[:, None])


def kernel(x_prompt, x_sample, state_s5_re, state_s5_im, state_lru, c, c_ctx, norm1_g, norm2_g, w_mod, b_mod, w_in_ab, s5_a_re, s5_a_im, s5_log_dt, s5_b_re, s5_b_im, s5_c_re, s5_c_im, s5_d, s5_w_glu, s5_b_glu, lru_conv_w, lru_conv_b, lru_w_a, lru_b_a, lru_w_x, lru_b_x, lru_lambda, w_out_ab, w_in_c, b_in_c, hy_conv_w, hy_conv_b, hy_w1, hy_b1, hy_freq1, hy_w2, hy_b2, hy_freq2, hy_w3, hy_decay, hy_bias, w_out_c, b_out_c, peer_wq, peer_keys, peer_u, peer_v, final_g):
    p = dict(norm1_g=norm1_g, norm2_g=norm2_g, w_mod=w_mod, b_mod=b_mod, w_in_ab=w_in_ab,
             s5_a_re=s5_a_re, s5_a_im=s5_a_im, s5_log_dt=s5_log_dt, s5_b_re=s5_b_re, s5_b_im=s5_b_im,
             s5_c_re=s5_c_re, s5_c_im=s5_c_im, s5_d=s5_d, s5_w_glu=s5_w_glu, s5_b_glu=s5_b_glu,
             lru_conv_w=lru_conv_w, lru_conv_b=lru_conv_b, lru_w_a=lru_w_a, lru_b_a=lru_b_a,
             lru_w_x=lru_w_x, lru_b_x=lru_b_x, lru_lambda=lru_lambda, w_out_ab=w_out_ab,
             w_in_c=w_in_c, b_in_c=b_in_c, hy_conv_w=hy_conv_w, hy_conv_b=hy_conv_b,
             hy_w1=hy_w1, hy_b1=hy_b1, hy_freq1=hy_freq1, hy_w2=hy_w2, hy_b2=hy_b2, hy_freq2=hy_freq2,
             hy_w3=hy_w3, hy_decay=hy_decay, hy_bias=hy_bias, w_out_c=w_out_c, b_out_c=b_out_c,
             peer_wq=peer_wq, peer_keys=peer_keys, peer_u=peer_u, peer_v=peer_v, final_g=final_g)
    return _forward(x_prompt, x_sample, state_s5_re, state_s5_im, state_lru, c, c_ctx, p,
                    tm=512, tq=512, ec=1024)
```

```python
import functools
import math

import jax
import jax.numpy as jnp
import numpy as np
from jax import lax
from jax.experimental import pallas as pl
from jax.experimental.pallas import tpu as pltpu

F32 = jnp.float32
BF16 = jnp.bfloat16
HIGHEST = lax.Precision.HIGHEST

DEPTH = 2
GRID_W = 64
EPS = 1e-6
S5_CH = 16
S5_STATE = 64
LRU_HEADS = 8
LRU_C = 8.0
HY_BANDS = 16
HY_SHIFT = 0.05
PEER_HEADS = 8
PEER_NKEYS = 128
PEER_TOPK = 16
PEER_GROUP = 512

V7X_LANES = 128
V7X_SUBLANES = 8
V7X_VMEM_BYTES = 64 * 1024 * 1024

SCAN_SEQS = V7X_SUBLANES
SAMPLE_SEGS = 4
S5_COLBLK = 4


def _vmem_limit(est_bytes):
    return int(min(V7X_VMEM_BYTES - (8 << 20), max(32 << 20, est_bytes * 3 // 2)))


def _cparams(est_bytes, sem):
    return pltpu.CompilerParams(dimension_semantics=sem, vmem_limit_bytes=_vmem_limit(est_bytes))


def _gelu(x):
    return 0.5 * x * (1.0 + jnp.tanh(0.7978845608028654 * (x + 0.044715 * (x * x * x))))


def _norm_mod(x, g, sc, sh):
    ms = jnp.mean(x * x, axis=-1, keepdims=True)
    return (x * lax.rsqrt(ms + EPS)) * g * (1.0 + sc) + sh


def _cond_index(row0, n_ctx, dec_seq):
    return jnp.where(row0 < n_ctx, 0, 1 + (row0 - n_ctx) // dec_seq)


def _mod_kernel(c_ref, w_ref, b_ref, o_ref):
    c = c_ref[...]
    s = c * jax.nn.sigmoid(c)
    o_ref[...] = jnp.dot(s, w_ref[...], precision=HIGHEST, preferred_element_type=F32) + b_ref[...]


def _modulation(cond, w_mod, b_mod):
    depth, d, n = w_mod.shape
    rows = cond.shape[0]
    tn = 1536
    est = 2 * d * tn * 4 + 4 * rows * (d + tn) * 4
    return pl.pallas_call(
        _mod_kernel,
        grid=(depth, n // tn),
        in_specs=[pl.BlockSpec((rows, d), lambda l, j: (0, 0)),
                  pl.BlockSpec((None, d, tn), lambda l, j: (l, 0, j)),
                  pl.BlockSpec((None, 1, tn), lambda l, j: (l, 0, j))],
        out_specs=pl.BlockSpec((None, rows, tn), lambda l, j: (l, 0, j)),
        out_shape=jax.ShapeDtypeStruct((depth, rows, n), F32),
        compiler_params=_cparams(est, ("arbitrary", "arbitrary")),
        name="modulation",
    )(cond, w_mod, b_mod.reshape(depth, 1, n))


def _dwconv_rows(x, w, bias, pos, row_len, pad_left):
    tm = x.shape[0]
    acc = None
    for k in range(w.shape[0]):
        d = k - pad_left
        if d == 0:
            term = x
        else:
            shifted = pltpu.roll(x, (-d) % tm, axis=0)
            q = pos + d
            term = jnp.where(q >= 0, jnp.where(q < row_len, shifted, 0.0), 0.0)
        term = term * w[k:k + 1]
        acc = term if acc is None else acc + term
    return acc + bias


def _ab_in_kernel(x_ref, mod_ref, g_ref, w_ref, cw_ref, cb_ref, wg_ref, bg_ref, c8_ref,
                  s_ref, xg_ref, *, n_ctx, ctx_row, width):
    tm = x_ref.shape[0]
    m = mod_ref[...]
    h = _norm_mod(x_ref[...], g_ref[...], m[1:2], m[0:1])
    z = jnp.dot(h.astype(BF16), w_ref[...], preferred_element_type=F32)
    u = z[:, :width]
    xr = z[:, width:2 * width]
    xg_ref[...] = z[:, 2 * width:]
    row_len = jnp.where(pl.program_id(0) * tm < n_ctx, ctx_row, GRID_W)
    pos = lax.broadcasted_iota(jnp.int32, (tm, width), 0) & (row_len - 1)
    xb = _dwconv_rows(xr, cw_ref[...], cb_ref[...], pos, row_len, 2)
    gates = jnp.dot(xb.astype(BF16), wg_ref[...], preferred_element_type=F32) + bg_ref[...]
    s_ref[:, :width] = u
    c8 = c8_ref[...]
    for d in range(2):
        r = jax.nn.sigmoid(gates[:, (2 * d) * width:(2 * d + 1) * width])
        i = jax.nn.sigmoid(gates[:, (2 * d + 1) * width:(2 * d + 2) * width])
        a = jnp.exp(-(c8[d:d + 1] * r))
        b = jnp.sqrt(1.0 - a * a) * (i * xb)
        s_ref[:, (1 + 2 * d) * width:(2 + 2 * d) * width] = a
        s_ref[:, (2 + 2 * d) * width:(3 + 2 * d) * width] = b


def _ab_in(x, mod, g, w_in, conv_w, conv_b, w_gates, b_gates, c8, *, n_ctx, ctx_row, dec_seq, tm):
    t, d = x.shape
    width = w_in.shape[1] // 3
    kern = functools.partial(_ab_in_kernel, n_ctx=n_ctx, ctx_row=ctx_row, width=width)
    const = lambda i: (0, 0)
    est = (2 * tm * d * 4 + d * 3 * width * 2 * 2 + width * 4 * width * 2 * 2
           + 2 * tm * 6 * width * 4 + tm * 12 * width * 4)
    return pl.pallas_call(
        kern,
        grid=(t // tm,),
        in_specs=[pl.BlockSpec((tm, d), lambda i: (i, 0)),
                  pl.BlockSpec((None, 6, d), lambda i: (_cond_index(i * tm, n_ctx, dec_seq), 0, 0)),
                  pl.BlockSpec((1, d), const),
                  pl.BlockSpec(w_in.shape, const),
                  pl.BlockSpec(conv_w.shape, const),
                  pl.BlockSpec((1, width), const),
                  pl.BlockSpec(w_gates.shape, const),
                  pl.BlockSpec((1, 4 * width), const),
                  pl.BlockSpec((2, width), const)],
        out_specs=[pl.BlockSpec((tm, 5 * width), lambda i: (i, 0)),
                   pl.BlockSpec((tm, width), lambda i: (i, 0))],
        out_shape=[jax.ShapeDtypeStruct((t, 5 * width), F32),
                   jax.ShapeDtypeStruct((t, width), F32)],
        compiler_params=_cparams(est, ("parallel",)),
        name="ab_in",
    )(x, mod, g, w_in, conv_w, conv_b, w_gates, b_gates, c8)


def _scan_kernel(u_ref, a_ref, b_ref, bblk_ref, cblk_ref, ar_ref, ai_ref, i5_ref, il_ref, *rest,
                 reverse, store):
    if store:
        y_ref, hl_ref, f5_ref, fl_ref, fp_ref, buf, st5, stl, stp = rest
    else:
        f5_ref, fl_ref, fp_ref, buf, st5, stl, stp = rest
    c = pl.program_id(1)
    rows = u_ref.shape[0]
    nsteps = rows // SCAN_SEQS
    sw = st5.shape[1] // (2 * S5_COLBLK)
    cw = u_ref.shape[1] // S5_COLBLK

    @pl.when(c == 0)
    def _():
        st5[...] = i5_ref[...]
        stl[...] = il_ref[...]
        stp[...] = jnp.ones_like(stp)

    def row0(k):
        t = (nsteps - 1 - k) if reverse else k
        return pl.multiple_of(t * SCAN_SEQS, SCAN_SEQS)

    u = u_ref[...].astype(BF16)
    for j in range(S5_COLBLK):
        buf[...] = jnp.dot(u[:, j * cw:(j + 1) * cw], bblk_ref[j], preferred_element_type=F32)
        arj = jnp.broadcast_to(ar_ref[j], (SCAN_SEQS, sw))
        aij = jnp.broadcast_to(ai_ref[j], (SCAN_SEQS, sw))

        def step(k, carry, arj=arj, aij=aij):
            hr, hi = carry
            r0 = row0(k)
            xr = buf[pl.ds(r0, SCAN_SEQS), 0:sw]
            xi = buf[pl.ds(r0, SCAN_SEQS), sw:2 * sw]
            nr = arj * hr - aij * hi + xr
            ni = arj * hi + aij * hr + xi
            buf[pl.ds(r0, SCAN_SEQS), 0:sw] = nr
            buf[pl.ds(r0, SCAN_SEQS), sw:2 * sw] = ni
            return nr, ni

        base = 2 * sw * j
        hr, hi = lax.fori_loop(0, nsteps, step, (st5[:, base:base + sw], st5[:, base + sw:base + 2 * sw]))
        st5[:, base:base + sw] = hr
        st5[:, base + sw:base + 2 * sw] = hi
        if store:
            y_ref[:, j * cw:(j + 1) * cw] = jnp.dot(buf[...].astype(BF16), cblk_ref[j],
                                                    preferred_element_type=F32)

    def lstep(k, carry):
        h, p = carry
        r0 = row0(k)
        a = a_ref[pl.ds(r0, SCAN_SEQS), :]
        h = a * h + b_ref[pl.ds(r0, SCAN_SEQS), :]
        if store:
            hl_ref[pl.ds(r0, SCAN_SEQS), :] = h
        return h, p * a

    h, p = lax.fori_loop(0, nsteps, lstep, (stl[...], stp[...]))
    stl[...] = h
    stp[...] = p

    @pl.when(c == pl.num_programs(1) - 1)
    def _():
        f5_ref[...] = st5[...]
        fl_ref[...] = stl[...]
        fp_ref[...] = stp[...]


def _scan(s_arr, bblk, cblk, ar, ai, init5, initl, *, direction, store, groups, chunk_rows):
    rows_total, cols = s_arr.shape
    width = cols // 5
    rows_per_group = rows_total // groups
    nc = rows_per_group // chunk_rows
    reverse = direction == 1
    nstate = init5.shape[-1]

    def chunk(g, c):
        return g * nc + ((nc - 1 - c) if reverse else c)

    kern = functools.partial(_scan_kernel, reverse=reverse, store=store)
    const3 = lambda g, c: (0, 0, 0)
    in_specs = [pl.BlockSpec((chunk_rows, width), lambda g, c: (chunk(g, c), 0)),
                pl.BlockSpec((chunk_rows, width), lambda g, c: (chunk(g, c), 1 + 2 * direction)),
                pl.BlockSpec((chunk_rows, width), lambda g, c: (chunk(g, c), 2 + 2 * direction)),
                pl.BlockSpec(bblk.shape, const3),
                pl.BlockSpec(cblk.shape, const3),
                pl.BlockSpec(ar.shape, const3),
                pl.BlockSpec(ai.shape, const3),
                pl.BlockSpec((None, SCAN_SEQS, nstate), lambda g, c: (g, 0, 0)),
                pl.BlockSpec((None, SCAN_SEQS, width), lambda g, c: (g, 0, 0))]
    fin_specs = [pl.BlockSpec((None, SCAN_SEQS, nstate), lambda g, c: (g, 0, 0)),
                 pl.BlockSpec((None, SCAN_SEQS, width), lambda g, c: (g, 0, 0)),
                 pl.BlockSpec((None, SCAN_SEQS, width), lambda g, c: (g, 0, 0))]
    fin_shapes = [jax.ShapeDtypeStruct((groups, SCAN_SEQS, nstate), F32),
                  jax.ShapeDtypeStruct((groups, SCAN_SEQS, width), F32),
                  jax.ShapeDtypeStruct((groups, SCAN_SEQS, width), F32)]
    out_specs, out_shapes = fin_specs, fin_shapes
    if store:
        out_specs = [pl.BlockSpec((chunk_rows, width), lambda g, c: (chunk(g, c), 0))] * 2 + fin_specs
        out_shapes = [jax.ShapeDtypeStruct((rows_total, width), F32)] * 2 + fin_shapes
    bufw = 2 * nstate // (2 * S5_COLBLK)
    est = (10 * chunk_rows * width * 4 + chunk_rows * bufw * 4 * 2
           + 4 * bblk.size * 2 + 8 * SCAN_SEQS * nstate * 4)
    return pl.pallas_call(
        kern,
        grid=(groups, nc),
        in_specs=in_specs,
        out_specs=out_specs,
        out_shape=out_shapes,
        scratch_shapes=[pltpu.VMEM((chunk_rows, bufw), F32),
                        pltpu.VMEM((SCAN_SEQS, nstate), F32),
                        pltpu.VMEM((SCAN_SEQS, width), F32),
                        pltpu.VMEM((SCAN_SEQS, width), F32)],
        compiler_params=_cparams(est, ("arbitrary", "arbitrary")),
        name="scan_store" if store else "scan_ends",
    )(s_arr, s_arr, s_arr, bblk, cblk, ar, ai, init5, initl)


def _chain_kernel(e5_ref, el_ref, ep_ref, h5_ref, hl_ref, arl_ref, ail_ref, i5_ref, il_ref, *, reverse):
    nseg = SAMPLE_SEGS
    sw = arl_ref.shape[-1]
    shift = (SCAN_SEQS - 1) if reverse else 1
    order = range(nseg - 2, -1, -1) if reverse else range(1, nseg)
    seg5 = lax.broadcasted_iota(jnp.int32, (SCAN_SEQS, sw), 0) % nseg
    for j in range(S5_COLBLK):
        base = 2 * sw * j
        re, im = slice(base, base + sw), slice(base + sw, base + 2 * sw)
        ar, ai = arl_ref[j], ail_ref[j]
        er = pltpu.roll(e5_ref[:, re], shift, axis=0)
        ei = pltpu.roll(e5_ref[:, im], shift, axis=0)
        hr, hi = h5_ref[:, re], h5_ref[:, im]
        for k in order:
            pr = pltpu.roll(hr, shift, axis=0)
            pi = pltpu.roll(hi, shift, axis=0)
            hr = jnp.where(seg5 == k, ar * pr - ai * pi + er, hr)
            hi = jnp.where(seg5 == k, ar * pi + ai * pr + ei, hi)
        i5_ref[:, re] = hr
        i5_ref[:, im] = hi
    segl = lax.broadcasted_iota(jnp.int32, el_ref.shape, 0) % nseg
    el = pltpu.roll(el_ref[...], shift, axis=0)
    ep = pltpu.roll(ep_ref[...], shift, axis=0)
    h = hl_ref[...]
    for k in order:
        h = jnp.where(segl == k, ep * pltpu.roll(h, shift, axis=0) + el, h)
    il_ref[...] = h


def _chain(end5, endl, endp, h5_rows, hl_rows, arl, ail, *, direction):
    kern = functools.partial(_chain_kernel, reverse=direction == 1)
    return pl.pallas_call(
        kern,
        out_shape=[jax.ShapeDtypeStruct(end5.shape, F32), jax.ShapeDtypeStruct(endl.shape, F32)],
        name="scan_chain",
    )(end5, endl, endp, h5_rows, hl_rows, arl, ail)


def _ab_out_kernel(x_ref, mod_ref, s_ref, xg_ref, yf_ref, yb_ref, hf_ref, hb_ref, d_ref,
                   wglu_ref, bglu_ref, wout_ref, o_ref):
    width = xg_ref.shape[1]
    m = mod_ref[...]
    y = s_ref[...] * d_ref[...] + yf_ref[...] + yb_ref[...]
    zs = _gelu(y)
    gate = jnp.dot(zs.astype(BF16), wglu_ref[...], preferred_element_type=F32) + bglu_ref[...]
    s5_out = zs * jax.nn.sigmoid(gate)
    lru_out = (hf_ref[...] + hb_ref[...]) * _gelu(xg_ref[...])
    out = (jnp.dot(s5_out.astype(BF16), wout_ref[:width, :], preferred_element_type=F32)
           + jnp.dot(lru_out.astype(BF16), wout_ref[width:, :], preferred_element_type=F32))
    o_ref[...] = x_ref[...] + m[2:3] * out


def _ab_out(x, mod, s_arr, xg, yf, yb, hf, hb, s5_d, w_glu, b_glu, w_out, *, n_ctx, dec_seq, tm):
    t, d = x.shape
    width = xg.shape[1]
    const = lambda i: (0, 0)
    row = lambda i: (i, 0)
    est = 4 * tm * d * 4 + 12 * tm * width * 4 + 4 * (width * width + 2 * width * d) + 8 * tm * width * 4
    return pl.pallas_call(
        _ab_out_kernel,
        grid=(t // tm,),
        in_specs=[pl.BlockSpec((tm, d), row),
                  pl.BlockSpec((None, 6, d), lambda i: (_cond_index(i * tm, n_ctx, dec_seq), 0, 0)),
                  pl.BlockSpec((tm, width), row),
                  pl.BlockSpec((tm, width), row),
                  pl.BlockSpec((tm, width), row),
                  pl.BlockSpec((tm, width), row),
                  pl.BlockSpec((tm, width), row),
                  pl.BlockSpec((tm, width), row),
                  pl.BlockSpec((1, width), const),
                  pl.BlockSpec(w_glu.shape, const),
                  pl.BlockSpec((1, width), const),
                  pl.BlockSpec(w_out.shape, const)],
        out_specs=pl.BlockSpec((tm, d), row),
        out_shape=jax.ShapeDtypeStruct((t, d), F32),
        compiler_params=_cparams(est, ("parallel",)),
        name="ab_out",
    )(x, mod, s_arr, xg, yf, yb, hf, hb, s5_d, w_glu, b_glu, w_out)


def _c_in_kernel(x_ref, mod_ref, g_ref, w_ref, b_ref, cw_ref, cb_ref, x0_ref, vg_ref, *, n_ctx, ctx_row):
    tm = x_ref.shape[0]
    width = x0_ref.shape[1]
    m = mod_ref[...]
    h = _norm_mod(x_ref[...], g_ref[...], m[1:2], m[0:1])
    z = jnp.dot(h.astype(BF16), w_ref[...], preferred_element_type=F32) + b_ref[...]
    row_len = jnp.where(pl.program_id(0) * tm < n_ctx, ctx_row, GRID_W)
    pos = lax.broadcasted_iota(jnp.int32, (tm, width), 0) & (row_len - 1)
    cw = cw_ref[...]
    cb = cb_ref[...]
    parts = [_dwconv_rows(z[:, k * width:(k + 1) * width], cw[:, k * width:(k + 1) * width],
                          cb[:, k * width:(k + 1) * width], pos, row_len, 1) for k in range(3)]
    x0_ref[...] = parts[0]
    vg_ref[...] = parts[2] * parts[1]


def _c_in(x, mod, g, w_in, b_in, conv_w, conv_b, *, n_ctx, ctx_row, dec_seq, tm):
    t, d = x.shape
    width = w_in.shape[1] // 3
    kern = functools.partial(_c_in_kernel, n_ctx=n_ctx, ctx_row=ctx_row)
    const = lambda i: (0, 0)
    est = 2 * tm * d * 4 + 4 * d * 3 * width + 4 * tm * width * 4 + 6 * tm * 3 * width * 4
    return pl.pallas_call(
        kern,
        grid=(t // tm,),
        in_specs=[pl.BlockSpec((tm, d), lambda i: (i, 0)),
                  pl.BlockSpec((None, 6, d), lambda i: (_cond_index(i * tm, n_ctx, dec_seq), 0, 0)),
                  pl.BlockSpec((1, d), const),
                  pl.BlockSpec(w_in.shape, const),
                  pl.BlockSpec((1, 3 * width), const),
                  pl.BlockSpec(conv_w.shape, const),
                  pl.BlockSpec((1, 3 * width), const)],
        out_specs=[pl.BlockSpec((tm, width), lambda i: (i, 0))] * 2,
        out_shape=[jax.ShapeDtypeStruct((t, width), F32)] * 2,
        compiler_params=_cparams(est, ("parallel",)),
        name="c_in",
    )(x, mod, g, w_in, b_in, conv_w, conv_b)


def _c_out_kernel(x_ref, mod_ref, x0_ref, vg_ref, yc_ref, hb_ref, w_ref, b_ref, o_ref):
    m = mod_ref[...]
    vg = vg_ref[...]
    y = (yc_ref[...] + hb_ref[...] * vg) * x0_ref[...]
    out = jnp.dot(y.astype(BF16), w_ref[...], preferred_element_type=F32) + b_ref[...]
    o_ref[...] = x_ref[...] + m[2:3] * out


def _c_out(x, mod, x0, vg, yconv, hy_bias, w_out, b_out, *, n_ctx, dec_seq, tm):
    t, d = x.shape
    width = x0.shape[1]
    const = lambda i: (0, 0)
    row = lambda i: (i, 0)
    est = 4 * tm * d * 4 + 6 * tm * width * 4 + 4 * width * d + 4 * tm * width * 4
    return pl.pallas_call(
        _c_out_kernel,
        grid=(t // tm,),
        in_specs=[pl.BlockSpec((tm, d), row),
                  pl.BlockSpec((None, 6, d), lambda i: (_cond_index(i * tm, n_ctx, dec_seq), 0, 0)),
                  pl.BlockSpec((tm, width), row),
                  pl.BlockSpec((tm, width), row),
                  pl.BlockSpec((tm, width), row),
                  pl.BlockSpec((1, width), const),
                  pl.BlockSpec(w_out.shape, const),
                  pl.BlockSpec((1, d), const)],
        out_specs=pl.BlockSpec((tm, d), row),
        out_shape=jax.ShapeDtypeStruct((t, d), F32),
        compiler_params=_cparams(est, ("parallel",)),
        name="c_out",
    )(x, mod, x0, vg, yconv, hy_bias, w_out, b_out)


def _hy_filter_kernel(w1_ref, b1_ref, f1_ref, w2_ref, b2_ref, f2_ref, w3_ref, dec_ref, o_ref, z_scr, *, length):
    ct = o_ref.shape[1]
    nb = HY_BANDS
    ecols = w1_ref.shape[0]
    row = lax.broadcasted_iota(jnp.int32, (length, 1), 0)
    positions = (row, length - row)

    @pl.when(pl.program_id(0) == 0)
    def _():
        for side in range(2):
            posf = positions[side].astype(F32)
            col = lax.broadcasted_iota(jnp.int32, (length, ecols), 1)
            band_idx = jnp.where(col <= nb, col - 1, col - 1 - nb).astype(F32)
            bands = 1e-4 + band_idx * ((nb - 1 - 1e-4) / (nb - 1))
            ang = (2.0 * math.pi / length) * posf * bands
            emb = jnp.where(col == 0, posf / length,
                            jnp.where(col <= nb, jnp.cos(ang),
                                      jnp.where(col <= 2 * nb, -jnp.sin(ang), 0.0)))
            z = jnp.sin(f1_ref[...] * (jnp.dot(emb, w1_ref[...], precision=HIGHEST,
                                               preferred_element_type=F32) + b1_ref[...]))
            z_scr[side] = jnp.sin(f2_ref[...] * (jnp.dot(z, w2_ref[...], precision=HIGHEST,
                                                         preferred_element_type=F32) + b2_ref[...]))

    def half(side):
        t = positions[side].astype(F32) / length
        filt = jnp.dot(z_scr[side], w3_ref[side], precision=HIGHEST, preferred_element_type=F32)
        return filt * (jnp.exp(-t * jnp.abs(dec_ref[side:side + 1, :])) + HY_SHIFT)

    fwd = half(0)
    bwd = jnp.where(lax.broadcasted_iota(jnp.int32, (length, ct), 0) == 0, 0.0, half(1))
    ss = jnp.sum(fwd * fwd, axis=0, keepdims=True) + jnp.sum(bwd * bwd, axis=0, keepdims=True)
    scale = lax.rsqrt(ss + EPS)
    o_ref[:length, :] = fwd * scale
    o_ref[length:, :] = bwd * scale


def _hy_filter(length, w1, b1, f1, w2, b2, f2, w3, decay, *, ct):
    emb, ff = w1.shape
    width = decay.shape[1]
    ecols = 128
    w1p = jnp.zeros((ecols, ff), F32).at[:emb].set(w1)
    w3s = w3.reshape(ff, 2, width).transpose(1, 0, 2)
    kern = functools.partial(_hy_filter_kernel, length=length)
    const = lambda j: (0, 0)
    est = 10 * length * ct * 4 + 4 * length * ecols * 4
    return pl.pallas_call(
        kern,
        grid=(width // ct,),
        in_specs=[pl.BlockSpec((ecols, ff), const),
                  pl.BlockSpec((1, ff), const),
                  pl.BlockSpec((1, ff), const),
                  pl.BlockSpec((ff, ff), const),
                  pl.BlockSpec((1, ff), const),
                  pl.BlockSpec((1, ff), const),
                  pl.BlockSpec((2, ff, ct), lambda j: (0, 0, j)),
                  pl.BlockSpec((2, ct), lambda j: (0, j))],
        out_specs=pl.BlockSpec((2 * length, ct), lambda j: (0, j)),
        out_shape=jax.ShapeDtypeStruct((2 * length, width), F32),
        scratch_shapes=[pltpu.VMEM((2, length, ff), F32)],
        compiler_params=_cparams(est, ("arbitrary",)),
        name="hy_filter",
    )(w1p, b1.reshape(1, ff), f1.reshape(1, ff), w2, b2.reshape(1, ff), f2.reshape(1, ff), w3s, decay)


def _fftconv_kernel(ct_ref, st_ref, x_ref, k_ref, c_ref, s_ref, twr_ref, twi_ref, o_ref, *scratch, n1):
    kfr, kfi = scratch if scratch else (None, None)
    h1 = n1 // 2
    n2 = c_ref.shape[0]
    n = n1 * n2
    cm = c_ref[...]
    sm = s_ref[...]

    def dft(ar, ai, inverse):
        ab, bb = ar.astype(BF16), ai.astype(BF16)
        ca = jnp.dot(cm, ab, preferred_element_type=F32)
        sa = jnp.dot(sm, ab, preferred_element_type=F32)
        cb = jnp.dot(cm, bb, preferred_element_type=F32)
        sb = jnp.dot(sm, bb, preferred_element_type=F32)
        if inverse:
            return ca - sb, cb + sa
        return ca + sb, cb - sa

    def slab(t1):
        return slice(t1 * n2, (t1 + 1) * n2)

    twr1, twi1 = twr_ref[...], twi_ref[...]
    tw0 = (jnp.ones_like(twr1), jnp.zeros_like(twi1))

    def next_twiddle(tw):
        return tw[0] * twr1 - tw[1] * twi1, tw[0] * twi1 + tw[1] * twr1

    def filter_dft(k1, tw):
        ar = ai = None
        for t1 in range(n1):
            kr = k_ref[slab(t1), :]
            tr, ti = ct_ref[k1, t1] * kr, -(st_ref[k1, t1] * kr)
            ar = tr if ar is None else ar + tr
            ai = ti if ai is None else ai + ti
        ar, ai = ar * (1.0 / n), ai * (1.0 / n)
        return dft(ar * tw[0] + ai * tw[1], ai * tw[0] - ar * tw[1], False)

    if kfr is not None:
        @pl.when(pl.program_id(1) == 0)
        def _():
            def fbody(k1, tw):
                kfr[k1], kfi[k1] = filter_dft(k1, tw)
                return next_twiddle(tw)
            lax.fori_loop(0, n1, fbody, tw0)

    o_ref[...] = jnp.zeros_like(o_ref)

    def body(k1, tw):
        twr, twi = tw
        ar = ai = None
        for t1 in range(h1):
            cc, ss = ct_ref[k1, t1], st_ref[k1, t1]
            xr, xi = x_ref[0, slab(t1), :], x_ref[1, slab(t1), :]
            tr, ti = cc * xr + ss * xi, cc * xi - ss * xr
            ar = tr if ar is None else ar + tr
            ai = ti if ai is None else ai + ti
        fr, fi = dft(ar * twr + ai * twi, ai * twr - ar * twi, False)
        gr, gi = (kfr[k1], kfi[k1]) if kfr is not None else filter_dft(k1, tw)
        br, bi = dft(fr * gr - fi * gi, fr * gi + fi * gr, True)
        br, bi = br * twr - bi * twi, bi * twr + br * twi
        for t1 in range(h1):
            cc, ss = ct_ref[k1, t1], st_ref[k1, t1]
            o_ref[0, slab(t1), :] += cc * br - ss * bi
            o_ref[1, slab(t1), :] += cc * bi + ss * br
        return next_twiddle(tw)

    lax.fori_loop(0, n1, body, tw0)


def _fftconv(v, circ, *, n1, ct):
    bsz, length, width = v.shape
    n = 2 * length
    n2 = n // n1
    idx = np.arange(n2)
    ang = 2.0 * np.pi * ((idx[:, None] * idx[None, :]) % n2) / n2
    cmat = jnp.asarray(np.cos(ang), BF16)
    smat = jnp.asarray(np.sin(ang), BF16)
    tang = 2.0 * np.pi * idx / n
    twr = jnp.asarray(np.broadcast_to(np.cos(tang)[:, None], (n2, ct)), F32)
    twi = jnp.asarray(np.broadcast_to(np.sin(tang)[:, None], (n2, ct)), F32)
    i1 = np.arange(n1)
    ang1 = 2.0 * np.pi * ((i1[:, None] * i1[None, :]) % n1) / n1
    ctab = jnp.asarray(np.round(np.cos(ang1), 12), F32)
    stab = jnp.asarray(np.round(np.sin(ang1), 12), F32)
    kern = functools.partial(_fftconv_kernel, n1=n1)
    const = lambda j, p: (0, 0)
    once = pl.Buffered(1)
    npairs = bsz // 2
    scratch = [pltpu.VMEM((n1, n2, ct), F32)] * 2 if npairs > 1 else []
    est = (4 * 2 * length * ct * 4 + n * ct * 4 + 2 * n2 * n2 * 2 + 2 * n2 * ct * 4
           + len(scratch) * n * ct * 4 + (8 + 2 * n1) * n2 * ct * 4)
    return pl.pallas_call(
        kern,
        grid=(width // ct, npairs),
        in_specs=[pl.BlockSpec(memory_space=pltpu.SMEM),
                  pl.BlockSpec(memory_space=pltpu.SMEM),
                  pl.BlockSpec((2, length, ct), lambda j, p: (p, 0, j)),
                  pl.BlockSpec((n, ct), lambda j, p: (0, j), pipeline_mode=once),
                  pl.BlockSpec((n2, n2), const, pipeline_mode=once),
                  pl.BlockSpec((n2, n2), const, pipeline_mode=once),
                  pl.BlockSpec((n2, ct), const, pipeline_mode=once),
                  pl.BlockSpec((n2, ct), const, pipeline_mode=once)],
        out_specs=pl.BlockSpec((2, length, ct), lambda j, p: (p, 0, j)),
        out_shape=jax.ShapeDtypeStruct((bsz, length, width), F32),
        scratch_shapes=scratch,
        compiler_params=_cparams(est, ("arbitrary", "arbitrary")),
        name="fftconv",
    )(ctab, stab, v, circ, cmat, smat, twr, twi)


def _peer_route_kernel(x_ref, mod_ref, g_ref, wq_ref, keys_ref, hb_ref, th_ref, e0_ref, s1_ref, e1_ref,
                       q_scr, sc_scr, top_scr, tie_scr):
    tq = x_ref.shape[0]
    nk = PEER_NKEYS
    k = PEER_TOPK
    nhp = keys_ref.shape[0]
    dkh = keys_ref.shape[2]
    neg = -jnp.inf
    m = mod_ref[...]
    h = _norm_mod(x_ref[...], g_ref[...], m[4:5], m[3:4]).astype(BF16)
    hb_ref[...] = h
    q = jnp.dot(h, wq_ref[...], preferred_element_type=F32)
    for hp in range(nhp):
        q_scr[hp] = q[:, hp * dkh:(hp + 1) * dkh].astype(BF16)

    lw = V7X_LANES
    sub = V7X_SUBLANES
    assert k == 2 * sub
    ncand = k + 7 * sub + (k - sub)
    npad = sum(sub - k // (i + 1) for i in range(2, sub))
    iota_k = lax.broadcasted_iota(jnp.int32, (nk, lw), 0)
    iota_c = lax.broadcasted_iota(jnp.int32, (ncand, lw), 0)
    iota_8 = lax.broadcasted_iota(jnp.int32, (sub, lw), 0)
    iota_r = lax.broadcasted_iota(jnp.int32, (k, lw), 0)

    def topk(vals, iota, exact):
        top = jnp.zeros((k, lw), F32)
        for r in range(k):
            mx = jnp.max(vals, axis=0, keepdims=True)
            if exact:
                first = jnp.min(jnp.where(vals == mx, iota, vals.shape[0]), axis=0, keepdims=True)
                vals = jnp.where(iota == first, neg, vals)
            else:
                vals = jnp.where(vals == mx, neg, vals)
            top = jnp.where(iota_r == r, mx, top)
        return top, vals

    def tied(rest, expected):
        gone = jnp.sum(jnp.where(rest == neg, 1.0, 0.0), axis=0, keepdims=True)
        return jnp.where(gone != expected, 1.0, 0.0)

    def route(exact):
        def half_body(hp, carry):
            sc_scr[hp] = lax.dot_general(keys_ref[hp], q_scr[hp], (((1,), (1,)), ((), ())),
                                         preferred_element_type=F32)
            for c in range(tq // lw):
                lanes = slice(c * lw, (c + 1) * lw)
                top, rest = topk(sc_scr[hp, :, lanes], iota_k, exact)
                top_scr[hp, :, lanes] = top
                if not exact:
                    tie_scr[:, lanes] = jnp.maximum(tie_scr[:, lanes], tied(rest, float(k)))
            return carry

        lax.fori_loop(0, nhp, half_body, 0)

        def head_body(hd, carry):
            for c in range(tq // lw):
                lanes = slice(c * lw, (c + 1) * lw)
                a = top_scr[2 * hd, :, lanes]
                b = top_scr[2 * hd + 1, :, lanes]
                pieces = [a[0:1] + b, a[1:2] + b[0:sub]]
                for i in range(2, sub):
                    pieces.append(jnp.where(iota_8 < k // (i + 1), a[i:i + 1] + b[0:sub], neg))
                pieces.append(a[sub:k] + b[0:1])
                vmax = a[0:1] + b[0:1]
                top, rest = topk(jnp.concatenate(pieces, axis=0), iota_c, exact)
                z = jnp.sum(jnp.exp(top - vmax), axis=0, keepdims=True)
                last = top[k - 1:k]
                nxt = jnp.max(rest, axis=0, keepdims=True)
                tau = jnp.where(nxt == neg, last, 0.5 * (last + nxt))
                if not exact:
                    tie_scr[:, lanes] = jnp.maximum(tie_scr[:, lanes], tied(rest, float(k + npad)))
                s0 = sc_scr[2 * hd, :, lanes]
                s1 = sc_scr[2 * hd + 1, :, lanes]
                th_ref[hd, :, lanes] = tau - s0
                e0_ref[hd, :, lanes] = jnp.exp(s0 - a[0:1]) / z
                s1_ref[hd, :, lanes] = s1
                e1_ref[hd, :, lanes] = jnp.exp(s1 - b[0:1])
            return carry

        lax.fori_loop(0, nhp // 2, head_body, 0)

    tie_scr[...] = jnp.zeros_like(tie_scr)
    route(False)

    @pl.when(jnp.max(tie_scr[...]) > 0.0)
    def _():
        route(True)


def _peer_route(x, mod, g, wq, keys, *, n_ctx, dec_seq, tq):
    t, d = x.shape
    nhp, nk, dkh = keys.shape
    nh = nhp // 2
    const = lambda i: (0, 0)
    hshape = jax.ShapeDtypeStruct((nh, nk, t), F32)
    hspec = pl.BlockSpec((nh, nk, tq), lambda i: (0, 0, i))
    est = (2 * tq * d * 4 + 4 * d * nhp * dkh + 2 * tq * d * 2 + 8 * nh * nk * tq * 4
           + 2 * nhp * nk * tq * 4 + 6 * tq * nhp * dkh * 4)
    return pl.pallas_call(
        _peer_route_kernel,
        grid=(t // tq,),
        in_specs=[pl.BlockSpec((tq, d), lambda i: (i, 0)),
                  pl.BlockSpec((None, 6, d), lambda i: (_cond_index(i * tq, n_ctx, dec_seq), 0, 0)),
                  pl.BlockSpec((1, d), const),
                  pl.BlockSpec(wq.shape, const),
                  pl.BlockSpec(keys.shape, lambda i: (0, 0, 0))],
        out_specs=[pl.BlockSpec((tq, d), lambda i: (i, 0)), hspec, hspec, hspec, hspec],
        out_shape=[jax.ShapeDtypeStruct((t, d), BF16), hshape, hshape, hshape, hshape],
        scratch_shapes=[pltpu.VMEM((nhp, tq, dkh), BF16),
                        pltpu.VMEM((nhp, nk, tq), F32),
                        pltpu.VMEM((nhp, PEER_TOPK, tq), F32),
                        pltpu.VMEM((1, tq), F32)],
        compiler_params=_cparams(est, ("parallel",)),
        name="peer_route",
    )(x, mod, g, wq, keys)


def _peer_expert_kernel(hb_ref, u_ref, vt_ref, th_ref, e0_ref, s1_ref, e1_ref, x_ref, mod_ref, o_ref,
                        acc, st_a, st_b, hbuf_a, hbuf_b, *, nc):
    g = pl.program_id(0)
    nh = th_ref.shape[0]
    nk = s1_ref.shape[1]
    ngroups, per = th_ref.shape[1], th_ref.shape[2]
    gw = per * nk
    tq = st_a.shape[1]
    kh = nk // 2
    c3 = jnp.maximum(g - 2, 0) % nc

    @pl.when(g == 0)
    def _():
        st_b[...] = jnp.zeros_like(st_b)
        hbuf_a[...] = jnp.zeros_like(hbuf_a)
        hbuf_b[...] = jnp.zeros_like(hbuf_b)

    @pl.when(c3 == 0)
    def _():
        acc[...] = jnp.zeros_like(acc)

    def step(st_w, st_r, hbuf_w, hbuf_r):
        def group(j, carry):
            rows = pl.ds(pl.multiple_of(j * gw, gw), gw)
            acc[...] += jnp.dot(vt_ref[j], hbuf_r[rows, :], preferred_element_type=F32)
            st_w[rows, :] = lax.dot_general(u_ref[rows, :], hb_ref[...], (((1,), (1,)), ((), ())),
                                            preferred_element_type=F32)
            for q in range(tq // V7X_LANES):
                lanes = slice(q * V7X_LANES, (q + 1) * V7X_LANES)
                for half in range(2):
                    keys = slice(half * kh, (half + 1) * kh)
                    w = [None] * per
                    for hd in range(nh):
                        s1 = s1_ref[hd, keys, lanes]
                        e1 = e1_ref[hd, keys, lanes]
                        for i in range(per):
                            sel = jnp.where(s1 >= th_ref[hd, j, i:i + 1, lanes], e1, 0.0) * e0_ref[hd, j, i:i + 1, lanes]
                            w[i] = sel if w[i] is None else w[i] + sel
                    for i in range(per):
                        tile = pl.ds(pl.multiple_of(j * gw + i * nk + half * kh, kh), kh)
                        hbuf_w[tile, lanes] = (_gelu(st_r[tile, lanes]) * w[i]).astype(BF16)
            return carry

        lax.fori_loop(0, ngroups, group, 0)

    @pl.when(g % 2 == 0)
    def _():
        step(st_a, st_b, hbuf_b, hbuf_a)

    @pl.when(g % 2 == 1)
    def _():
        step(st_b, st_a, hbuf_a, hbuf_b)

    @pl.when(jnp.logical_and(g >= 2, c3 == nc - 1))
    def _():
        o_ref[...] = x_ref[...] + mod_ref[5:6, :] * acc[...].T


def _peer_expert(hb, u_tab, vt_tab, th, e0, s1, e1, x, mod, *, n_ctx, dec_seq, tq, ec):
    t, d = x.shape
    ne = u_tab.shape[0]
    nh, nk, _ = th.shape
    per = PEER_GROUP // nk
    ngroups = ec // PEER_GROUP
    nc = ne // ec
    npairs = (t // tq) * nc
    th = th.reshape(nh, nk // per, per, t)
    e0 = e0.reshape(nh, nk // per, per, t)
    vt_tab = vt_tab.reshape(d, ne // PEER_GROUP, PEER_GROUP).transpose(1, 0, 2)

    def pair(g, lag):
        p = jnp.clip(g - lag, 0, npairs - 1)
        return p // nc, p % nc

    def cond3(g):
        return _cond_index(pair(g, 2)[0] * tq, n_ctx, dec_seq)

    est = (2 * tq * d * 2 + 4 * ec * d * 2 + 4 * nh * ngroups * V7X_SUBLANES * tq * 4 + 4 * nh * nk * tq * 4
           + 4 * tq * d * 4 + d * tq * 4 + 2 * ec * tq * 2 + 2 * ec * tq * 4 + 3 * PEER_GROUP * tq * 4)
    return pl.pallas_call(
        functools.partial(_peer_expert_kernel, nc=nc),
        grid=(npairs + 2,),
        in_specs=[pl.BlockSpec((tq, d), lambda g: (pair(g, 0)[0], 0)),
                  pl.BlockSpec((ec, d), lambda g: (pair(g, 0)[1], 0)),
                  pl.BlockSpec((ngroups, d, PEER_GROUP), lambda g: (pair(g, 2)[1], 0, 0)),
                  pl.BlockSpec((nh, ngroups, per, tq), lambda g: (0, pair(g, 1)[1], 0, pair(g, 1)[0])),
                  pl.BlockSpec((nh, ngroups, per, tq), lambda g: (0, pair(g, 1)[1], 0, pair(g, 1)[0])),
                  pl.BlockSpec((nh, nk, tq), lambda g: (0, 0, pair(g, 1)[0])),
                  pl.BlockSpec((nh, nk, tq), lambda g: (0, 0, pair(g, 1)[0])),
                  pl.BlockSpec((tq, d), lambda g: (pair(g, 2)[0], 0)),
                  pl.BlockSpec((None, 6, d), lambda g: (cond3(g), 0, 0))],
        out_specs=pl.BlockSpec((tq, d), lambda g: (pair(g, 2)[0], 0)),
        out_shape=jax.ShapeDtypeStruct((t, d), F32),
        scratch_shapes=[pltpu.VMEM((d, tq), F32), pltpu.VMEM((ec, tq), F32), pltpu.VMEM((ec, tq), F32),
                        pltpu.VMEM((ec, tq), BF16), pltpu.VMEM((ec, tq), BF16)],
        compiler_params=_cparams(est, ("arbitrary",)),
        name="peer_expert",
    )(hb, u_tab, vt_tab, th, e0, s1, e1, x, mod)


def _final_kernel(x_ref, g_ref, o_ref):
    x = x_ref[...]
    ms = jnp.mean(x * x, axis=-1, keepdims=True)
    o_ref[...] = (x * lax.rsqrt(ms + EPS)) * g_ref[...]


def _final_norm(x, g, *, tm):
    t, d = x.shape
    return pl.pallas_call(
        _final_kernel,
        grid=(t // tm,),
        in_specs=[pl.BlockSpec((tm, d), lambda i: (i, 0)), pl.BlockSpec((1, d), lambda i: (0, 0))],
        out_specs=pl.BlockSpec((tm, d), lambda i: (i, 0)),
        out_shape=jax.ShapeDtypeStruct((t, d), F32),
        compiler_params=_cparams(6 * tm * d * 4, ("parallel",)),
        name="final_norm",
    )(x, g)


def _block_diag(w):
    h, i, j = w.shape
    return jnp.einsum('hij,hg->higj', w, jnp.eye(h, dtype=w.dtype)).reshape(h * i, h * j)


def _s5_discretize(a_re, a_im, log_dt, b_re, b_im):
    dt = jnp.exp(log_dt)[:, None]
    mag = jnp.exp(a_re * dt)
    abar_r = mag * jnp.cos(a_im * dt)
    abar_i = mag * jnp.sin(a_im * dt)
    den = a_re * a_re + a_im * a_im
    num_r = abar_r - 1.0
    coef_r = ((num_r * a_re + abar_i * a_im) / den)[..., None]
    coef_i = ((abar_i * a_re - num_r * a_im) / den)[..., None]
    return abar_r, abar_i, coef_r * b_re - coef_i * b_im, coef_r * b_im + coef_i * b_re


def _s5_blocks(abar_r, abar_i, bbar_r, bbar_i, c_re, c_im):
    g, p, ch = bbar_r.shape
    gb = g // S5_COLBLK

    def vec(v):
        return v.reshape(S5_COLBLK, 1, gb * p)

    def bmat(b):
        bt = b.transpose(0, 2, 1).reshape(S5_COLBLK, gb, ch, p)
        return jnp.stack([_block_diag(bt[j]) for j in range(S5_COLBLK)])

    def cmat(cc):
        ct = cc.transpose(0, 2, 1).reshape(S5_COLBLK, gb, p, ch)
        return jnp.stack([_block_diag(ct[j]) for j in range(S5_COLBLK)])

    bblk = jnp.concatenate([bmat(bbar_r), bmat(bbar_i)], axis=2).astype(BF16)
    cblk = jnp.concatenate([cmat(c_re), cmat(-c_im)], axis=1).astype(BF16)
    return bblk, cblk, vec(abar_r), vec(abar_i)


def _state_to_blocked(re, im):
    n, g, p = re.shape
    gb = g // S5_COLBLK
    both = jnp.stack([re.reshape(n, S5_COLBLK, gb * p), im.reshape(n, S5_COLBLK, gb * p)], axis=2)
    return both.reshape(n, -1)


def _state_from_blocked(st, g, p):
    n = st.shape[0]
    both = st.reshape(n, S5_COLBLK, 2, (g // S5_COLBLK) * p)
    return both[:, :, 0].reshape(n, g, p), both[:, :, 1].reshape(n, g, p)


def _complex_power(ar, ai, n):
    for _ in range(int(math.log2(n))):
        ar, ai = ar * ar - ai * ai, 2.0 * ar * ai
    return ar, ai


def _mixer_ab(x, mod, p, le, h0, *, bc, lc, bs, ls, tm):
    n_ctx = bc * lc
    width = p['s5_d'].shape[1]
    g_cnt, p_cnt = p['s5_a_re'].shape[2:]
    w_gates = jnp.concatenate([_block_diag(p['lru_w_a'][le, 0]), _block_diag(p['lru_w_x'][le, 0]),
                               _block_diag(p['lru_w_a'][le, 1]), _block_diag(p['lru_w_x'][le, 1])],
                              axis=1).astype(BF16)
    b_gates = jnp.concatenate([p['lru_b_a'][le, 0], p['lru_b_x'][le, 0],
                               p['lru_b_a'][le, 1], p['lru_b_x'][le, 1]]).reshape(1, 4 * width)
    c8 = LRU_C * jax.nn.softplus(-p['lru_lambda'][le])
    s_arr, xg = _ab_in(x, mod, p['norm1_g'][2 * le].reshape(1, -1), p['w_in_ab'][le].astype(BF16),
                       p['lru_conv_w'][le], p['lru_conv_b'][le].reshape(1, width), w_gates, b_gates, c8,
                       n_ctx=n_ctx, ctx_row=lc, dec_seq=ls, tm=tm)

    cols = s_arr.shape[1]
    gc = bc // SCAN_SEQS
    lseg = ls // SAMPLE_SEGS
    s_ctx = (s_arr[:n_ctx].reshape(gc, SCAN_SEQS, lc, cols).transpose(0, 2, 1, 3)
             .reshape(n_ctx, cols))
    s_smp = (s_arr[n_ctx:].reshape(bs * SAMPLE_SEGS, lseg, cols).transpose(1, 0, 2)
             .reshape(bs * ls, cols))
    chunk_rows = min(1024, lc * SCAN_SEQS, lseg * SCAN_SEQS)

    outs = {}
    finals = {}
    for d in range(2):
        abar_r, abar_i, bbar_r, bbar_i = _s5_discretize(
            p['s5_a_re'][le, d], p['s5_a_im'][le, d], p['s5_log_dt'][le, d],
            p['s5_b_re'][le, d], p['s5_b_im'][le, d])
        bblk, cblk, ar, ai = _s5_blocks(abar_r, abar_i, bbar_r, bbar_i,
                                        p['s5_c_re'][le, d], p['s5_c_im'][le, d])
        nstate = 2 * g_cnt * p_cnt
        z5 = jnp.zeros((gc, SCAN_SEQS, nstate), F32)
        zl = jnp.zeros((gc, SCAN_SEQS, width), F32)
        y, hl, f5, fl, _ = _scan(s_ctx, bblk, cblk, ar, ai, z5, zl, direction=d, store=True,
                                 groups=gc, chunk_rows=chunk_rows)
        outs[('ctx', d)] = (y, hl)
        finals[d] = (f5.reshape(bc, nstate), fl.reshape(bc, width))
        z5 = jnp.zeros((1, SCAN_SEQS, nstate), F32)
        zl = jnp.zeros((1, SCAN_SEQS, width), F32)
        e5, el, ep = _scan(s_smp, bblk, cblk, ar, ai, z5, zl, direction=d, store=False,
                           groups=1, chunk_rows=chunk_rows)
        first = (SAMPLE_SEGS - 1) if d == 1 else 0
        h5 = _state_to_blocked(h0[0][:, d], h0[1][:, d])
        h5_rows = jnp.zeros((bs, SAMPLE_SEGS, nstate), F32).at[:, first].set(h5).reshape(SCAN_SEQS, nstate)
        hl_rows = jnp.zeros((bs, SAMPLE_SEGS, width), F32).at[:, first].set(h0[2][:, d]).reshape(SCAN_SEQS, width)
        arl, ail = _complex_power(ar, ai, lseg)
        i5, il = _chain(e5[0], el[0], ep[0], h5_rows, hl_rows, arl, ail, direction=d)
        y, hl, _, _, _ = _scan(s_smp, bblk, cblk, ar, ai, i5[None], il[None], direction=d, store=True,
                               groups=1, chunk_rows=chunk_rows)
        outs[('smp', d)] = (y, hl)

    def to_tokens(ctx_arr, smp_arr):
        c = ctx_arr.reshape(gc, lc, SCAN_SEQS, width).transpose(0, 2, 1, 3).reshape(n_ctx, width)
        s = smp_arr.reshape(lseg, bs * SAMPLE_SEGS, width).transpose(1, 0, 2).reshape(bs * ls, width)
        return jnp.concatenate([c, s], axis=0)

    yf = to_tokens(outs[('ctx', 0)][0], outs[('smp', 0)][0])
    yb = to_tokens(outs[('ctx', 1)][0], outs[('smp', 1)][0])
    hf = to_tokens(outs[('ctx', 0)][1], outs[('smp', 0)][1])
    hb = to_tokens(outs[('ctx', 1)][1], outs[('smp', 1)][1])
    x1 = _ab_out(x, mod, s_arr, xg, yf, yb, hf, hb, p['s5_d'][le].reshape(1, width),
                 p['s5_w_glu'][le].astype(BF16), p['s5_b_glu'][le].reshape(1, width),
                 p['w_out_ab'][le].astype(BF16), n_ctx=n_ctx, dec_seq=ls, tm=tm)
    st = [_state_from_blocked(finals[d][0], g_cnt, p_cnt) for d in range(2)]
    new_re = jnp.stack([st[0][0], st[1][0]], axis=1)
    new_im = jnp.stack([st[0][1], st[1][1]], axis=1)
    new_lru = jnp.stack([finals[0][1], finals[1][1]], axis=1)
    return x1, (new_re, new_im, new_lru)


def _fft_split(length):
    n = 2 * length
    n1 = 2
    while n // n1 > 1024:
        n1 *= 2
    return n1


def _mixer_c(x, mod, p, lo, *, bc, lc, bs, ls, tm):
    n_ctx = bc * lc
    width = p['hy_bias'].shape[1]
    x0, vg = _c_in(x, mod, p['norm1_g'][2 * lo + 1].reshape(1, -1), p['w_in_c'][lo].astype(BF16),
                   p['b_in_c'][lo].reshape(1, -1), p['hy_conv_w'][lo], p['hy_conv_b'][lo].reshape(1, -1),
                   n_ctx=n_ctx, ctx_row=lc, dec_seq=ls, tm=tm)
    ys = []
    for (b, l, v) in ((bc, lc, vg[:n_ctx]), (bs, ls, vg[n_ctx:])):
        circ = _hy_filter(l, p['hy_w1'][lo], p['hy_b1'][lo], p['hy_freq1'][lo], p['hy_w2'][lo],
                          p['hy_b2'][lo], p['hy_freq2'][lo], p['hy_w3'][lo], p['hy_decay'][lo],
                          ct=min(width, 256 if l <= 1024 else 128))
        y = _fftconv(v.reshape(b, l, width), circ, n1=_fft_split(l), ct=128)
        ys.append(y.reshape(b * l, width))
    yconv = jnp.concatenate(ys, axis=0)
    return _c_out(x, mod, x0, vg, yconv, p['hy_bias'][lo].reshape(1, width), p['w_out_c'][lo].astype(BF16),
                  p['b_out_c'][lo].reshape(1, -1), n_ctx=n_ctx, dec_seq=ls, tm=tm)


def _peer(x, mod, p, l, *, n_ctx, dec_seq, tq, ec):
    keys = p['peer_keys'][l]
    nh, _, nk, dkh = keys.shape
    hb, th, e0, s1, e1 = _peer_route(x, mod, p['norm2_g'][l].reshape(1, -1), p['peer_wq'][l].astype(BF16),
                                     keys.reshape(nh * 2, nk, dkh).astype(BF16),
                                     n_ctx=n_ctx, dec_seq=dec_seq, tq=tq)
    return _peer_expert(hb, p['peer_u'][l].astype(BF16), p['peer_v'][l].T.astype(BF16), th, e0, s1, e1,
                        x, mod, n_ctx=n_ctx, dec_seq=dec_seq, tq=tq, ec=ec)


def _forward(x_prompt, x_sample, state_s5_re, state_s5_im, state_lru, c, c_ctx, p, *, tm, tq, ec):
    bc, lc, d = x_prompt.shape
    bs, ls, _ = x_sample.shape
    assert bc % SCAN_SEQS == 0 and bs * SAMPLE_SEGS == SCAN_SEQS
    assert tm % lc == 0 and ls % tm == 0 and (ls // SAMPLE_SEGS) % GRID_W == 0
    n_ctx = bc * lc
    x = jnp.concatenate([x_prompt.reshape(n_ctx, d), x_sample.reshape(bs * ls, d)], axis=0)
    cond = jnp.zeros((V7X_SUBLANES, d), F32).at[0].set(c_ctx).at[1:1 + bs].set(c)
    mod_all = _modulation(cond, p['w_mod'], p['b_mod'])
    states = None
    for l in range(DEPTH):
        mod = mod_all[l].reshape(V7X_SUBLANES, 6, d)
        if l % 2 == 0:
            le = l // 2
            h0 = (state_s5_re[:, le], state_s5_im[:, le], state_lru[:, le])
            x, st = _mixer_ab(x, mod, p, le, h0, bc=bc, lc=lc, bs=bs, ls=ls, tm=tm)
            states = st if states is None else states
        else:
            x = _mixer_c(x, mod, p, l // 2, bc=bc, lc=lc, bs=bs, ls=ls, tm=tm)
        x = _peer(x, mod, p, l, n_ctx=n_ctx, dec_seq=ls, tq=tq, ec=ec)
    y = _final_norm(x, p['final_g'].reshape(1, d), tm=tm)
    new_re, new_im, new_lru = states
    return (y[:n_ctx].reshape(bc, lc, d), y[n_ctx:].reshape(bs, ls, d),
            new_re[:, None], new_im[:, None], new_lru[:, None])


def kernel(x_prompt, x_sample, state_s5_re, state_s5_im, state_lru, c, c_ctx, norm1_g, norm2_g, w_mod, b_mod, w_in_ab, s5_a_re, s5_a_im, s5_log_dt, s5_b_re, s5_b_im, s5_c_re, s5_c_im, s5_d, s5_w_glu, s5_b_glu, lru_conv_w, lru_conv_b, lru_w_a, lru_b_a, lru_w_x, lru_b_x, lru_lambda, w_out_ab, w_in_c, b_in_c, hy_conv_w, hy_conv_b, hy_w1, hy_b1, hy_freq1, hy_w2, hy_b2, hy_freq2, hy_w3, hy_decay, hy_bias, w_out_c, b_out_c, peer_wq, peer_keys, peer_u, peer_v, final_g):
    p = dict(norm1_g=norm1_g, norm2_g=norm2_g, w_mod=w_mod, b_mod=b_mod, w_in_ab=w_in_ab,
             s5_a_re=s5_a_re, s5_a_im=s5_a_im, s5_log_dt=s5_log_dt, s5_b_re=s5_b_re, s5_b_im=s5_b_im,
             s5_c_re=s5_c_re, s5_c_im=s5_c_im, s5_d=s5_d, s5_w_glu=s5_w_glu, s5_b_glu=s5_b_glu,
             lru_conv_w=lru_conv_w, lru_conv_b=lru_conv_b, lru_w_a=lru_w_a, lru_b_a=lru_b_a,
             lru_w_x=lru_w_x, lru_b_x=lru_b_x, lru_lambda=lru_lambda, w_out_ab=w_out_ab,
             w_in_c=w_in_c, b_in_c=b_in_c, hy_conv_w=hy_conv_w, hy_conv_b=hy_conv_b,
             hy_w1=hy_w1, hy_b1=hy_b1, hy_freq1=hy_freq1, hy_w2=hy_w2, hy_b2=hy_b2, hy_freq2=hy_freq2,
             hy_w3=hy_w3, hy_decay=hy_decay, hy_bias=hy_bias, w_out_c=w_out_c, b_out_c=b_out_c,
             peer_wq=peer_wq, peer_keys=peer_keys, peer_u=peer_u, peer_v=peer_v, final_g=final_g)
    return _forward(x_prompt, x_sample, state_s5_re, state_s5_im, state_lru, c, c_ctx, p,
                    tm=512, tq=512, ec=1024)
```

```python
import functools
import math

import jax
import jax.numpy as jnp
import numpy as np
from jax import lax
from jax.experimental import pallas as pl
from jax.experimental.pallas import tpu as pltpu

F32 = jnp.float32
BF16 = jnp.bfloat16
HIGHEST = lax.Precision.HIGHEST

DEPTH = 2
GRID_W = 64
EPS = 1e-6
S5_CH = 16
S5_STATE = 64
LRU_HEADS = 8
LRU_C = 8.0
HY_BANDS = 16
HY_SHIFT = 0.05
PEER_HEADS = 8
PEER_NKEYS = 128
PEER_TOPK = 16
PEER_GROUP = 512

V7X_LANES = 128
V7X_SUBLANES = 8
V7X_VMEM_BYTES = 64 * 1024 * 1024

SCAN_SEQS = V7X_SUBLANES
SAMPLE_SEGS = 4
S5_COLBLK = 4


def _vmem_limit(est_bytes):
    return int(min(V7X_VMEM_BYTES - (8 << 20), max(32 << 20, est_bytes * 3 // 2)))


def _cparams(est_bytes, sem):
    return pltpu.CompilerParams(dimension_semantics=sem, vmem_limit_bytes=_vmem_limit(est_bytes))


def _gelu(x):
    return 0.5 * x * (1.0 + jnp.tanh(0.7978845608028654 * (x + 0.044715 * (x * x * x))))


def _norm_mod(x, g, sc, sh):
    ms = jnp.mean(x * x, axis=-1, keepdims=True)
    return (x * lax.rsqrt(ms + EPS)) * g * (1.0 + sc) + sh


def _cond_index(row0, n_ctx, dec_seq):
    return jnp.where(row0 < n_ctx, 0, 1 + (row0 - n_ctx) // dec_seq)


def _mod_kernel(c_ref, w_ref, b_ref, o_ref):
    c = c_ref[...]
    s = c * jax.nn.sigmoid(c)
    o_ref[...] = jnp.dot(s, w_ref[...], precision=HIGHEST, preferred_element_type=F32) + b_ref[...]


def _modulation(cond, w_mod, b_mod):
    depth, d, n = w_mod.shape
    rows = cond.shape[0]
    tn = 1536
    est = 2 * d * tn * 4 + 4 * rows * (d + tn) * 4
    return pl.pallas_call(
        _mod_kernel,
        grid=(depth, n // tn),
        in_specs=[pl.BlockSpec((rows, d), lambda l, j: (0, 0)),
                  pl.BlockSpec((None, d, tn), lambda l, j: (l, 0, j)),
                  pl.BlockSpec((None, 1, tn), lambda l, j: (l, 0, j))],
        out_specs=pl.BlockSpec((None, rows, tn), lambda l, j: (l, 0, j)),
        out_shape=jax.ShapeDtypeStruct((depth, rows, n), F32),
        compiler_params=_cparams(est, ("arbitrary", "arbitrary")),
        name="modulation",
    )(cond, w_mod, b_mod.reshape(depth, 1, n))


def _dwconv_rows(x, w, bias, pos, row_len, pad_left):
    tm = x.shape[0]
    acc = None
    for k in range(w.shape[0]):
        d = k - pad_left
        if d == 0:
            term = x
        else:
            shifted = pltpu.roll(x, (-d) % tm, axis=0)
            q = pos + d
            term = jnp.where(q >= 0, jnp.where(q < row_len, shifted, 0.0), 0.0)
        term = term * w[k:k + 1]
        acc = term if acc is None else acc + term
    return acc + bias


def _ab_in_kernel(x_ref, mod_ref, g_ref, w_ref, cw_ref, cb_ref, wg_ref, bg_ref, c8_ref,
                  s_ref, xg_ref, *, n_ctx, ctx_row, width):
    tm = x_ref.shape[0]
    m = mod_ref[...]
    h = _norm_mod(x_ref[...], g_ref[...], m[1:2], m[0:1])
    z = jnp.dot(h.astype(BF16), w_ref[...], preferred_element_type=F32)
    u = z[:, :width]
    xr = z[:, width:2 * width]
    xg_ref[...] = z[:, 2 * width:]
    row_len = jnp.where(pl.program_id(0) * tm < n_ctx, ctx_row, GRID_W)
    pos = lax.broadcasted_iota(jnp.int32, (tm, width), 0) & (row_len - 1)
    xb = _dwconv_rows(xr, cw_ref[...], cb_ref[...], pos, row_len, 2)
    gates = jnp.dot(xb.astype(BF16), wg_ref[...], preferred_element_type=F32) + bg_ref[...]
    s_ref[:, :width] = u
    c8 = c8_ref[...]
    for d in range(2):
        r = jax.nn.sigmoid(gates[:, (2 * d) * width:(2 * d + 1) * width])
        i = jax.nn.sigmoid(gates[:, (2 * d + 1) * width:(2 * d + 2) * width])
        a = jnp.exp(-(c8[d:d + 1] * r))
        b = jnp.sqrt(1.0 - a * a) * (i * xb)
        s_ref[:, (1 + 2 * d) * width:(2 + 2 * d) * width] = a
        s_ref[:, (2 + 2 * d) * width:(3 + 2 * d) * width] = b


def _ab_in(x, mod, g, w_in, conv_w, conv_b, w_gates, b_gates, c8, *, n_ctx, ctx_row, dec_seq, tm):
    t, d = x.shape
    width = w_in.shape[1] // 3
    kern = functools.partial(_ab_in_kernel, n_ctx=n_ctx, ctx_row=ctx_row, width=width)
    const = lambda i: (0, 0)
    est = (2 * tm * d * 4 + d * 3 * width * 2 * 2 + width * 4 * width * 2 * 2
           + 2 * tm * 6 * width * 4 + tm * 12 * width * 4)
    return pl.pallas_call(
        kern,
        grid=(t // tm,),
        in_specs=[pl.BlockSpec((tm, d), lambda i: (i, 0)),
                  pl.BlockSpec((None, 6, d), lambda i: (_cond_index(i * tm, n_ctx, dec_seq), 0, 0)),
                  pl.BlockSpec((1, d), const),
                  pl.BlockSpec(w_in.shape, const),
                  pl.BlockSpec(conv_w.shape, const),
                  pl.BlockSpec((1, width), const),
                  pl.BlockSpec(w_gates.shape, const),
                  pl.BlockSpec((1, 4 * width), const),
                  pl.BlockSpec((2, width), const)],
        out_specs=[pl.BlockSpec((tm, 5 * width), lambda i: (i, 0)),
                   pl.BlockSpec((tm, width), lambda i: (i, 0))],
        out_shape=[jax.ShapeDtypeStruct((t, 5 * width), F32),
                   jax.ShapeDtypeStruct((t, width), F32)],
        compiler_params=_cparams(est, ("parallel",)),
        name="ab_in",
    )(x, mod, g, w_in, conv_w, conv_b, w_gates, b_gates, c8)


def _scan_kernel(u_ref, a_ref, b_ref, bblk_ref, cblk_ref, ar_ref, ai_ref, i5_ref, il_ref, *rest,
                 reverse, store):
    if store:
        y_ref, hl_ref, f5_ref, fl_ref, fp_ref, buf, st5, stl, stp = rest
    else:
        f5_ref, fl_ref, fp_ref, buf, st5, stl, stp = rest
    c = pl.program_id(1)
    rows = u_ref.shape[0]
    nsteps = rows // SCAN_SEQS
    sw = st5.shape[1] // (2 * S5_COLBLK)
    cw = u_ref.shape[1] // S5_COLBLK

    @pl.when(c == 0)
    def _():
        st5[...] = i5_ref[...]
        stl[...] = il_ref[...]
        stp[...] = jnp.ones_like(stp)

    def row0(k):
        t = (nsteps - 1 - k) if reverse else k
        return pl.multiple_of(t * SCAN_SEQS, SCAN_SEQS)

    u = u_ref[...].astype(BF16)
    for j in range(S5_COLBLK):
        buf[...] = jnp.dot(u[:, j * cw:(j + 1) * cw], bblk_ref[j], preferred_element_type=F32)
        arj = jnp.broadcast_to(ar_ref[j], (SCAN_SEQS, sw))
        aij = jnp.broadcast_to(ai_ref[j], (SCAN_SEQS, sw))

        def step(k, carry, arj=arj, aij=aij):
            hr, hi = carry
            r0 = row0(k)
            xr = buf[pl.ds(r0, SCAN_SEQS), 0:sw]
            xi = buf[pl.ds(r0, SCAN_SEQS), sw:2 * sw]
            nr = arj * hr - aij * hi + xr
            ni = arj * hi + aij * hr + xi
            buf[pl.ds(r0, SCAN_SEQS), 0:sw] = nr
            buf[pl.ds(r0, SCAN_SEQS), sw:2 * sw] = ni
            return nr, ni

        base = 2 * sw * j
        hr, hi = lax.fori_loop(0, nsteps, step, (st5[:, base:base + sw], st5[:, base + sw:base + 2 * sw]))
        st5[:, base:base + sw] = hr
        st5[:, base + sw:base + 2 * sw] = hi
        if store:
            y_ref[:, j * cw:(j + 1) * cw] = jnp.dot(buf[...].astype(BF16), cblk_ref[j],
                                                    preferred_element_type=F32)

    def lstep(k, carry):
        h, p = carry
        r0 = row0(k)
        a = a_ref[pl.ds(r0, SCAN_SEQS), :]
        h = a * h + b_ref[pl.ds(r0, SCAN_SEQS), :]
        if store:
            hl_ref[pl.ds(r0, SCAN_SEQS), :] = h
        return h, p * a

    h, p = lax.fori_loop(0, nsteps, lstep, (stl[...], stp[...]))
    stl[...] = h
    stp[...] = p

    @pl.when(c == pl.num_programs(1) - 1)
    def _():
        f5_ref[...] = st5[...]
        fl_ref[...] = stl[...]
        fp_ref[...] = stp[...]


def _scan(s_arr, bblk, cblk, ar, ai, init5, initl, *, direction, store, groups, chunk_rows):
    rows_total, cols = s_arr.shape
    width = cols // 5
    rows_per_group = rows_total // groups
    nc = rows_per_group // chunk_rows
    reverse = direction == 1
    nstate = init5.shape[-1]

    def chunk(g, c):
        return g * nc + ((nc - 1 - c) if reverse else c)

    kern = functools.partial(_scan_kernel, reverse=reverse, store=store)
    const3 = lambda g, c: (0, 0, 0)
    in_specs = [pl.BlockSpec((chunk_rows, width), lambda g, c: (chunk(g, c), 0)),
                pl.BlockSpec((chunk_rows, width), lambda g, c: (chunk(g, c), 1 + 2 * direction)),
                pl.BlockSpec((chunk_rows, width), lambda g, c: (chunk(g, c), 2 + 2 * direction)),
                pl.BlockSpec(bblk.shape, const3),
                pl.BlockSpec(cblk.shape, const3),
                pl.BlockSpec(ar.shape, const3),
                pl.BlockSpec(ai.shape, const3),
                pl.BlockSpec((None, SCAN_SEQS, nstate), lambda g, c: (g, 0, 0)),
                pl.BlockSpec((None, SCAN_SEQS, width), lambda g, c: (g, 0, 0))]
    fin_specs = [pl.BlockSpec((None, SCAN_SEQS, nstate), lambda g, c: (g, 0, 0)),
                 pl.BlockSpec((None, SCAN_SEQS, width), lambda g, c: (g, 0, 0)),
                 pl.BlockSpec((None, SCAN_SEQS, width), lambda g, c: (g, 0, 0))]
    fin_shapes = [jax.ShapeDtypeStruct((groups, SCAN_SEQS, nstate), F32),
                  jax.ShapeDtypeStruct((groups, SCAN_SEQS, width), F32),
                  jax.ShapeDtypeStruct((groups, SCAN_SEQS, width), F32)]
    out_specs, out_shapes = fin_specs, fin_shapes
    if store:
        out_specs = [pl.BlockSpec((chunk_rows, width), lambda g, c: (chunk(g, c), 0))] * 2 + fin_specs
        out_shapes = [jax.ShapeDtypeStruct((rows_total, width), F32)] * 2 + fin_shapes
    bufw = 2 * nstate // (2 * S5_COLBLK)
    est = (10 * chunk_rows * width * 4 + chunk_rows * bufw * 4 * 2
           + 4 * bblk.size * 2 + 8 * SCAN_SEQS * nstate * 4)
    return pl.pallas_call(
        kern,
        grid=(groups, nc),
        in_specs=in_specs,
        out_specs=out_specs,
        out_shape=out_shapes,
        scratch_shapes=[pltpu.VMEM((chunk_rows, bufw), F32),
                        pltpu.VMEM((SCAN_SEQS, nstate), F32),
                        pltpu.VMEM((SCAN_SEQS, width), F32),
                        pltpu.VMEM((SCAN_SEQS, width), F32)],
        compiler_params=_cparams(est, ("arbitrary", "arbitrary")),
        name="scan_store" if store else "scan_ends",
    )(s_arr, s_arr, s_arr, bblk, cblk, ar, ai, init5, initl)


def _chain_kernel(e5_ref, el_ref, ep_ref, h5_ref, hl_ref, arl_ref, ail_ref, i5_ref, il_ref, *, reverse):
    nseg = SAMPLE_SEGS
    sw = arl_ref.shape[-1]
    shift = (SCAN_SEQS - 1) if reverse else 1
    order = range(nseg - 2, -1, -1) if reverse else range(1, nseg)
    seg5 = lax.broadcasted_iota(jnp.int32, (SCAN_SEQS, sw), 0) % nseg
    for j in range(S5_COLBLK):
        base = 2 * sw * j
        re, im = slice(base, base + sw), slice(base + sw, base + 2 * sw)
        ar, ai = arl_ref[j], ail_ref[j]
        er = pltpu.roll(e5_ref[:, re], shift, axis=0)
        ei = pltpu.roll(e5_ref[:, im], shift, axis=0)
        hr, hi = h5_ref[:, re], h5_ref[:, im]
        for k in order:
            pr = pltpu.roll(hr, shift, axis=0)
            pi = pltpu.roll(hi, shift, axis=0)
            hr = jnp.where(seg5 == k, ar * pr - ai * pi + er, hr)
            hi = jnp.where(seg5 == k, ar * pi + ai * pr + ei, hi)
        i5_ref[:, re] = hr
        i5_ref[:, im] = hi
    segl = lax.broadcasted_iota(jnp.int32, el_ref.shape, 0) % nseg
    el = pltpu.roll(el_ref[...], shift, axis=0)
    ep = pltpu.roll(ep_ref[...], shift, axis=0)
    h = hl_ref[...]
    for k in order:
        h = jnp.where(segl == k, ep * pltpu.roll(h, shift, axis=0) + el, h)
    il_ref[...] = h


def _chain(end5, endl, endp, h5_rows, hl_rows, arl, ail, *, direction):
    kern = functools.partial(_chain_kernel, reverse=direction == 1)
    return pl.pallas_call(
        kern,
        out_shape=[jax.ShapeDtypeStruct(end5.shape, F32), jax.ShapeDtypeStruct(endl.shape, F32)],
        name="scan_chain",
    )(end5, endl, endp, h5_rows, hl_rows, arl, ail)


def _ab_out_kernel(x_ref, mod_ref, s_ref, xg_ref, yf_ref, yb_ref, hf_ref, hb_ref, d_ref,
                   wglu_ref, bglu_ref, wout_ref, o_ref):
    width = xg_ref.shape[1]
    m = mod_ref[...]
    y = s_ref[...] * d_ref[...] + yf_ref[...] + yb_ref[...]
    zs = _gelu(y)
    gate = jnp.dot(zs.astype(BF16), wglu_ref[...], preferred_element_type=F32) + bglu_ref[...]
    s5_out = zs * jax.nn.sigmoid(gate)
    lru_out = (hf_ref[...] + hb_ref[...]) * _gelu(xg_ref[...])
    out = (jnp.dot(s5_out.astype(BF16), wout_ref[:width, :], preferred_element_type=F32)
           + jnp.dot(lru_out.astype(BF16), wout_ref[width:, :], preferred_element_type=F32))
    o_ref[...] = x_ref[...] + m[2:3] * out


def _ab_out(x, mod, s_arr, xg, yf, yb, hf, hb, s5_d, w_glu, b_glu, w_out, *, n_ctx, dec_seq, tm):
    t, d = x.shape
    width = xg.shape[1]
    const = lambda i: (0, 0)
    row = lambda i: (i, 0)
    est = 4 * tm * d * 4 + 12 * tm * width * 4 + 4 * (width * width + 2 * width * d) + 8 * tm * width * 4
    return pl.pallas_call(
        _ab_out_kernel,
        grid=(t // tm,),
        in_specs=[pl.BlockSpec((tm, d), row),
                  pl.BlockSpec((None, 6, d), lambda i: (_cond_index(i * tm, n_ctx, dec_seq), 0, 0)),
                  pl.BlockSpec((tm, width), row),
                  pl.BlockSpec((tm, width), row),
                  pl.BlockSpec((tm, width), row),
                  pl.BlockSpec((tm, width), row),
                  pl.BlockSpec((tm, width), row),
                  pl.BlockSpec((tm, width), row),
                  pl.BlockSpec((1, width), const),
                  pl.BlockSpec(w_glu.shape, const),
                  pl.BlockSpec((1, width), const),
                  pl.BlockSpec(w_out.shape, const)],
        out_specs=pl.BlockSpec((tm, d), row),
        out_shape=jax.ShapeDtypeStruct((t, d), F32),
        compiler_params=_cparams(est, ("parallel",)),
        name="ab_out",
    )(x, mod, s_arr, xg, yf, yb, hf, hb, s5_d, w_glu, b_glu, w_out)


def _c_in_kernel(x_ref, mod_ref, g_ref, w_ref, b_ref, cw_ref, cb_ref, x0_ref, vg_ref, *, n_ctx, ctx_row):
    tm = x_ref.shape[0]
    width = x0_ref.shape[1]
    m = mod_ref[...]
    h = _norm_mod(x_ref[...], g_ref[...], m[1:2], m[0:1])
    z = jnp.dot(h.astype(BF16), w_ref[...], preferred_element_type=F32) + b_ref[...]
    row_len = jnp.where(pl.program_id(0) * tm < n_ctx, ctx_row, GRID_W)
    pos = lax.broadcasted_iota(jnp.int32, (tm, width), 0) & (row_len - 1)
    cw = cw_ref[...]
    cb = cb_ref[...]
    parts = [_dwconv_rows(z[:, k * width:(k + 1) * width], cw[:, k * width:(k + 1) * width],
                          cb[:, k * width:(k + 1) * width], pos, row_len, 1) for k in range(3)]
    x0_ref[...] = parts[0]
    vg_ref[...] = parts[2] * parts[1]


def _c_in(x, mod, g, w_in, b_in, conv_w, conv_b, *, n_ctx, ctx_row, dec_seq, tm):
    t, d = x.shape
    width = w_in.shape[1] // 3
    kern = functools.partial(_c_in_kernel, n_ctx=n_ctx, ctx_row=ctx_row)
    const = lambda i: (0, 0)
    est = 2 * tm * d * 4 + 4 * d * 3 * width + 4 * tm * width * 4 + 6 * tm * 3 * width * 4
    return pl.pallas_call(
        kern,
        grid=(t // tm,),
        in_specs=[pl.BlockSpec((tm, d), lambda i: (i, 0)),
                  pl.BlockSpec((None, 6, d), lambda i: (_cond_index(i * tm, n_ctx, dec_seq), 0, 0)),
                  pl.BlockSpec((1, d), const),
                  pl.BlockSpec(w_in.shape, const),
                  pl.BlockSpec((1, 3 * width), const),
                  pl.BlockSpec(conv_w.shape, const),
                  pl.BlockSpec((1, 3 * width), const)],
        out_specs=[pl.BlockSpec((tm, width), lambda i: (i, 0))] * 2,
        out_shape=[jax.ShapeDtypeStruct((t, width), F32)] * 2,
        compiler_params=_cparams(est, ("parallel",)),
        name="c_in",
    )(x, mod, g, w_in, b_in, conv_w, conv_b)


def _c_out_kernel(x_ref, mod_ref, x0_ref, vg_ref, yc_ref, hb_ref, w_ref, b_ref, o_ref):
    m = mod_ref[...]
    vg = vg_ref[...]
    y = (yc_ref[...] + hb_ref[...] * vg) * x0_ref[...]
    out = jnp.dot(y.astype(BF16), w_ref[...], preferred_element_type=F32) + b_ref[...]
    o_ref[...] = x_ref[...] + m[2:3] * out


def _c_out(x, mod, x0, vg, yconv, hy_bias, w_out, b_out, *, n_ctx, dec_seq, tm):
    t, d = x.shape
    width = x0.shape[1]
    const = lambda i: (0, 0)
    row = lambda i: (i, 0)
    est = 4 * tm * d * 4 + 6 * tm * width * 4 + 4 * width * d + 4 * tm * width * 4
    return pl.pallas_call(
        _c_out_kernel,
        grid=(t // tm,),
        in_specs=[pl.BlockSpec((tm, d), row),
                  pl.BlockSpec((None, 6, d), lambda i: (_cond_index(i * tm, n_ctx, dec_seq), 0, 0)),
                  pl.BlockSpec((tm, width), row),
                  pl.BlockSpec((tm, width), row),
                  pl.BlockSpec((tm, width), row),
                  pl.BlockSpec((1, width), const),
                  pl.BlockSpec(w_out.shape, const),
                  pl.BlockSpec((1, d), const)],
        out_specs=pl.BlockSpec((tm, d), row),
        out_shape=jax.ShapeDtypeStruct((t, d), F32),
        compiler_params=_cparams(est, ("parallel",)),
        name="c_out",
    )(x, mod, x0, vg, yconv, hy_bias, w_out, b_out)


def _hy_filter_kernel(w1_ref, b1_ref, f1_ref, w2_ref, b2_ref, f2_ref, w3_ref, dec_ref, o_ref, z_scr, *, length):
    ct = o_ref.shape[1]
    nb = HY_BANDS
    ecols = w1_ref.shape[0]
    row = lax.broadcasted_iota(jnp.int32, (length, 1), 0)
    positions = (row, length - row)

    @pl.when(pl.program_id(0) == 0)
    def _():
        for side in range(2):
            posf = positions[side].astype(F32)
            col = lax.broadcasted_iota(jnp.int32, (length, ecols), 1)
            band_idx = jnp.where(col <= nb, col - 1, col - 1 - nb).astype(F32)
            bands = 1e-4 + band_idx * ((nb - 1 - 1e-4) / (nb - 1))
            ang = (2.0 * math.pi / length) * posf * bands
            emb = jnp.where(col == 0, posf / length,
                            jnp.where(col <= nb, jnp.cos(ang),
                                      jnp.where(col <= 2 * nb, -jnp.sin(ang), 0.0)))
            z = jnp.sin(f1_ref[...] * (jnp.dot(emb, w1_ref[...], precision=HIGHEST,
                                               preferred_element_type=F32) + b1_ref[...]))
            z_scr[side] = jnp.sin(f2_ref[...] * (jnp.dot(z, w2_ref[...], precision=HIGHEST,
                                                         preferred_element_type=F32) + b2_ref[...]))

    def half(side):
        t = positions[side].astype(F32) / length
        filt = jnp.dot(z_scr[side], w3_ref[side], precision=HIGHEST, preferred_element_type=F32)
        return filt * (jnp.exp(-t * jnp.abs(dec_ref[side:side + 1, :])) + HY_SHIFT)

    fwd = half(0)
    bwd = jnp.where(lax.broadcasted_iota(jnp.int32, (length, ct), 0) == 0, 0.0, half(1))
    ss = jnp.sum(fwd * fwd, axis=0, keepdims=True) + jnp.sum(bwd * bwd, axis=0, keepdims=True)
    scale = lax.rsqrt(ss + EPS)
    o_ref[:length, :] = fwd * scale
    o_ref[length:, :] = bwd * scale


def _hy_filter(length, w1, b1, f1, w2, b2, f2, w3, decay, *, ct):
    emb, ff = w1.shape
    width = decay.shape[1]
    ecols = 128
    w1p = jnp.zeros((ecols, ff), F32).at[:emb].set(w1)
    w3s = w3.reshape(ff, 2, width).transpose(1, 0, 2)
    kern = functools.partial(_hy_filter_kernel, length=length)
    const = lambda j: (0, 0)
    est = 10 * length * ct * 4 + 4 * length * ecols * 4
    return pl.pallas_call(
        kern,
        grid=(width // ct,),
        in_specs=[pl.BlockSpec((ecols, ff), const),
                  pl.BlockSpec((1, ff), const),
                  pl.BlockSpec((1, ff), const),
                  pl.BlockSpec((ff, ff), const),
                  pl.BlockSpec((1, ff), const),
                  pl.BlockSpec((1, ff), const),
                  pl.BlockSpec((2, ff, ct), lambda j: (0, 0, j)),
                  pl.BlockSpec((2, ct), lambda j: (0, j))],
        out_specs=pl.BlockSpec((2 * length, ct), lambda j: (0, j)),
        out_shape=jax.ShapeDtypeStruct((2 * length, width), F32),
        scratch_shapes=[pltpu.VMEM((2, length, ff), F32)],
        compiler_params=_cparams(est, ("arbitrary",)),
        name="hy_filter",
    )(w1p, b1.reshape(1, ff), f1.reshape(1, ff), w2, b2.reshape(1, ff), f2.reshape(1, ff), w3s, decay)


def _fftconv_kernel(ct_ref, st_ref, x_ref, k_ref, c_ref, s_ref, twr_ref, twi_ref, o_ref, *scratch, n1):
    kfr, kfi = scratch if scratch else (None, None)
    h1 = n1 // 2
    n2 = c_ref.shape[0]
    n = n1 * n2
    cm = c_ref[...]
    sm = s_ref[...]

    def dft(ar, ai, inverse):
        ab, bb = ar.astype(BF16), ai.astype(BF16)
        ca = jnp.dot(cm, ab, preferred_element_type=F32)
        sa = jnp.dot(sm, ab, preferred_element_type=F32)
        cb = jnp.dot(cm, bb, preferred_element_type=F32)
        sb = jnp.dot(sm, bb, preferred_element_type=F32)
        if inverse:
            return ca - sb, cb + sa
        return ca + sb, cb - sa

    def slab(t1):
        return slice(t1 * n2, (t1 + 1) * n2)

    twr1, twi1 = twr_ref[...], twi_ref[...]
    tw0 = (jnp.ones_like(twr1), jnp.zeros_like(twi1))

    def next_twiddle(tw):
        return tw[0] * twr1 - tw[1] * twi1, tw[0] * twi1 + tw[1] * twr1

    def filter_dft(k1, tw):
        ar = ai = None
        for t1 in range(n1):
            kr = k_ref[slab(t1), :]
            tr, ti = ct_ref[k1, t1] * kr, -(st_ref[k1, t1] * kr)
            ar = tr if ar is None else ar + tr
            ai = ti if ai is None else ai + ti
        ar, ai = ar * (1.0 / n), ai * (1.0 / n)
        return dft(ar * tw[0] + ai * tw[1], ai * tw[0] - ar * tw[1], False)

    if kfr is not None:
        @pl.when(pl.program_id(1) == 0)
        def _():
            def fbody(k1, tw):
                kfr[k1], kfi[k1] = filter_dft(k1, tw)
                return next_twiddle(tw)
            lax.fori_loop(0, n1, fbody, tw0)

    o_ref[...] = jnp.zeros_like(o_ref)

    def body(k1, tw):
        twr, twi = tw
        ar = ai = None
        for t1 in range(h1):
            cc, ss = ct_ref[k1, t1], st_ref[k1, t1]
            xr, xi = x_ref[0, slab(t1), :], x_ref[1, slab(t1), :]
            tr, ti = cc * xr + ss * xi, cc * xi - ss * xr
            ar = tr if ar is None else ar + tr
            ai = ti if ai is None else ai + ti
        fr, fi = dft(ar * twr + ai * twi, ai * twr - ar * twi, False)
        gr, gi = (kfr[k1], kfi[k1]) if kfr is not None else filter_dft(k1, tw)
        br, bi = dft(fr * gr - fi * gi, fr * gi + fi * gr, True)
        br, bi = br * twr - bi * twi, bi * twr + br * twi
        for t1 in range(h1):
            cc, ss = ct_ref[k1, t1], st_ref[k1, t1]
            o_ref[0, slab(t1), :] += cc * br - ss * bi
            o_ref[1, slab(t1), :] += cc * bi + ss * br
        return next_twiddle(tw)

    lax.fori_loop(0, n1, body, tw0)


def _fftconv(v, circ, *, n1, ct):
    bsz, length, width = v.shape
    n = 2 * length
    n2 = n // n1
    idx = np.arange(n2)
    ang = 2.0 * np.pi * ((idx[:, None] * idx[None, :]) % n2) / n2
    cmat = jnp.asarray(np.cos(ang), F32).astype(BF16)
    smat = jnp.asarray(np.sin(ang), F32).astype(BF16)
    tang = 2.0 * np.pi * idx / n
    twr = jnp.asarray(np.broadcast_to(np.cos(tang)[:, None], (n2, ct)), F32)
    twi = jnp.asarray(np.broadcast_to(np.sin(tang)[:, None], (n2, ct)), F32)
    i1 = np.arange(n1)
    ang1 = 2.0 * np.pi * ((i1[:, None] * i1[None, :]) % n1) / n1
    ctab = jnp.asarray(np.round(np.cos(ang1), 12), F32)
    stab = jnp.asarray(np.round(np.sin(ang1), 12), F32)
    kern = functools.partial(_fftconv_kernel, n1=n1)
    const = lambda j, p: (0, 0)
    once = pl.Buffered(1)
    npairs = bsz // 2
    scratch = [pltpu.VMEM((n1, n2, ct), F32)] * 2 if npairs > 1 else []
    est = (4 * 2 * length * ct * 4 + n * ct * 4 + 2 * n2 * n2 * 2 + 2 * n2 * ct * 4
           + len(scratch) * n * ct * 4 + (8 + 2 * n1) * n2 * ct * 4)
    return pl.pallas_call(
        kern,
        grid=(width // ct, npairs),
        in_specs=[pl.BlockSpec(memory_space=pltpu.SMEM),
                  pl.BlockSpec(memory_space=pltpu.SMEM),
                  pl.BlockSpec((2, length, ct), lambda j, p: (p, 0, j)),
                  pl.BlockSpec((n, ct), lambda j, p: (0, j), pipeline_mode=once),
                  pl.BlockSpec((n2, n2), const, pipeline_mode=once),
                  pl.BlockSpec((n2, n2), const, pipeline_mode=once),
                  pl.BlockSpec((n2, ct), const, pipeline_mode=once),
                  pl.BlockSpec((n2, ct), const, pipeline_mode=once)],
        out_specs=pl.BlockSpec((2, length, ct), lambda j, p: (p, 0, j)),
        out_shape=jax.ShapeDtypeStruct((bsz, length, width), F32),
        scratch_shapes=scratch,
        compiler_params=_cparams(est, ("arbitrary", "arbitrary")),
        name="fftconv",
    )(ctab, stab, v, circ, cmat, smat, twr, twi)


def _peer_route_kernel(x_ref, mod_ref, g_ref, wq_ref, keys_ref, hb_ref, th_ref, e0_ref, s1_ref, e1_ref,
                       q_scr, sc_scr, top_scr):
    tq = x_ref.shape[0]
    nk = PEER_NKEYS
    k = PEER_TOPK
    nhp = keys_ref.shape[0]
    dkh = keys_ref.shape[2]
    neg = -jnp.inf
    m = mod_ref[...]
    h = _norm_mod(x_ref[...], g_ref[...], m[4:5], m[3:4]).astype(BF16)
    hb_ref[...] = h
    q = jnp.dot(h, wq_ref[...], preferred_element_type=F32)
    for hp in range(nhp):
        q_scr[hp] = q[:, hp * dkh:(hp + 1) * dkh].astype(BF16)

    lw = V7X_LANES
    sub = V7X_SUBLANES
    assert k == 2 * sub
    ncand = k + 7 * sub + (k - sub)
    npad = sum(sub - k // (i + 1) for i in range(2, sub))
    iota_k = lax.broadcasted_iota(jnp.int32, (nk, lw), 0)
    iota_c = lax.broadcasted_iota(jnp.int32, (ncand, lw), 0)
    iota_8 = lax.broadcasted_iota(jnp.int32, (sub, lw), 0)
    iota_r = lax.broadcasted_iota(jnp.int32, (k, lw), 0)

    def topk(vals, iota, exact):
        top = jnp.zeros((k, lw), F32)
        for r in range(k):
            mx = jnp.max(vals, axis=0, keepdims=True)
            if exact:
                first = jnp.min(jnp.where(vals == mx, iota, vals.shape[0]), axis=0, keepdims=True)
                vals = jnp.where(iota == first, neg, vals)
            else:
                vals = jnp.where(vals == mx, neg, vals)
            top = jnp.where(iota_r == r, mx, top)
        return top, vals

    def tied(rest, expected):
        gone = jnp.sum(jnp.where(rest == neg, 1.0, 0.0), axis=0, keepdims=True)
        return jnp.where(gone != expected, 1.0, 0.0)

    def halves(hd, exact):
        flag = jnp.zeros((1, lw), F32)
        for p in range(2):
            for c in range(tq // lw):
                lanes = slice(c * lw, (c + 1) * lw)
                top, rest = topk(sc_scr[2 * hd + p, :, lanes], iota_k, exact)
                top_scr[2 * hd + p, :, lanes] = top
                flag = jnp.maximum(flag, tied(rest, float(k)))
        return jnp.max(flag)

    def pairs(hd, exact):
        flag = jnp.zeros((1, lw), F32)
        for c in range(tq // lw):
            lanes = slice(c * lw, (c + 1) * lw)
            a = top_scr[2 * hd, :, lanes]
            b = top_scr[2 * hd + 1, :, lanes]
            pieces = [a[0:1] + b, a[1:2] + b[0:sub]]
            for i in range(2, sub):
                pieces.append(jnp.where(iota_8 < k // (i + 1), a[i:i + 1] + b[0:sub], neg))
            pieces.append(a[sub:k] + b[0:1])
            vmax = a[0:1] + b[0:1]
            top, rest = topk(jnp.concatenate(pieces, axis=0), iota_c, exact)
            z = jnp.sum(jnp.exp(top - vmax), axis=0, keepdims=True)
            last = top[k - 1:k]
            nxt = jnp.max(rest, axis=0, keepdims=True)
            tau = jnp.where(nxt == neg, last, 0.5 * (last + nxt))
            flag = jnp.maximum(flag, tied(rest, float(k + npad)))
            s0 = sc_scr[2 * hd, :, lanes]
            s1 = sc_scr[2 * hd + 1, :, lanes]
            th_ref[hd, :, lanes] = tau - s0
            e0_ref[hd, :, lanes] = jnp.exp(s0 - a[0:1]) / z
            s1_ref[hd, :, lanes] = s1
            e1_ref[hd, :, lanes] = jnp.exp(s1 - b[0:1])
        return jnp.max(flag)

    def head_body(hd, carry):
        for p in range(2):
            sc_scr[2 * hd + p] = lax.dot_general(keys_ref[2 * hd + p], q_scr[2 * hd + p], (((1,), (1,)), ((), ())),
                                                 preferred_element_type=F32)
        @pl.when(halves(hd, False) > 0.0)
        def _():
            halves(hd, True)

        @pl.when(pairs(hd, False) > 0.0)
        def _():
            pairs(hd, True)

        return carry

    lax.fori_loop(0, nhp // 2, head_body, 0)


def _peer_route(x, mod, g, wq, keys, *, n_ctx, dec_seq, tq):
    t, d = x.shape
    nhp, nk, dkh = keys.shape
    nh = nhp // 2
    const = lambda i: (0, 0)
    hshape = jax.ShapeDtypeStruct((nh, nk, t), F32)
    hspec = pl.BlockSpec((nh, nk, tq), lambda i: (0, 0, i))
    est = (2 * tq * d * 4 + 4 * d * nhp * dkh + 2 * tq * d * 2 + 8 * nh * nk * tq * 4
           + 2 * nhp * nk * tq * 4 + 6 * tq * nhp * dkh * 4)
    return pl.pallas_call(
        _peer_route_kernel,
        grid=(t // tq,),
        in_specs=[pl.BlockSpec((tq, d), lambda i: (i, 0)),
                  pl.BlockSpec((None, 6, d), lambda i: (_cond_index(i * tq, n_ctx, dec_seq), 0, 0)),
                  pl.BlockSpec((1, d), const),
                  pl.BlockSpec(wq.shape, const),
                  pl.BlockSpec(keys.shape, lambda i: (0, 0, 0))],
        out_specs=[pl.BlockSpec((tq, d), lambda i: (i, 0)), hspec, hspec, hspec, hspec],
        out_shape=[jax.ShapeDtypeStruct((t, d), BF16), hshape, hshape, hshape, hshape],
        scratch_shapes=[pltpu.VMEM((nhp, tq, dkh), BF16),
                        pltpu.VMEM((nhp, nk, tq), F32),
                        pltpu.VMEM((nhp, PEER_TOPK, tq), F32)],
        compiler_params=_cparams(est, ("parallel",)),
        name="peer_route",
    )(x, mod, g, wq, keys)


def _peer_expert_kernel(hb_ref, u_ref, vt_ref, th_ref, e0_ref, s1_ref, e1_ref, x_ref, mod_ref, o_ref,
                        acc, st_a, st_b, hbuf_a, hbuf_b, *, nc):
    g = pl.program_id(0)
    nh = th_ref.shape[0]
    nk = s1_ref.shape[1]
    ngroups, per = th_ref.shape[1], th_ref.shape[2]
    gw = per * nk
    tq = st_a.shape[1]
    kh = nk // 2
    c3 = jnp.maximum(g - 2, 0) % nc

    @pl.when(g == 0)
    def _():
        st_b[...] = jnp.zeros_like(st_b)
        hbuf_a[...] = jnp.zeros_like(hbuf_a)
        hbuf_b[...] = jnp.zeros_like(hbuf_b)

    @pl.when(c3 == 0)
    def _():
        acc[...] = jnp.zeros_like(acc)

    def step(st_w, st_r, hbuf_w, hbuf_r):
        def group(j, carry):
            rows = pl.ds(pl.multiple_of(j * gw, gw), gw)
            acc[...] += jnp.dot(vt_ref[j], hbuf_r[rows, :], preferred_element_type=F32)
            st_w[rows, :] = lax.dot_general(u_ref[rows, :], hb_ref[...], (((1,), (1,)), ((), ())),
                                            preferred_element_type=F32)
            for q in range(tq // V7X_LANES):
                lanes = slice(q * V7X_LANES, (q + 1) * V7X_LANES)
                for half in range(2):
                    keys = slice(half * kh, (half + 1) * kh)
                    w = [None] * per
                    for hd in range(nh):
                        s1 = s1_ref[hd, keys, lanes]
                        e1 = e1_ref[hd, keys, lanes]
                        for i in range(per):
                            sel = jnp.where(s1 >= th_ref[hd, j, i:i + 1, lanes], e1, 0.0) * e0_ref[hd, j, i:i + 1, lanes]
                            w[i] = sel if w[i] is None else w[i] + sel
                    for i in range(per):
                        tile = pl.ds(pl.multiple_of(j * gw + i * nk + half * kh, kh), kh)
                        hbuf_w[tile, lanes] = (_gelu(st_r[tile, lanes]) * w[i]).astype(BF16)
            return carry

        lax.fori_loop(0, ngroups, group, 0)

    @pl.when(g % 2 == 0)
    def _():
        step(st_a, st_b, hbuf_b, hbuf_a)

    @pl.when(g % 2 == 1)
    def _():
        step(st_b, st_a, hbuf_a, hbuf_b)

    @pl.when(jnp.logical_and(g >= 2, c3 == nc - 1))
    def _():
        o_ref[...] = x_ref[...] + mod_ref[5:6, :] * acc[...].T


def _peer_expert(hb, u_tab, vt_tab, th, e0, s1, e1, x, mod, *, n_ctx, dec_seq, tq, ec):
    t, d = x.shape
    ne = u_tab.shape[0]
    nh, nk, _ = th.shape
    per = PEER_GROUP // nk
    ngroups = ec // PEER_GROUP
    nc = ne // ec
    npairs = (t // tq) * nc
    th = th.reshape(nh, nk // per, per, t)
    e0 = e0.reshape(nh, nk // per, per, t)
    vt_tab = vt_tab.reshape(d, ne // PEER_GROUP, PEER_GROUP).transpose(1, 0, 2)

    def pair(g, lag):
        p = jnp.clip(g - lag, 0, npairs - 1)
        return p // nc, p % nc

    def cond3(g):
        return _cond_index(pair(g, 2)[0] * tq, n_ctx, dec_seq)

    est = (2 * tq * d * 2 + 4 * ec * d * 2 + 4 * nh * ngroups * V7X_SUBLANES * tq * 4 + 4 * nh * nk * tq * 4
           + 4 * tq * d * 4 + d * tq * 4 + 2 * ec * tq * 2 + 2 * ec * tq * 4 + 3 * PEER_GROUP * tq * 4)
    return pl.pallas_call(
        functools.partial(_peer_expert_kernel, nc=nc),
        grid=(npairs + 2,),
        in_specs=[pl.BlockSpec((tq, d), lambda g: (pair(g, 0)[0], 0)),
                  pl.BlockSpec((ec, d), lambda g: (pair(g, 0)[1], 0)),
                  pl.BlockSpec((ngroups, d, PEER_GROUP), lambda g: (pair(g, 2)[1], 0, 0)),
                  pl.BlockSpec((nh, ngroups, per, tq), lambda g: (0, pair(g, 1)[1], 0, pair(g, 1)[0])),
                  pl.BlockSpec((nh, ngroups, per, tq), lambda g: (0, pair(g, 1)[1], 0, pair(g, 1)[0])),
                  pl.BlockSpec((nh, nk, tq), lambda g: (0, 0, pair(g, 1)[0])),
                  pl.BlockSpec((nh, nk, tq), lambda g: (0, 0, pair(g, 1)[0])),
                  pl.BlockSpec((tq, d), lambda g: (pair(g, 2)[0], 0)),
                  pl.BlockSpec((None, 6, d), lambda g: (cond3(g), 0, 0))],
        out_specs=pl.BlockSpec((tq, d), lambda g: (pair(g, 2)[0], 0)),
        out_shape=jax.ShapeDtypeStruct((t, d), F32),
        scratch_shapes=[pltpu.VMEM((d, tq), F32), pltpu.VMEM((ec, tq), F32), pltpu.VMEM((ec, tq), F32),
                        pltpu.VMEM((ec, tq), BF16), pltpu.VMEM((ec, tq), BF16)],
        compiler_params=_cparams(est, ("arbitrary",)),
        name="peer_expert",
    )(hb, u_tab, vt_tab, th, e0, s1, e1, x, mod)


def _final_kernel(x_ref, g_ref, o_ref):
    x = x_ref[...]
    ms = jnp.mean(x * x, axis=-1, keepdims=True)
    o_ref[...] = (x * lax.rsqrt(ms + EPS)) * g_ref[...]


def _final_norm(x, g, *, tm):
    t, d = x.shape
    return pl.pallas_call(
        _final_kernel,
        grid=(t // tm,),
        in_specs=[pl.BlockSpec((tm, d), lambda i: (i, 0)), pl.BlockSpec((1, d), lambda i: (0, 0))],
        out_specs=pl.BlockSpec((tm, d), lambda i: (i, 0)),
        out_shape=jax.ShapeDtypeStruct((t, d), F32),
        compiler_params=_cparams(6 * tm * d * 4, ("parallel",)),
        name="final_norm",
    )(x, g)


def _block_diag(w):
    h, i, j = w.shape
    return jnp.einsum('hij,hg->higj', w, jnp.eye(h, dtype=w.dtype)).reshape(h * i, h * j)


def _s5_discretize(a_re, a_im, log_dt, b_re, b_im):
    dt = jnp.exp(log_dt)[:, None]
    mag = jnp.exp(a_re * dt)
    abar_r = mag * jnp.cos(a_im * dt)
    abar_i = mag * jnp.sin(a_im * dt)
    den = a_re * a_re + a_im * a_im
    num_r = abar_r - 1.0
    coef_r = ((num_r * a_re + abar_i * a_im) / den)[..., None]
    coef_i = ((abar_i * a_re - num_r * a_im) / den)[..., None]
    return abar_r, abar_i, coef_r * b_re - coef_i * b_im, coef_r * b_im + coef_i * b_re


def _s5_blocks(abar_r, abar_i, bbar_r, bbar_i, c_re, c_im):
    g, p, ch = bbar_r.shape
    gb = g // S5_COLBLK

    def vec(v):
        return v.reshape(S5_COLBLK, 1, gb * p)

    def bmat(b):
        bt = b.transpose(0, 2, 1).reshape(S5_COLBLK, gb, ch, p)
        return jnp.stack([_block_diag(bt[j]) for j in range(S5_COLBLK)])

    def cmat(cc):
        ct = cc.transpose(0, 2, 1).reshape(S5_COLBLK, gb, p, ch)
        return jnp.stack([_block_diag(ct[j]) for j in range(S5_COLBLK)])

    bblk = jnp.concatenate([bmat(bbar_r), bmat(bbar_i)], axis=2).astype(BF16)
    cblk = jnp.concatenate([cmat(c_re), cmat(-c_im)], axis=1).astype(BF16)
    return bblk, cblk, vec(abar_r), vec(abar_i)


def _state_to_blocked(re, im):
    n, g, p = re.shape
    gb = g // S5_COLBLK
    both = jnp.stack([re.reshape(n, S5_COLBLK, gb * p), im.reshape(n, S5_COLBLK, gb * p)], axis=2)
    return both.reshape(n, -1)


def _state_from_blocked(st, g, p):
    n = st.shape[0]
    both = st.reshape(n, S5_COLBLK, 2, (g // S5_COLBLK) * p)
    return both[:, :, 0].reshape(n, g, p), both[:, :, 1].reshape(n, g, p)


def _complex_power(ar, ai, n):
    for _ in range(int(math.log2(n))):
        ar, ai = ar * ar - ai * ai, 2.0 * ar * ai
    return ar, ai


def _mixer_ab(x, mod, p, le, h0, *, bc, lc, bs, ls, tm):
    n_ctx = bc * lc
    width = p['s5_d'].shape[1]
    g_cnt, p_cnt = p['s5_a_re'].shape[2:]
    w_gates = jnp.concatenate([_block_diag(p['lru_w_a'][le, 0]), _block_diag(p['lru_w_x'][le, 0]),
                               _block_diag(p['lru_w_a'][le, 1]), _block_diag(p['lru_w_x'][le, 1])],
                              axis=1).astype(BF16)
    b_gates = jnp.concatenate([p['lru_b_a'][le, 0], p['lru_b_x'][le, 0],
                               p['lru_b_a'][le, 1], p['lru_b_x'][le, 1]]).reshape(1, 4 * width)
    c8 = LRU_C * jax.nn.softplus(-p['lru_lambda'][le])
    s_arr, xg = _ab_in(x, mod, p['norm1_g'][2 * le].reshape(1, -1), p['w_in_ab'][le].astype(BF16),
                       p['lru_conv_w'][le], p['lru_conv_b'][le].reshape(1, width), w_gates, b_gates, c8,
                       n_ctx=n_ctx, ctx_row=lc, dec_seq=ls, tm=tm)

    cols = s_arr.shape[1]
    gc = bc // SCAN_SEQS
    lseg = ls // SAMPLE_SEGS
    s_ctx = (s_arr[:n_ctx].reshape(gc, SCAN_SEQS, lc, cols).transpose(0, 2, 1, 3)
             .reshape(n_ctx, cols))
    s_smp = (s_arr[n_ctx:].reshape(bs * SAMPLE_SEGS, lseg, cols).transpose(1, 0, 2)
             .reshape(bs * ls, cols))
    chunk_rows = min(1024, lc * SCAN_SEQS, lseg * SCAN_SEQS)

    outs = {}
    finals = {}
    for d in range(2):
        abar_r, abar_i, bbar_r, bbar_i = _s5_discretize(
            p['s5_a_re'][le, d], p['s5_a_im'][le, d], p['s5_log_dt'][le, d],
            p['s5_b_re'][le, d], p['s5_b_im'][le, d])
        bblk, cblk, ar, ai = _s5_blocks(abar_r, abar_i, bbar_r, bbar_i,
                                        p['s5_c_re'][le, d], p['s5_c_im'][le, d])
        nstate = 2 * g_cnt * p_cnt
        z5 = jnp.zeros((gc, SCAN_SEQS, nstate), F32)
        zl = jnp.zeros((gc, SCAN_SEQS, width), F32)
        y, hl, f5, fl, _ = _scan(s_ctx, bblk, cblk, ar, ai, z5, zl, direction=d, store=True,
                                 groups=gc, chunk_rows=chunk_rows)
        outs[('ctx', d)] = (y, hl)
        finals[d] = (f5.reshape(bc, nstate), fl.reshape(bc, width))
        z5 = jnp.zeros((1, SCAN_SEQS, nstate), F32)
        zl = jnp.zeros((1, SCAN_SEQS, width), F32)
        e5, el, ep = _scan(s_smp, bblk, cblk, ar, ai, z5, zl, direction=d, store=False,
                           groups=1, chunk_rows=chunk_rows)
        first = (SAMPLE_SEGS - 1) if d == 1 else 0
        h5 = _state_to_blocked(h0[0][:, d], h0[1][:, d])
        h5_rows = jnp.zeros((bs, SAMPLE_SEGS, nstate), F32).at[:, first].set(h5).reshape(SCAN_SEQS, nstate)
        hl_rows = jnp.zeros((bs, SAMPLE_SEGS, width), F32).at[:, first].set(h0[2][:, d]).reshape(SCAN_SEQS, width)
        arl, ail = _complex_power(ar, ai, lseg)
        i5, il = _chain(e5[0], el[0], ep[0], h5_rows, hl_rows, arl, ail, direction=d)
        y, hl, _, _, _ = _scan(s_smp, bblk, cblk, ar, ai, i5[None], il[None], direction=d, store=True,
                               groups=1, chunk_rows=chunk_rows)
        outs[('smp', d)] = (y, hl)

    def to_tokens(ctx_arr, smp_arr):
        c = ctx_arr.reshape(gc, lc, SCAN_SEQS, width).transpose(0, 2, 1, 3).reshape(n_ctx, width)
        s = smp_arr.reshape(lseg, bs * SAMPLE_SEGS, width).transpose(1, 0, 2).reshape(bs * ls, width)
        return jnp.concatenate([c, s], axis=0)

    yf = to_tokens(outs[('ctx', 0)][0], outs[('smp', 0)][0])
    yb = to_tokens(outs[('ctx', 1)][0], outs[('smp', 1)][0])
    hf = to_tokens(outs[('ctx', 0)][1], outs[('smp', 0)][1])
    hb = to_tokens(outs[('ctx', 1)][1], outs[('smp', 1)][1])
    x1 = _ab_out(x, mod, s_arr, xg, yf, yb, hf, hb, p['s5_d'][le].reshape(1, width),
                 p['s5_w_glu'][le].astype(BF16), p['s5_b_glu'][le].reshape(1, width),
                 p['w_out_ab'][le].astype(BF16), n_ctx=n_ctx, dec_seq=ls, tm=tm)
    st = [_state_from_blocked(finals[d][0], g_cnt, p_cnt) for d in range(2)]
    new_re = jnp.stack([st[0][0], st[1][0]], axis=1)
    new_im = jnp.stack([st[0][1], st[1][1]], axis=1)
    new_lru = jnp.stack([finals[0][1], finals[1][1]], axis=1)
    return x1, (new_re, new_im, new_lru)


def _fft_split(length):
    n = 2 * length
    n1 = 2
    while n // n1 > 1024:
        n1 *= 2
    return n1


def _mixer_c(x, mod, p, lo, *, bc, lc, bs, ls, tm):
    n_ctx = bc * lc
    width = p['hy_bias'].shape[1]
    x0, vg = _c_in(x, mod, p['norm1_g'][2 * lo + 1].reshape(1, -1), p['w_in_c'][lo].astype(BF16),
                   p['b_in_c'][lo].reshape(1, -1), p['hy_conv_w'][lo], p['hy_conv_b'][lo].reshape(1, -1),
                   n_ctx=n_ctx, ctx_row=lc, dec_seq=ls, tm=tm)
    ys = []
    for (b, l, v) in ((bc, lc, vg[:n_ctx]), (bs, ls, vg[n_ctx:])):
        circ = _hy_filter(l, p['hy_w1'][lo], p['hy_b1'][lo], p['hy_freq1'][lo], p['hy_w2'][lo],
                          p['hy_b2'][lo], p['hy_freq2'][lo], p['hy_w3'][lo], p['hy_decay'][lo],
                          ct=min(width, 256 if l <= 1024 else 128))
        y = _fftconv(v.reshape(b, l, width), circ, n1=_fft_split(l), ct=128)
        ys.append(y.reshape(b * l, width))
    yconv = jnp.concatenate(ys, axis=0)
    return _c_out(x, mod, x0, vg, yconv, p['hy_bias'][lo].reshape(1, width), p['w_out_c'][lo].astype(BF16),
                  p['b_out_c'][lo].reshape(1, -1), n_ctx=n_ctx, dec_seq=ls, tm=tm)


def _peer(x, mod, p, l, *, n_ctx, dec_seq, tq, ec):
    keys = p['peer_keys'][l]
    nh, _, nk, dkh = keys.shape
    hb, th, e0, s1, e1 = _peer_route(x, mod, p['norm2_g'][l].reshape(1, -1), p['peer_wq'][l].astype(BF16),
                                     keys.reshape(nh * 2, nk, dkh).astype(BF16),
                                     n_ctx=n_ctx, dec_seq=dec_seq, tq=tq)
    return _peer_expert(hb, p['peer_u'][l].astype(BF16), p['peer_v'][l].T.astype(BF16), th, e0, s1, e1,
                        x, mod, n_ctx=n_ctx, dec_seq=dec_seq, tq=tq, ec=ec)


def _forward(x_prompt, x_sample, state_s5_re, state_s5_im, state_lru, c, c_ctx, p, *, tm, tq, ec):
    bc, lc, d = x_prompt.shape
    bs, ls, _ = x_sample.shape
    assert bc % SCAN_SEQS == 0 and bs * SAMPLE_SEGS == SCAN_SEQS
    assert tm % lc == 0 and ls % tm == 0 and (ls // SAMPLE_SEGS) % GRID_W == 0
    n_ctx = bc * lc
    x = jnp.concatenate([x_prompt.reshape(n_ctx, d), x_sample.reshape(bs * ls, d)], axis=0)
    cond = jnp.zeros((V7X_SUBLANES, d), F32).at[0].set(c_ctx).at[1:1 + bs].set(c)
    mod_all = _modulation(cond, p['w_mod'], p['b_mod'])
    states = None
    for l in range(DEPTH):
        mod = mod_all[l].reshape(V7X_SUBLANES, 6, d)
        if l % 2 == 0:
            le = l // 2
            h0 = (state_s5_re[:, le], state_s5_im[:, le], state_lru[:, le])
            x, st = _mixer_ab(x, mod, p, le, h0, bc=bc, lc=lc, bs=bs, ls=ls, tm=tm)
            states = st if states is None else states
        else:
            x = _mixer_c(x, mod, p, l // 2, bc=bc, lc=lc, bs=bs, ls=ls, tm=tm)
        x = _peer(x, mod, p, l, n_ctx=n_ctx, dec_seq=ls, tq=tq, ec=ec)
    y = _final_norm(x, p['final_g'].reshape(1, d), tm=tm)
    new_re, new_im, new_lru = states
    return (y[:n_ctx].reshape(bc, lc, d), y[n_ctx:].reshape(bs, ls, d),
            new_re[:, None], new_im[:, None], new_lru[:, None])


def kernel(x_prompt, x_sample, state_s5_re, state_s5_im, state_lru, c, c_ctx, norm1_g, norm2_g, w_mod, b_mod, w_in_ab, s5_a_re, s5_a_im, s5_log_dt, s5_b_re, s5_b_im, s5_c_re, s5_c_im, s5_d, s5_w_glu, s5_b_glu, lru_conv_w, lru_conv_b, lru_w_a, lru_b_a, lru_w_x, lru_b_x, lru_lambda, w_out_ab, w_in_c, b_in_c, hy_conv_w, hy_conv_b, hy_w1, hy_b1, hy_freq1, hy_w2, hy_b2, hy_freq2, hy_w3, hy_decay, hy_bias, w_out_c, b_out_c, peer_wq, peer_keys, peer_u, peer_v, final_g):
    p = dict(norm1_g=norm1_g, norm2_g=norm2_g, w_mod=w_mod, b_mod=b_mod, w_in_ab=w_in_ab,
             s5_a_re=s5_a_re, s5_a_im=s5_a_im, s5_log_dt=s5_log_dt, s5_b_re=s5_b_re, s5_b_im=s5_b_im,
             s5_c_re=s5_c_re, s5_c_im=s5_c_im, s5_d=s5_d, s5_w_glu=s5_w_glu, s5_b_glu=s5_b_glu,
             lru_conv_w=lru_conv_w, lru_conv_b=lru_conv_b, lru_w_a=lru_w_a, lru_b_a=lru_b_a,
             lru_w_x=lru_w_x, lru_b_x=lru_b_x, lru_lambda=lru_lambda, w_out_ab=w_out_ab,
             w_in_c=w_in_c, b_in_c=b_in_c, hy_conv_w=hy_conv_w, hy_conv_b=hy_conv_b,
             hy_w1=hy_w1, hy_b1=hy_b1, hy_freq1=hy_freq1, hy_w2=hy_w2, hy_b2=hy_b2, hy_freq2=hy_freq2,
             hy_w3=hy_w3, hy_decay=hy_decay, hy_bias=hy_bias, w_out_c=w_out_c, b_out_c=b_out_c,
             peer_wq=peer_wq, peer_keys=peer_keys, peer_u=peer_u, peer_v=peer_v, final_g=final_g)
    return _forward(x_prompt, x_sample, state_s5_re, state_s5_im, state_lru, c, c_ctx, p,
                    tm=512, tq=512, ec=1024)
```

```python
import functools
import math

import jax
import jax.numpy as jnp
import numpy as np
from jax import lax
from jax.experimental import pallas as pl
from jax.experimental.pallas import tpu as pltpu

F32 = jnp.float32
BF16 = jnp.bfloat16
HIGHEST = lax.Precision.HIGHEST

DEPTH = 2
GRID_W = 64
EPS = 1e-6
S5_CH = 16
S5_STATE = 64
LRU_HEADS = 8
LRU_C = 8.0
HY_BANDS = 16
HY_SHIFT = 0.05
PEER_HEADS = 8
PEER_NKEYS = 128
PEER_TOPK = 16
PEER_GROUP = 512
PEER_ACC_ROWS = 256

V7X_LANES = 128
V7X_SUBLANES = 8
V7X_VMEM_BYTES = 64 * 1024 * 1024

SCAN_SEQS = V7X_SUBLANES
SAMPLE_SEGS = 4
S5_COLBLK = 4


def _vmem_limit(est_bytes):
    return int(min(V7X_VMEM_BYTES - (8 << 20), max(32 << 20, est_bytes * 3 // 2)))


def _cparams(est_bytes, sem):
    return pltpu.CompilerParams(dimension_semantics=sem, vmem_limit_bytes=_vmem_limit(est_bytes))


def _gelu(x):
    return 0.5 * x * (1.0 + jnp.tanh(0.7978845608028654 * (x + 0.044715 * (x * x * x))))


def _norm_mod(x, g, sc, sh):
    ms = jnp.mean(x * x, axis=-1, keepdims=True)
    return (x * lax.rsqrt(ms + EPS)) * g * (1.0 + sc) + sh


def _cond_index(row0, n_ctx, dec_seq):
    return jnp.where(row0 < n_ctx, 0, 1 + (row0 - n_ctx) // dec_seq)


def _mod_kernel(c_ref, w_ref, b_ref, o_ref):
    c = c_ref[...]
    s = c * jax.nn.sigmoid(c)
    o_ref[...] = jnp.dot(s, w_ref[...], precision=HIGHEST, preferred_element_type=F32) + b_ref[...]


def _modulation(cond, w_mod, b_mod):
    depth, d, n = w_mod.shape
    rows = cond.shape[0]
    tn = 1536
    est = 2 * d * tn * 4 + 4 * rows * (d + tn) * 4
    return pl.pallas_call(
        _mod_kernel,
        grid=(depth, n // tn),
        in_specs=[pl.BlockSpec((rows, d), lambda l, j: (0, 0)),
                  pl.BlockSpec((None, d, tn), lambda l, j: (l, 0, j)),
                  pl.BlockSpec((None, 1, tn), lambda l, j: (l, 0, j))],
        out_specs=pl.BlockSpec((None, rows, tn), lambda l, j: (l, 0, j)),
        out_shape=jax.ShapeDtypeStruct((depth, rows, n), F32),
        compiler_params=_cparams(est, ("arbitrary", "arbitrary")),
        name="modulation",
    )(cond, w_mod, b_mod.reshape(depth, 1, n))


def _dwconv_rows(x, w, bias, pos, row_len, pad_left):
    tm = x.shape[0]
    acc = None
    for k in range(w.shape[0]):
        d = k - pad_left
        if d == 0:
            term = x
        else:
            shifted = pltpu.roll(x, (-d) % tm, axis=0)
            q = pos + d
            term = jnp.where(q >= 0, jnp.where(q < row_len, shifted, 0.0), 0.0)
        term = term * w[k:k + 1]
        acc = term if acc is None else acc + term
    return acc + bias


def _ab_in_kernel(x_ref, mod_ref, g_ref, w_ref, cw_ref, cb_ref, wg_ref, bg_ref, c8_ref,
                  s_ref, xg_ref, *, n_ctx, ctx_row, width):
    tm = x_ref.shape[0]
    m = mod_ref[...]
    h = _norm_mod(x_ref[...], g_ref[...], m[1:2], m[0:1])
    z = jnp.dot(h.astype(BF16), w_ref[...], preferred_element_type=F32)
    u = z[:, :width]
    xr = z[:, width:2 * width]
    xg_ref[...] = z[:, 2 * width:]
    row_len = jnp.where(pl.program_id(0) * tm < n_ctx, ctx_row, GRID_W)
    pos = lax.broadcasted_iota(jnp.int32, (tm, width), 0) & (row_len - 1)
    xb = _dwconv_rows(xr, cw_ref[...], cb_ref[...], pos, row_len, 2)
    gates = jnp.dot(xb.astype(BF16), wg_ref[...], preferred_element_type=F32) + bg_ref[...]
    s_ref[:, :width] = u
    c8 = c8_ref[...]
    for d in range(2):
        r = jax.nn.sigmoid(gates[:, (2 * d) * width:(2 * d + 1) * width])
        i = jax.nn.sigmoid(gates[:, (2 * d + 1) * width:(2 * d + 2) * width])
        a = jnp.exp(-(c8[d:d + 1] * r))
        b = jnp.sqrt(1.0 - a * a) * (i * xb)
        s_ref[:, (1 + 2 * d) * width:(2 + 2 * d) * width] = a
        s_ref[:, (2 + 2 * d) * width:(3 + 2 * d) * width] = b


def _ab_in(x, mod, g, w_in, conv_w, conv_b, w_gates, b_gates, c8, *, n_ctx, ctx_row, dec_seq, tm):
    t, d = x.shape
    width = w_in.shape[1] // 3
    kern = functools.partial(_ab_in_kernel, n_ctx=n_ctx, ctx_row=ctx_row, width=width)
    const = lambda i: (0, 0)
    est = (2 * tm * d * 4 + d * 3 * width * 2 * 2 + width * 4 * width * 2 * 2
           + 2 * tm * 6 * width * 4 + tm * 12 * width * 4)
    return pl.pallas_call(
        kern,
        grid=(t // tm,),
        in_specs=[pl.BlockSpec((tm, d), lambda i: (i, 0)),
                  pl.BlockSpec((None, 6, d), lambda i: (_cond_index(i * tm, n_ctx, dec_seq), 0, 0)),
                  pl.BlockSpec((1, d), const),
                  pl.BlockSpec(w_in.shape, const),
                  pl.BlockSpec(conv_w.shape, const),
                  pl.BlockSpec((1, width), const),
                  pl.BlockSpec(w_gates.shape, const),
                  pl.BlockSpec((1, 4 * width), const),
                  pl.BlockSpec((2, width), const)],
        out_specs=[pl.BlockSpec((tm, 5 * width), lambda i: (i, 0)),
                   pl.BlockSpec((tm, width), lambda i: (i, 0))],
        out_shape=[jax.ShapeDtypeStruct((t, 5 * width), F32),
                   jax.ShapeDtypeStruct((t, width), F32)],
        compiler_params=_cparams(est, ("parallel",)),
        name="ab_in",
    )(x, mod, g, w_in, conv_w, conv_b, w_gates, b_gates, c8)


def _scan_kernel(u_ref, a_ref, b_ref, bblk_ref, cblk_ref, ar_ref, ai_ref, i5_ref, il_ref, *rest,
                 reverse, store):
    if store:
        y_ref, hl_ref, f5_ref, fl_ref, fp_ref, buf, st5, stl, stp = rest
    else:
        f5_ref, fl_ref, fp_ref, buf, st5, stl, stp = rest
    c = pl.program_id(1)
    rows = u_ref.shape[0]
    nsteps = rows // SCAN_SEQS
    sw = st5.shape[1] // (2 * S5_COLBLK)
    cw = u_ref.shape[1] // S5_COLBLK

    @pl.when(c == 0)
    def _():
        st5[...] = i5_ref[...]
        stl[...] = il_ref[...]
        stp[...] = jnp.ones_like(stp)

    def row0(k):
        t = (nsteps - 1 - k) if reverse else k
        return pl.multiple_of(t * SCAN_SEQS, SCAN_SEQS)

    u = u_ref[...].astype(BF16)
    for j in range(S5_COLBLK):
        buf[...] = jnp.dot(u[:, j * cw:(j + 1) * cw], bblk_ref[j], preferred_element_type=F32)
        arj = jnp.broadcast_to(ar_ref[j], (SCAN_SEQS, sw))
        aij = jnp.broadcast_to(ai_ref[j], (SCAN_SEQS, sw))

        def step(k, carry, arj=arj, aij=aij):
            hr, hi = carry
            r0 = row0(k)
            xr = buf[pl.ds(r0, SCAN_SEQS), 0:sw]
            xi = buf[pl.ds(r0, SCAN_SEQS), sw:2 * sw]
            nr = arj * hr - aij * hi + xr
            ni = arj * hi + aij * hr + xi
            buf[pl.ds(r0, SCAN_SEQS), 0:sw] = nr
            buf[pl.ds(r0, SCAN_SEQS), sw:2 * sw] = ni
            return nr, ni

        base = 2 * sw * j
        hr, hi = lax.fori_loop(0, nsteps, step, (st5[:, base:base + sw], st5[:, base + sw:base + 2 * sw]))
        st5[:, base:base + sw] = hr
        st5[:, base + sw:base + 2 * sw] = hi
        if store:
            y_ref[:, j * cw:(j + 1) * cw] = jnp.dot(buf[...].astype(BF16), cblk_ref[j],
                                                    preferred_element_type=F32)

    def lstep(k, carry):
        h, p = carry
        r0 = row0(k)
        a = a_ref[pl.ds(r0, SCAN_SEQS), :]
        h = a * h + b_ref[pl.ds(r0, SCAN_SEQS), :]
        if store:
            hl_ref[pl.ds(r0, SCAN_SEQS), :] = h
        return h, p * a

    h, p = lax.fori_loop(0, nsteps, lstep, (stl[...], stp[...]))
    stl[...] = h
    stp[...] = p

    @pl.when(c == pl.num_programs(1) - 1)
    def _():
        f5_ref[...] = st5[...]
        fl_ref[...] = stl[...]
        fp_ref[...] = stp[...]


def _scan(s_arr, bblk, cblk, ar, ai, init5, initl, *, direction, store, groups, chunk_rows):
    rows_total, cols = s_arr.shape
    width = cols // 5
    rows_per_group = rows_total // groups
    nc = rows_per_group // chunk_rows
    reverse = direction == 1
    nstate = init5.shape[-1]

    def chunk(g, c):
        return g * nc + ((nc - 1 - c) if reverse else c)

    kern = functools.partial(_scan_kernel, reverse=reverse, store=store)
    const3 = lambda g, c: (0, 0, 0)
    in_specs = [pl.BlockSpec((chunk_rows, width), lambda g, c: (chunk(g, c), 0)),
                pl.BlockSpec((chunk_rows, width), lambda g, c: (chunk(g, c), 1 + 2 * direction)),
                pl.BlockSpec((chunk_rows, width), lambda g, c: (chunk(g, c), 2 + 2 * direction)),
                pl.BlockSpec(bblk.shape, const3),
                pl.BlockSpec(cblk.shape, const3),
                pl.BlockSpec(ar.shape, const3),
                pl.BlockSpec(ai.shape, const3),
                pl.BlockSpec((None, SCAN_SEQS, nstate), lambda g, c: (g, 0, 0)),
                pl.BlockSpec((None, SCAN_SEQS, width), lambda g, c: (g, 0, 0))]
    fin_specs = [pl.BlockSpec((None, SCAN_SEQS, nstate), lambda g, c: (g, 0, 0)),
                 pl.BlockSpec((None, SCAN_SEQS, width), lambda g, c: (g, 0, 0)),
                 pl.BlockSpec((None, SCAN_SEQS, width), lambda g, c: (g, 0, 0))]
    fin_shapes = [jax.ShapeDtypeStruct((groups, SCAN_SEQS, nstate), F32),
                  jax.ShapeDtypeStruct((groups, SCAN_SEQS, width), F32),
                  jax.ShapeDtypeStruct((groups, SCAN_SEQS, width), F32)]
    out_specs, out_shapes = fin_specs, fin_shapes
    if store:
        out_specs = [pl.BlockSpec((chunk_rows, width), lambda g, c: (chunk(g, c), 0))] * 2 + fin_specs
        out_shapes = [jax.ShapeDtypeStruct((rows_total, width), F32)] * 2 + fin_shapes
    bufw = 2 * nstate // (2 * S5_COLBLK)
    est = (10 * chunk_rows * width * 4 + chunk_rows * bufw * 4 * 2
           + 4 * bblk.size * 2 + 8 * SCAN_SEQS * nstate * 4)
    return pl.pallas_call(
        kern,
        grid=(groups, nc),
        in_specs=in_specs,
        out_specs=out_specs,
        out_shape=out_shapes,
        scratch_shapes=[pltpu.VMEM((chunk_rows, bufw), F32),
                        pltpu.VMEM((SCAN_SEQS, nstate), F32),
                        pltpu.VMEM((SCAN_SEQS, width), F32),
                        pltpu.VMEM((SCAN_SEQS, width), F32)],
        compiler_params=_cparams(est, ("arbitrary", "arbitrary")),
        name="scan_store" if store else "scan_ends",
    )(s_arr, s_arr, s_arr, bblk, cblk, ar, ai, init5, initl)


def _chain_kernel(e5_ref, el_ref, ep_ref, h5_ref, hl_ref, arl_ref, ail_ref, i5_ref, il_ref, *, reverse):
    nseg = SAMPLE_SEGS
    sw = arl_ref.shape[-1]
    shift = (SCAN_SEQS - 1) if reverse else 1
    order = range(nseg - 2, -1, -1) if reverse else range(1, nseg)
    seg5 = lax.broadcasted_iota(jnp.int32, (SCAN_SEQS, sw), 0) % nseg
    for j in range(S5_COLBLK):
        base = 2 * sw * j
        re, im = slice(base, base + sw), slice(base + sw, base + 2 * sw)
        ar, ai = arl_ref[j], ail_ref[j]
        er = pltpu.roll(e5_ref[:, re], shift, axis=0)
        ei = pltpu.roll(e5_ref[:, im], shift, axis=0)
        hr, hi = h5_ref[:, re], h5_ref[:, im]
        for k in order:
            pr = pltpu.roll(hr, shift, axis=0)
            pi = pltpu.roll(hi, shift, axis=0)
            hr = jnp.where(seg5 == k, ar * pr - ai * pi + er, hr)
            hi = jnp.where(seg5 == k, ar * pi + ai * pr + ei, hi)
        i5_ref[:, re] = hr
        i5_ref[:, im] = hi
    segl = lax.broadcasted_iota(jnp.int32, el_ref.shape, 0) % nseg
    el = pltpu.roll(el_ref[...], shift, axis=0)
    ep = pltpu.roll(ep_ref[...], shift, axis=0)
    h = hl_ref[...]
    for k in order:
        h = jnp.where(segl == k, ep * pltpu.roll(h, shift, axis=0) + el, h)
    il_ref[...] = h


def _chain(end5, endl, endp, h5_rows, hl_rows, arl, ail, *, direction):
    kern = functools.partial(_chain_kernel, reverse=direction == 1)
    return pl.pallas_call(
        kern,
        out_shape=[jax.ShapeDtypeStruct(end5.shape, F32), jax.ShapeDtypeStruct(endl.shape, F32)],
        name="scan_chain",
    )(end5, endl, endp, h5_rows, hl_rows, arl, ail)


def _ab_out_kernel(x_ref, mod_ref, s_ref, xg_ref, yf_ref, yb_ref, hf_ref, hb_ref, d_ref,
                   wglu_ref, bglu_ref, wout_ref, o_ref):
    width = xg_ref.shape[1]
    m = mod_ref[...]
    y = s_ref[...] * d_ref[...] + yf_ref[...] + yb_ref[...]
    zs = _gelu(y)
    gate = jnp.dot(zs.astype(BF16), wglu_ref[...], preferred_element_type=F32) + bglu_ref[...]
    s5_out = zs * jax.nn.sigmoid(gate)
    lru_out = (hf_ref[...] + hb_ref[...]) * _gelu(xg_ref[...])
    out = (jnp.dot(s5_out.astype(BF16), wout_ref[:width, :], preferred_element_type=F32)
           + jnp.dot(lru_out.astype(BF16), wout_ref[width:, :], preferred_element_type=F32))
    o_ref[...] = x_ref[...] + m[2:3] * out


def _ab_out(x, mod, s_arr, xg, yf, yb, hf, hb, s5_d, w_glu, b_glu, w_out, *, n_ctx, dec_seq, tm):
    t, d = x.shape
    width = xg.shape[1]
    const = lambda i: (0, 0)
    row = lambda i: (i, 0)
    est = 4 * tm * d * 4 + 12 * tm * width * 4 + 4 * (width * width + 2 * width * d) + 8 * tm * width * 4
    return pl.pallas_call(
        _ab_out_kernel,
        grid=(t // tm,),
        in_specs=[pl.BlockSpec((tm, d), row),
                  pl.BlockSpec((None, 6, d), lambda i: (_cond_index(i * tm, n_ctx, dec_seq), 0, 0)),
                  pl.BlockSpec((tm, width), row),
                  pl.BlockSpec((tm, width), row),
                  pl.BlockSpec((tm, width), row),
                  pl.BlockSpec((tm, width), row),
                  pl.BlockSpec((tm, width), row),
                  pl.BlockSpec((tm, width), row),
                  pl.BlockSpec((1, width), const),
                  pl.BlockSpec(w_glu.shape, const),
                  pl.BlockSpec((1, width), const),
                  pl.BlockSpec(w_out.shape, const)],
        out_specs=pl.BlockSpec((tm, d), row),
        out_shape=jax.ShapeDtypeStruct((t, d), F32),
        compiler_params=_cparams(est, ("parallel",)),
        name="ab_out",
    )(x, mod, s_arr, xg, yf, yb, hf, hb, s5_d, w_glu, b_glu, w_out)


def _c_in_kernel(x_ref, mod_ref, g_ref, w_ref, b_ref, cw_ref, cb_ref, x0_ref, vg_ref, *, n_ctx, ctx_row):
    tm = x_ref.shape[0]
    width = x0_ref.shape[1]
    m = mod_ref[...]
    h = _norm_mod(x_ref[...], g_ref[...], m[1:2], m[0:1])
    z = jnp.dot(h.astype(BF16), w_ref[...], preferred_element_type=F32) + b_ref[...]
    row_len = jnp.where(pl.program_id(0) * tm < n_ctx, ctx_row, GRID_W)
    pos = lax.broadcasted_iota(jnp.int32, (tm, width), 0) & (row_len - 1)
    cw = cw_ref[...]
    cb = cb_ref[...]
    parts = [_dwconv_rows(z[:, k * width:(k + 1) * width], cw[:, k * width:(k + 1) * width],
                          cb[:, k * width:(k + 1) * width], pos, row_len, 1) for k in range(3)]
    x0_ref[...] = parts[0]
    vg_ref[...] = parts[2] * parts[1]


def _c_in(x, mod, g, w_in, b_in, conv_w, conv_b, *, n_ctx, ctx_row, dec_seq, tm):
    t, d = x.shape
    width = w_in.shape[1] // 3
    kern = functools.partial(_c_in_kernel, n_ctx=n_ctx, ctx_row=ctx_row)
    const = lambda i: (0, 0)
    est = 2 * tm * d * 4 + 4 * d * 3 * width + 4 * tm * width * 4 + 6 * tm * 3 * width * 4
    return pl.pallas_call(
        kern,
        grid=(t // tm,),
        in_specs=[pl.BlockSpec((tm, d), lambda i: (i, 0)),
                  pl.BlockSpec((None, 6, d), lambda i: (_cond_index(i * tm, n_ctx, dec_seq), 0, 0)),
                  pl.BlockSpec((1, d), const),
                  pl.BlockSpec(w_in.shape, const),
                  pl.BlockSpec((1, 3 * width), const),
                  pl.BlockSpec(conv_w.shape, const),
                  pl.BlockSpec((1, 3 * width), const)],
        out_specs=[pl.BlockSpec((tm, width), lambda i: (i, 0))] * 2,
        out_shape=[jax.ShapeDtypeStruct((t, width), F32)] * 2,
        compiler_params=_cparams(est, ("parallel",)),
        name="c_in",
    )(x, mod, g, w_in, b_in, conv_w, conv_b)


def _c_out_kernel(x_ref, mod_ref, x0_ref, vg_ref, yc_ref, hb_ref, w_ref, b_ref, o_ref):
    m = mod_ref[...]
    vg = vg_ref[...]
    y = (yc_ref[...] + hb_ref[...] * vg) * x0_ref[...]
    out = jnp.dot(y.astype(BF16), w_ref[...], preferred_element_type=F32) + b_ref[...]
    o_ref[...] = x_ref[...] + m[2:3] * out


def _c_out(x, mod, x0, vg, yconv, hy_bias, w_out, b_out, *, n_ctx, dec_seq, tm):
    t, d = x.shape
    width = x0.shape[1]
    const = lambda i: (0, 0)
    row = lambda i: (i, 0)
    est = 4 * tm * d * 4 + 6 * tm * width * 4 + 4 * width * d + 4 * tm * width * 4
    return pl.pallas_call(
        _c_out_kernel,
        grid=(t // tm,),
        in_specs=[pl.BlockSpec((tm, d), row),
                  pl.BlockSpec((None, 6, d), lambda i: (_cond_index(i * tm, n_ctx, dec_seq), 0, 0)),
                  pl.BlockSpec((tm, width), row),
                  pl.BlockSpec((tm, width), row),
                  pl.BlockSpec((tm, width), row),
                  pl.BlockSpec((1, width), const),
                  pl.BlockSpec(w_out.shape, const),
                  pl.BlockSpec((1, d), const)],
        out_specs=pl.BlockSpec((tm, d), row),
        out_shape=jax.ShapeDtypeStruct((t, d), F32),
        compiler_params=_cparams(est, ("parallel",)),
        name="c_out",
    )(x, mod, x0, vg, yconv, hy_bias, w_out, b_out)


def _hy_filter_kernel(w1_ref, b1_ref, f1_ref, w2_ref, b2_ref, f2_ref, w3_ref, dec_ref, o_ref, z_scr, *, length):
    ct = o_ref.shape[1]
    nb = HY_BANDS
    ecols = w1_ref.shape[0]
    row = lax.broadcasted_iota(jnp.int32, (length, 1), 0)
    positions = (row, length - row)

    @pl.when(pl.program_id(0) == 0)
    def _():
        for side in range(2):
            posf = positions[side].astype(F32)
            col = lax.broadcasted_iota(jnp.int32, (length, ecols), 1)
            band_idx = jnp.where(col <= nb, col - 1, col - 1 - nb).astype(F32)
            bands = 1e-4 + band_idx * ((nb - 1 - 1e-4) / (nb - 1))
            ang = (2.0 * math.pi / length) * posf * bands
            emb = jnp.where(col == 0, posf / length,
                            jnp.where(col <= nb, jnp.cos(ang),
                                      jnp.where(col <= 2 * nb, -jnp.sin(ang), 0.0)))
            z = jnp.sin(f1_ref[...] * (jnp.dot(emb, w1_ref[...], precision=HIGHEST,
                                               preferred_element_type=F32) + b1_ref[...]))
            z_scr[side] = jnp.sin(f2_ref[...] * (jnp.dot(z, w2_ref[...], precision=HIGHEST,
                                                         preferred_element_type=F32) + b2_ref[...]))

    def half(side):
        t = positions[side].astype(F32) / length
        filt = jnp.dot(z_scr[side], w3_ref[side], precision=HIGHEST, preferred_element_type=F32)
        return filt * (jnp.exp(-t * jnp.abs(dec_ref[side:side + 1, :])) + HY_SHIFT)

    fwd = half(0)
    bwd = jnp.where(lax.broadcasted_iota(jnp.int32, (length, ct), 0) == 0, 0.0, half(1))
    ss = jnp.sum(fwd * fwd, axis=0, keepdims=True) + jnp.sum(bwd * bwd, axis=0, keepdims=True)
    scale = lax.rsqrt(ss + EPS)
    o_ref[:length, :] = fwd * scale
    o_ref[length:, :] = bwd * scale


def _hy_filter(length, w1, b1, f1, w2, b2, f2, w3, decay, *, ct):
    emb, ff = w1.shape
    width = decay.shape[1]
    ecols = 128
    w1p = jnp.zeros((ecols, ff), F32).at[:emb].set(w1)
    w3s = w3.reshape(ff, 2, width).transpose(1, 0, 2)
    kern = functools.partial(_hy_filter_kernel, length=length)
    const = lambda j: (0, 0)
    est = 10 * length * ct * 4 + 4 * length * ecols * 4
    return pl.pallas_call(
        kern,
        grid=(width // ct,),
        in_specs=[pl.BlockSpec((ecols, ff), const),
                  pl.BlockSpec((1, ff), const),
                  pl.BlockSpec((1, ff), const),
                  pl.BlockSpec((ff, ff), const),
                  pl.BlockSpec((1, ff), const),
                  pl.BlockSpec((1, ff), const),
                  pl.BlockSpec((2, ff, ct), lambda j: (0, 0, j)),
                  pl.BlockSpec((2, ct), lambda j: (0, j))],
        out_specs=pl.BlockSpec((2 * length, ct), lambda j: (0, j)),
        out_shape=jax.ShapeDtypeStruct((2 * length, width), F32),
        scratch_shapes=[pltpu.VMEM((2, length, ff), F32)],
        compiler_params=_cparams(est, ("arbitrary",)),
        name="hy_filter",
    )(w1p, b1.reshape(1, ff), f1.reshape(1, ff), w2, b2.reshape(1, ff), f2.reshape(1, ff), w3s, decay)


def _fftconv_kernel(ct_ref, st_ref, x_ref, k_ref, c_ref, s_ref, twr_ref, twi_ref, o_ref, *scratch, n1):
    kfr, kfi = scratch if scratch else (None, None)
    h1 = n1 // 2
    n2 = c_ref.shape[0]
    n = n1 * n2
    cm = c_ref[...]
    sm = s_ref[...]

    def dft(pairs, inverse):
        w = pairs[0][0].shape[1]
        parts = jnp.concatenate([p.astype(BF16) for pair in pairs for p in pair], axis=1)
        cp = jnp.dot(cm, parts, preferred_element_type=F32)
        sp = jnp.dot(sm, parts, preferred_element_type=F32)
        out = []
        for i in range(len(pairs)):
            ca, cb = cp[:, 2 * i * w:(2 * i + 1) * w], cp[:, (2 * i + 1) * w:(2 * i + 2) * w]
            sa, sb = sp[:, 2 * i * w:(2 * i + 1) * w], sp[:, (2 * i + 1) * w:(2 * i + 2) * w]
            out.append((ca - sb, cb + sa) if inverse else (ca + sb, cb - sa))
        return out

    def slab(t1):
        return slice(t1 * n2, (t1 + 1) * n2)

    twr1, twi1 = twr_ref[...], twi_ref[...]
    tw0 = (jnp.ones_like(twr1), jnp.zeros_like(twi1))

    def next_twiddle(tw):
        return tw[0] * twr1 - tw[1] * twi1, tw[0] * twi1 + tw[1] * twr1

    def filter_stage1(k1, tw):
        ar = ai = None
        for t1 in range(n1):
            kr = k_ref[slab(t1), :]
            tr, ti = ct_ref[k1, t1] * kr, -(st_ref[k1, t1] * kr)
            ar = tr if ar is None else ar + tr
            ai = ti if ai is None else ai + ti
        ar, ai = ar * (1.0 / n), ai * (1.0 / n)
        return ar * tw[0] + ai * tw[1], ai * tw[0] - ar * tw[1]

    if kfr is not None:
        @pl.when(pl.program_id(1) == 0)
        def _():
            def fbody(k1, tw):
                (kfr[k1], kfi[k1]), = dft([filter_stage1(k1, tw)], False)
                return next_twiddle(tw)
            lax.fori_loop(0, n1, fbody, tw0)

    o_ref[...] = jnp.zeros_like(o_ref)

    def body(k1, tw):
        twr, twi = tw
        ar = ai = None
        for t1 in range(h1):
            cc, ss = ct_ref[k1, t1], st_ref[k1, t1]
            xr, xi = x_ref[0, slab(t1), :], x_ref[1, slab(t1), :]
            tr, ti = cc * xr + ss * xi, cc * xi - ss * xr
            ar = tr if ar is None else ar + tr
            ai = ti if ai is None else ai + ti
        data = (ar * twr + ai * twi, ai * twr - ar * twi)
        if kfr is not None:
            (fr, fi), = dft([data], False)
            gr, gi = kfr[k1], kfi[k1]
        else:
            (fr, fi), (gr, gi) = dft([data, filter_stage1(k1, tw)], False)
        (br, bi), = dft([(fr * gr - fi * gi, fr * gi + fi * gr)], True)
        br, bi = br * twr - bi * twi, bi * twr + br * twi
        for t1 in range(h1):
            cc, ss = ct_ref[k1, t1], st_ref[k1, t1]
            o_ref[0, slab(t1), :] += cc * br - ss * bi
            o_ref[1, slab(t1), :] += cc * bi + ss * br
        return next_twiddle(tw)

    lax.fori_loop(0, n1, body, tw0)


def _fftconv(v, circ, *, n1, ct):
    bsz, length, width = v.shape
    n = 2 * length
    n2 = n // n1
    idx = np.arange(n2)
    ang = 2.0 * np.pi * ((idx[:, None] * idx[None, :]) % n2) / n2
    cmat = jnp.asarray(np.cos(ang), F32).astype(BF16)
    smat = jnp.asarray(np.sin(ang), F32).astype(BF16)
    tang = 2.0 * np.pi * idx / n
    twr = jnp.asarray(np.broadcast_to(np.cos(tang)[:, None], (n2, ct)), F32)
    twi = jnp.asarray(np.broadcast_to(np.sin(tang)[:, None], (n2, ct)), F32)
    i1 = np.arange(n1)
    ang1 = 2.0 * np.pi * ((i1[:, None] * i1[None, :]) % n1) / n1
    ctab = jnp.asarray(np.round(np.cos(ang1), 12), F32)
    stab = jnp.asarray(np.round(np.sin(ang1), 12), F32)
    kern = functools.partial(_fftconv_kernel, n1=n1)
    const = lambda j, p: (0, 0)
    once = pl.Buffered(1)
    npairs = bsz // 2
    scratch = [pltpu.VMEM((n1, n2, ct), F32)] * 2 if npairs > 1 else []
    est = (4 * 2 * length * ct * 4 + n * ct * 4 + 2 * n2 * n2 * 2 + 2 * n2 * ct * 4
           + len(scratch) * n * ct * 4 + (8 + 2 * n1) * n2 * ct * 4)
    return pl.pallas_call(
        kern,
        grid=(width // ct, npairs),
        in_specs=[pl.BlockSpec(memory_space=pltpu.SMEM),
                  pl.BlockSpec(memory_space=pltpu.SMEM),
                  pl.BlockSpec((2, length, ct), lambda j, p: (p, 0, j)),
                  pl.BlockSpec((n, ct), lambda j, p: (0, j), pipeline_mode=once),
                  pl.BlockSpec((n2, n2), const, pipeline_mode=once),
                  pl.BlockSpec((n2, n2), const, pipeline_mode=once),
                  pl.BlockSpec((n2, ct), const, pipeline_mode=once),
                  pl.BlockSpec((n2, ct), const, pipeline_mode=once)],
        out_specs=pl.BlockSpec((2, length, ct), lambda j, p: (p, 0, j)),
        out_shape=jax.ShapeDtypeStruct((bsz, length, width), F32),
        scratch_shapes=scratch,
        compiler_params=_cparams(est, ("arbitrary", "arbitrary")),
        name="fftconv",
    )(ctab, stab, v, circ, cmat, smat, twr, twi)


def _peer_route_kernel(x_ref, mod_ref, g_ref, wq_ref, keys_ref, hb_ref, th_ref, e0_ref, s1_ref, e1_ref,
                       q_scr, sc_scr, top_scr):
    tq = x_ref.shape[0]
    nk = PEER_NKEYS
    k = PEER_TOPK
    nhp = keys_ref.shape[0]
    dkh = keys_ref.shape[2]
    neg = -jnp.inf
    m = mod_ref[...]
    h = _norm_mod(x_ref[...], g_ref[...], m[4:5], m[3:4]).astype(BF16)
    hb_ref[...] = h
    q = jnp.dot(h, wq_ref[...], preferred_element_type=F32)
    for hp in range(nhp):
        q_scr[hp] = q[:, hp * dkh:(hp + 1) * dkh].astype(BF16)

    lw = V7X_LANES
    sub = V7X_SUBLANES
    assert k == 2 * sub
    ncand = k + 7 * sub + (k - sub)
    npad = sum(sub - k // (i + 1) for i in range(2, sub))
    iota_k = lax.broadcasted_iota(jnp.int32, (nk, lw), 0)
    iota_c = lax.broadcasted_iota(jnp.int32, (ncand, lw), 0)
    iota_8 = lax.broadcasted_iota(jnp.int32, (sub, lw), 0)
    iota_r = lax.broadcasted_iota(jnp.int32, (k, lw), 0)

    def topk(vals, iota, exact):
        top = jnp.zeros((k, lw), F32)
        for r in range(k):
            mx = jnp.max(vals, axis=0, keepdims=True)
            if exact:
                first = jnp.min(jnp.where(vals == mx, iota, vals.shape[0]), axis=0, keepdims=True)
                vals = jnp.where(iota == first, neg, vals)
            else:
                vals = jnp.where(vals == mx, neg, vals)
            top = jnp.where(iota_r == r, mx, top)
        return top, vals

    def tied(rest, expected):
        gone = jnp.sum(jnp.where(rest == neg, 1.0, 0.0), axis=0, keepdims=True)
        return jnp.where(gone != expected, 1.0, 0.0)

    def halves(hd, exact):
        flag = jnp.zeros((1, lw), F32)
        for p in range(2):
            for c in range(tq // lw):
                lanes = slice(c * lw, (c + 1) * lw)
                top, rest = topk(sc_scr[2 * hd + p, :, lanes], iota_k, exact)
                top_scr[2 * hd + p, :, lanes] = top
                flag = jnp.maximum(flag, tied(rest, float(k)))
        return jnp.max(flag)

    def pairs(hd, exact):
        flag = jnp.zeros((1, lw), F32)
        for c in range(tq // lw):
            lanes = slice(c * lw, (c + 1) * lw)
            a = top_scr[2 * hd, :, lanes]
            b = top_scr[2 * hd + 1, :, lanes]
            pieces = [a[0:1] + b, a[1:2] + b[0:sub]]
            for i in range(2, sub):
                pieces.append(jnp.where(iota_8 < k // (i + 1), a[i:i + 1] + b[0:sub], neg))
            pieces.append(a[sub:k] + b[0:1])
            vmax = a[0:1] + b[0:1]
            top, rest = topk(jnp.concatenate(pieces, axis=0), iota_c, exact)
            z = jnp.sum(jnp.exp(top - vmax), axis=0, keepdims=True)
            last = top[k - 1:k]
            nxt = jnp.max(rest, axis=0, keepdims=True)
            tau = jnp.where(nxt == neg, last, 0.5 * (last + nxt))
            flag = jnp.maximum(flag, tied(rest, float(k + npad)))
            s0 = sc_scr[2 * hd, :, lanes]
            s1 = sc_scr[2 * hd + 1, :, lanes]
            th = tau - s0
            e0 = jnp.exp(s0 - a[0:1]) / z
            per = th_ref.shape[2]
            for r in range(nk // per):
                th_ref[hd, r, :, lanes] = th[r * per:(r + 1) * per]
                e0_ref[hd, r, :, lanes] = e0[r * per:(r + 1) * per]
            s1_ref[hd, :, lanes] = s1
            e1_ref[hd, :, lanes] = jnp.exp(s1 - b[0:1])
        return jnp.max(flag)

    def head_body(hd, carry):
        for p in range(2):
            sc_scr[2 * hd + p] = lax.dot_general(keys_ref[2 * hd + p], q_scr[2 * hd + p], (((1,), (1,)), ((), ())),
                                                 preferred_element_type=F32)
        @pl.when(halves(hd, False) > 0.0)
        def _():
            halves(hd, True)

        @pl.when(pairs(hd, False) > 0.0)
        def _():
            pairs(hd, True)

        return carry

    lax.fori_loop(0, nhp // 2, head_body, 0)


def _peer_route(x, mod, g, wq, keys, *, n_ctx, dec_seq, tq):
    t, d = x.shape
    nhp, nk, dkh = keys.shape
    nh = nhp // 2
    const = lambda i: (0, 0)
    hshape = jax.ShapeDtypeStruct((nh, nk, t), F32)
    hspec = pl.BlockSpec((nh, nk, tq), lambda i: (0, 0, i))
    per = PEER_GROUP // nk
    gshape = jax.ShapeDtypeStruct((nh, nk // per, per, t), F32)
    gspec = pl.BlockSpec((nh, nk // per, per, tq), lambda i: (0, 0, 0, i))
    pad = V7X_SUBLANES // per
    est = (2 * tq * d * 4 + 4 * d * nhp * dkh + 2 * tq * d * 2 + (4 + 4 * pad) * nh * nk * tq * 4
           + 2 * nhp * nk * tq * 4 + 6 * tq * nhp * dkh * 4)
    return pl.pallas_call(
        _peer_route_kernel,
        grid=(t // tq,),
        in_specs=[pl.BlockSpec((tq, d), lambda i: (i, 0)),
                  pl.BlockSpec((None, 6, d), lambda i: (_cond_index(i * tq, n_ctx, dec_seq), 0, 0)),
                  pl.BlockSpec((1, d), const),
                  pl.BlockSpec(wq.shape, const),
                  pl.BlockSpec(keys.shape, lambda i: (0, 0, 0))],
        out_specs=[pl.BlockSpec((tq, d), lambda i: (i, 0)), gspec, gspec, hspec, hspec],
        out_shape=[jax.ShapeDtypeStruct((t, d), BF16), gshape, gshape, hshape, hshape],
        scratch_shapes=[pltpu.VMEM((nhp, tq, dkh), BF16),
                        pltpu.VMEM((nhp, nk, tq), F32),
                        pltpu.VMEM((nhp, PEER_TOPK, tq), F32)],
        compiler_params=_cparams(est, ("parallel",)),
        name="peer_route",
    )(x, mod, g, wq, keys)


def _peer_expert_kernel(hb_ref, u_ref, vt_ref, th_ref, e0_ref, s1_ref, e1_ref, x_ref, mod_ref, o_ref,
                        acc, st_a, st_b, hbuf_a, hbuf_b, *, nc):
    g = pl.program_id(0)
    nh = th_ref.shape[0]
    nk = s1_ref.shape[1]
    ngroups, per = th_ref.shape[1], th_ref.shape[2]
    gw = per * nk
    tq = st_a.shape[1]
    kh = PEER_ACC_ROWS // per
    c3 = jnp.maximum(g - 2, 0) % nc

    @pl.when(g == 0)
    def _():
        st_b[...] = jnp.zeros_like(st_b)
        hbuf_a[...] = jnp.zeros_like(hbuf_a)
        hbuf_b[...] = jnp.zeros_like(hbuf_b)

    @pl.when(c3 == 0)
    def _():
        acc[...] = jnp.zeros_like(acc)

    def step(st_w, st_r, hbuf_w, hbuf_r):
        def group(j, carry):
            rows = pl.ds(pl.multiple_of(j * gw, gw), gw)
            acc[...] += jnp.dot(vt_ref[j], hbuf_r[rows, :], preferred_element_type=F32)
            st_w[rows, :] = lax.dot_general(u_ref[rows, :], hb_ref[...], (((1,), (1,)), ((), ())),
                                            preferred_element_type=F32)
            for q in range(tq // V7X_LANES):
                lanes = slice(q * V7X_LANES, (q + 1) * V7X_LANES)
                for half in range(nk // kh):
                    keys = slice(half * kh, (half + 1) * kh)
                    w = [None] * per
                    for hd in range(nh):
                        s1 = s1_ref[hd, keys, lanes]
                        e1 = e1_ref[hd, keys, lanes]
                        for i in range(per):
                            sel = jnp.where(s1 >= th_ref[hd, j, i:i + 1, lanes], e1, 0.0) * e0_ref[hd, j, i:i + 1, lanes]
                            w[i] = sel if w[i] is None else w[i] + sel
                    for i in range(per):
                        tile = pl.ds(pl.multiple_of(j * gw + i * nk + half * kh, kh), kh)
                        hbuf_w[tile, lanes] = (_gelu(st_r[tile, lanes]) * w[i]).astype(BF16)
            return carry

        lax.fori_loop(0, ngroups, group, 0)

    @pl.when(g % 2 == 0)
    def _():
        step(st_a, st_b, hbuf_b, hbuf_a)

    @pl.when(g % 2 == 1)
    def _():
        step(st_b, st_a, hbuf_a, hbuf_b)

    @pl.when(jnp.logical_and(g >= 2, c3 == nc - 1))
    def _():
        o_ref[...] = x_ref[...] + mod_ref[5:6, :] * acc[...].T


def _peer_expert(hb, u_tab, vt_tab, th, e0, s1, e1, x, mod, *, n_ctx, dec_seq, tq, ec):
    t, d = x.shape
    ne = u_tab.shape[0]
    nh, nk, _ = s1.shape
    per = th.shape[2]
    assert per * nk == PEER_GROUP
    ngroups = ec // PEER_GROUP
    nc = ne // ec
    npairs = (t // tq) * nc
    vt_tab = vt_tab.reshape(d, ne // PEER_GROUP, PEER_GROUP).transpose(1, 0, 2)

    def pair(g, lag):
        p = jnp.clip(g - lag, 0, npairs - 1)
        return p // nc, p % nc

    def cond3(g):
        return _cond_index(pair(g, 2)[0] * tq, n_ctx, dec_seq)

    est = (2 * tq * d * 2 + 4 * ec * d * 2 + 4 * nh * ngroups * V7X_SUBLANES * tq * 4 + 4 * nh * nk * tq * 4
           + 4 * tq * d * 4 + d * tq * 4 + 2 * ec * tq * 2 + 2 * ec * tq * 4 + 3 * PEER_GROUP * tq * 4)
    return pl.pallas_call(
        functools.partial(_peer_expert_kernel, nc=nc),
        grid=(npairs + 2,),
        in_specs=[pl.BlockSpec((tq, d), lambda g: (pair(g, 0)[0], 0)),
                  pl.BlockSpec((ec, d), lambda g: (pair(g, 0)[1], 0)),
                  pl.BlockSpec((ngroups, d, PEER_GROUP), lambda g: (pair(g, 2)[1], 0, 0)),
                  pl.BlockSpec((nh, ngroups, per, tq), lambda g: (0, pair(g, 1)[1], 0, pair(g, 1)[0])),
                  pl.BlockSpec((nh, ngroups, per, tq), lambda g: (0, pair(g, 1)[1], 0, pair(g, 1)[0])),
                  pl.BlockSpec((nh, nk, tq), lambda g: (0, 0, pair(g, 1)[0])),
                  pl.BlockSpec((nh, nk, tq), lambda g: (0, 0, pair(g, 1)[0])),
                  pl.BlockSpec((tq, d), lambda g: (pair(g, 2)[0], 0)),
                  pl.BlockSpec((None, 6, d), lambda g: (cond3(g), 0, 0))],
        out_specs=pl.BlockSpec((tq, d), lambda g: (pair(g, 2)[0], 0)),
        out_shape=jax.ShapeDtypeStruct((t, d), F32),
        scratch_shapes=[pltpu.VMEM((d, tq), F32), pltpu.VMEM((ec, tq), F32), pltpu.VMEM((ec, tq), F32),
                        pltpu.VMEM((ec, tq), BF16), pltpu.VMEM((ec, tq), BF16)],
        compiler_params=_cparams(est, ("arbitrary",)),
        name="peer_expert",
    )(hb, u_tab, vt_tab, th, e0, s1, e1, x, mod)


def _final_kernel(x_ref, g_ref, o_ref):
    x = x_ref[...]
    ms = jnp.mean(x * x, axis=-1, keepdims=True)
    o_ref[...] = (x * lax.rsqrt(ms + EPS)) * g_ref[...]


def _final_norm(x, g, *, tm):
    t, d = x.shape
    return pl.pallas_call(
        _final_kernel,
        grid=(t // tm,),
        in_specs=[pl.BlockSpec((tm, d), lambda i: (i, 0)), pl.BlockSpec((1, d), lambda i: (0, 0))],
        out_specs=pl.BlockSpec((tm, d), lambda i: (i, 0)),
        out_shape=jax.ShapeDtypeStruct((t, d), F32),
        compiler_params=_cparams(6 * tm * d * 4, ("parallel",)),
        name="final_norm",
    )(x, g)


def _block_diag(w):
    h, i, j = w.shape
    return jnp.einsum('hij,hg->higj', w, jnp.eye(h, dtype=w.dtype)).reshape(h * i, h * j)


def _s5_discretize(a_re, a_im, log_dt, b_re, b_im):
    dt = jnp.exp(log_dt)[:, None]
    mag = jnp.exp(a_re * dt)
    abar_r = mag * jnp.cos(a_im * dt)
    abar_i = mag * jnp.sin(a_im * dt)
    den = a_re * a_re + a_im * a_im
    num_r = abar_r - 1.0
    coef_r = ((num_r * a_re + abar_i * a_im) / den)[..., None]
    coef_i = ((abar_i * a_re - num_r * a_im) / den)[..., None]
    return abar_r, abar_i, coef_r * b_re - coef_i * b_im, coef_r * b_im + coef_i * b_re


def _s5_blocks(abar_r, abar_i, bbar_r, bbar_i, c_re, c_im):
    g, p, ch = bbar_r.shape
    gb = g // S5_COLBLK

    def vec(v):
        return v.reshape(S5_COLBLK, 1, gb * p)

    def bmat(b):
        bt = b.transpose(0, 2, 1).reshape(S5_COLBLK, gb, ch, p)
        return jnp.stack([_block_diag(bt[j]) for j in range(S5_COLBLK)])

    def cmat(cc):
        ct = cc.transpose(0, 2, 1).reshape(S5_COLBLK, gb, p, ch)
        return jnp.stack([_block_diag(ct[j]) for j in range(S5_COLBLK)])

    bblk = jnp.concatenate([bmat(bbar_r), bmat(bbar_i)], axis=2).astype(BF16)
    cblk = jnp.concatenate([cmat(c_re), cmat(-c_im)], axis=1).astype(BF16)
    return bblk, cblk, vec(abar_r), vec(abar_i)


def _state_to_blocked(re, im):
    n, g, p = re.shape
    gb = g // S5_COLBLK
    both = jnp.stack([re.reshape(n, S5_COLBLK, gb * p), im.reshape(n, S5_COLBLK, gb * p)], axis=2)
    return both.reshape(n, -1)


def _state_from_blocked(st, g, p):
    n = st.shape[0]
    both = st.reshape(n, S5_COLBLK, 2, (g // S5_COLBLK) * p)
    return both[:, :, 0].reshape(n, g, p), both[:, :, 1].reshape(n, g, p)


def _complex_power(ar, ai, n):
    for _ in range(int(math.log2(n))):
        ar, ai = ar * ar - ai * ai, 2.0 * ar * ai
    return ar, ai


def _mixer_ab(x, mod, p, le, h0, *, bc, lc, bs, ls, tm):
    n_ctx = bc * lc
    width = p['s5_d'].shape[1]
    g_cnt, p_cnt = p['s5_a_re'].shape[2:]
    w_gates = jnp.concatenate([_block_diag(p['lru_w_a'][le, 0]), _block_diag(p['lru_w_x'][le, 0]),
                               _block_diag(p['lru_w_a'][le, 1]), _block_diag(p['lru_w_x'][le, 1])],
                              axis=1).astype(BF16)
    b_gates = jnp.concatenate([p['lru_b_a'][le, 0], p['lru_b_x'][le, 0],
                               p['lru_b_a'][le, 1], p['lru_b_x'][le, 1]]).reshape(1, 4 * width)
    c8 = LRU_C * jax.nn.softplus(-p['lru_lambda'][le])
    s_arr, xg = _ab_in(x, mod, p['norm1_g'][2 * le].reshape(1, -1), p['w_in_ab'][le].astype(BF16),
                       p['lru_conv_w'][le], p['lru_conv_b'][le].reshape(1, width), w_gates, b_gates, c8,
                       n_ctx=n_ctx, ctx_row=lc, dec_seq=ls, tm=tm)

    cols = s_arr.shape[1]
    gc = bc // SCAN_SEQS
    lseg = ls // SAMPLE_SEGS
    s_ctx = (s_arr[:n_ctx].reshape(gc, SCAN_SEQS, lc, cols).transpose(0, 2, 1, 3)
             .reshape(n_ctx, cols))
    s_smp = (s_arr[n_ctx:].reshape(bs * SAMPLE_SEGS, lseg, cols).transpose(1, 0, 2)
             .reshape(bs * ls, cols))
    chunk_rows = min(1024, lc * SCAN_SEQS, lseg * SCAN_SEQS)

    outs = {}
    finals = {}
    for d in range(2):
        abar_r, abar_i, bbar_r, bbar_i = _s5_discretize(
            p['s5_a_re'][le, d], p['s5_a_im'][le, d], p['s5_log_dt'][le, d],
            p['s5_b_re'][le, d], p['s5_b_im'][le, d])
        bblk, cblk, ar, ai = _s5_blocks(abar_r, abar_i, bbar_r, bbar_i,
                                        p['s5_c_re'][le, d], p['s5_c_im'][le, d])
        nstate = 2 * g_cnt * p_cnt
        z5 = jnp.zeros((gc, SCAN_SEQS, nstate), F32)
        zl = jnp.zeros((gc, SCAN_SEQS, width), F32)
        y, hl, f5, fl, _ = _scan(s_ctx, bblk, cblk, ar, ai, z5, zl, direction=d, store=True,
                                 groups=gc, chunk_rows=chunk_rows)
        outs[('ctx', d)] = (y, hl)
        finals[d] = (f5.reshape(bc, nstate), fl.reshape(bc, width))
        z5 = jnp.zeros((1, SCAN_SEQS, nstate), F32)
        zl = jnp.zeros((1, SCAN_SEQS, width), F32)
        e5, el, ep = _scan(s_smp, bblk, cblk, ar, ai, z5, zl, direction=d, store=False,
                           groups=1, chunk_rows=chunk_rows)
        first = (SAMPLE_SEGS - 1) if d == 1 else 0
        h5 = _state_to_blocked(h0[0][:, d], h0[1][:, d])
        h5_rows = jnp.zeros((bs, SAMPLE_SEGS, nstate), F32).at[:, first].set(h5).reshape(SCAN_SEQS, nstate)
        hl_rows = jnp.zeros((bs, SAMPLE_SEGS, width), F32).at[:, first].set(h0[2][:, d]).reshape(SCAN_SEQS, width)
        arl, ail = _complex_power(ar, ai, lseg)
        i5, il = _chain(e5[0], el[0], ep[0], h5_rows, hl_rows, arl, ail, direction=d)
        y, hl, _, _, _ = _scan(s_smp, bblk, cblk, ar, ai, i5[None], il[None], direction=d, store=True,
                               groups=1, chunk_rows=chunk_rows)
        outs[('smp', d)] = (y, hl)

    def to_tokens(ctx_arr, smp_arr):
        c = ctx_arr.reshape(gc, lc, SCAN_SEQS, width).transpose(0, 2, 1, 3).reshape(n_ctx, width)
        s = smp_arr.reshape(lseg, bs * SAMPLE_SEGS, width).transpose(1, 0, 2).reshape(bs * ls, width)
        return jnp.concatenate([c, s], axis=0)

    yf = to_tokens(outs[('ctx', 0)][0], outs[('smp', 0)][0])
    yb = to_tokens(outs[('ctx', 1)][0], outs[('smp', 1)][0])
    hf = to_tokens(outs[('ctx', 0)][1], outs[('smp', 0)][1])
    hb = to_tokens(outs[('ctx', 1)][1], outs[('smp', 1)][1])
    x1 = _ab_out(x, mod, s_arr, xg, yf, yb, hf, hb, p['s5_d'][le].reshape(1, width),
                 p['s5_w_glu'][le].astype(BF16), p['s5_b_glu'][le].reshape(1, width),
                 p['w_out_ab'][le].astype(BF16), n_ctx=n_ctx, dec_seq=ls, tm=tm)
    st = [_state_from_blocked(finals[d][0], g_cnt, p_cnt) for d in range(2)]
    new_re = jnp.stack([st[0][0], st[1][0]], axis=1)
    new_im = jnp.stack([st[0][1], st[1][1]], axis=1)
    new_lru = jnp.stack([finals[0][1], finals[1][1]], axis=1)
    return x1, (new_re, new_im, new_lru)


def _fft_split(length):
    n = 2 * length
    n1 = 2
    while n // n1 > 1024:
        n1 *= 2
    return n1


def _mixer_c(x, mod, p, lo, *, bc, lc, bs, ls, tm):
    n_ctx = bc * lc
    width = p['hy_bias'].shape[1]
    x0, vg = _c_in(x, mod, p['norm1_g'][2 * lo + 1].reshape(1, -1), p['w_in_c'][lo].astype(BF16),
                   p['b_in_c'][lo].reshape(1, -1), p['hy_conv_w'][lo], p['hy_conv_b'][lo].reshape(1, -1),
                   n_ctx=n_ctx, ctx_row=lc, dec_seq=ls, tm=tm)
    ys = []
    for (b, l, v) in ((bc, lc, vg[:n_ctx]), (bs, ls, vg[n_ctx:])):
        circ = _hy_filter(l, p['hy_w1'][lo], p['hy_b1'][lo], p['hy_freq1'][lo], p['hy_w2'][lo],
                          p['hy_b2'][lo], p['hy_freq2'][lo], p['hy_w3'][lo], p['hy_decay'][lo],
                          ct=min(width, 256 if l <= 1024 else 128))
        y = _fftconv(v.reshape(b, l, width), circ, n1=_fft_split(l), ct=128)
        ys.append(y.reshape(b * l, width))
    yconv = jnp.concatenate(ys, axis=0)
    return _c_out(x, mod, x0, vg, yconv, p['hy_bias'][lo].reshape(1, width), p['w_out_c'][lo].astype(BF16),
                  p['b_out_c'][lo].reshape(1, -1), n_ctx=n_ctx, dec_seq=ls, tm=tm)


def _peer(x, mod, p, l, *, n_ctx, dec_seq, tq, ec):
    keys = p['peer_keys'][l]
    nh, _, nk, dkh = keys.shape
    hb, th, e0, s1, e1 = _peer_route(x, mod, p['norm2_g'][l].reshape(1, -1), p['peer_wq'][l].astype(BF16),
                                     keys.reshape(nh * 2, nk, dkh).astype(BF16),
                                     n_ctx=n_ctx, dec_seq=dec_seq, tq=tq)
    return _peer_expert(hb, p['peer_u'][l].astype(BF16), p['peer_v'][l].T.astype(BF16), th, e0, s1, e1,
                        x, mod, n_ctx=n_ctx, dec_seq=dec_seq, tq=tq, ec=ec)


def _forward(x_prompt, x_sample, state_s5_re, state_s5_im, state_lru, c, c_ctx, p, *, tm, tq, ec):
    bc, lc, d = x_prompt.shape
    bs, ls, _ = x_sample.shape
    assert bc % SCAN_SEQS == 0 and bs * SAMPLE_SEGS == SCAN_SEQS
    assert tm % lc == 0 and ls % tm == 0 and (ls // SAMPLE_SEGS) % GRID_W == 0
    n_ctx = bc * lc
    x = jnp.concatenate([x_prompt.reshape(n_ctx, d), x_sample.reshape(bs * ls, d)], axis=0)
    cond = jnp.zeros((V7X_SUBLANES, d), F32).at[0].set(c_ctx).at[1:1 + bs].set(c)
    mod_all = _modulation(cond, p['w_mod'], p['b_mod'])
    states = None
    for l in range(DEPTH):
        mod = mod_all[l].reshape(V7X_SUBLANES, 6, d)
        if l % 2 == 0:
            le = l // 2
            h0 = (state_s5_re[:, le], state_s5_im[:, le], state_lru[:, le])
            x, st = _mixer_ab(x, mod, p, le, h0, bc=bc, lc=lc, bs=bs, ls=ls, tm=tm)
            states = st if states is None else states
        else:
            x = _mixer_c(x, mod, p, l // 2, bc=bc, lc=lc, bs=bs, ls=ls, tm=tm)
        x = _peer(x, mod, p, l, n_ctx=n_ctx, dec_seq=ls, tq=tq, ec=ec)
    y = _final_norm(x, p['final_g'].reshape(1, d), tm=tm)
    new_re, new_im, new_lru = states
    return (y[:n_ctx].reshape(bc, lc, d), y[n_ctx:].reshape(bs, ls, d),
            new_re[:, None], new_im[:, None], new_lru[:, None])


def kernel(x_prompt, x_sample, state_s5_re, state_s5_im, state_lru, c, c_ctx, norm1_g, norm2_g, w_mod, b_mod, w_in_ab, s5_a_re, s5_a_im, s5_log_dt, s5_b_re, s5_b_im, s5_c_re, s5_c_im, s5_d, s5_w_glu, s5_b_glu, lru_conv_w, lru_conv_b, lru_w_a, lru_b_a, lru_w_x, lru_b_x, lru_lambda, w_out_ab, w_in_c, b_in_c, hy_conv_w, hy_conv_b, hy_w1, hy_b1, hy_freq1, hy_w2, hy_b2, hy_freq2, hy_w3, hy_decay, hy_bias, w_out_c, b_out_c, peer_wq, peer_keys, peer_u, peer_v, final_g):
    p = dict(norm1_g=norm1_g, norm2_g=norm2_g, w_mod=w_mod, b_mod=b_mod, w_in_ab=w_in_ab,
             s5_a_re=s5_a_re, s5_a_im=s5_a_im, s5_log_dt=s5_log_dt, s5_b_re=s5_b_re, s5_b_im=s5_b_im,
             s5_c_re=s5_c_re, s5_c_im=s5_c_im, s5_d=s5_d, s5_w_glu=s5_w_glu, s5_b_glu=s5_b_glu,
             lru_conv_w=lru_conv_w, lru_conv_b=lru_conv_b, lru_w_a=lru_w_a, lru_b_a=lru_b_a,
             lru_w_x=lru_w_x, lru_b_x=lru_b_x, lru_lambda=lru_lambda, w_out_ab=w_out_ab,
             w_in_c=w_in_c, b_in_c=b_in_c, hy_conv_w=hy_conv_w, hy_conv_b=hy_conv_b,
             hy_w1=hy_w1, hy_b1=hy_b1, hy_freq1=hy_freq1, hy_w2=hy_w2, hy_b2=hy_b2, hy_freq2=hy_freq2,
             hy_w3=hy_w3, hy_decay=hy_decay, hy_bias=hy_bias, w_out_c=w_out_c, b_out_c=b_out_c,
             peer_wq=peer_wq, peer_keys=peer_keys, peer_u=peer_u, peer_v=peer_v, final_g=final_g)
    return _forward(x_prompt, x_sample, state_s5_re, state_s5_im, state_lru, c, c_ctx, p,
                    tm=512, tq=512, ec=1024)
```

```python
import functools
import math

import jax
import jax.numpy as jnp
import numpy as np
from jax import lax
from jax.experimental import pallas as pl
from jax.experimental.pallas import tpu as pltpu

F32 = jnp.float32
BF16 = jnp.bfloat16
HIGHEST = lax.Precision.HIGHEST

DEPTH = 2
GRID_W = 64
EPS = 1e-6
S5_CH = 16
S5_STATE = 64
LRU_HEADS = 8
LRU_C = 8.0
HY_BANDS = 16
HY_SHIFT = 0.05
PEER_HEADS = 8
PEER_NKEYS = 128
PEER_TOPK = 16
PEER_GROUP = 512
PEER_ACC_ROWS = 256

V7X_LANES = 128
V7X_SUBLANES = 8
V7X_VMEM_BYTES = 64 * 1024 * 1024

SCAN_SEQS = V7X_SUBLANES
SAMPLE_SEGS = 4
S5_COLBLK = 4


def _vmem_limit(est_bytes):
    return int(min(V7X_VMEM_BYTES - (8 << 20), max(32 << 20, est_bytes * 3 // 2)))


def _cparams(est_bytes, sem):
    return pltpu.CompilerParams(dimension_semantics=sem, vmem_limit_bytes=_vmem_limit(est_bytes))


def _gelu(x):
    k = 2.0 * 0.7978845608028654 * math.log2(math.e)
    return x / (1.0 + jnp.exp2(x * (-k - (k * 0.044715) * (x * x))))


def _norm_mod(x, g, sc, sh):
    ms = jnp.mean(x * x, axis=-1, keepdims=True)
    return (x * lax.rsqrt(ms + EPS)) * g * (1.0 + sc) + sh


def _cond_index(row0, n_ctx, dec_seq):
    return jnp.where(row0 < n_ctx, 0, 1 + (row0 - n_ctx) // dec_seq)


def _mod_kernel(c_ref, w_ref, b_ref, o_ref):
    c = c_ref[...]
    s = c * jax.nn.sigmoid(c)
    o_ref[...] = jnp.dot(s, w_ref[...], precision=HIGHEST, preferred_element_type=F32) + b_ref[...]


def _modulation(cond, w_mod, b_mod):
    depth, d, n = w_mod.shape
    rows = cond.shape[0]
    tn = 1536
    est = 2 * d * tn * 4 + 4 * rows * (d + tn) * 4
    return pl.pallas_call(
        _mod_kernel,
        grid=(depth, n // tn),
        in_specs=[pl.BlockSpec((rows, d), lambda l, j: (0, 0)),
                  pl.BlockSpec((None, d, tn), lambda l, j: (l, 0, j)),
                  pl.BlockSpec((None, 1, tn), lambda l, j: (l, 0, j))],
        out_specs=pl.BlockSpec((None, rows, tn), lambda l, j: (l, 0, j)),
        out_shape=jax.ShapeDtypeStruct((depth, rows, n), F32),
        compiler_params=_cparams(est, ("arbitrary", "arbitrary")),
        name="modulation",
    )(cond, w_mod, b_mod.reshape(depth, 1, n))


def _dwconv_rows(x, w, bias, pos, row_len, pad_left):
    tm = x.shape[0]
    acc = None
    for k in range(w.shape[0]):
        d = k - pad_left
        if d == 0:
            term = x
        else:
            shifted = pltpu.roll(x, (-d) % tm, axis=0)
            q = pos + d
            term = jnp.where(q >= 0, jnp.where(q < row_len, shifted, 0.0), 0.0)
        term = term * w[k:k + 1]
        acc = term if acc is None else acc + term
    return acc + bias


def _ab_in_kernel(x_ref, mod_ref, g_ref, w_ref, cw_ref, cb_ref, wg_ref, bg_ref, c8_ref,
                  s_ref, xg_ref, *, n_ctx, ctx_row, width):
    tm = x_ref.shape[0]
    m = mod_ref[...]
    h = _norm_mod(x_ref[...], g_ref[...], m[1:2], m[0:1])
    z = jnp.dot(h.astype(BF16), w_ref[...], preferred_element_type=F32)
    u = z[:, :width]
    xr = z[:, width:2 * width]
    xg_ref[...] = z[:, 2 * width:]
    row_len = jnp.where(pl.program_id(0) * tm < n_ctx, ctx_row, GRID_W)
    pos = lax.broadcasted_iota(jnp.int32, (tm, width), 0) & (row_len - 1)
    xb = _dwconv_rows(xr, cw_ref[...], cb_ref[...], pos, row_len, 2)
    gates = jnp.dot(xb.astype(BF16), wg_ref[...], preferred_element_type=F32) + bg_ref[...]
    s_ref[:, :width] = u
    c8 = c8_ref[...]
    for d in range(2):
        r = jax.nn.sigmoid(gates[:, (2 * d) * width:(2 * d + 1) * width])
        i = jax.nn.sigmoid(gates[:, (2 * d + 1) * width:(2 * d + 2) * width])
        a = jnp.exp(-(c8[d:d + 1] * r))
        b = jnp.sqrt(1.0 - a * a) * (i * xb)
        s_ref[:, (1 + 2 * d) * width:(2 + 2 * d) * width] = a
        s_ref[:, (2 + 2 * d) * width:(3 + 2 * d) * width] = b


def _ab_in(x, mod, g, w_in, conv_w, conv_b, w_gates, b_gates, c8, *, n_ctx, ctx_row, dec_seq, tm):
    t, d = x.shape
    width = w_in.shape[1] // 3
    kern = functools.partial(_ab_in_kernel, n_ctx=n_ctx, ctx_row=ctx_row, width=width)
    const = lambda i: (0, 0)
    est = (2 * tm * d * 4 + d * 3 * width * 2 * 2 + width * 4 * width * 2 * 2
           + 2 * tm * 6 * width * 4 + tm * 12 * width * 4)
    return pl.pallas_call(
        kern,
        grid=(t // tm,),
        in_specs=[pl.BlockSpec((tm, d), lambda i: (i, 0)),
                  pl.BlockSpec((None, 6, d), lambda i: (_cond_index(i * tm, n_ctx, dec_seq), 0, 0)),
                  pl.BlockSpec((1, d), const),
                  pl.BlockSpec(w_in.shape, const),
                  pl.BlockSpec(conv_w.shape, const),
                  pl.BlockSpec((1, width), const),
                  pl.BlockSpec(w_gates.shape, const),
                  pl.BlockSpec((1, 4 * width), const),
                  pl.BlockSpec((2, width), const)],
        out_specs=[pl.BlockSpec((tm, 5 * width), lambda i: (i, 0)),
                   pl.BlockSpec((tm, width), lambda i: (i, 0))],
        out_shape=[jax.ShapeDtypeStruct((t, 5 * width), F32),
                   jax.ShapeDtypeStruct((t, width), F32)],
        compiler_params=_cparams(est, ("parallel",)),
        name="ab_in",
    )(x, mod, g, w_in, conv_w, conv_b, w_gates, b_gates, c8)


def _scan_kernel(u_ref, a_ref, b_ref, bblk_ref, cblk_ref, ar_ref, ai_ref, i5_ref, il_ref, *rest,
                 reverse, store):
    if store:
        y_ref, hl_ref, f5_ref, fl_ref, fp_ref, buf, st5, stl, stp = rest
    else:
        f5_ref, fl_ref, fp_ref, buf, st5, stl, stp = rest
    c = pl.program_id(1)
    rows = u_ref.shape[0]
    nsteps = rows // SCAN_SEQS
    sw = st5.shape[1] // (2 * S5_COLBLK)
    cw = u_ref.shape[1] // S5_COLBLK

    @pl.when(c == 0)
    def _():
        st5[...] = i5_ref[...]
        stl[...] = il_ref[...]
        stp[...] = jnp.ones_like(stp)

    def row0(k):
        t = (nsteps - 1 - k) if reverse else k
        return pl.multiple_of(t * SCAN_SEQS, SCAN_SEQS)

    u = u_ref[...].astype(BF16)
    for j in range(S5_COLBLK):
        buf[...] = jnp.dot(u[:, j * cw:(j + 1) * cw], bblk_ref[j], preferred_element_type=F32)
        arj = jnp.broadcast_to(ar_ref[j], (SCAN_SEQS, sw))
        aij = jnp.broadcast_to(ai_ref[j], (SCAN_SEQS, sw))

        def step(k, carry, arj=arj, aij=aij):
            hr, hi = carry
            r0 = row0(k)
            xr = buf[pl.ds(r0, SCAN_SEQS), 0:sw]
            xi = buf[pl.ds(r0, SCAN_SEQS), sw:2 * sw]
            nr = arj * hr - aij * hi + xr
            ni = arj * hi + aij * hr + xi
            buf[pl.ds(r0, SCAN_SEQS), 0:sw] = nr
            buf[pl.ds(r0, SCAN_SEQS), sw:2 * sw] = ni
            return nr, ni

        base = 2 * sw * j
        hr, hi = lax.fori_loop(0, nsteps, step, (st5[:, base:base + sw], st5[:, base + sw:base + 2 * sw]))
        st5[:, base:base + sw] = hr
        st5[:, base + sw:base + 2 * sw] = hi
        if store:
            y_ref[:, j * cw:(j + 1) * cw] = jnp.dot(buf[...].astype(BF16), cblk_ref[j],
                                                    preferred_element_type=F32)

    def lstep(k, carry):
        h, p = carry
        r0 = row0(k)
        a = a_ref[pl.ds(r0, SCAN_SEQS), :]
        h = a * h + b_ref[pl.ds(r0, SCAN_SEQS), :]
        if store:
            hl_ref[pl.ds(r0, SCAN_SEQS), :] = h
        return h, p * a

    h, p = lax.fori_loop(0, nsteps, lstep, (stl[...], stp[...]))
    stl[...] = h
    stp[...] = p

    @pl.when(c == pl.num_programs(1) - 1)
    def _():
        f5_ref[...] = st5[...]
        fl_ref[...] = stl[...]
        fp_ref[...] = stp[...]


def _scan(s_arr, bblk, cblk, ar, ai, init5, initl, *, direction, store, groups, chunk_rows):
    rows_total, cols = s_arr.shape
    width = cols // 5
    rows_per_group = rows_total // groups
    nc = rows_per_group // chunk_rows
    reverse = direction == 1
    nstate = init5.shape[-1]

    def chunk(g, c):
        return g * nc + ((nc - 1 - c) if reverse else c)

    kern = functools.partial(_scan_kernel, reverse=reverse, store=store)
    const3 = lambda g, c: (0, 0, 0)
    in_specs = [pl.BlockSpec((chunk_rows, width), lambda g, c: (chunk(g, c), 0)),
                pl.BlockSpec((chunk_rows, width), lambda g, c: (chunk(g, c), 1 + 2 * direction)),
                pl.BlockSpec((chunk_rows, width), lambda g, c: (chunk(g, c), 2 + 2 * direction)),
                pl.BlockSpec(bblk.shape, const3),
                pl.BlockSpec(cblk.shape, const3),
                pl.BlockSpec(ar.shape, const3),
                pl.BlockSpec(ai.shape, const3),
                pl.BlockSpec((None, SCAN_SEQS, nstate), lambda g, c: (g, 0, 0)),
                pl.BlockSpec((None, SCAN_SEQS, width), lambda g, c: (g, 0, 0))]
    fin_specs = [pl.BlockSpec((None, SCAN_SEQS, nstate), lambda g, c: (g, 0, 0)),
                 pl.BlockSpec((None, SCAN_SEQS, width), lambda g, c: (g, 0, 0)),
                 pl.BlockSpec((None, SCAN_SEQS, width), lambda g, c: (g, 0, 0))]
    fin_shapes = [jax.ShapeDtypeStruct((groups, SCAN_SEQS, nstate), F32),
                  jax.ShapeDtypeStruct((groups, SCAN_SEQS, width), F32),
                  jax.ShapeDtypeStruct((groups, SCAN_SEQS, width), F32)]
    out_specs, out_shapes = fin_specs, fin_shapes
    if store:
        out_specs = [pl.BlockSpec((chunk_rows, width), lambda g, c: (chunk(g, c), 0))] * 2 + fin_specs
        out_shapes = [jax.ShapeDtypeStruct((rows_total, width), F32)] * 2 + fin_shapes
    bufw = 2 * nstate // (2 * S5_COLBLK)
    est = (10 * chunk_rows * width * 4 + chunk_rows * bufw * 4 * 2
           + 4 * bblk.size * 2 + 8 * SCAN_SEQS * nstate * 4)
    return pl.pallas_call(
        kern,
        grid=(groups, nc),
        in_specs=in_specs,
        out_specs=out_specs,
        out_shape=out_shapes,
        scratch_shapes=[pltpu.VMEM((chunk_rows, bufw), F32),
                        pltpu.VMEM((SCAN_SEQS, nstate), F32),
                        pltpu.VMEM((SCAN_SEQS, width), F32),
                        pltpu.VMEM((SCAN_SEQS, width), F32)],
        compiler_params=_cparams(est, ("arbitrary", "arbitrary")),
        name="scan_store" if store else "scan_ends",
    )(s_arr, s_arr, s_arr, bblk, cblk, ar, ai, init5, initl)


def _chain_kernel(e5_ref, el_ref, ep_ref, h5_ref, hl_ref, arl_ref, ail_ref, i5_ref, il_ref, *, reverse):
    nseg = SAMPLE_SEGS
    sw = arl_ref.shape[-1]
    shift = (SCAN_SEQS - 1) if reverse else 1
    order = range(nseg - 2, -1, -1) if reverse else range(1, nseg)
    seg5 = lax.broadcasted_iota(jnp.int32, (SCAN_SEQS, sw), 0) % nseg
    for j in range(S5_COLBLK):
        base = 2 * sw * j
        re, im = slice(base, base + sw), slice(base + sw, base + 2 * sw)
        ar, ai = arl_ref[j], ail_ref[j]
        er = pltpu.roll(e5_ref[:, re], shift, axis=0)
        ei = pltpu.roll(e5_ref[:, im], shift, axis=0)
        hr, hi = h5_ref[:, re], h5_ref[:, im]
        for k in order:
            pr = pltpu.roll(hr, shift, axis=0)
            pi = pltpu.roll(hi, shift, axis=0)
            hr = jnp.where(seg5 == k, ar * pr - ai * pi + er, hr)
            hi = jnp.where(seg5 == k, ar * pi + ai * pr + ei, hi)
        i5_ref[:, re] = hr
        i5_ref[:, im] = hi
    segl = lax.broadcasted_iota(jnp.int32, el_ref.shape, 0) % nseg
    el = pltpu.roll(el_ref[...], shift, axis=0)
    ep = pltpu.roll(ep_ref[...], shift, axis=0)
    h = hl_ref[...]
    for k in order:
        h = jnp.where(segl == k, ep * pltpu.roll(h, shift, axis=0) + el, h)
    il_ref[...] = h


def _chain(end5, endl, endp, h5_rows, hl_rows, arl, ail, *, direction):
    kern = functools.partial(_chain_kernel, reverse=direction == 1)
    return pl.pallas_call(
        kern,
        out_shape=[jax.ShapeDtypeStruct(end5.shape, F32), jax.ShapeDtypeStruct(endl.shape, F32)],
        name="scan_chain",
    )(end5, endl, endp, h5_rows, hl_rows, arl, ail)


def _ab_out_kernel(x_ref, mod_ref, s_ref, xg_ref, yf_ref, yb_ref, hf_ref, hb_ref, d_ref,
                   wglu_ref, bglu_ref, wout_ref, o_ref):
    width = xg_ref.shape[1]
    m = mod_ref[...]
    y = s_ref[...] * d_ref[...] + yf_ref[...] + yb_ref[...]
    zs = _gelu(y)
    gate = jnp.dot(zs.astype(BF16), wglu_ref[...], preferred_element_type=F32) + bglu_ref[...]
    s5_out = zs * jax.nn.sigmoid(gate)
    lru_out = (hf_ref[...] + hb_ref[...]) * _gelu(xg_ref[...])
    out = (jnp.dot(s5_out.astype(BF16), wout_ref[:width, :], preferred_element_type=F32)
           + jnp.dot(lru_out.astype(BF16), wout_ref[width:, :], preferred_element_type=F32))
    o_ref[...] = x_ref[...] + m[2:3] * out


def _ab_out(x, mod, s_arr, xg, yf, yb, hf, hb, s5_d, w_glu, b_glu, w_out, *, n_ctx, dec_seq, tm):
    t, d = x.shape
    width = xg.shape[1]
    const = lambda i: (0, 0)
    row = lambda i: (i, 0)
    est = 4 * tm * d * 4 + 12 * tm * width * 4 + 4 * (width * width + 2 * width * d) + 8 * tm * width * 4
    return pl.pallas_call(
        _ab_out_kernel,
        grid=(t // tm,),
        in_specs=[pl.BlockSpec((tm, d), row),
                  pl.BlockSpec((None, 6, d), lambda i: (_cond_index(i * tm, n_ctx, dec_seq), 0, 0)),
                  pl.BlockSpec((tm, width), row),
                  pl.BlockSpec((tm, width), row),
                  pl.BlockSpec((tm, width), row),
                  pl.BlockSpec((tm, width), row),
                  pl.BlockSpec((tm, width), row),
                  pl.BlockSpec((tm, width), row),
                  pl.BlockSpec((1, width), const),
                  pl.BlockSpec(w_glu.shape, const),
                  pl.BlockSpec((1, width), const),
                  pl.BlockSpec(w_out.shape, const)],
        out_specs=pl.BlockSpec((tm, d), row),
        out_shape=jax.ShapeDtypeStruct((t, d), F32),
        compiler_params=_cparams(est, ("parallel",)),
        name="ab_out",
    )(x, mod, s_arr, xg, yf, yb, hf, hb, s5_d, w_glu, b_glu, w_out)


def _c_in_kernel(x_ref, mod_ref, g_ref, w_ref, b_ref, cw_ref, cb_ref, x0_ref, vg_ref, *, n_ctx, ctx_row):
    tm = x_ref.shape[0]
    width = x0_ref.shape[1]
    m = mod_ref[...]
    h = _norm_mod(x_ref[...], g_ref[...], m[1:2], m[0:1])
    z = jnp.dot(h.astype(BF16), w_ref[...], preferred_element_type=F32) + b_ref[...]
    row_len = jnp.where(pl.program_id(0) * tm < n_ctx, ctx_row, GRID_W)
    pos = lax.broadcasted_iota(jnp.int32, (tm, width), 0) & (row_len - 1)
    cw = cw_ref[...]
    cb = cb_ref[...]
    parts = [_dwconv_rows(z[:, k * width:(k + 1) * width], cw[:, k * width:(k + 1) * width],
                          cb[:, k * width:(k + 1) * width], pos, row_len, 1) for k in range(3)]
    x0_ref[...] = parts[0]
    vg_ref[...] = parts[2] * parts[1]


def _c_in(x, mod, g, w_in, b_in, conv_w, conv_b, *, n_ctx, ctx_row, dec_seq, tm):
    t, d = x.shape
    width = w_in.shape[1] // 3
    kern = functools.partial(_c_in_kernel, n_ctx=n_ctx, ctx_row=ctx_row)
    const = lambda i: (0, 0)
    est = 2 * tm * d * 4 + 4 * d * 3 * width + 4 * tm * width * 4 + 6 * tm * 3 * width * 4
    return pl.pallas_call(
        kern,
        grid=(t // tm,),
        in_specs=[pl.BlockSpec((tm, d), lambda i: (i, 0)),
                  pl.BlockSpec((None, 6, d), lambda i: (_cond_index(i * tm, n_ctx, dec_seq), 0, 0)),
                  pl.BlockSpec((1, d), const),
                  pl.BlockSpec(w_in.shape, const),
                  pl.BlockSpec((1, 3 * width), const),
                  pl.BlockSpec(conv_w.shape, const),
                  pl.BlockSpec((1, 3 * width), const)],
        out_specs=[pl.BlockSpec((tm, width), lambda i: (i, 0))] * 2,
        out_shape=[jax.ShapeDtypeStruct((t, width), F32)] * 2,
        compiler_params=_cparams(est, ("parallel",)),
        name="c_in",
    )(x, mod, g, w_in, b_in, conv_w, conv_b)


def _c_out_kernel(x_ref, mod_ref, x0_ref, vg_ref, yc_ref, hb_ref, w_ref, b_ref, o_ref):
    m = mod_ref[...]
    vg = vg_ref[...]
    y = (yc_ref[...] + hb_ref[...] * vg) * x0_ref[...]
    out = jnp.dot(y.astype(BF16), w_ref[...], preferred_element_type=F32) + b_ref[...]
    o_ref[...] = x_ref[...] + m[2:3] * out


def _c_out(x, mod, x0, vg, yconv, hy_bias, w_out, b_out, *, n_ctx, dec_seq, tm):
    t, d = x.shape
    width = x0.shape[1]
    const = lambda i: (0, 0)
    row = lambda i: (i, 0)
    est = 4 * tm * d * 4 + 6 * tm * width * 4 + 4 * width * d + 4 * tm * width * 4
    return pl.pallas_call(
        _c_out_kernel,
        grid=(t // tm,),
        in_specs=[pl.BlockSpec((tm, d), row),
                  pl.BlockSpec((None, 6, d), lambda i: (_cond_index(i * tm, n_ctx, dec_seq), 0, 0)),
                  pl.BlockSpec((tm, width), row),
                  pl.BlockSpec((tm, width), row),
                  pl.BlockSpec((tm, width), row),
                  pl.BlockSpec((1, width), const),
                  pl.BlockSpec(w_out.shape, const),
                  pl.BlockSpec((1, d), const)],
        out_specs=pl.BlockSpec((tm, d), row),
        out_shape=jax.ShapeDtypeStruct((t, d), F32),
        compiler_params=_cparams(est, ("parallel",)),
        name="c_out",
    )(x, mod, x0, vg, yconv, hy_bias, w_out, b_out)


def _hy_filter_kernel(w1_ref, b1_ref, f1_ref, w2_ref, b2_ref, f2_ref, w3_ref, dec_ref, o_ref, z_scr, *, length):
    ct = o_ref.shape[1]
    nb = HY_BANDS
    ecols = w1_ref.shape[0]
    row = lax.broadcasted_iota(jnp.int32, (length, 1), 0)
    positions = (row, length - row)

    @pl.when(pl.program_id(0) == 0)
    def _():
        for side in range(2):
            posf = positions[side].astype(F32)
            col = lax.broadcasted_iota(jnp.int32, (length, ecols), 1)
            band_idx = jnp.where(col <= nb, col - 1, col - 1 - nb).astype(F32)
            bands = 1e-4 + band_idx * ((nb - 1 - 1e-4) / (nb - 1))
            ang = (2.0 * math.pi / length) * posf * bands
            emb = jnp.where(col == 0, posf / length,
                            jnp.where(col <= nb, jnp.cos(ang),
                                      jnp.where(col <= 2 * nb, -jnp.sin(ang), 0.0)))
            z = jnp.sin(f1_ref[...] * (jnp.dot(emb, w1_ref[...], precision=HIGHEST,
                                               preferred_element_type=F32) + b1_ref[...]))
            z_scr[side] = jnp.sin(f2_ref[...] * (jnp.dot(z, w2_ref[...], precision=HIGHEST,
                                                         preferred_element_type=F32) + b2_ref[...]))

    def half(side):
        t = positions[side].astype(F32) / length
        filt = jnp.dot(z_scr[side], w3_ref[side], precision=HIGHEST, preferred_element_type=F32)
        return filt * (jnp.exp(-t * jnp.abs(dec_ref[side:side + 1, :])) + HY_SHIFT)

    fwd = half(0)
    bwd = jnp.where(lax.broadcasted_iota(jnp.int32, (length, ct), 0) == 0, 0.0, half(1))
    ss = jnp.sum(fwd * fwd, axis=0, keepdims=True) + jnp.sum(bwd * bwd, axis=0, keepdims=True)
    scale = lax.rsqrt(ss + EPS)
    o_ref[:length, :] = fwd * scale
    o_ref[length:, :] = bwd * scale


def _hy_filter(length, w1, b1, f1, w2, b2, f2, w3, decay, *, ct):
    emb, ff = w1.shape
    width = decay.shape[1]
    ecols = 128
    w1p = jnp.zeros((ecols, ff), F32).at[:emb].set(w1)
    w3s = w3.reshape(ff, 2, width).transpose(1, 0, 2)
    kern = functools.partial(_hy_filter_kernel, length=length)
    const = lambda j: (0, 0)
    est = 10 * length * ct * 4 + 4 * length * ecols * 4
    return pl.pallas_call(
        kern,
        grid=(width // ct,),
        in_specs=[pl.BlockSpec((ecols, ff), const),
                  pl.BlockSpec((1, ff), const),
                  pl.BlockSpec((1, ff), const),
                  pl.BlockSpec((ff, ff), const),
                  pl.BlockSpec((1, ff), const),
                  pl.BlockSpec((1, ff), const),
                  pl.BlockSpec((2, ff, ct), lambda j: (0, 0, j)),
                  pl.BlockSpec((2, ct), lambda j: (0, j))],
        out_specs=pl.BlockSpec((2 * length, ct), lambda j: (0, j)),
        out_shape=jax.ShapeDtypeStruct((2 * length, width), F32),
        scratch_shapes=[pltpu.VMEM((2, length, ff), F32)],
        compiler_params=_cparams(est, ("arbitrary",)),
        name="hy_filter",
    )(w1p, b1.reshape(1, ff), f1.reshape(1, ff), w2, b2.reshape(1, ff), f2.reshape(1, ff), w3s, decay)


def _fftconv_kernel(ct_ref, st_ref, x_ref, k_ref, c_ref, s_ref, twr_ref, twi_ref, o_ref, *scratch, n1):
    kfr, kfi = scratch if scratch else (None, None)
    h1 = n1 // 2
    n2 = c_ref.shape[0]
    n = n1 * n2
    cm = c_ref[...]
    sm = s_ref[...]

    def dft(pairs, inverse):
        w = pairs[0][0].shape[1]
        parts = jnp.concatenate([p.astype(BF16) for pair in pairs for p in pair], axis=1)
        cp = jnp.dot(cm, parts, preferred_element_type=F32)
        sp = jnp.dot(sm, parts, preferred_element_type=F32)
        out = []
        for i in range(len(pairs)):
            ca, cb = cp[:, 2 * i * w:(2 * i + 1) * w], cp[:, (2 * i + 1) * w:(2 * i + 2) * w]
            sa, sb = sp[:, 2 * i * w:(2 * i + 1) * w], sp[:, (2 * i + 1) * w:(2 * i + 2) * w]
            out.append((ca - sb, cb + sa) if inverse else (ca + sb, cb - sa))
        return out

    def slab(t1):
        return slice(t1 * n2, (t1 + 1) * n2)

    twr1, twi1 = twr_ref[...], twi_ref[...]
    tw0 = (jnp.ones_like(twr1), jnp.zeros_like(twi1))

    def next_twiddle(tw):
        return tw[0] * twr1 - tw[1] * twi1, tw[0] * twi1 + tw[1] * twr1

    def filter_stage1(k1, tw):
        ar = ai = None
        for t1 in range(n1):
            kr = k_ref[slab(t1), :]
            tr, ti = ct_ref[k1, t1] * kr, -(st_ref[k1, t1] * kr)
            ar = tr if ar is None else ar + tr
            ai = ti if ai is None else ai + ti
        ar, ai = ar * (1.0 / n), ai * (1.0 / n)
        return ar * tw[0] + ai * tw[1], ai * tw[0] - ar * tw[1]

    if kfr is not None:
        @pl.when(pl.program_id(1) == 0)
        def _():
            def fbody(k1, tw):
                (kfr[k1], kfi[k1]), = dft([filter_stage1(k1, tw)], False)
                return next_twiddle(tw)
            lax.fori_loop(0, n1, fbody, tw0)

    o_ref[...] = jnp.zeros_like(o_ref)

    def body(k1, tw):
        twr, twi = tw
        ar = ai = None
        for t1 in range(h1):
            cc, ss = ct_ref[k1, t1], st_ref[k1, t1]
            xr, xi = x_ref[0, slab(t1), :], x_ref[1, slab(t1), :]
            tr, ti = cc * xr + ss * xi, cc * xi - ss * xr
            ar = tr if ar is None else ar + tr
            ai = ti if ai is None else ai + ti
        data = (ar * twr + ai * twi, ai * twr - ar * twi)
        if kfr is not None:
            (fr, fi), = dft([data], False)
            gr, gi = kfr[k1], kfi[k1]
        else:
            (fr, fi), (gr, gi) = dft([data, filter_stage1(k1, tw)], False)
        (br, bi), = dft([(fr * gr - fi * gi, fr * gi + fi * gr)], True)
        br, bi = br * twr - bi * twi, bi * twr + br * twi
        for t1 in range(h1):
            cc, ss = ct_ref[k1, t1], st_ref[k1, t1]
            o_ref[0, slab(t1), :] += cc * br - ss * bi
            o_ref[1, slab(t1), :] += cc * bi + ss * br
        return next_twiddle(tw)

    lax.fori_loop(0, n1, body, tw0)


def _fftconv(v, circ, *, n1, ct):
    bsz, length, width = v.shape
    n = 2 * length
    n2 = n // n1
    idx = np.arange(n2)
    ang = 2.0 * np.pi * ((idx[:, None] * idx[None, :]) % n2) / n2
    cmat = jnp.asarray(np.cos(ang), F32).astype(BF16)
    smat = jnp.asarray(np.sin(ang), F32).astype(BF16)
    tang = 2.0 * np.pi * idx / n
    twr = jnp.asarray(np.broadcast_to(np.cos(tang)[:, None], (n2, ct)), F32)
    twi = jnp.asarray(np.broadcast_to(np.sin(tang)[:, None], (n2, ct)), F32)
    i1 = np.arange(n1)
    ang1 = 2.0 * np.pi * ((i1[:, None] * i1[None, :]) % n1) / n1
    ctab = jnp.asarray(np.round(np.cos(ang1), 12), F32)
    stab = jnp.asarray(np.round(np.sin(ang1), 12), F32)
    kern = functools.partial(_fftconv_kernel, n1=n1)
    const = lambda j, p: (0, 0)
    once = pl.Buffered(1)
    npairs = bsz // 2
    scratch = [pltpu.VMEM((n1, n2, ct), F32)] * 2 if npairs > 1 else []
    est = (4 * 2 * length * ct * 4 + n * ct * 4 + 2 * n2 * n2 * 2 + 2 * n2 * ct * 4
           + len(scratch) * n * ct * 4 + (8 + 2 * n1) * n2 * ct * 4)
    return pl.pallas_call(
        kern,
        grid=(width // ct, npairs),
        in_specs=[pl.BlockSpec(memory_space=pltpu.SMEM),
                  pl.BlockSpec(memory_space=pltpu.SMEM),
                  pl.BlockSpec((2, length, ct), lambda j, p: (p, 0, j)),
                  pl.BlockSpec((n, ct), lambda j, p: (0, j), pipeline_mode=once),
                  pl.BlockSpec((n2, n2), const, pipeline_mode=once),
                  pl.BlockSpec((n2, n2), const, pipeline_mode=once),
                  pl.BlockSpec((n2, ct), const, pipeline_mode=once),
                  pl.BlockSpec((n2, ct), const, pipeline_mode=once)],
        out_specs=pl.BlockSpec((2, length, ct), lambda j, p: (p, 0, j)),
        out_shape=jax.ShapeDtypeStruct((bsz, length, width), F32),
        scratch_shapes=scratch,
        compiler_params=_cparams(est, ("arbitrary", "arbitrary")),
        name="fftconv",
    )(ctab, stab, v, circ, cmat, smat, twr, twi)


def _peer_route_kernel(x_ref, mod_ref, g_ref, wq_ref, keys_ref, hb_ref, th_ref, e0_ref, s1_ref, e1_ref,
                       q_scr, sc_scr, top_scr):
    tq = x_ref.shape[0]
    nk = PEER_NKEYS
    k = PEER_TOPK
    nhp = keys_ref.shape[0]
    dkh = keys_ref.shape[2]
    neg = -jnp.inf
    m = mod_ref[...]
    h = _norm_mod(x_ref[...], g_ref[...], m[4:5], m[3:4]).astype(BF16)
    hb_ref[...] = h
    q = jnp.dot(h, wq_ref[...], preferred_element_type=F32)
    for hp in range(nhp):
        q_scr[hp] = q[:, hp * dkh:(hp + 1) * dkh].astype(BF16)

    lw = V7X_LANES
    sub = V7X_SUBLANES
    assert k == 2 * sub
    ncand = k + 7 * sub + (k - sub)
    npad = sum(sub - k // (i + 1) for i in range(2, sub))
    iota_k = lax.broadcasted_iota(jnp.int32, (nk, lw), 0)
    iota_c = lax.broadcasted_iota(jnp.int32, (ncand, lw), 0)
    iota_8 = lax.broadcasted_iota(jnp.int32, (sub, lw), 0)
    iota_r = lax.broadcasted_iota(jnp.int32, (k, lw), 0)

    def topk(vals, iota, exact):
        top = jnp.zeros((k, lw), F32)
        for r in range(k):
            mx = jnp.max(vals, axis=0, keepdims=True)
            if exact:
                first = jnp.min(jnp.where(vals == mx, iota, vals.shape[0]), axis=0, keepdims=True)
                vals = jnp.where(iota == first, neg, vals)
            else:
                vals = jnp.where(vals == mx, neg, vals)
            top = jnp.where(iota_r == r, mx, top)
        return top, vals

    def tied(rest, expected):
        gone = jnp.sum(jnp.where(rest == neg, 1.0, 0.0), axis=0, keepdims=True)
        return jnp.where(gone != expected, 1.0, 0.0)

    def halves(hd, exact):
        flag = jnp.zeros((1, lw), F32)
        for p in range(2):
            for c in range(tq // lw):
                lanes = slice(c * lw, (c + 1) * lw)
                top, rest = topk(sc_scr[2 * hd + p, :, lanes], iota_k, exact)
                top_scr[2 * hd + p, :, lanes] = top
                flag = jnp.maximum(flag, tied(rest, float(k)))
        return jnp.max(flag)

    def pairs(hd, exact):
        flag = jnp.zeros((1, lw), F32)
        for c in range(tq // lw):
            lanes = slice(c * lw, (c + 1) * lw)
            a = top_scr[2 * hd, :, lanes]
            b = top_scr[2 * hd + 1, :, lanes]
            pieces = [a[0:1] + b, a[1:2] + b[0:sub]]
            for i in range(2, sub):
                pieces.append(jnp.where(iota_8 < k // (i + 1), a[i:i + 1] + b[0:sub], neg))
            pieces.append(a[sub:k] + b[0:1])
            vmax = a[0:1] + b[0:1]
            top, rest = topk(jnp.concatenate(pieces, axis=0), iota_c, exact)
            z = jnp.sum(jnp.exp(top - vmax), axis=0, keepdims=True)
            last = top[k - 1:k]
            nxt = jnp.max(rest, axis=0, keepdims=True)
            tau = jnp.where(nxt == neg, last, 0.5 * (last + nxt))
            flag = jnp.maximum(flag, tied(rest, float(k + npad)))
            s0 = sc_scr[2 * hd, :, lanes]
            s1 = sc_scr[2 * hd + 1, :, lanes]
            th = tau - s0
            e0 = jnp.exp(s0 - a[0:1]) / z
            per = th_ref.shape[2]
            for r in range(nk // per):
                th_ref[hd, r, :, lanes] = th[r * per:(r + 1) * per]
                e0_ref[hd, r, :, lanes] = e0[r * per:(r + 1) * per]
            s1_ref[hd, :, lanes] = s1
            e1_ref[hd, :, lanes] = jnp.exp(s1 - b[0:1])
        return jnp.max(flag)

    def head_body(hd, carry):
        for p in range(2):
            sc_scr[2 * hd + p] = lax.dot_general(keys_ref[2 * hd + p], q_scr[2 * hd + p], (((1,), (1,)), ((), ())),
                                                 preferred_element_type=F32)
        @pl.when(halves(hd, False) > 0.0)
        def _():
            halves(hd, True)

        @pl.when(pairs(hd, False) > 0.0)
        def _():
            pairs(hd, True)

        return carry

    lax.fori_loop(0, nhp // 2, head_body, 0)


def _peer_route(x, mod, g, wq, keys, *, n_ctx, dec_seq, tq):
    t, d = x.shape
    nhp, nk, dkh = keys.shape
    nh = nhp // 2
    const = lambda i: (0, 0)
    hshape = jax.ShapeDtypeStruct((nh, nk, t), F32)
    hspec = pl.BlockSpec((nh, nk, tq), lambda i: (0, 0, i))
    per = PEER_GROUP // nk
    gshape = jax.ShapeDtypeStruct((nh, nk // per, per, t), F32)
    gspec = pl.BlockSpec((nh, nk // per, per, tq), lambda i: (0, 0, 0, i))
    pad = V7X_SUBLANES // per
    est = (2 * tq * d * 4 + 4 * d * nhp * dkh + 2 * tq * d * 2 + (4 + 4 * pad) * nh * nk * tq * 4
           + 2 * nhp * nk * tq * 4 + 6 * tq * nhp * dkh * 4)
    return pl.pallas_call(
        _peer_route_kernel,
        grid=(t // tq,),
        in_specs=[pl.BlockSpec((tq, d), lambda i: (i, 0)),
                  pl.BlockSpec((None, 6, d), lambda i: (_cond_index(i * tq, n_ctx, dec_seq), 0, 0)),
                  pl.BlockSpec((1, d), const),
                  pl.BlockSpec(wq.shape, const),
                  pl.BlockSpec(keys.shape, lambda i: (0, 0, 0))],
        out_specs=[pl.BlockSpec((tq, d), lambda i: (i, 0)), gspec, gspec, hspec, hspec],
        out_shape=[jax.ShapeDtypeStruct((t, d), BF16), gshape, gshape, hshape, hshape],
        scratch_shapes=[pltpu.VMEM((nhp, tq, dkh), BF16),
                        pltpu.VMEM((nhp, nk, tq), F32),
                        pltpu.VMEM((nhp, PEER_TOPK, tq), F32)],
        compiler_params=_cparams(est, ("parallel",)),
        name="peer_route",
    )(x, mod, g, wq, keys)


def _peer_expert_kernel(hbt_ref, u_ref, vt_ref, th_ref, e0_ref, s1_ref, e1_ref, x_ref, mod_ref, o_ref,
                        acc, st_a, st_b, hbuf_a, hbuf_b, *, nc):
    g = pl.program_id(0)
    nh = th_ref.shape[0]
    nk = s1_ref.shape[1]
    ngroups, per = th_ref.shape[1], th_ref.shape[2]
    gw = per * nk
    tq = st_a.shape[1]
    kh = PEER_ACC_ROWS // per
    c3 = jnp.maximum(g - 2, 0) % nc

    @pl.when(g == 0)
    def _():
        st_b[...] = jnp.zeros_like(st_b)
        hbuf_a[...] = jnp.zeros_like(hbuf_a)
        hbuf_b[...] = jnp.zeros_like(hbuf_b)

    @pl.when(c3 == 0)
    def _():
        acc[...] = jnp.zeros_like(acc)

    def step(st_w, st_r, hbuf_w, hbuf_r):
        def group(j, carry):
            rows = pl.ds(pl.multiple_of(j * gw, gw), gw)
            acc[...] += jnp.dot(vt_ref[j], hbuf_r[rows, :], preferred_element_type=F32)
            st_w[rows, :] = jnp.dot(u_ref[rows, :], hbt_ref[...], preferred_element_type=F32)
            for q in range(tq // V7X_LANES):
                lanes = slice(q * V7X_LANES, (q + 1) * V7X_LANES)
                for half in range(nk // kh):
                    keys = slice(half * kh, (half + 1) * kh)
                    w = [None] * per
                    for hd in range(nh):
                        s1 = s1_ref[hd, keys, lanes]
                        e1 = e1_ref[hd, keys, lanes]
                        for i in range(per):
                            sel = jnp.where(s1 >= th_ref[hd, j, i:i + 1, lanes], e1, 0.0) * e0_ref[hd, j, i:i + 1, lanes]
                            w[i] = sel if w[i] is None else w[i] + sel
                    for i in range(per):
                        tile = pl.ds(pl.multiple_of(j * gw + i * nk + half * kh, kh), kh)
                        hbuf_w[tile, lanes] = (_gelu(st_r[tile, lanes]) * w[i]).astype(BF16)
            return carry

        lax.fori_loop(0, ngroups, group, 0)

    @pl.when(g % 2 == 0)
    def _():
        step(st_a, st_b, hbuf_b, hbuf_a)

    @pl.when(g % 2 == 1)
    def _():
        step(st_b, st_a, hbuf_a, hbuf_b)

    @pl.when(jnp.logical_and(g >= 2, c3 == nc - 1))
    def _():
        o_ref[...] = x_ref[...] + mod_ref[5:6, :] * acc[...].T


def _peer_expert(hb, u_tab, vt_tab, th, e0, s1, e1, x, mod, *, n_ctx, dec_seq, tq, ec):
    t, d = x.shape
    ne = u_tab.shape[0]
    nh, nk, _ = s1.shape
    per = th.shape[2]
    assert per * nk == PEER_GROUP
    ngroups = ec // PEER_GROUP
    nc = ne // ec
    npairs = (t // tq) * nc

    def pair(g, lag):
        p = jnp.clip(g - lag, 0, npairs - 1)
        return p // nc, p % nc

    def cond3(g):
        return _cond_index(pair(g, 2)[0] * tq, n_ctx, dec_seq)

    est = (2 * tq * d * 2 + 4 * ec * d * 2 + 4 * nh * ngroups * V7X_SUBLANES * tq * 4 + 4 * nh * nk * tq * 4
           + 4 * tq * d * 4 + d * tq * 4 + 2 * ec * tq * 2 + 2 * ec * tq * 4 + 3 * PEER_GROUP * tq * 4)
    return pl.pallas_call(
        functools.partial(_peer_expert_kernel, nc=nc),
        grid=(npairs + 2,),
        in_specs=[pl.BlockSpec((d, tq), lambda g: (0, pair(g, 0)[0])),
                  pl.BlockSpec((ec, d), lambda g: (pair(g, 0)[1], 0)),
                  pl.BlockSpec((ngroups, d, PEER_GROUP), lambda g: (pair(g, 2)[1], 0, 0)),
                  pl.BlockSpec((nh, ngroups, per, tq), lambda g: (0, pair(g, 1)[1], 0, pair(g, 1)[0])),
                  pl.BlockSpec((nh, ngroups, per, tq), lambda g: (0, pair(g, 1)[1], 0, pair(g, 1)[0])),
                  pl.BlockSpec((nh, nk, tq), lambda g: (0, 0, pair(g, 1)[0])),
                  pl.BlockSpec((nh, nk, tq), lambda g: (0, 0, pair(g, 1)[0])),
                  pl.BlockSpec((tq, d), lambda g: (pair(g, 2)[0], 0)),
                  pl.BlockSpec((None, 6, d), lambda g: (cond3(g), 0, 0))],
        out_specs=pl.BlockSpec((tq, d), lambda g: (pair(g, 2)[0], 0)),
        out_shape=jax.ShapeDtypeStruct((t, d), F32),
        scratch_shapes=[pltpu.VMEM((d, tq), F32), pltpu.VMEM((ec, tq), F32), pltpu.VMEM((ec, tq), F32),
                        pltpu.VMEM((ec, tq), BF16), pltpu.VMEM((ec, tq), BF16)],
        compiler_params=_cparams(est, ("arbitrary",)),
        name="peer_expert",
    )(hb.T, u_tab, vt_tab, th, e0, s1, e1, x, mod)


def _final_kernel(x_ref, g_ref, o_ref):
    x = x_ref[...]
    ms = jnp.mean(x * x, axis=-1, keepdims=True)
    o_ref[...] = (x * lax.rsqrt(ms + EPS)) * g_ref[...]


def _final_norm(x, g, *, tm):
    t, d = x.shape
    return pl.pallas_call(
        _final_kernel,
        grid=(t // tm,),
        in_specs=[pl.BlockSpec((tm, d), lambda i: (i, 0)), pl.BlockSpec((1, d), lambda i: (0, 0))],
        out_specs=pl.BlockSpec((tm, d), lambda i: (i, 0)),
        out_shape=jax.ShapeDtypeStruct((t, d), F32),
        compiler_params=_cparams(6 * tm * d * 4, ("parallel",)),
        name="final_norm",
    )(x, g)


def _block_diag(w):
    h, i, j = w.shape
    return jnp.einsum('hij,hg->higj', w, jnp.eye(h, dtype=w.dtype)).reshape(h * i, h * j)


def _s5_discretize(a_re, a_im, log_dt, b_re, b_im):
    dt = jnp.exp(log_dt)[:, None]
    mag = jnp.exp(a_re * dt)
    abar_r = mag * jnp.cos(a_im * dt)
    abar_i = mag * jnp.sin(a_im * dt)
    den = a_re * a_re + a_im * a_im
    num_r = abar_r - 1.0
    coef_r = ((num_r * a_re + abar_i * a_im) / den)[..., None]
    coef_i = ((abar_i * a_re - num_r * a_im) / den)[..., None]
    return abar_r, abar_i, coef_r * b_re - coef_i * b_im, coef_r * b_im + coef_i * b_re


def _s5_blocks(abar_r, abar_i, bbar_r, bbar_i, c_re, c_im):
    g, p, ch = bbar_r.shape
    gb = g // S5_COLBLK

    def vec(v):
        return v.reshape(S5_COLBLK, 1, gb * p)

    def bmat(b):
        bt = b.transpose(0, 2, 1).reshape(S5_COLBLK, gb, ch, p)
        return jnp.stack([_block_diag(bt[j]) for j in range(S5_COLBLK)])

    def cmat(cc):
        ct = cc.transpose(0, 2, 1).reshape(S5_COLBLK, gb, p, ch)
        return jnp.stack([_block_diag(ct[j]) for j in range(S5_COLBLK)])

    bblk = jnp.concatenate([bmat(bbar_r), bmat(bbar_i)], axis=2).astype(BF16)
    cblk = jnp.concatenate([cmat(c_re), cmat(-c_im)], axis=1).astype(BF16)
    return bblk, cblk, vec(abar_r), vec(abar_i)


def _state_to_blocked(re, im):
    n, g, p = re.shape
    gb = g // S5_COLBLK
    both = jnp.stack([re.reshape(n, S5_COLBLK, gb * p), im.reshape(n, S5_COLBLK, gb * p)], axis=2)
    return both.reshape(n, -1)


def _state_from_blocked(st, g, p):
    n = st.shape[0]
    both = st.reshape(n, S5_COLBLK, 2, (g // S5_COLBLK) * p)
    return both[:, :, 0].reshape(n, g, p), both[:, :, 1].reshape(n, g, p)


def _complex_power(ar, ai, n):
    for _ in range(int(math.log2(n))):
        ar, ai = ar * ar - ai * ai, 2.0 * ar * ai
    return ar, ai


def _mixer_ab(x, mod, p, le, h0, *, bc, lc, bs, ls, tm):
    n_ctx = bc * lc
    width = p['s5_d'].shape[1]
    g_cnt, p_cnt = p['s5_a_re'].shape[2:]
    w_gates = jnp.concatenate([_block_diag(p['lru_w_a'][le, 0]), _block_diag(p['lru_w_x'][le, 0]),
                               _block_diag(p['lru_w_a'][le, 1]), _block_diag(p['lru_w_x'][le, 1])],
                              axis=1).astype(BF16)
    b_gates = jnp.concatenate([p['lru_b_a'][le, 0], p['lru_b_x'][le, 0],
                               p['lru_b_a'][le, 1], p['lru_b_x'][le, 1]]).reshape(1, 4 * width)
    c8 = LRU_C * jax.nn.softplus(-p['lru_lambda'][le])
    s_arr, xg = _ab_in(x, mod, p['norm1_g'][2 * le].reshape(1, -1), p['w_in_ab'][le].astype(BF16),
                       p['lru_conv_w'][le], p['lru_conv_b'][le].reshape(1, width), w_gates, b_gates, c8,
                       n_ctx=n_ctx, ctx_row=lc, dec_seq=ls, tm=tm)

    cols = s_arr.shape[1]
    gc = bc // SCAN_SEQS
    lseg = ls // SAMPLE_SEGS
    s_ctx = (s_arr[:n_ctx].reshape(gc, SCAN_SEQS, lc, cols).transpose(0, 2, 1, 3)
             .reshape(n_ctx, cols))
    s_smp = (s_arr[n_ctx:].reshape(bs * SAMPLE_SEGS, lseg, cols).transpose(1, 0, 2)
             .reshape(bs * ls, cols))
    chunk_rows = min(1024, lc * SCAN_SEQS, lseg * SCAN_SEQS)

    outs = {}
    finals = {}
    for d in range(2):
        abar_r, abar_i, bbar_r, bbar_i = _s5_discretize(
            p['s5_a_re'][le, d], p['s5_a_im'][le, d], p['s5_log_dt'][le, d],
            p['s5_b_re'][le, d], p['s5_b_im'][le, d])
        bblk, cblk, ar, ai = _s5_blocks(abar_r, abar_i, bbar_r, bbar_i,
                                        p['s5_c_re'][le, d], p['s5_c_im'][le, d])
        nstate = 2 * g_cnt * p_cnt
        z5 = jnp.zeros((gc, SCAN_SEQS, nstate), F32)
        zl = jnp.zeros((gc, SCAN_SEQS, width), F32)
        y, hl, f5, fl, _ = _scan(s_ctx, bblk, cblk, ar, ai, z5, zl, direction=d, store=True,
                                 groups=gc, chunk_rows=chunk_rows)
        outs[('ctx', d)] = (y, hl)
        finals[d] = (f5.reshape(bc, nstate), fl.reshape(bc, width))
        z5 = jnp.zeros((1, SCAN_SEQS, nstate), F32)
        zl = jnp.zeros((1, SCAN_SEQS, width), F32)
        e5, el, ep = _scan(s_smp, bblk, cblk, ar, ai, z5, zl, direction=d, store=False,
                           groups=1, chunk_rows=chunk_rows)
        first = (SAMPLE_SEGS - 1) if d == 1 else 0
        h5 = _state_to_blocked(h0[0][:, d], h0[1][:, d])
        h5_rows = jnp.zeros((bs, SAMPLE_SEGS, nstate), F32).at[:, first].set(h5).reshape(SCAN_SEQS, nstate)
        hl_rows = jnp.zeros((bs, SAMPLE_SEGS, width), F32).at[:, first].set(h0[2][:, d]).reshape(SCAN_SEQS, width)
        arl, ail = _complex_power(ar, ai, lseg)
        i5, il = _chain(e5[0], el[0], ep[0], h5_rows, hl_rows, arl, ail, direction=d)
        y, hl, _, _, _ = _scan(s_smp, bblk, cblk, ar, ai, i5[None], il[None], direction=d, store=True,
                               groups=1, chunk_rows=chunk_rows)
        outs[('smp', d)] = (y, hl)

    def to_tokens(ctx_arr, smp_arr):
        c = ctx_arr.reshape(gc, lc, SCAN_SEQS, width).transpose(0, 2, 1, 3).reshape(n_ctx, width)
        s = smp_arr.reshape(lseg, bs * SAMPLE_SEGS, width).transpose(1, 0, 2).reshape(bs * ls, width)
        return jnp.concatenate([c, s], axis=0)

    yf = to_tokens(outs[('ctx', 0)][0], outs[('smp', 0)][0])
    yb = to_tokens(outs[('ctx', 1)][0], outs[('smp', 1)][0])
    hf = to_tokens(outs[('ctx', 0)][1], outs[('smp', 0)][1])
    hb = to_tokens(outs[('ctx', 1)][1], outs[('smp', 1)][1])
    x1 = _ab_out(x, mod, s_arr, xg, yf, yb, hf, hb, p['s5_d'][le].reshape(1, width),
                 p['s5_w_glu'][le].astype(BF16), p['s5_b_glu'][le].reshape(1, width),
                 p['w_out_ab'][le].astype(BF16), n_ctx=n_ctx, dec_seq=ls, tm=tm)
    st = [_state_from_blocked(finals[d][0], g_cnt, p_cnt) for d in range(2)]
    new_re = jnp.stack([st[0][0], st[1][0]], axis=1)
    new_im = jnp.stack([st[0][1], st[1][1]], axis=1)
    new_lru = jnp.stack([finals[0][1], finals[1][1]], axis=1)
    return x1, (new_re, new_im, new_lru)


def _fft_split(length):
    n = 2 * length
    n1 = 2
    while n // n1 > 1024:
        n1 *= 2
    return n1


def _mixer_c(x, mod, p, lo, *, bc, lc, bs, ls, tm):
    n_ctx = bc * lc
    width = p['hy_bias'].shape[1]
    x0, vg = _c_in(x, mod, p['norm1_g'][2 * lo + 1].reshape(1, -1), p['w_in_c'][lo].astype(BF16),
                   p['b_in_c'][lo].reshape(1, -1), p['hy_conv_w'][lo], p['hy_conv_b'][lo].reshape(1, -1),
                   n_ctx=n_ctx, ctx_row=lc, dec_seq=ls, tm=tm)
    ys = []
    for (b, l, v) in ((bc, lc, vg[:n_ctx]), (bs, ls, vg[n_ctx:])):
        circ = _hy_filter(l, p['hy_w1'][lo], p['hy_b1'][lo], p['hy_freq1'][lo], p['hy_w2'][lo],
                          p['hy_b2'][lo], p['hy_freq2'][lo], p['hy_w3'][lo], p['hy_decay'][lo],
                          ct=min(width, 256 if l <= 1024 else 128))
        y = _fftconv(v.reshape(b, l, width), circ, n1=_fft_split(l), ct=min(width, 256 if l <= 1024 else 128))
        ys.append(y.reshape(b * l, width))
    yconv = jnp.concatenate(ys, axis=0)
    return _c_out(x, mod, x0, vg, yconv, p['hy_bias'][lo].reshape(1, width), p['w_out_c'][lo].astype(BF16),
                  p['b_out_c'][lo].reshape(1, -1), n_ctx=n_ctx, dec_seq=ls, tm=tm)


def _peer(x, mod, p, l, *, n_ctx, dec_seq, tq, ec):
    keys = p['peer_keys'][l]
    nh, _, nk, dkh = keys.shape
    hb, th, e0, s1, e1 = _peer_route(x, mod, p['norm2_g'][l].reshape(1, -1), p['peer_wq'][l].astype(BF16),
                                     keys.reshape(nh * 2, nk, dkh).astype(BF16),
                                     n_ctx=n_ctx, dec_seq=dec_seq, tq=tq)
    ne, d = p['peer_v'][l].shape
    vt = p['peer_v'][l].reshape(ne // PEER_GROUP, PEER_GROUP, d).transpose(0, 2, 1).astype(BF16)
    return _peer_expert(hb, p['peer_u'][l].astype(BF16), vt, th, e0, s1, e1,
                        x, mod, n_ctx=n_ctx, dec_seq=dec_seq, tq=tq, ec=ec)


def _forward(x_prompt, x_sample, state_s5_re, state_s5_im, state_lru, c, c_ctx, p, *, tm, tq, ec):
    bc, lc, d = x_prompt.shape
    bs, ls, _ = x_sample.shape
    assert bc % SCAN_SEQS == 0 and bs * SAMPLE_SEGS == SCAN_SEQS
    assert tm % lc == 0 and ls % tm == 0 and (ls // SAMPLE_SEGS) % GRID_W == 0
    n_ctx = bc * lc
    x = jnp.concatenate([x_prompt.reshape(n_ctx, d), x_sample.reshape(bs * ls, d)], axis=0)
    cond = jnp.zeros((V7X_SUBLANES, d), F32).at[0].set(c_ctx).at[1:1 + bs].set(c)
    mod_all = _modulation(cond, p['w_mod'], p['b_mod'])
    states = None
    for l in range(DEPTH):
        mod = mod_all[l].reshape(V7X_SUBLANES, 6, d)
        if l % 2 == 0:
            le = l // 2
            h0 = (state_s5_re[:, le], state_s5_im[:, le], state_lru[:, le])
            x, st = _mixer_ab(x, mod, p, le, h0, bc=bc, lc=lc, bs=bs, ls=ls, tm=tm)
            states = st if states is None else states
        else:
            x = _mixer_c(x, mod, p, l // 2, bc=bc, lc=lc, bs=bs, ls=ls, tm=tm)
        x = _peer(x, mod, p, l, n_ctx=n_ctx, dec_seq=ls, tq=tq, ec=ec)
    y = _final_norm(x, p['final_g'].reshape(1, d), tm=tm)
    new_re, new_im, new_lru = states
    return (y[:n_ctx].reshape(bc, lc, d), y[n_ctx:].reshape(bs, ls, d),
            new_re[:, None], new_im[:, None], new_lru[:, None])


def kernel(x_prompt, x_sample, state_s5_re, state_s5_im, state_lru, c, c_ctx, norm1_g, norm2_g, w_mod, b_mod, w_in_ab, s5_a_re, s5_a_im, s5_log_dt, s5_b_re, s5_b_im, s5_c_re, s5_c_im, s5_d, s5_w_glu, s5_b_glu, lru_conv_w, lru_conv_b, lru_w_a, lru_b_a, lru_w_x, lru_b_x, lru_lambda, w_out_ab, w_in_c, b_in_c, hy_conv_w, hy_conv_b, hy_w1, hy_b1, hy_freq1, hy_w2, hy_b2, hy_freq2, hy_w3, hy_decay, hy_bias, w_out_c, b_out_c, peer_wq, peer_keys, peer_u, peer_v, final_g):
    p = dict(norm1_g=norm1_g, norm2_g=norm2_g, w_mod=w_mod, b_mod=b_mod, w_in_ab=w_in_ab,
             s5_a_re=s5_a_re, s5_a_im=s5_a_im, s5_log_dt=s5_log_dt, s5_b_re=s5_b_re, s5_b_im=s5_b_im,
             s5_c_re=s5_c_re, s5_c_im=s5_c_im, s5_d=s5_d, s5_w_glu=s5_w_glu, s5_b_glu=s5_b_glu,
             lru_conv_w=lru_conv_w, lru_conv_b=lru_conv_b, lru_w_a=lru_w_a, lru_b_a=lru_b_a,
             lru_w_x=lru_w_x, lru_b_x=lru_b_x, lru_lambda=lru_lambda, w_out_ab=w_out_ab,
             w_in_c=w_in_c, b_in_c=b_in_c, hy_conv_w=hy_conv_w, hy_conv_b=hy_conv_b,
             hy_w1=hy_w1, hy_b1=hy_b1, hy_freq1=hy_freq1, hy_w2=hy_w2, hy_b2=hy_b2, hy_freq2=hy_freq2,
             hy_w3=hy_w3, hy_decay=hy_decay, hy_bias=hy_bias, w_out_c=w_out_c, b_out_c=b_out_c,
             peer_wq=peer_wq, peer_keys=peer_keys, peer_u=peer_u, peer_v=peer_v, final_g=final_g)
    return _forward(x_prompt, x_sample, state_s5_re, state_s5_im, state_lru, c, c_ctx, p,
                    tm=512, tq=512, ec=1024)
```

```python
import functools
import math

import jax
import jax.numpy as jnp
import numpy as np
from jax import lax
from jax.experimental import pallas as pl
from jax.experimental.pallas import tpu as pltpu

F32 = jnp.float32
BF16 = jnp.bfloat16
HIGHEST = lax.Precision.HIGHEST

DEPTH = 2
GRID_W = 64
EPS = 1e-6
S5_CH = 16
S5_STATE = 64
LRU_HEADS = 8
LRU_C = 8.0
HY_BANDS = 16
HY_SHIFT = 0.05
PEER_HEADS = 8
PEER_NKEYS = 128
PEER_TOPK = 16
PEER_GROUP = 512
PEER_ACC_ROWS = 256

V7X_LANES = 128
V7X_SUBLANES = 8
V7X_VMEM_BYTES = 64 * 1024 * 1024

SCAN_SEQS = V7X_SUBLANES
SAMPLE_SEGS = 4
S5_COLBLK = 4


def _vmem_limit(est_bytes):
    return int(min(V7X_VMEM_BYTES - (8 << 20), max(32 << 20, est_bytes * 3 // 2)))


def _cparams(est_bytes, sem):
    return pltpu.CompilerParams(dimension_semantics=sem, vmem_limit_bytes=_vmem_limit(est_bytes))


def _gelu(x):
    k = 2.0 * 0.7978845608028654 * math.log2(math.e)
    return x / (1.0 + jnp.exp2(x * (-k - (k * 0.044715) * (x * x))))


def _norm_mod(x, g, sc, sh):
    ms = jnp.mean(x * x, axis=-1, keepdims=True)
    return (x * lax.rsqrt(ms + EPS)) * g * (1.0 + sc) + sh


def _cond_index(row0, n_ctx, dec_seq):
    return jnp.where(row0 < n_ctx, 0, 1 + (row0 - n_ctx) // dec_seq)


def _mod_kernel(c_ref, w_ref, b_ref, o_ref):
    c = c_ref[...]
    s = c * jax.nn.sigmoid(c)
    o_ref[...] = jnp.dot(s, w_ref[...], precision=HIGHEST, preferred_element_type=F32) + b_ref[...]


def _modulation(cond, w_mod, b_mod):
    depth, d, n = w_mod.shape
    rows = cond.shape[0]
    tn = 1536
    est = 2 * d * tn * 4 + 4 * rows * (d + tn) * 4
    return pl.pallas_call(
        _mod_kernel,
        grid=(depth, n // tn),
        in_specs=[pl.BlockSpec((rows, d), lambda l, j: (0, 0)),
                  pl.BlockSpec((None, d, tn), lambda l, j: (l, 0, j)),
                  pl.BlockSpec((None, 1, tn), lambda l, j: (l, 0, j))],
        out_specs=pl.BlockSpec((None, rows, tn), lambda l, j: (l, 0, j)),
        out_shape=jax.ShapeDtypeStruct((depth, rows, n), F32),
        compiler_params=_cparams(est, ("arbitrary", "arbitrary")),
        name="modulation",
    )(cond, w_mod, b_mod.reshape(depth, 1, n))


def _dwconv_rows(x, w, bias, pos, row_len, pad_left):
    tm = x.shape[0]
    acc = None
    for k in range(w.shape[0]):
        d = k - pad_left
        if d == 0:
            term = x
        else:
            shifted = pltpu.roll(x, (-d) % tm, axis=0)
            q = pos + d
            term = jnp.where(q >= 0, jnp.where(q < row_len, shifted, 0.0), 0.0)
        term = term * w[k:k + 1]
        acc = term if acc is None else acc + term
    return acc + bias


def _ab_in_kernel(x_ref, mod_ref, g_ref, w_ref, cw_ref, cb_ref, wg_ref, bg_ref, c8_ref,
                  s_ref, xg_ref, *, n_ctx, ctx_row, width):
    tm = x_ref.shape[0]
    m = mod_ref[...]
    h = _norm_mod(x_ref[...], g_ref[...], m[1:2], m[0:1])
    z = jnp.dot(h.astype(BF16), w_ref[...], preferred_element_type=F32)
    u = z[:, :width]
    xr = z[:, width:2 * width]
    xg_ref[...] = z[:, 2 * width:]
    row_len = jnp.where(pl.program_id(0) * tm < n_ctx, ctx_row, GRID_W)
    pos = lax.broadcasted_iota(jnp.int32, (tm, width), 0) & (row_len - 1)
    xb = _dwconv_rows(xr, cw_ref[...], cb_ref[...], pos, row_len, 2)
    gates = jnp.dot(xb.astype(BF16), wg_ref[...], preferred_element_type=F32) + bg_ref[...]
    s_ref[:, :width] = u
    c8 = c8_ref[...]
    for d in range(2):
        r = jax.nn.sigmoid(gates[:, (2 * d) * width:(2 * d + 1) * width])
        i = jax.nn.sigmoid(gates[:, (2 * d + 1) * width:(2 * d + 2) * width])
        a = jnp.exp(-(c8[d:d + 1] * r))
        b = jnp.sqrt(1.0 - a * a) * (i * xb)
        s_ref[:, (1 + 2 * d) * width:(2 + 2 * d) * width] = a
        s_ref[:, (2 + 2 * d) * width:(3 + 2 * d) * width] = b


def _ab_in(x, mod, g, w_in, conv_w, conv_b, w_gates, b_gates, c8, *, n_ctx, ctx_row, dec_seq, tm):
    t, d = x.shape
    width = w_in.shape[1] // 3
    kern = functools.partial(_ab_in_kernel, n_ctx=n_ctx, ctx_row=ctx_row, width=width)
    const = lambda i: (0, 0)
    est = (2 * tm * d * 4 + d * 3 * width * 2 * 2 + width * 4 * width * 2 * 2
           + 2 * tm * 6 * width * 4 + tm * 12 * width * 4)
    return pl.pallas_call(
        kern,
        grid=(t // tm,),
        in_specs=[pl.BlockSpec((tm, d), lambda i: (i, 0)),
                  pl.BlockSpec((None, 6, d), lambda i: (_cond_index(i * tm, n_ctx, dec_seq), 0, 0)),
                  pl.BlockSpec((1, d), const),
                  pl.BlockSpec(w_in.shape, const),
                  pl.BlockSpec(conv_w.shape, const),
                  pl.BlockSpec((1, width), const),
                  pl.BlockSpec(w_gates.shape, const),
                  pl.BlockSpec((1, 4 * width), const),
                  pl.BlockSpec((2, width), const)],
        out_specs=[pl.BlockSpec((tm, 5 * width), lambda i: (i, 0)),
                   pl.BlockSpec((tm, width), lambda i: (i, 0))],
        out_shape=[jax.ShapeDtypeStruct((t, 5 * width), F32),
                   jax.ShapeDtypeStruct((t, width), F32)],
        compiler_params=_cparams(est, ("parallel",)),
        name="ab_in",
    )(x, mod, g, w_in, conv_w, conv_b, w_gates, b_gates, c8)


def _scan_kernel(u_ref, a_ref, b_ref, bblk_ref, cblk_ref, ar_ref, ai_ref, i5_ref, il_ref, *rest,
                 reverse, store):
    if store:
        y_ref, hl_ref, f5_ref, fl_ref, fp_ref, buf, st5, stl, stp = rest
    else:
        f5_ref, fl_ref, fp_ref, buf, st5, stl, stp = rest
    c = pl.program_id(1)
    rows = u_ref.shape[0]
    nsteps = rows // SCAN_SEQS
    sw = st5.shape[1] // (2 * S5_COLBLK)
    cw = u_ref.shape[1] // S5_COLBLK

    @pl.when(c == 0)
    def _():
        st5[...] = i5_ref[...]
        stl[...] = il_ref[...]
        stp[...] = jnp.ones_like(stp)

    def row0(k):
        t = (nsteps - 1 - k) if reverse else k
        return pl.multiple_of(t * SCAN_SEQS, SCAN_SEQS)

    u = u_ref[...].astype(BF16)
    for j in range(S5_COLBLK):
        buf[...] = jnp.dot(u[:, j * cw:(j + 1) * cw], bblk_ref[j], preferred_element_type=F32)
        arj = jnp.broadcast_to(ar_ref[j], (SCAN_SEQS, sw))
        aij = jnp.broadcast_to(ai_ref[j], (SCAN_SEQS, sw))

        def step(k, carry, arj=arj, aij=aij):
            hr, hi = carry
            r0 = row0(k)
            xr = buf[pl.ds(r0, SCAN_SEQS), 0:sw]
            xi = buf[pl.ds(r0, SCAN_SEQS), sw:2 * sw]
            nr = arj * hr - aij * hi + xr
            ni = arj * hi + aij * hr + xi
            buf[pl.ds(r0, SCAN_SEQS), 0:sw] = nr
            buf[pl.ds(r0, SCAN_SEQS), sw:2 * sw] = ni
            return nr, ni

        base = 2 * sw * j
        hr, hi = lax.fori_loop(0, nsteps, step, (st5[:, base:base + sw], st5[:, base + sw:base + 2 * sw]))
        st5[:, base:base + sw] = hr
        st5[:, base + sw:base + 2 * sw] = hi
        if store:
            y_ref[:, j * cw:(j + 1) * cw] = jnp.dot(buf[...].astype(BF16), cblk_ref[j],
                                                    preferred_element_type=F32)

    def lstep(k, carry):
        h, p = carry
        r0 = row0(k)
        a = a_ref[pl.ds(r0, SCAN_SEQS), :]
        h = a * h + b_ref[pl.ds(r0, SCAN_SEQS), :]
        if store:
            hl_ref[pl.ds(r0, SCAN_SEQS), :] = h
        return h, p * a

    h, p = lax.fori_loop(0, nsteps, lstep, (stl[...], stp[...]))
    stl[...] = h
    stp[...] = p

    @pl.when(c == pl.num_programs(1) - 1)
    def _():
        f5_ref[...] = st5[...]
        fl_ref[...] = stl[...]
        fp_ref[...] = stp[...]


def _scan(s_arr, bblk, cblk, ar, ai, init5, initl, *, direction, store, groups, chunk_rows):
    rows_total, cols = s_arr.shape
    width = cols // 5
    rows_per_group = rows_total // groups
    nc = rows_per_group // chunk_rows
    reverse = direction == 1
    nstate = init5.shape[-1]

    def chunk(g, c):
        return g * nc + ((nc - 1 - c) if reverse else c)

    kern = functools.partial(_scan_kernel, reverse=reverse, store=store)
    const3 = lambda g, c: (0, 0, 0)
    in_specs = [pl.BlockSpec((chunk_rows, width), lambda g, c: (chunk(g, c), 0)),
                pl.BlockSpec((chunk_rows, width), lambda g, c: (chunk(g, c), 1 + 2 * direction)),
                pl.BlockSpec((chunk_rows, width), lambda g, c: (chunk(g, c), 2 + 2 * direction)),
                pl.BlockSpec(bblk.shape, const3),
                pl.BlockSpec(cblk.shape, const3),
                pl.BlockSpec(ar.shape, const3),
                pl.BlockSpec(ai.shape, const3),
                pl.BlockSpec((None, SCAN_SEQS, nstate), lambda g, c: (g, 0, 0)),
                pl.BlockSpec((None, SCAN_SEQS, width), lambda g, c: (g, 0, 0))]
    fin_specs = [pl.BlockSpec((None, SCAN_SEQS, nstate), lambda g, c: (g, 0, 0)),
                 pl.BlockSpec((None, SCAN_SEQS, width), lambda g, c: (g, 0, 0)),
                 pl.BlockSpec((None, SCAN_SEQS, width), lambda g, c: (g, 0, 0))]
    fin_shapes = [jax.ShapeDtypeStruct((groups, SCAN_SEQS, nstate), F32),
                  jax.ShapeDtypeStruct((groups, SCAN_SEQS, width), F32),
                  jax.ShapeDtypeStruct((groups, SCAN_SEQS, width), F32)]
    out_specs, out_shapes = fin_specs, fin_shapes
    if store:
        out_specs = [pl.BlockSpec((chunk_rows, width), lambda g, c: (chunk(g, c), 0))] * 2 + fin_specs
        out_shapes = [jax.ShapeDtypeStruct((rows_total, width), F32)] * 2 + fin_shapes
    bufw = 2 * nstate // (2 * S5_COLBLK)
    est = (10 * chunk_rows * width * 4 + chunk_rows * bufw * 4 * 2
           + 4 * bblk.size * 2 + 8 * SCAN_SEQS * nstate * 4)
    return pl.pallas_call(
        kern,
        grid=(groups, nc),
        in_specs=in_specs,
        out_specs=out_specs,
        out_shape=out_shapes,
        scratch_shapes=[pltpu.VMEM((chunk_rows, bufw), F32),
                        pltpu.VMEM((SCAN_SEQS, nstate), F32),
                        pltpu.VMEM((SCAN_SEQS, width), F32),
                        pltpu.VMEM((SCAN_SEQS, width), F32)],
        compiler_params=_cparams(est, ("arbitrary", "arbitrary")),
        name="scan_store" if store else "scan_ends",
    )(s_arr, s_arr, s_arr, bblk, cblk, ar, ai, init5, initl)


def _chain_kernel(e5_ref, el_ref, ep_ref, h5_ref, hl_ref, arl_ref, ail_ref, i5_ref, il_ref, *, reverse):
    nseg = SAMPLE_SEGS
    sw = arl_ref.shape[-1]
    shift = (SCAN_SEQS - 1) if reverse else 1
    order = range(nseg - 2, -1, -1) if reverse else range(1, nseg)
    seg5 = lax.broadcasted_iota(jnp.int32, (SCAN_SEQS, sw), 0) % nseg
    for j in range(S5_COLBLK):
        base = 2 * sw * j
        re, im = slice(base, base + sw), slice(base + sw, base + 2 * sw)
        ar, ai = arl_ref[j], ail_ref[j]
        er = pltpu.roll(e5_ref[:, re], shift, axis=0)
        ei = pltpu.roll(e5_ref[:, im], shift, axis=0)
        hr, hi = h5_ref[:, re], h5_ref[:, im]
        for k in order:
            pr = pltpu.roll(hr, shift, axis=0)
            pi = pltpu.roll(hi, shift, axis=0)
            hr = jnp.where(seg5 == k, ar * pr - ai * pi + er, hr)
            hi = jnp.where(seg5 == k, ar * pi + ai * pr + ei, hi)
        i5_ref[:, re] = hr
        i5_ref[:, im] = hi
    segl = lax.broadcasted_iota(jnp.int32, el_ref.shape, 0) % nseg
    el = pltpu.roll(el_ref[...], shift, axis=0)
    ep = pltpu.roll(ep_ref[...], shift, axis=0)
    h = hl_ref[...]
    for k in order:
        h = jnp.where(segl == k, ep * pltpu.roll(h, shift, axis=0) + el, h)
    il_ref[...] = h


def _chain(end5, endl, endp, h5_rows, hl_rows, arl, ail, *, direction):
    kern = functools.partial(_chain_kernel, reverse=direction == 1)
    return pl.pallas_call(
        kern,
        out_shape=[jax.ShapeDtypeStruct(end5.shape, F32), jax.ShapeDtypeStruct(endl.shape, F32)],
        name="scan_chain",
    )(end5, endl, endp, h5_rows, hl_rows, arl, ail)


def _ab_out_kernel(x_ref, mod_ref, s_ref, xg_ref, yf_ref, yb_ref, hf_ref, hb_ref, d_ref,
                   wglu_ref, bglu_ref, wout_ref, o_ref):
    width = xg_ref.shape[1]
    m = mod_ref[...]
    y = s_ref[...] * d_ref[...] + yf_ref[...] + yb_ref[...]
    zs = _gelu(y)
    gate = jnp.dot(zs.astype(BF16), wglu_ref[...], preferred_element_type=F32) + bglu_ref[...]
    s5_out = zs * jax.nn.sigmoid(gate)
    lru_out = (hf_ref[...] + hb_ref[...]) * _gelu(xg_ref[...])
    out = (jnp.dot(s5_out.astype(BF16), wout_ref[:width, :], preferred_element_type=F32)
           + jnp.dot(lru_out.astype(BF16), wout_ref[width:, :], preferred_element_type=F32))
    o_ref[...] = x_ref[...] + m[2:3] * out


def _ab_out(x, mod, s_arr, xg, yf, yb, hf, hb, s5_d, w_glu, b_glu, w_out, *, n_ctx, dec_seq, tm):
    t, d = x.shape
    width = xg.shape[1]
    const = lambda i: (0, 0)
    row = lambda i: (i, 0)
    est = 4 * tm * d * 4 + 12 * tm * width * 4 + 4 * (width * width + 2 * width * d) + 8 * tm * width * 4
    return pl.pallas_call(
        _ab_out_kernel,
        grid=(t // tm,),
        in_specs=[pl.BlockSpec((tm, d), row),
                  pl.BlockSpec((None, 6, d), lambda i: (_cond_index(i * tm, n_ctx, dec_seq), 0, 0)),
                  pl.BlockSpec((tm, width), row),
                  pl.BlockSpec((tm, width), row),
                  pl.BlockSpec((tm, width), row),
                  pl.BlockSpec((tm, width), row),
                  pl.BlockSpec((tm, width), row),
                  pl.BlockSpec((tm, width), row),
                  pl.BlockSpec((1, width), const),
                  pl.BlockSpec(w_glu.shape, const),
                  pl.BlockSpec((1, width), const),
                  pl.BlockSpec(w_out.shape, const)],
        out_specs=pl.BlockSpec((tm, d), row),
        out_shape=jax.ShapeDtypeStruct((t, d), F32),
        compiler_params=_cparams(est, ("parallel",)),
        name="ab_out",
    )(x, mod, s_arr, xg, yf, yb, hf, hb, s5_d, w_glu, b_glu, w_out)


def _c_in_kernel(x_ref, mod_ref, g_ref, w_ref, b_ref, cw_ref, cb_ref, x0_ref, vg_ref, *, n_ctx, ctx_row):
    tm = x_ref.shape[0]
    width = x0_ref.shape[1]
    m = mod_ref[...]
    h = _norm_mod(x_ref[...], g_ref[...], m[1:2], m[0:1])
    z = jnp.dot(h.astype(BF16), w_ref[...], preferred_element_type=F32) + b_ref[...]
    row_len = jnp.where(pl.program_id(0) * tm < n_ctx, ctx_row, GRID_W)
    pos = lax.broadcasted_iota(jnp.int32, (tm, width), 0) & (row_len - 1)
    cw = cw_ref[...]
    cb = cb_ref[...]
    parts = [_dwconv_rows(z[:, k * width:(k + 1) * width], cw[:, k * width:(k + 1) * width],
                          cb[:, k * width:(k + 1) * width], pos, row_len, 1) for k in range(3)]
    x0_ref[...] = parts[0]
    vg_ref[...] = parts[2] * parts[1]


def _c_in(x, mod, g, w_in, b_in, conv_w, conv_b, *, n_ctx, ctx_row, dec_seq, tm):
    t, d = x.shape
    width = w_in.shape[1] // 3
    kern = functools.partial(_c_in_kernel, n_ctx=n_ctx, ctx_row=ctx_row)
    const = lambda i: (0, 0)
    est = 2 * tm * d * 4 + 4 * d * 3 * width + 4 * tm * width * 4 + 6 * tm * 3 * width * 4
    return pl.pallas_call(
        kern,
        grid=(t // tm,),
        in_specs=[pl.BlockSpec((tm, d), lambda i: (i, 0)),
                  pl.BlockSpec((None, 6, d), lambda i: (_cond_index(i * tm, n_ctx, dec_seq), 0, 0)),
                  pl.BlockSpec((1, d), const),
                  pl.BlockSpec(w_in.shape, const),
                  pl.BlockSpec((1, 3 * width), const),
                  pl.BlockSpec(conv_w.shape, const),
                  pl.BlockSpec((1, 3 * width), const)],
        out_specs=[pl.BlockSpec((tm, width), lambda i: (i, 0))] * 2,
        out_shape=[jax.ShapeDtypeStruct((t, width), F32)] * 2,
        compiler_params=_cparams(est, ("parallel",)),
        name="c_in",
    )(x, mod, g, w_in, b_in, conv_w, conv_b)


def _c_out_kernel(x_ref, mod_ref, x0_ref, vg_ref, yc_ref, hb_ref, w_ref, b_ref, o_ref):
    m = mod_ref[...]
    vg = vg_ref[...]
    y = (yc_ref[...] + hb_ref[...] * vg) * x0_ref[...]
    out = jnp.dot(y.astype(BF16), w_ref[...], preferred_element_type=F32) + b_ref[...]
    o_ref[...] = x_ref[...] + m[2:3] * out


def _c_out(x, mod, x0, vg, yconv, hy_bias, w_out, b_out, *, n_ctx, dec_seq, tm):
    t, d = x.shape
    width = x0.shape[1]
    const = lambda i: (0, 0)
    row = lambda i: (i, 0)
    est = 4 * tm * d * 4 + 6 * tm * width * 4 + 4 * width * d + 4 * tm * width * 4
    return pl.pallas_call(
        _c_out_kernel,
        grid=(t // tm,),
        in_specs=[pl.BlockSpec((tm, d), row),
                  pl.BlockSpec((None, 6, d), lambda i: (_cond_index(i * tm, n_ctx, dec_seq), 0, 0)),
                  pl.BlockSpec((tm, width), row),
                  pl.BlockSpec((tm, width), row),
                  pl.BlockSpec((tm, width), row),
                  pl.BlockSpec((1, width), const),
                  pl.BlockSpec(w_out.shape, const),
                  pl.BlockSpec((1, d), const)],
        out_specs=pl.BlockSpec((tm, d), row),
        out_shape=jax.ShapeDtypeStruct((t, d), F32),
        compiler_params=_cparams(est, ("parallel",)),
        name="c_out",
    )(x, mod, x0, vg, yconv, hy_bias, w_out, b_out)


def _hy_filter_kernel(w1_ref, b1_ref, f1_ref, w2_ref, b2_ref, f2_ref, w3_ref, dec_ref, o_ref, z_scr, *, length):
    ct = o_ref.shape[1]
    nb = HY_BANDS
    ecols = w1_ref.shape[0]
    row = lax.broadcasted_iota(jnp.int32, (length, 1), 0)
    positions = (row, length - row)

    @pl.when(pl.program_id(0) == 0)
    def _():
        for side in range(2):
            posf = positions[side].astype(F32)
            col = lax.broadcasted_iota(jnp.int32, (length, ecols), 1)
            band_idx = jnp.where(col <= nb, col - 1, col - 1 - nb).astype(F32)
            bands = 1e-4 + band_idx * ((nb - 1 - 1e-4) / (nb - 1))
            ang = (2.0 * math.pi / length) * posf * bands
            emb = jnp.where(col == 0, posf / length,
                            jnp.where(col <= nb, jnp.cos(ang),
                                      jnp.where(col <= 2 * nb, -jnp.sin(ang), 0.0)))
            z = jnp.sin(f1_ref[...] * (jnp.dot(emb, w1_ref[...], precision=HIGHEST,
                                               preferred_element_type=F32) + b1_ref[...]))
            z_scr[side] = jnp.sin(f2_ref[...] * (jnp.dot(z, w2_ref[...], precision=HIGHEST,
                                                         preferred_element_type=F32) + b2_ref[...]))

    def half(side):
        t = positions[side].astype(F32) / length
        filt = jnp.dot(z_scr[side], w3_ref[side], precision=HIGHEST, preferred_element_type=F32)
        return filt * (jnp.exp(-t * jnp.abs(dec_ref[side:side + 1, :])) + HY_SHIFT)

    fwd = half(0)
    bwd = jnp.where(lax.broadcasted_iota(jnp.int32, (length, ct), 0) == 0, 0.0, half(1))
    ss = jnp.sum(fwd * fwd, axis=0, keepdims=True) + jnp.sum(bwd * bwd, axis=0, keepdims=True)
    scale = lax.rsqrt(ss + EPS)
    o_ref[:length, :] = fwd * scale
    o_ref[length:, :] = bwd * scale


def _hy_filter(length, w1, b1, f1, w2, b2, f2, w3, decay, *, ct):
    emb, ff = w1.shape
    width = decay.shape[1]
    ecols = 128
    w1p = jnp.zeros((ecols, ff), F32).at[:emb].set(w1)
    w3s = w3.reshape(ff, 2, width).transpose(1, 0, 2)
    kern = functools.partial(_hy_filter_kernel, length=length)
    const = lambda j: (0, 0)
    est = 10 * length * ct * 4 + 4 * length * ecols * 4
    return pl.pallas_call(
        kern,
        grid=(width // ct,),
        in_specs=[pl.BlockSpec((ecols, ff), const),
                  pl.BlockSpec((1, ff), const),
                  pl.BlockSpec((1, ff), const),
                  pl.BlockSpec((ff, ff), const),
                  pl.BlockSpec((1, ff), const),
                  pl.BlockSpec((1, ff), const),
                  pl.BlockSpec((2, ff, ct), lambda j: (0, 0, j)),
                  pl.BlockSpec((2, ct), lambda j: (0, j))],
        out_specs=pl.BlockSpec((2 * length, ct), lambda j: (0, j)),
        out_shape=jax.ShapeDtypeStruct((2 * length, width), F32),
        scratch_shapes=[pltpu.VMEM((2, length, ff), F32)],
        compiler_params=_cparams(est, ("arbitrary",)),
        name="hy_filter",
    )(w1p, b1.reshape(1, ff), f1.reshape(1, ff), w2, b2.reshape(1, ff), f2.reshape(1, ff), w3s, decay)


def _fftconv_kernel(ct_ref, st_ref, x_ref, k_ref, c_ref, s_ref, twr_ref, twi_ref, o_ref, *scratch, n1):
    kfr, kfi = scratch if scratch else (None, None)
    h1 = n1 // 2
    n2 = c_ref.shape[0]
    n = n1 * n2
    cm = c_ref[...]
    sm = s_ref[...]

    def dft(pairs, inverse):
        w = pairs[0][0].shape[1]
        parts = jnp.concatenate([p.astype(BF16) for pair in pairs for p in pair], axis=1)
        cp = jnp.dot(cm, parts, preferred_element_type=F32)
        sp = jnp.dot(sm, parts, preferred_element_type=F32)
        out = []
        for i in range(len(pairs)):
            ca, cb = cp[:, 2 * i * w:(2 * i + 1) * w], cp[:, (2 * i + 1) * w:(2 * i + 2) * w]
            sa, sb = sp[:, 2 * i * w:(2 * i + 1) * w], sp[:, (2 * i + 1) * w:(2 * i + 2) * w]
            out.append((ca - sb, cb + sa) if inverse else (ca + sb, cb - sa))
        return out

    def slab(t1):
        return slice(t1 * n2, (t1 + 1) * n2)

    twr1, twi1 = twr_ref[...], twi_ref[...]
    tw0 = (jnp.ones_like(twr1), jnp.zeros_like(twi1))

    def next_twiddle(tw):
        return tw[0] * twr1 - tw[1] * twi1, tw[0] * twi1 + tw[1] * twr1

    def filter_stage1(k1, tw):
        ar = ai = None
        for t1 in range(n1):
            kr = k_ref[slab(t1), :]
            tr, ti = ct_ref[k1, t1] * kr, -(st_ref[k1, t1] * kr)
            ar = tr if ar is None else ar + tr
            ai = ti if ai is None else ai + ti
        ar, ai = ar * (1.0 / n), ai * (1.0 / n)
        return ar * tw[0] + ai * tw[1], ai * tw[0] - ar * tw[1]

    if kfr is not None:
        @pl.when(pl.program_id(1) == 0)
        def _():
            def fbody(k1, tw):
                (kfr[k1], kfi[k1]), = dft([filter_stage1(k1, tw)], False)
                return next_twiddle(tw)
            lax.fori_loop(0, n1, fbody, tw0)

    o_ref[...] = jnp.zeros_like(o_ref)

    def body(k1, tw):
        twr, twi = tw
        ar = ai = None
        for t1 in range(h1):
            cc, ss = ct_ref[k1, t1], st_ref[k1, t1]
            xr, xi = x_ref[0, slab(t1), :], x_ref[1, slab(t1), :]
            tr, ti = cc * xr + ss * xi, cc * xi - ss * xr
            ar = tr if ar is None else ar + tr
            ai = ti if ai is None else ai + ti
        data = (ar * twr + ai * twi, ai * twr - ar * twi)
        if kfr is not None:
            (fr, fi), = dft([data], False)
            gr, gi = kfr[k1], kfi[k1]
        else:
            (fr, fi), (gr, gi) = dft([data, filter_stage1(k1, tw)], False)
        (br, bi), = dft([(fr * gr - fi * gi, fr * gi + fi * gr)], True)
        br, bi = br * twr - bi * twi, bi * twr + br * twi
        for t1 in range(h1):
            cc, ss = ct_ref[k1, t1], st_ref[k1, t1]
            o_ref[0, slab(t1), :] += cc * br - ss * bi
            o_ref[1, slab(t1), :] += cc * bi + ss * br
        return next_twiddle(tw)

    lax.fori_loop(0, n1, body, tw0)


def _fftconv(v, circ, *, n1, ct):
    bsz, length, width = v.shape
    n = 2 * length
    n2 = n // n1
    idx = np.arange(n2)
    ang = 2.0 * np.pi * ((idx[:, None] * idx[None, :]) % n2) / n2
    cmat = jnp.asarray(np.cos(ang), F32).astype(BF16)
    smat = jnp.asarray(np.sin(ang), F32).astype(BF16)
    tang = 2.0 * np.pi * idx / n
    twr = jnp.asarray(np.broadcast_to(np.cos(tang)[:, None], (n2, ct)), F32)
    twi = jnp.asarray(np.broadcast_to(np.sin(tang)[:, None], (n2, ct)), F32)
    i1 = np.arange(n1)
    ang1 = 2.0 * np.pi * ((i1[:, None] * i1[None, :]) % n1) / n1
    ctab = jnp.asarray(np.round(np.cos(ang1), 12), F32)
    stab = jnp.asarray(np.round(np.sin(ang1), 12), F32)
    kern = functools.partial(_fftconv_kernel, n1=n1)
    const = lambda j, p: (0, 0)
    once = pl.Buffered(1)
    npairs = bsz // 2
    scratch = [pltpu.VMEM((n1, n2, ct), F32)] * 2 if npairs > 1 else []
    est = (4 * 2 * length * ct * 4 + n * ct * 4 + 2 * n2 * n2 * 2 + 2 * n2 * ct * 4
           + len(scratch) * n * ct * 4 + (8 + 2 * n1) * n2 * ct * 4)
    return pl.pallas_call(
        kern,
        grid=(width // ct, npairs),
        in_specs=[pl.BlockSpec(memory_space=pltpu.SMEM),
                  pl.BlockSpec(memory_space=pltpu.SMEM),
                  pl.BlockSpec((2, length, ct), lambda j, p: (p, 0, j)),
                  pl.BlockSpec((n, ct), lambda j, p: (0, j), pipeline_mode=once),
                  pl.BlockSpec((n2, n2), const, pipeline_mode=once),
                  pl.BlockSpec((n2, n2), const, pipeline_mode=once),
                  pl.BlockSpec((n2, ct), const, pipeline_mode=once),
                  pl.BlockSpec((n2, ct), const, pipeline_mode=once)],
        out_specs=pl.BlockSpec((2, length, ct), lambda j, p: (p, 0, j)),
        out_shape=jax.ShapeDtypeStruct((bsz, length, width), F32),
        scratch_shapes=scratch,
        compiler_params=_cparams(est, ("arbitrary", "arbitrary")),
        name="fftconv",
    )(ctab, stab, v, circ, cmat, smat, twr, twi)


def _peer_route_kernel(x_ref, mod_ref, g_ref, wq_ref, keys_ref, hb_ref, th_ref, e0_ref, s1_ref, e1_ref,
                       q_scr, sc_scr, top_scr):
    tq = x_ref.shape[0]
    nk = PEER_NKEYS
    k = PEER_TOPK
    nhp = keys_ref.shape[0]
    dkh = keys_ref.shape[2]
    neg = -jnp.inf
    m = mod_ref[...]
    h = _norm_mod(x_ref[...], g_ref[...], m[4:5], m[3:4]).astype(BF16)
    hb_ref[...] = h
    q = jnp.dot(h, wq_ref[...], preferred_element_type=F32)
    for hp in range(nhp):
        q_scr[hp] = q[:, hp * dkh:(hp + 1) * dkh].astype(BF16)

    lw = V7X_LANES
    sub = V7X_SUBLANES
    assert k == 2 * sub
    ncand = k + 7 * sub + (k - sub)
    npad = sum(sub - k // (i + 1) for i in range(2, sub))
    iota_k = lax.broadcasted_iota(jnp.int32, (nk, lw), 0)
    iota_c = lax.broadcasted_iota(jnp.int32, (ncand, lw), 0)
    iota_8 = lax.broadcasted_iota(jnp.int32, (sub, lw), 0)
    iota_r = lax.broadcasted_iota(jnp.int32, (k, lw), 0)

    def topk(vals, iota, exact):
        top = jnp.zeros((k, lw), F32)
        for r in range(k):
            mx = jnp.max(vals, axis=0, keepdims=True)
            if exact:
                first = jnp.min(jnp.where(vals == mx, iota, vals.shape[0]), axis=0, keepdims=True)
                vals = jnp.where(iota == first, neg, vals)
            else:
                vals = jnp.where(vals == mx, neg, vals)
            top = jnp.where(iota_r == r, mx, top)
        return top, vals

    def tied(rest, expected):
        gone = jnp.sum(jnp.where(rest == neg, 1.0, 0.0), axis=0, keepdims=True)
        return jnp.where(gone != expected, 1.0, 0.0)

    def halves(hd, exact):
        flag = jnp.zeros((1, lw), F32)
        for p in range(2):
            for c in range(tq // lw):
                lanes = slice(c * lw, (c + 1) * lw)
                top, rest = topk(sc_scr[2 * hd + p, :, lanes], iota_k, exact)
                top_scr[2 * hd + p, :, lanes] = top
                flag = jnp.maximum(flag, tied(rest, float(k)))
        return jnp.max(flag)

    def pairs(hd, exact):
        flag = jnp.zeros((1, lw), F32)
        for c in range(tq // lw):
            lanes = slice(c * lw, (c + 1) * lw)
            a = top_scr[2 * hd, :, lanes]
            b = top_scr[2 * hd + 1, :, lanes]
            pieces = [a[0:1] + b, a[1:2] + b[0:sub]]
            for i in range(2, sub):
                pieces.append(jnp.where(iota_8 < k // (i + 1), a[i:i + 1] + b[0:sub], neg))
            pieces.append(a[sub:k] + b[0:1])
            vmax = a[0:1] + b[0:1]
            top, rest = topk(jnp.concatenate(pieces, axis=0), iota_c, exact)
            z = jnp.sum(jnp.exp(top - vmax), axis=0, keepdims=True)
            last = top[k - 1:k]
            nxt = jnp.max(rest, axis=0, keepdims=True)
            tau = jnp.where(nxt == neg, last, 0.5 * (last + nxt))
            flag = jnp.maximum(flag, tied(rest, float(k + npad)))
            s0 = sc_scr[2 * hd, :, lanes]
            s1 = sc_scr[2 * hd + 1, :, lanes]
            th = tau - s0
            e0 = jnp.exp(s0 - a[0:1]) / z
            per = th_ref.shape[2]
            for r in range(nk // per):
                th_ref[hd, r, :, lanes] = th[r * per:(r + 1) * per]
                e0_ref[hd, r, :, lanes] = e0[r * per:(r + 1) * per]
            s1_ref[hd, :, lanes] = s1
            e1_ref[hd, :, lanes] = jnp.exp(s1 - b[0:1])
        return jnp.max(flag)

    def head_body(hd, carry):
        for p in range(2):
            sc_scr[2 * hd + p] = lax.dot_general(keys_ref[2 * hd + p], q_scr[2 * hd + p], (((1,), (1,)), ((), ())),
                                                 preferred_element_type=F32)
        @pl.when(halves(hd, False) > 0.0)
        def _():
            halves(hd, True)

        @pl.when(pairs(hd, False) > 0.0)
        def _():
            pairs(hd, True)

        return carry

    lax.fori_loop(0, nhp // 2, head_body, 0)


def _peer_route(x, mod, g, wq, keys, *, n_ctx, dec_seq, tq):
    t, d = x.shape
    nhp, nk, dkh = keys.shape
    nh = nhp // 2
    const = lambda i: (0, 0)
    hshape = jax.ShapeDtypeStruct((nh, nk, t), F32)
    hspec = pl.BlockSpec((nh, nk, tq), lambda i: (0, 0, i))
    per = PEER_GROUP // nk
    gshape = jax.ShapeDtypeStruct((nh, nk // per, per, t), F32)
    gspec = pl.BlockSpec((nh, nk // per, per, tq), lambda i: (0, 0, 0, i))
    pad = V7X_SUBLANES // per
    est = (2 * tq * d * 4 + 4 * d * nhp * dkh + 2 * tq * d * 2 + (4 + 4 * pad) * nh * nk * tq * 4
           + 2 * nhp * nk * tq * 4 + 6 * tq * nhp * dkh * 4)
    return pl.pallas_call(
        _peer_route_kernel,
        grid=(t // tq,),
        in_specs=[pl.BlockSpec((tq, d), lambda i: (i, 0)),
                  pl.BlockSpec((None, 6, d), lambda i: (_cond_index(i * tq, n_ctx, dec_seq), 0, 0)),
                  pl.BlockSpec((1, d), const),
                  pl.BlockSpec(wq.shape, const),
                  pl.BlockSpec(keys.shape, lambda i: (0, 0, 0))],
        out_specs=[pl.BlockSpec((tq, d), lambda i: (i, 0)), gspec, gspec, hspec, hspec],
        out_shape=[jax.ShapeDtypeStruct((t, d), BF16), gshape, gshape, hshape, hshape],
        scratch_shapes=[pltpu.VMEM((nhp, tq, dkh), BF16),
                        pltpu.VMEM((nhp, nk, tq), F32),
                        pltpu.VMEM((nhp, PEER_TOPK, tq), F32)],
        compiler_params=_cparams(est, ("parallel",)),
        name="peer_route",
    )(x, mod, g, wq, keys)


def _peer_expert_kernel(hbt_ref, u_ref, vt_ref, th_ref, e0_ref, s1_ref, e1_ref, x_ref, mod_ref, o_ref,
                        acc, st_a, st_b, hbuf_a, hbuf_b, *, nc):
    g = pl.program_id(0)
    nh = th_ref.shape[0]
    nk = s1_ref.shape[1]
    ngroups, per = th_ref.shape[1], th_ref.shape[2]
    gw = per * nk
    tq = st_a.shape[1]
    kh = PEER_ACC_ROWS // per
    c3 = jnp.maximum(g - 2, 0) % nc

    @pl.when(g == 0)
    def _():
        st_b[...] = jnp.zeros_like(st_b)
        hbuf_a[...] = jnp.zeros_like(hbuf_a)
        hbuf_b[...] = jnp.zeros_like(hbuf_b)

    @pl.when(c3 == 0)
    def _():
        acc[...] = jnp.zeros_like(acc)

    def step(st_w, st_r, hbuf_w, hbuf_r):
        def group(j, carry):
            rows = pl.ds(pl.multiple_of(j * gw, gw), gw)
            acc[...] += jnp.dot(vt_ref[j], hbuf_r[rows, :], preferred_element_type=F32)
            st_w[rows, :] = jnp.dot(u_ref[rows, :], hbt_ref[...], preferred_element_type=F32)
            for q in range(tq // V7X_LANES):
                lanes = slice(q * V7X_LANES, (q + 1) * V7X_LANES)
                for half in range(nk // kh):
                    keys = slice(half * kh, (half + 1) * kh)
                    w = [None] * per
                    for hd in range(nh):
                        s1 = s1_ref[hd, keys, lanes]
                        e1 = e1_ref[hd, keys, lanes]
                        for i in range(per):
                            sel = jnp.where(s1 >= th_ref[hd, j, i:i + 1, lanes], e1, 0.0) * e0_ref[hd, j, i:i + 1, lanes]
                            w[i] = sel if w[i] is None else w[i] + sel
                    for i in range(per):
                        tile = pl.ds(pl.multiple_of(j * gw + i * nk + half * kh, kh), kh)
                        hbuf_w[tile, lanes] = (_gelu(st_r[tile, lanes]) * w[i]).astype(BF16)
            return carry

        lax.fori_loop(0, ngroups, group, 0)

    @pl.when(g % 2 == 0)
    def _():
        step(st_a, st_b, hbuf_b, hbuf_a)

    @pl.when(g % 2 == 1)
    def _():
        step(st_b, st_a, hbuf_a, hbuf_b)

    @pl.when(jnp.logical_and(g >= 2, c3 == nc - 1))
    def _():
        o_ref[...] = x_ref[...] + mod_ref[5:6, :] * acc[...].T


def _peer_expert(hb, u_tab, vt_tab, th, e0, s1, e1, x, mod, *, n_ctx, dec_seq, tq, ec):
    t, d = x.shape
    ne = u_tab.shape[0]
    nh, nk, _ = s1.shape
    per = th.shape[2]
    assert per * nk == PEER_GROUP
    ngroups = ec // PEER_GROUP
    nc = ne // ec
    npairs = (t // tq) * nc

    def pair(g, lag):
        p = jnp.clip(g - lag, 0, npairs - 1)
        return p // nc, p % nc

    def cond3(g):
        return _cond_index(pair(g, 2)[0] * tq, n_ctx, dec_seq)

    est = (2 * tq * d * 2 + 4 * ec * d * 2 + 4 * nh * ngroups * V7X_SUBLANES * tq * 4 + 4 * nh * nk * tq * 4
           + 4 * tq * d * 4 + d * tq * 4 + 2 * ec * tq * 2 + 2 * ec * tq * 4 + 3 * PEER_GROUP * tq * 4)
    return pl.pallas_call(
        functools.partial(_peer_expert_kernel, nc=nc),
        grid=(npairs + 2,),
        in_specs=[pl.BlockSpec((d, tq), lambda g: (0, pair(g, 0)[0])),
                  pl.BlockSpec((ec, d), lambda g: (pair(g, 0)[1], 0)),
                  pl.BlockSpec((ngroups, d, PEER_GROUP), lambda g: (pair(g, 2)[1], 0, 0)),
                  pl.BlockSpec((nh, ngroups, per, tq), lambda g: (0, pair(g, 1)[1], 0, pair(g, 1)[0])),
                  pl.BlockSpec((nh, ngroups, per, tq), lambda g: (0, pair(g, 1)[1], 0, pair(g, 1)[0])),
                  pl.BlockSpec((nh, nk, tq), lambda g: (0, 0, pair(g, 1)[0])),
                  pl.BlockSpec((nh, nk, tq), lambda g: (0, 0, pair(g, 1)[0])),
                  pl.BlockSpec((tq, d), lambda g: (pair(g, 2)[0], 0)),
                  pl.BlockSpec((None, 6, d), lambda g: (cond3(g), 0, 0))],
        out_specs=pl.BlockSpec((tq, d), lambda g: (pair(g, 2)[0], 0)),
        out_shape=jax.ShapeDtypeStruct((t, d), F32),
        scratch_shapes=[pltpu.VMEM((d, tq), F32), pltpu.VMEM((ec, tq), F32), pltpu.VMEM((ec, tq), F32),
                        pltpu.VMEM((ec, tq), BF16), pltpu.VMEM((ec, tq), BF16)],
        compiler_params=_cparams(est, ("arbitrary",)),
        name="peer_expert",
    )(hb.T, u_tab, vt_tab, th, e0, s1, e1, x, mod)


def _final_kernel(x_ref, g_ref, o_ref):
    x = x_ref[...]
    ms = jnp.mean(x * x, axis=-1, keepdims=True)
    o_ref[...] = (x * lax.rsqrt(ms + EPS)) * g_ref[...]


def _final_norm(x, g, *, tm):
    t, d = x.shape
    return pl.pallas_call(
        _final_kernel,
        grid=(t // tm,),
        in_specs=[pl.BlockSpec((tm, d), lambda i: (i, 0)), pl.BlockSpec((1, d), lambda i: (0, 0))],
        out_specs=pl.BlockSpec((tm, d), lambda i: (i, 0)),
        out_shape=jax.ShapeDtypeStruct((t, d), F32),
        compiler_params=_cparams(6 * tm * d * 4, ("parallel",)),
        name="final_norm",
    )(x, g)


def _block_diag(w):
    h, i, j = w.shape
    return jnp.einsum('hij,hg->higj', w, jnp.eye(h, dtype=w.dtype)).reshape(h * i, h * j)


def _s5_discretize(a_re, a_im, log_dt, b_re, b_im):
    dt = jnp.exp(log_dt)[:, None]
    mag = jnp.exp(a_re * dt)
    abar_r = mag * jnp.cos(a_im * dt)
    abar_i = mag * jnp.sin(a_im * dt)
    den = a_re * a_re + a_im * a_im
    num_r = abar_r - 1.0
    coef_r = ((num_r * a_re + abar_i * a_im) / den)[..., None]
    coef_i = ((abar_i * a_re - num_r * a_im) / den)[..., None]
    return abar_r, abar_i, coef_r * b_re - coef_i * b_im, coef_r * b_im + coef_i * b_re


def _s5_blocks(abar_r, abar_i, bbar_r, bbar_i, c_re, c_im):
    g, p, ch = bbar_r.shape
    gb = g // S5_COLBLK

    def vec(v):
        return v.reshape(S5_COLBLK, 1, gb * p)

    def bmat(b):
        bt = b.transpose(0, 2, 1).reshape(S5_COLBLK, gb, ch, p)
        return jnp.stack([_block_diag(bt[j]) for j in range(S5_COLBLK)])

    def cmat(cc):
        ct = cc.transpose(0, 2, 1).reshape(S5_COLBLK, gb, p, ch)
        return jnp.stack([_block_diag(ct[j]) for j in range(S5_COLBLK)])

    bblk = jnp.concatenate([bmat(bbar_r), bmat(bbar_i)], axis=2).astype(BF16)
    cblk = jnp.concatenate([cmat(c_re), cmat(-c_im)], axis=1).astype(BF16)
    return bblk, cblk, vec(abar_r), vec(abar_i)


def _state_to_blocked(re, im):
    n, g, p = re.shape
    gb = g // S5_COLBLK
    both = jnp.stack([re.reshape(n, S5_COLBLK, gb * p), im.reshape(n, S5_COLBLK, gb * p)], axis=2)
    return both.reshape(n, -1)


def _state_from_blocked(st, g, p):
    n = st.shape[0]
    both = st.reshape(n, S5_COLBLK, 2, (g // S5_COLBLK) * p)
    return both[:, :, 0].reshape(n, g, p), both[:, :, 1].reshape(n, g, p)


def _complex_power(ar, ai, n):
    for _ in range(int(math.log2(n))):
        ar, ai = ar * ar - ai * ai, 2.0 * ar * ai
    return ar, ai


def _mixer_ab(x, mod, p, le, h0, *, bc, lc, bs, ls, tm):
    n_ctx = bc * lc
    width = p['s5_d'].shape[1]
    g_cnt, p_cnt = p['s5_a_re'].shape[2:]
    w_gates = jnp.concatenate([_block_diag(p['lru_w_a'][le, 0]), _block_diag(p['lru_w_x'][le, 0]),
                               _block_diag(p['lru_w_a'][le, 1]), _block_diag(p['lru_w_x'][le, 1])],
                              axis=1).astype(BF16)
    b_gates = jnp.concatenate([p['lru_b_a'][le, 0], p['lru_b_x'][le, 0],
                               p['lru_b_a'][le, 1], p['lru_b_x'][le, 1]]).reshape(1, 4 * width)
    c8 = LRU_C * jax.nn.softplus(-p['lru_lambda'][le])
    s_arr, xg = _ab_in(x, mod, p['norm1_g'][2 * le].reshape(1, -1), p['w_in_ab'][le].astype(BF16),
                       p['lru_conv_w'][le], p['lru_conv_b'][le].reshape(1, width), w_gates, b_gates, c8,
                       n_ctx=n_ctx, ctx_row=lc, dec_seq=ls, tm=tm)

    cols = s_arr.shape[1]
    gc = bc // SCAN_SEQS
    lseg = ls // SAMPLE_SEGS
    s_ctx = (s_arr[:n_ctx].reshape(gc, SCAN_SEQS, lc, cols).transpose(0, 2, 1, 3)
             .reshape(n_ctx, cols))
    s_smp = (s_arr[n_ctx:].reshape(bs * SAMPLE_SEGS, lseg, cols).transpose(1, 0, 2)
             .reshape(bs * ls, cols))
    chunk_rows = min(1024, lc * SCAN_SEQS, lseg * SCAN_SEQS)

    outs = {}
    finals = {}
    for d in range(2):
        abar_r, abar_i, bbar_r, bbar_i = _s5_discretize(
            p['s5_a_re'][le, d], p['s5_a_im'][le, d], p['s5_log_dt'][le, d],
            p['s5_b_re'][le, d], p['s5_b_im'][le, d])
        bblk, cblk, ar, ai = _s5_blocks(abar_r, abar_i, bbar_r, bbar_i,
                                        p['s5_c_re'][le, d], p['s5_c_im'][le, d])
        nstate = 2 * g_cnt * p_cnt
        z5 = jnp.zeros((gc, SCAN_SEQS, nstate), F32)
        zl = jnp.zeros((gc, SCAN_SEQS, width), F32)
        y, hl, f5, fl, _ = _scan(s_ctx, bblk, cblk, ar, ai, z5, zl, direction=d, store=True,
                                 groups=gc, chunk_rows=chunk_rows)
        outs[('ctx', d)] = (y, hl)
        finals[d] = (f5.reshape(bc, nstate), fl.reshape(bc, width))
        z5 = jnp.zeros((1, SCAN_SEQS, nstate), F32)
        zl = jnp.zeros((1, SCAN_SEQS, width), F32)
        e5, el, ep = _scan(s_smp, bblk, cblk, ar, ai, z5, zl, direction=d, store=False,
                           groups=1, chunk_rows=chunk_rows)
        first = (SAMPLE_SEGS - 1) if d == 1 else 0
        h5 = _state_to_blocked(h0[0][:, d], h0[1][:, d])
        h5_rows = jnp.zeros((bs, SAMPLE_SEGS, nstate), F32).at[:, first].set(h5).reshape(SCAN_SEQS, nstate)
        hl_rows = jnp.zeros((bs, SAMPLE_SEGS, width), F32).at[:, first].set(h0[2][:, d]).reshape(SCAN_SEQS, width)
        arl, ail = _complex_power(ar, ai, lseg)
        i5, il = _chain(e5[0], el[0], ep[0], h5_rows, hl_rows, arl, ail, direction=d)
        y, hl, _, _, _ = _scan(s_smp, bblk, cblk, ar, ai, i5[None], il[None], direction=d, store=True,
                               groups=1, chunk_rows=chunk_rows)
        outs[('smp', d)] = (y, hl)

    def to_tokens(ctx_arr, smp_arr):
        c = ctx_arr.reshape(gc, lc, SCAN_SEQS, width).transpose(0, 2, 1, 3).reshape(n_ctx, width)
        s = smp_arr.reshape(lseg, bs * SAMPLE_SEGS, width).transpose(1, 0, 2).reshape(bs * ls, width)
        return jnp.concatenate([c, s], axis=0)

    yf = to_tokens(outs[('ctx', 0)][0], outs[('smp', 0)][0])
    yb = to_tokens(outs[('ctx', 1)][0], outs[('smp', 1)][0])
    hf = to_tokens(outs[('ctx', 0)][1], outs[('smp', 0)][1])
    hb = to_tokens(outs[('ctx', 1)][1], outs[('smp', 1)][1])
    x1 = _ab_out(x, mod, s_arr, xg, yf, yb, hf, hb, p['s5_d'][le].reshape(1, width),
                 p['s5_w_glu'][le].astype(BF16), p['s5_b_glu'][le].reshape(1, width),
                 p['w_out_ab'][le].astype(BF16), n_ctx=n_ctx, dec_seq=ls, tm=tm)
    st = [_state_from_blocked(finals[d][0], g_cnt, p_cnt) for d in range(2)]
    new_re = jnp.stack([st[0][0], st[1][0]], axis=1)
    new_im = jnp.stack([st[0][1], st[1][1]], axis=1)
    new_lru = jnp.stack([finals[0][1], finals[1][1]], axis=1)
    return x1, (new_re, new_im, new_lru)


def _fft_split(length):
    n = 2 * length
    n1 = 2
    while n // n1 > 1024:
        n1 *= 2
    return n1


def _mixer_c(x, mod, p, lo, *, bc, lc, bs, ls, tm):
    n_ctx = bc * lc
    width = p['hy_bias'].shape[1]
    x0, vg = _c_in(x, mod, p['norm1_g'][2 * lo + 1].reshape(1, -1), p['w_in_c'][lo].astype(BF16),
                   p['b_in_c'][lo].reshape(1, -1), p['hy_conv_w'][lo], p['hy_conv_b'][lo].reshape(1, -1),
                   n_ctx=n_ctx, ctx_row=lc, dec_seq=ls, tm=tm)
    ys = []
    for (b, l, v) in ((bc, lc, vg[:n_ctx]), (bs, ls, vg[n_ctx:])):
        circ = _hy_filter(l, p['hy_w1'][lo], p['hy_b1'][lo], p['hy_freq1'][lo], p['hy_w2'][lo],
                          p['hy_b2'][lo], p['hy_freq2'][lo], p['hy_w3'][lo], p['hy_decay'][lo],
                          ct=min(width, 256 if l <= 1024 else 128))
        y = _fftconv(v.reshape(b, l, width), circ, n1=_fft_split(l), ct=min(width, 256 if l <= 1024 else 128))
        ys.append(y.reshape(b * l, width))
    yconv = jnp.concatenate(ys, axis=0)
    return _c_out(x, mod, x0, vg, yconv, p['hy_bias'][lo].reshape(1, width), p['w_out_c'][lo].astype(BF16),
                  p['b_out_c'][lo].reshape(1, -1), n_ctx=n_ctx, dec_seq=ls, tm=tm)


def _peer(x, mod, p, l, *, n_ctx, dec_seq, tq, ec):
    keys = p['peer_keys'][l]
    nh, _, nk, dkh = keys.shape
    hb, th, e0, s1, e1 = _peer_route(x, mod, p['norm2_g'][l].reshape(1, -1), p['peer_wq'][l].astype(BF16),
                                     keys.reshape(nh * 2, nk, dkh).astype(BF16),
                                     n_ctx=n_ctx, dec_seq=dec_seq, tq=tq)
    ne, d = p['peer_v'][l].shape
    vt = p['peer_v'][l].reshape(ne // PEER_GROUP, PEER_GROUP, d).transpose(0, 2, 1).astype(BF16)
    return _peer_expert(hb, p['peer_u'][l].astype(BF16), vt, th, e0, s1, e1,
                        x, mod, n_ctx=n_ctx, dec_seq=dec_seq, tq=tq, ec=ec)


def _forward(x_prompt, x_sample, state_s5_re, state_s5_im, state_lru, c, c_ctx, p, *, tm, tq, ec):
    bc, lc, d = x_prompt.shape
    bs, ls, _ = x_sample.shape
    assert bc % SCAN_SEQS == 0 and bs * SAMPLE_SEGS == SCAN_SEQS
    assert tm % lc == 0 and ls % tm == 0 and (ls // SAMPLE_SEGS) % GRID_W == 0
    n_ctx = bc * lc
    x = jnp.concatenate([x_prompt.reshape(n_ctx, d), x_sample.reshape(bs * ls, d)], axis=0)
    cond = jnp.zeros((V7X_SUBLANES, d), F32).at[0].set(c_ctx).at[1:1 + bs].set(c)
    mod_all = _modulation(cond, p['w_mod'], p['b_mod'])
    states = None
    for l in range(DEPTH):
        mod = mod_all[l].reshape(V7X_SUBLANES, 6, d)
        if l % 2 == 0:
            le = l // 2
            h0 = (state_s5_re[:, le], state_s5_im[:, le], state_lru[:, le])
            x, st = _mixer_ab(x, mod, p, le, h0, bc=bc, lc=lc, bs=bs, ls=ls, tm=tm)
            states = st if states is None else states
        else:
            x = _mixer_c(x, mod, p, l // 2, bc=bc, lc=lc, bs=bs, ls=ls, tm=tm)
        x = _peer(x, mod, p, l, n_ctx=n_ctx, dec_seq=ls, tq=tq, ec=ec)
    y = _final_norm(x, p['final_g'].reshape(1, d), tm=tm)
    new_re, new_im, new_lru = states
    return (y[:n_ctx].reshape(bc, lc, d), y[n_ctx:].reshape(bs, ls, d),
            new_re[:, None], new_im[:, None], new_lru[:, None])


def kernel(x_prompt, x_sample, state_s5_re, state_s5_im, state_lru, c, c_ctx, norm1_g, norm2_g, w_mod, b_mod, w_in_ab, s5_a_re, s5_a_im, s5_log_dt, s5_b_re, s5_b_im, s5_c_re, s5_c_im, s5_d, s5_w_glu, s5_b_glu, lru_conv_w, lru_conv_b, lru_w_a, lru_b_a, lru_w_x, lru_b_x, lru_lambda, w_out_ab, w_in_c, b_in_c, hy_conv_w, hy_conv_b, hy_w1, hy_b1, hy_freq1, hy_w2, hy_b2, hy_freq2, hy_w3, hy_decay, hy_bias, w_out_c, b_out_c, peer_wq, peer_keys, peer_u, peer_v, final_g):
    p = dict(norm1_g=norm1_g, norm2_g=norm2_g, w_mod=w_mod, b_mod=b_mod, w_in_ab=w_in_ab,
             s5_a_re=s5_a_re, s5_a_im=s5_a_im, s5_log_dt=s5_log_dt, s5_b_re=s5_b_re, s5_b_im=s5_b_im,
             s5_c_re=s5_c_re, s5_c_im=s5_c_im, s5_d=s5_d, s5_w_glu=s5_w_glu, s5_b_glu=s5_b_glu,
             lru_conv_w=lru_conv_w, lru_conv_b=lru_conv_b, lru_w_a=lru_w_a, lru_b_a=lru_b_a,
             lru_w_x=lru_w_x, lru_b_x=lru_b_x, lru_lambda=lru_lambda, w_out_ab=w_out_ab,
             w_in_c=w_in_c, b_in_c=b_in_c, hy_conv_w=hy_conv_w, hy_conv_b=hy_conv_b,
             hy_w1=hy_w1, hy_b1=hy_b1, hy_freq1=hy_freq1, hy_w2=hy_w2, hy_b2=hy_b2, hy_freq2=hy_freq2,
             hy_w3=hy_w3, hy_decay=hy_decay, hy_bias=hy_bias, w_out_c=w_out_c, b_out_c=b_out_c,
             peer_wq=peer_wq, peer_keys=peer_keys, peer_u=peer_u, peer_v=peer_v, final_g=final_g)
    return _forward(x_prompt, x_sample, state_s5_re, state_s5_im, state_lru, c, c_ctx, p,
                    tm=512, tq=512, ec=2048)
```

```python
import functools
import math

import jax
import jax.numpy as jnp
import numpy as np
from jax import lax
from jax.experimental import pallas as pl
from jax.experimental.pallas import tpu as pltpu

F32 = jnp.float32
BF16 = jnp.bfloat16
HIGHEST = lax.Precision.HIGHEST

DEPTH = 2
GRID_W = 64
EPS = 1e-6
S5_CH = 16
S5_STATE = 64
LRU_HEADS = 8
LRU_C = 8.0
HY_BANDS = 16
HY_SHIFT = 0.05
PEER_HEADS = 8
PEER_NKEYS = 128
PEER_TOPK = 16
PEER_GROUP = 512
PEER_ACC_ROWS = 256

V7X_LANES = 128
V7X_SUBLANES = 8
V7X_VMEM_BYTES = 64 * 1024 * 1024

SCAN_SEQS = V7X_SUBLANES
SAMPLE_SEGS = 4
S5_COLBLK = 4


def _vmem_limit(est_bytes):
    return int(min(V7X_VMEM_BYTES - (8 << 20), max(32 << 20, est_bytes * 3 // 2)))


def _cparams(est_bytes, sem):
    return pltpu.CompilerParams(dimension_semantics=sem, vmem_limit_bytes=_vmem_limit(est_bytes))


def _gelu(x):
    k = 2.0 * 0.7978845608028654 * math.log2(math.e)
    return x / (1.0 + jnp.exp2(x * (-k - (k * 0.044715) * (x * x))))


def _norm_mod(x, g, sc, sh):
    ms = jnp.mean(x * x, axis=-1, keepdims=True)
    return (x * lax.rsqrt(ms + EPS)) * g * (1.0 + sc) + sh


def _cond_index(row0, n_ctx, dec_seq):
    return jnp.where(row0 < n_ctx, 0, 1 + (row0 - n_ctx) // dec_seq)


def _mod_kernel(c_ref, w_ref, b_ref, o_ref):
    c = c_ref[...]
    s = c * jax.nn.sigmoid(c)
    o_ref[...] = jnp.dot(s, w_ref[...], precision=HIGHEST, preferred_element_type=F32) + b_ref[...]


def _modulation(cond, w_mod, b_mod):
    depth, d, n = w_mod.shape
    rows = cond.shape[0]
    tn = 1536
    est = 2 * d * tn * 4 + 4 * rows * (d + tn) * 4
    return pl.pallas_call(
        _mod_kernel,
        grid=(depth, n // tn),
        in_specs=[pl.BlockSpec((rows, d), lambda l, j: (0, 0)),
                  pl.BlockSpec((None, d, tn), lambda l, j: (l, 0, j)),
                  pl.BlockSpec((None, 1, tn), lambda l, j: (l, 0, j))],
        out_specs=pl.BlockSpec((None, rows, tn), lambda l, j: (l, 0, j)),
        out_shape=jax.ShapeDtypeStruct((depth, rows, n), F32),
        compiler_params=_cparams(est, ("arbitrary", "arbitrary")),
        name="modulation",
    )(cond, w_mod, b_mod.reshape(depth, 1, n))


def _dwconv_rows(x, w, bias, pos, row_len, pad_left):
    tm = x.shape[0]
    acc = None
    for k in range(w.shape[0]):
        d = k - pad_left
        if d == 0:
            term = x
        else:
            shifted = pltpu.roll(x, (-d) % tm, axis=0)
            q = pos + d
            term = jnp.where(q >= 0, jnp.where(q < row_len, shifted, 0.0), 0.0)
        term = term * w[k:k + 1]
        acc = term if acc is None else acc + term
    return acc + bias


def _ab_in_kernel(x_ref, mod_ref, g_ref, w_ref, cw_ref, cb_ref, wg_ref, bg_ref, c8_ref,
                  s_ref, xg_ref, *, n_ctx, ctx_row, width):
    tm = x_ref.shape[0]
    m = mod_ref[...]
    h = _norm_mod(x_ref[...], g_ref[...], m[1:2], m[0:1])
    z = jnp.dot(h.astype(BF16), w_ref[...], preferred_element_type=F32)
    u = z[:, :width]
    xr = z[:, width:2 * width]
    xg_ref[...] = z[:, 2 * width:]
    row_len = jnp.where(pl.program_id(0) * tm < n_ctx, ctx_row, GRID_W)
    pos = lax.broadcasted_iota(jnp.int32, (tm, width), 0) & (row_len - 1)
    xb = _dwconv_rows(xr, cw_ref[...], cb_ref[...], pos, row_len, 2)
    gates = jnp.dot(xb.astype(BF16), wg_ref[...], preferred_element_type=F32) + bg_ref[...]
    s_ref[:, :width] = u
    c8 = c8_ref[...]
    for d in range(2):
        r = jax.nn.sigmoid(gates[:, (2 * d) * width:(2 * d + 1) * width])
        i = jax.nn.sigmoid(gates[:, (2 * d + 1) * width:(2 * d + 2) * width])
        a = jnp.exp(-(c8[d:d + 1] * r))
        b = jnp.sqrt(1.0 - a * a) * (i * xb)
        s_ref[:, (1 + 2 * d) * width:(2 + 2 * d) * width] = a
        s_ref[:, (2 + 2 * d) * width:(3 + 2 * d) * width] = b


def _ab_in(x, mod, g, w_in, conv_w, conv_b, w_gates, b_gates, c8, *, n_ctx, ctx_row, dec_seq, tm):
    t, d = x.shape
    width = w_in.shape[1] // 3
    kern = functools.partial(_ab_in_kernel, n_ctx=n_ctx, ctx_row=ctx_row, width=width)
    const = lambda i: (0, 0)
    est = (2 * tm * d * 4 + d * 3 * width * 2 * 2 + width * 4 * width * 2 * 2
           + 2 * tm * 6 * width * 4 + tm * 12 * width * 4)
    return pl.pallas_call(
        kern,
        grid=(t // tm,),
        in_specs=[pl.BlockSpec((tm, d), lambda i: (i, 0)),
                  pl.BlockSpec((None, 6, d), lambda i: (_cond_index(i * tm, n_ctx, dec_seq), 0, 0)),
                  pl.BlockSpec((1, d), const),
                  pl.BlockSpec(w_in.shape, const),
                  pl.BlockSpec(conv_w.shape, const),
                  pl.BlockSpec((1, width), const),
                  pl.BlockSpec(w_gates.shape, const),
                  pl.BlockSpec((1, 4 * width), const),
                  pl.BlockSpec((2, width), const)],
        out_specs=[pl.BlockSpec((tm, 5 * width), lambda i: (i, 0)),
                   pl.BlockSpec((tm, width), lambda i: (i, 0))],
        out_shape=[jax.ShapeDtypeStruct((t, 5 * width), F32),
                   jax.ShapeDtypeStruct((t, width), F32)],
        compiler_params=_cparams(est, ("parallel",)),
        name="ab_in",
    )(x, mod, g, w_in, conv_w, conv_b, w_gates, b_gates, c8)


def _scan_kernel(u_ref, a_ref, b_ref, bblk_ref, cblk_ref, ar_ref, ai_ref, i5_ref, il_ref, *rest,
                 reverse, store):
    if store:
        y_ref, hl_ref, f5_ref, fl_ref, fp_ref, buf, st5, stl, stp = rest
    else:
        f5_ref, fl_ref, fp_ref, buf, st5, stl, stp = rest
    c = pl.program_id(1)
    rows = u_ref.shape[0]
    nsteps = rows // SCAN_SEQS
    sw = st5.shape[1] // (2 * S5_COLBLK)
    cw = u_ref.shape[1] // S5_COLBLK

    @pl.when(c == 0)
    def _():
        st5[...] = i5_ref[...]
        stl[...] = il_ref[...]
        stp[...] = jnp.ones_like(stp)

    def row0(k):
        t = (nsteps - 1 - k) if reverse else k
        return pl.multiple_of(t * SCAN_SEQS, SCAN_SEQS)

    u = u_ref[...].astype(BF16)
    for j in range(S5_COLBLK):
        buf[...] = jnp.dot(u[:, j * cw:(j + 1) * cw], bblk_ref[j], preferred_element_type=F32)
        arj = jnp.broadcast_to(ar_ref[j], (SCAN_SEQS, sw))
        aij = jnp.broadcast_to(ai_ref[j], (SCAN_SEQS, sw))

        def step(k, carry, arj=arj, aij=aij):
            hr, hi = carry
            r0 = row0(k)
            xr = buf[pl.ds(r0, SCAN_SEQS), 0:sw]
            xi = buf[pl.ds(r0, SCAN_SEQS), sw:2 * sw]
            nr = arj * hr - aij * hi + xr
            ni = arj * hi + aij * hr + xi
            buf[pl.ds(r0, SCAN_SEQS), 0:sw] = nr
            buf[pl.ds(r0, SCAN_SEQS), sw:2 * sw] = ni
            return nr, ni

        base = 2 * sw * j
        hr, hi = lax.fori_loop(0, nsteps, step, (st5[:, base:base + sw], st5[:, base + sw:base + 2 * sw]))
        st5[:, base:base + sw] = hr
        st5[:, base + sw:base + 2 * sw] = hi
        if store:
            y_ref[:, j * cw:(j + 1) * cw] = jnp.dot(buf[...].astype(BF16), cblk_ref[j],
                                                    preferred_element_type=F32)

    def lstep(k, carry):
        h, p = carry
        r0 = row0(k)
        a = a_ref[pl.ds(r0, SCAN_SEQS), :]
        h = a * h + b_ref[pl.ds(r0, SCAN_SEQS), :]
        if store:
            hl_ref[pl.ds(r0, SCAN_SEQS), :] = h
        return h, p * a

    h, p = lax.fori_loop(0, nsteps, lstep, (stl[...], stp[...]))
    stl[...] = h
    stp[...] = p

    @pl.when(c == pl.num_programs(1) - 1)
    def _():
        f5_ref[...] = st5[...]
        fl_ref[...] = stl[...]
        fp_ref[...] = stp[...]


def _scan(s_arr, bblk, cblk, ar, ai, init5, initl, *, direction, store, groups, chunk_rows):
    rows_total, cols = s_arr.shape
    width = cols // 5
    rows_per_group = rows_total // groups
    nc = rows_per_group // chunk_rows
    reverse = direction == 1
    nstate = init5.shape[-1]

    def chunk(g, c):
        return g * nc + ((nc - 1 - c) if reverse else c)

    kern = functools.partial(_scan_kernel, reverse=reverse, store=store)
    const3 = lambda g, c: (0, 0, 0)
    in_specs = [pl.BlockSpec((chunk_rows, width), lambda g, c: (chunk(g, c), 0)),
                pl.BlockSpec((chunk_rows, width), lambda g, c: (chunk(g, c), 1 + 2 * direction)),
                pl.BlockSpec((chunk_rows, width), lambda g, c: (chunk(g, c), 2 + 2 * direction)),
                pl.BlockSpec(bblk.shape, const3),
                pl.BlockSpec(cblk.shape, const3),
                pl.BlockSpec(ar.shape, const3),
                pl.BlockSpec(ai.shape, const3),
                pl.BlockSpec((None, SCAN_SEQS, nstate), lambda g, c: (g, 0, 0)),
                pl.BlockSpec((None, SCAN_SEQS, width), lambda g, c: (g, 0, 0))]
    fin_specs = [pl.BlockSpec((None, SCAN_SEQS, nstate), lambda g, c: (g, 0, 0)),
                 pl.BlockSpec((None, SCAN_SEQS, width), lambda g, c: (g, 0, 0)),
                 pl.BlockSpec((None, SCAN_SEQS, width), lambda g, c: (g, 0, 0))]
    fin_shapes = [jax.ShapeDtypeStruct((groups, SCAN_SEQS, nstate), F32),
                  jax.ShapeDtypeStruct((groups, SCAN_SEQS, width), F32),
                  jax.ShapeDtypeStruct((groups, SCAN_SEQS, width), F32)]
    out_specs, out_shapes = fin_specs, fin_shapes
    if store:
        out_specs = [pl.BlockSpec((chunk_rows, width), lambda g, c: (chunk(g, c), 0))] * 2 + fin_specs
        out_shapes = [jax.ShapeDtypeStruct((rows_total, width), F32)] * 2 + fin_shapes
    bufw = 2 * nstate // (2 * S5_COLBLK)
    est = (10 * chunk_rows * width * 4 + chunk_rows * bufw * 4 * 2
           + 4 * bblk.size * 2 + 8 * SCAN_SEQS * nstate * 4)
    return pl.pallas_call(
        kern,
        grid=(groups, nc),
        in_specs=in_specs,
        out_specs=out_specs,
        out_shape=out_shapes,
        scratch_shapes=[pltpu.VMEM((chunk_rows, bufw), F32),
                        pltpu.VMEM((SCAN_SEQS, nstate), F32),
                        pltpu.VMEM((SCAN_SEQS, width), F32),
                        pltpu.VMEM((SCAN_SEQS, width), F32)],
        compiler_params=_cparams(est, ("arbitrary", "arbitrary")),
        name="scan_store" if store else "scan_ends",
    )(s_arr, s_arr, s_arr, bblk, cblk, ar, ai, init5, initl)


def _chain_kernel(e5_ref, el_ref, ep_ref, h5_ref, hl_ref, arl_ref, ail_ref, i5_ref, il_ref, *, reverse):
    nseg = SAMPLE_SEGS
    sw = arl_ref.shape[-1]
    shift = (SCAN_SEQS - 1) if reverse else 1
    order = range(nseg - 2, -1, -1) if reverse else range(1, nseg)
    seg5 = lax.broadcasted_iota(jnp.int32, (SCAN_SEQS, sw), 0) % nseg
    for j in range(S5_COLBLK):
        base = 2 * sw * j
        re, im = slice(base, base + sw), slice(base + sw, base + 2 * sw)
        ar, ai = arl_ref[j], ail_ref[j]
        er = pltpu.roll(e5_ref[:, re], shift, axis=0)
        ei = pltpu.roll(e5_ref[:, im], shift, axis=0)
        hr, hi = h5_ref[:, re], h5_ref[:, im]
        for k in order:
            pr = pltpu.roll(hr, shift, axis=0)
            pi = pltpu.roll(hi, shift, axis=0)
            hr = jnp.where(seg5 == k, ar * pr - ai * pi + er, hr)
            hi = jnp.where(seg5 == k, ar * pi + ai * pr + ei, hi)
        i5_ref[:, re] = hr
        i5_ref[:, im] = hi
    segl = lax.broadcasted_iota(jnp.int32, el_ref.shape, 0) % nseg
    el = pltpu.roll(el_ref[...], shift, axis=0)
    ep = pltpu.roll(ep_ref[...], shift, axis=0)
    h = hl_ref[...]
    for k in order:
        h = jnp.where(segl == k, ep * pltpu.roll(h, shift, axis=0) + el, h)
    il_ref[...] = h


def _chain(end5, endl, endp, h5_rows, hl_rows, arl, ail, *, direction):
    kern = functools.partial(_chain_kernel, reverse=direction == 1)
    return pl.pallas_call(
        kern,
        out_shape=[jax.ShapeDtypeStruct(end5.shape, F32), jax.ShapeDtypeStruct(endl.shape, F32)],
        name="scan_chain",
    )(end5, endl, endp, h5_rows, hl_rows, arl, ail)


def _ab_out_kernel(x_ref, mod_ref, s_ref, xg_ref, yf_ref, yb_ref, hf_ref, hb_ref, d_ref,
                   wglu_ref, bglu_ref, wout_ref, o_ref):
    width = xg_ref.shape[1]
    m = mod_ref[...]
    y = s_ref[...] * d_ref[...] + yf_ref[...] + yb_ref[...]
    zs = _gelu(y)
    gate = jnp.dot(zs.astype(BF16), wglu_ref[...], preferred_element_type=F32) + bglu_ref[...]
    s5_out = zs * jax.nn.sigmoid(gate)
    lru_out = (hf_ref[...] + hb_ref[...]) * _gelu(xg_ref[...])
    out = (jnp.dot(s5_out.astype(BF16), wout_ref[:width, :], preferred_element_type=F32)
           + jnp.dot(lru_out.astype(BF16), wout_ref[width:, :], preferred_element_type=F32))
    o_ref[...] = x_ref[...] + m[2:3] * out


def _ab_out(x, mod, s_arr, xg, yf, yb, hf, hb, s5_d, w_glu, b_glu, w_out, *, n_ctx, dec_seq, tm):
    t, d = x.shape
    width = xg.shape[1]
    const = lambda i: (0, 0)
    row = lambda i: (i, 0)
    est = 4 * tm * d * 4 + 12 * tm * width * 4 + 4 * (width * width + 2 * width * d) + 8 * tm * width * 4
    return pl.pallas_call(
        _ab_out_kernel,
        grid=(t // tm,),
        in_specs=[pl.BlockSpec((tm, d), row),
                  pl.BlockSpec((None, 6, d), lambda i: (_cond_index(i * tm, n_ctx, dec_seq), 0, 0)),
                  pl.BlockSpec((tm, width), row),
                  pl.BlockSpec((tm, width), row),
                  pl.BlockSpec((tm, width), row),
                  pl.BlockSpec((tm, width), row),
                  pl.BlockSpec((tm, width), row),
                  pl.BlockSpec((tm, width), row),
                  pl.BlockSpec((1, width), const),
                  pl.BlockSpec(w_glu.shape, const),
                  pl.BlockSpec((1, width), const),
                  pl.BlockSpec(w_out.shape, const)],
        out_specs=pl.BlockSpec((tm, d), row),
        out_shape=jax.ShapeDtypeStruct((t, d), F32),
        compiler_params=_cparams(est, ("parallel",)),
        name="ab_out",
    )(x, mod, s_arr, xg, yf, yb, hf, hb, s5_d, w_glu, b_glu, w_out)


def _c_in_kernel(x_ref, mod_ref, g_ref, w_ref, b_ref, cw_ref, cb_ref, x0_ref, vg_ref, *, n_ctx, ctx_row):
    tm = x_ref.shape[0]
    width = x0_ref.shape[1]
    m = mod_ref[...]
    h = _norm_mod(x_ref[...], g_ref[...], m[1:2], m[0:1])
    z = jnp.dot(h.astype(BF16), w_ref[...], preferred_element_type=F32) + b_ref[...]
    row_len = jnp.where(pl.program_id(0) * tm < n_ctx, ctx_row, GRID_W)
    pos = lax.broadcasted_iota(jnp.int32, (tm, width), 0) & (row_len - 1)
    cw = cw_ref[...]
    cb = cb_ref[...]
    parts = [_dwconv_rows(z[:, k * width:(k + 1) * width], cw[:, k * width:(k + 1) * width],
                          cb[:, k * width:(k + 1) * width], pos, row_len, 1) for k in range(3)]
    x0_ref[...] = parts[0]
    vg_ref[...] = parts[2] * parts[1]


def _c_in(x, mod, g, w_in, b_in, conv_w, conv_b, *, n_ctx, ctx_row, dec_seq, tm):
    t, d = x.shape
    width = w_in.shape[1] // 3
    kern = functools.partial(_c_in_kernel, n_ctx=n_ctx, ctx_row=ctx_row)
    const = lambda i: (0, 0)
    est = 2 * tm * d * 4 + 4 * d * 3 * width + 4 * tm * width * 4 + 6 * tm * 3 * width * 4
    return pl.pallas_call(
        kern,
        grid=(t // tm,),
        in_specs=[pl.BlockSpec((tm, d), lambda i: (i, 0)),
                  pl.BlockSpec((None, 6, d), lambda i: (_cond_index(i * tm, n_ctx, dec_seq), 0, 0)),
                  pl.BlockSpec((1, d), const),
                  pl.BlockSpec(w_in.shape, const),
                  pl.BlockSpec((1, 3 * width), const),
                  pl.BlockSpec(conv_w.shape, const),
                  pl.BlockSpec((1, 3 * width), const)],
        out_specs=[pl.BlockSpec((tm, width), lambda i: (i, 0))] * 2,
        out_shape=[jax.ShapeDtypeStruct((t, width), F32)] * 2,
        compiler_params=_cparams(est, ("parallel",)),
        name="c_in",
    )(x, mod, g, w_in, b_in, conv_w, conv_b)


def _c_out_kernel(x_ref, mod_ref, x0_ref, vg_ref, yc_ref, hb_ref, w_ref, b_ref, o_ref):
    m = mod_ref[...]
    vg = vg_ref[...]
    y = (yc_ref[...] + hb_ref[...] * vg) * x0_ref[...]
    out = jnp.dot(y.astype(BF16), w_ref[...], preferred_element_type=F32) + b_ref[...]
    o_ref[...] = x_ref[...] + m[2:3] * out


def _c_out(x, mod, x0, vg, yconv, hy_bias, w_out, b_out, *, n_ctx, dec_seq, tm):
    t, d = x.shape
    width = x0.shape[1]
    const = lambda i: (0, 0)
    row = lambda i: (i, 0)
    est = 4 * tm * d * 4 + 6 * tm * width * 4 + 4 * width * d + 4 * tm * width * 4
    return pl.pallas_call(
        _c_out_kernel,
        grid=(t // tm,),
        in_specs=[pl.BlockSpec((tm, d), row),
                  pl.BlockSpec((None, 6, d), lambda i: (_cond_index(i * tm, n_ctx, dec_seq), 0, 0)),
                  pl.BlockSpec((tm, width), row),
                  pl.BlockSpec((tm, width), row),
                  pl.BlockSpec((tm, width), row),
                  pl.BlockSpec((1, width), const),
                  pl.BlockSpec(w_out.shape, const),
                  pl.BlockSpec((1, d), const)],
        out_specs=pl.BlockSpec((tm, d), row),
        out_shape=jax.ShapeDtypeStruct((t, d), F32),
        compiler_params=_cparams(est, ("parallel",)),
        name="c_out",
    )(x, mod, x0, vg, yconv, hy_bias, w_out, b_out)


def _hy_filter_kernel(w1_ref, b1_ref, f1_ref, w2_ref, b2_ref, f2_ref, w3_ref, dec_ref, o_ref, z_scr, *, length):
    ct = o_ref.shape[1]
    nb = HY_BANDS
    ecols = w1_ref.shape[0]
    row = lax.broadcasted_iota(jnp.int32, (length, 1), 0)
    positions = (row, length - row)

    @pl.when(pl.program_id(0) == 0)
    def _():
        for side in range(2):
            posf = positions[side].astype(F32)
            col = lax.broadcasted_iota(jnp.int32, (length, ecols), 1)
            band_idx = jnp.where(col <= nb, col - 1, col - 1 - nb).astype(F32)
            bands = 1e-4 + band_idx * ((nb - 1 - 1e-4) / (nb - 1))
            ang = (2.0 * math.pi / length) * posf * bands
            emb = jnp.where(col == 0, posf / length,
                            jnp.where(col <= nb, jnp.cos(ang),
                                      jnp.where(col <= 2 * nb, -jnp.sin(ang), 0.0)))
            z = jnp.sin(f1_ref[...] * (jnp.dot(emb, w1_ref[...], precision=HIGHEST,
                                               preferred_element_type=F32) + b1_ref[...]))
            z_scr[side] = jnp.sin(f2_ref[...] * (jnp.dot(z, w2_ref[...], precision=HIGHEST,
                                                         preferred_element_type=F32) + b2_ref[...]))

    def half(side):
        t = positions[side].astype(F32) / length
        filt = jnp.dot(z_scr[side], w3_ref[side], precision=HIGHEST, preferred_element_type=F32)
        return filt * (jnp.exp(-t * jnp.abs(dec_ref[side:side + 1, :])) + HY_SHIFT)

    fwd = half(0)
    bwd = jnp.where(lax.broadcasted_iota(jnp.int32, (length, ct), 0) == 0, 0.0, half(1))
    ss = jnp.sum(fwd * fwd, axis=0, keepdims=True) + jnp.sum(bwd * bwd, axis=0, keepdims=True)
    scale = lax.rsqrt(ss + EPS)
    o_ref[:length, :] = fwd * scale
    o_ref[length:, :] = bwd * scale


def _hy_filter(length, w1, b1, f1, w2, b2, f2, w3, decay, *, ct):
    emb, ff = w1.shape
    width = decay.shape[1]
    ecols = 128
    w1p = jnp.zeros((ecols, ff), F32).at[:emb].set(w1)
    w3s = w3.reshape(ff, 2, width).transpose(1, 0, 2)
    kern = functools.partial(_hy_filter_kernel, length=length)
    const = lambda j: (0, 0)
    est = 10 * length * ct * 4 + 4 * length * ecols * 4
    return pl.pallas_call(
        kern,
        grid=(width // ct,),
        in_specs=[pl.BlockSpec((ecols, ff), const),
                  pl.BlockSpec((1, ff), const),
                  pl.BlockSpec((1, ff), const),
                  pl.BlockSpec((ff, ff), const),
                  pl.BlockSpec((1, ff), const),
                  pl.BlockSpec((1, ff), const),
                  pl.BlockSpec((2, ff, ct), lambda j: (0, 0, j)),
                  pl.BlockSpec((2, ct), lambda j: (0, j))],
        out_specs=pl.BlockSpec((2 * length, ct), lambda j: (0, j)),
        out_shape=jax.ShapeDtypeStruct((2 * length, width), F32),
        scratch_shapes=[pltpu.VMEM((2, length, ff), F32)],
        compiler_params=_cparams(est, ("arbitrary",)),
        name="hy_filter",
    )(w1p, b1.reshape(1, ff), f1.reshape(1, ff), w2, b2.reshape(1, ff), f2.reshape(1, ff), w3s, decay)


def _fftconv_kernel(ct_ref, st_ref, x_ref, k_ref, c_ref, s_ref, twr_ref, twi_ref, o_ref, *scratch, n1):
    kfr, kfi = scratch if scratch else (None, None)
    h1 = n1 // 2
    n2 = c_ref.shape[0]
    n = n1 * n2
    cm = c_ref[...]
    sm = s_ref[...]

    def dft(pairs, inverse):
        w = pairs[0][0].shape[1]
        parts = jnp.concatenate([p.astype(BF16) for pair in pairs for p in pair], axis=1)
        cp = jnp.dot(cm, parts, preferred_element_type=F32)
        sp = jnp.dot(sm, parts, preferred_element_type=F32)
        out = []
        for i in range(len(pairs)):
            ca, cb = cp[:, 2 * i * w:(2 * i + 1) * w], cp[:, (2 * i + 1) * w:(2 * i + 2) * w]
            sa, sb = sp[:, 2 * i * w:(2 * i + 1) * w], sp[:, (2 * i + 1) * w:(2 * i + 2) * w]
            out.append((ca - sb, cb + sa) if inverse else (ca + sb, cb - sa))
        return out

    def slab(t1):
        return slice(t1 * n2, (t1 + 1) * n2)

    twr1, twi1 = twr_ref[...], twi_ref[...]
    tw0 = (jnp.ones_like(twr1), jnp.zeros_like(twi1))

    def next_twiddle(tw):
        return tw[0] * twr1 - tw[1] * twi1, tw[0] * twi1 + tw[1] * twr1

    def filter_stage1(k1, tw):
        ar = ai = None
        for t1 in range(n1):
            kr = k_ref[slab(t1), :]
            tr, ti = ct_ref[k1, t1] * kr, -(st_ref[k1, t1] * kr)
            ar = tr if ar is None else ar + tr
            ai = ti if ai is None else ai + ti
        ar, ai = ar * (1.0 / n), ai * (1.0 / n)
        return ar * tw[0] + ai * tw[1], ai * tw[0] - ar * tw[1]

    if kfr is not None:
        @pl.when(pl.program_id(1) == 0)
        def _():
            def fbody(k1, tw):
                (kfr[k1], kfi[k1]), = dft([filter_stage1(k1, tw)], False)
                return next_twiddle(tw)
            lax.fori_loop(0, n1, fbody, tw0)

    o_ref[...] = jnp.zeros_like(o_ref)

    def body(k1, tw):
        twr, twi = tw
        ar = ai = None
        for t1 in range(h1):
            cc, ss = ct_ref[k1, t1], st_ref[k1, t1]
            xr, xi = x_ref[0, slab(t1), :], x_ref[1, slab(t1), :]
            tr, ti = cc * xr + ss * xi, cc * xi - ss * xr
            ar = tr if ar is None else ar + tr
            ai = ti if ai is None else ai + ti
        data = (ar * twr + ai * twi, ai * twr - ar * twi)
        if kfr is not None:
            (fr, fi), = dft([data], False)
            gr, gi = kfr[k1], kfi[k1]
        else:
            (fr, fi), (gr, gi) = dft([data, filter_stage1(k1, tw)], False)
        (br, bi), = dft([(fr * gr - fi * gi, fr * gi + fi * gr)], True)
        br, bi = br * twr - bi * twi, bi * twr + br * twi
        for t1 in range(h1):
            cc, ss = ct_ref[k1, t1], st_ref[k1, t1]
            o_ref[0, slab(t1), :] += cc * br - ss * bi
            o_ref[1, slab(t1), :] += cc * bi + ss * br
        return next_twiddle(tw)

    lax.fori_loop(0, n1, body, tw0)


def _fftconv(v, circ, *, n1, ct):
    bsz, length, width = v.shape
    n = 2 * length
    n2 = n // n1
    idx = np.arange(n2)
    ang = 2.0 * np.pi * ((idx[:, None] * idx[None, :]) % n2) / n2
    cmat = jnp.asarray(np.cos(ang), F32).astype(BF16)
    smat = jnp.asarray(np.sin(ang), F32).astype(BF16)
    tang = 2.0 * np.pi * idx / n
    twr = jnp.asarray(np.broadcast_to(np.cos(tang)[:, None], (n2, ct)), F32)
    twi = jnp.asarray(np.broadcast_to(np.sin(tang)[:, None], (n2, ct)), F32)
    i1 = np.arange(n1)
    ang1 = 2.0 * np.pi * ((i1[:, None] * i1[None, :]) % n1) / n1
    ctab = jnp.asarray(np.round(np.cos(ang1), 12), F32)
    stab = jnp.asarray(np.round(np.sin(ang1), 12), F32)
    kern = functools.partial(_fftconv_kernel, n1=n1)
    const = lambda j, p: (0, 0)
    once = pl.Buffered(1)
    npairs = bsz // 2
    scratch = [pltpu.VMEM((n1, n2, ct), F32)] * 2 if npairs > 1 else []
    est = (4 * 2 * length * ct * 4 + n * ct * 4 + 2 * n2 * n2 * 2 + 2 * n2 * ct * 4
           + len(scratch) * n * ct * 4 + (8 + 2 * n1) * n2 * ct * 4)
    return pl.pallas_call(
        kern,
        grid=(width // ct, npairs),
        in_specs=[pl.BlockSpec(memory_space=pltpu.SMEM),
                  pl.BlockSpec(memory_space=pltpu.SMEM),
                  pl.BlockSpec((2, length, ct), lambda j, p: (p, 0, j)),
                  pl.BlockSpec((n, ct), lambda j, p: (0, j), pipeline_mode=once),
                  pl.BlockSpec((n2, n2), const, pipeline_mode=once),
                  pl.BlockSpec((n2, n2), const, pipeline_mode=once),
                  pl.BlockSpec((n2, ct), const, pipeline_mode=once),
                  pl.BlockSpec((n2, ct), const, pipeline_mode=once)],
        out_specs=pl.BlockSpec((2, length, ct), lambda j, p: (p, 0, j)),
        out_shape=jax.ShapeDtypeStruct((bsz, length, width), F32),
        scratch_shapes=scratch,
        compiler_params=_cparams(est, ("arbitrary", "arbitrary")),
        name="fftconv",
    )(ctab, stab, v, circ, cmat, smat, twr, twi)


def _peer_route_kernel(x_ref, mod_ref, g_ref, wq_ref, keys_ref, hb_ref, th_ref, e0_ref, s1_ref, e1_ref,
                       q_scr, sc_scr, top_scr):
    tq = x_ref.shape[0]
    nk = PEER_NKEYS
    k = PEER_TOPK
    nhp = keys_ref.shape[0]
    dkh = keys_ref.shape[2]
    neg = -jnp.inf
    m = mod_ref[...]
    h = _norm_mod(x_ref[...], g_ref[...], m[4:5], m[3:4]).astype(BF16)
    hb_ref[...] = h
    q = jnp.dot(h, wq_ref[...], preferred_element_type=F32)
    for hp in range(nhp):
        q_scr[hp] = q[:, hp * dkh:(hp + 1) * dkh].astype(BF16)

    lw = V7X_LANES
    sub = V7X_SUBLANES
    assert k == 2 * sub
    ncand = k + 7 * sub + (k - sub)
    npad = sum(sub - k // (i + 1) for i in range(2, sub))
    iota_k = lax.broadcasted_iota(jnp.int32, (nk, lw), 0)
    iota_c = lax.broadcasted_iota(jnp.int32, (ncand, lw), 0)
    iota_8 = lax.broadcasted_iota(jnp.int32, (sub, lw), 0)
    iota_r = lax.broadcasted_iota(jnp.int32, (k, lw), 0)

    def topk(vals, iota, exact):
        top = jnp.zeros((k, lw), F32)
        for r in range(k):
            mx = jnp.max(vals, axis=0, keepdims=True)
            if exact:
                first = jnp.min(jnp.where(vals == mx, iota, vals.shape[0]), axis=0, keepdims=True)
                vals = jnp.where(iota == first, neg, vals)
            else:
                vals = jnp.where(vals == mx, neg, vals)
            top = jnp.where(iota_r == r, mx, top)
        return top, vals

    def tied(rest, expected):
        gone = jnp.sum(jnp.where(rest == neg, 1.0, 0.0), axis=0, keepdims=True)
        return jnp.where(gone != expected, 1.0, 0.0)

    def halves(hd, exact):
        flag = jnp.zeros((1, lw), F32)
        for p in range(2):
            for c in range(tq // lw):
                lanes = slice(c * lw, (c + 1) * lw)
                top, rest = topk(sc_scr[2 * hd + p, :, lanes], iota_k, exact)
                top_scr[2 * hd + p, :, lanes] = top
                flag = jnp.maximum(flag, tied(rest, float(k)))
        return jnp.max(flag)

    def pairs(hd, exact):
        flag = jnp.zeros((1, lw), F32)
        for c in range(tq // lw):
            lanes = slice(c * lw, (c + 1) * lw)
            a = top_scr[2 * hd, :, lanes]
            b = top_scr[2 * hd + 1, :, lanes]
            pieces = [a[0:1] + b, a[1:2] + b[0:sub]]
            for i in range(2, sub):
                pieces.append(jnp.where(iota_8 < k // (i + 1), a[i:i + 1] + b[0:sub], neg))
            pieces.append(a[sub:k] + b[0:1])
            vmax = a[0:1] + b[0:1]
            top, rest = topk(jnp.concatenate(pieces, axis=0), iota_c, exact)
            z = jnp.sum(jnp.exp(top - vmax), axis=0, keepdims=True)
            last = top[k - 1:k]
            nxt = jnp.max(rest, axis=0, keepdims=True)
            tau = jnp.where(nxt == neg, last, 0.5 * (last + nxt))
            flag = jnp.maximum(flag, tied(rest, float(k + npad)))
            s0 = sc_scr[2 * hd, :, lanes]
            s1 = sc_scr[2 * hd + 1, :, lanes]
            th = tau - s0
            e0 = jnp.exp(s0 - a[0:1]) / z
            per = th_ref.shape[2]
            for r in range(nk // per):
                th_ref[hd, r, :, lanes] = th[r * per:(r + 1) * per]
                e0_ref[hd, r, :, lanes] = e0[r * per:(r + 1) * per]
            s1_ref[hd, :, lanes] = s1
            e1_ref[hd, :, lanes] = jnp.exp(s1 - b[0:1])
        return jnp.max(flag)

    def head_body(hd, carry):
        for p in range(2):
            sc_scr[2 * hd + p] = lax.dot_general(keys_ref[2 * hd + p], q_scr[2 * hd + p], (((1,), (1,)), ((), ())),
                                                 preferred_element_type=F32)
        @pl.when(halves(hd, False) > 0.0)
        def _():
            halves(hd, True)

        @pl.when(pairs(hd, False) > 0.0)
        def _():
            pairs(hd, True)

        return carry

    lax.fori_loop(0, nhp // 2, head_body, 0)


def _peer_route(x, mod, g, wq, keys, *, n_ctx, dec_seq, tq):
    t, d = x.shape
    nhp, nk, dkh = keys.shape
    nh = nhp // 2
    const = lambda i: (0, 0)
    hshape = jax.ShapeDtypeStruct((nh, nk, t), F32)
    hspec = pl.BlockSpec((nh, nk, tq), lambda i: (0, 0, i))
    per = PEER_GROUP // nk
    gshape = jax.ShapeDtypeStruct((nh, nk // per, per, t), F32)
    gspec = pl.BlockSpec((nh, nk // per, per, tq), lambda i: (0, 0, 0, i))
    pad = V7X_SUBLANES // per
    est = (2 * tq * d * 4 + 4 * d * nhp * dkh + 2 * tq * d * 2 + (4 + 4 * pad) * nh * nk * tq * 4
           + 2 * nhp * nk * tq * 4 + 6 * tq * nhp * dkh * 4)
    return pl.pallas_call(
        _peer_route_kernel,
        grid=(t // tq,),
        in_specs=[pl.BlockSpec((tq, d), lambda i: (i, 0)),
                  pl.BlockSpec((None, 6, d), lambda i: (_cond_index(i * tq, n_ctx, dec_seq), 0, 0)),
                  pl.BlockSpec((1, d), const),
                  pl.BlockSpec(wq.shape, const),
                  pl.BlockSpec(keys.shape, lambda i: (0, 0, 0))],
        out_specs=[pl.BlockSpec((tq, d), lambda i: (i, 0)), gspec, gspec, hspec, hspec],
        out_shape=[jax.ShapeDtypeStruct((t, d), BF16), gshape, gshape, hshape, hshape],
        scratch_shapes=[pltpu.VMEM((nhp, tq, dkh), BF16),
                        pltpu.VMEM((nhp, nk, tq), F32),
                        pltpu.VMEM((nhp, PEER_TOPK, tq), F32)],
        compiler_params=_cparams(est, ("parallel",)),
        name="peer_route",
    )(x, mod, g, wq, keys)


def _peer_expert_kernel(hbt_ref, u_ref, vt_ref, th_ref, e0_ref, s1_ref, e1_ref, x_ref, mod_ref, o_ref,
                        acc, st_a, st_b, hbuf_a, hbuf_b, *, nc):
    g = pl.program_id(0)
    nh = th_ref.shape[0]
    nk = s1_ref.shape[1]
    ngroups, per = th_ref.shape[1], th_ref.shape[2]
    gw = per * nk
    tq = st_a.shape[1]
    kh = PEER_ACC_ROWS // per
    c3 = jnp.maximum(g - 2, 0) % nc

    @pl.when(g == 0)
    def _():
        st_b[...] = jnp.zeros_like(st_b)
        hbuf_a[...] = jnp.zeros_like(hbuf_a)
        hbuf_b[...] = jnp.zeros_like(hbuf_b)

    @pl.when(c3 == 0)
    def _():
        acc[...] = jnp.zeros_like(acc)

    def step(st_w, st_r, hbuf_w, hbuf_r):
        def group(j, carry):
            rows = pl.ds(pl.multiple_of(j * gw, gw), gw)
            acc[...] += jnp.dot(vt_ref[j], hbuf_r[rows, :], preferred_element_type=F32)
            st_w[rows, :] = jnp.dot(u_ref[rows, :], hbt_ref[...], preferred_element_type=F32)
            for q in range(tq // V7X_LANES):
                lanes = slice(q * V7X_LANES, (q + 1) * V7X_LANES)
                for half in range(nk // kh):
                    keys = slice(half * kh, (half + 1) * kh)
                    w = [None] * per
                    for hd in range(nh):
                        s1 = s1_ref[hd, keys, lanes]
                        e1 = e1_ref[hd, keys, lanes]
                        for i in range(per):
                            sel = jnp.where(s1 >= th_ref[hd, j, i:i + 1, lanes], e1, 0.0) * e0_ref[hd, j, i:i + 1, lanes]
                            w[i] = sel if w[i] is None else w[i] + sel
                    for i in range(per):
                        tile = pl.ds(pl.multiple_of(j * gw + i * nk + half * kh, kh), kh)
                        hbuf_w[tile, lanes] = (_gelu(st_r[tile, lanes]) * w[i]).astype(BF16)
            return carry

        lax.fori_loop(0, ngroups, group, 0)

    @pl.when(g % 2 == 0)
    def _():
        step(st_a, st_b, hbuf_b, hbuf_a)

    @pl.when(g % 2 == 1)
    def _():
        step(st_b, st_a, hbuf_a, hbuf_b)

    @pl.when(jnp.logical_and(g >= 2, c3 == nc - 1))
    def _():
        o_ref[...] = x_ref[...] + mod_ref[5:6, :] * acc[...].T


def _peer_expert(hb, u_tab, vt_tab, th, e0, s1, e1, x, mod, *, n_ctx, dec_seq, tq, ec):
    t, d = x.shape
    ne = u_tab.shape[0]
    nh, nk, _ = s1.shape
    per = th.shape[2]
    assert per * nk == PEER_GROUP
    ngroups = ec // PEER_GROUP
    nc = ne // ec
    npairs = (t // tq) * nc

    def pair(g, lag):
        p = jnp.clip(g - lag, 0, npairs - 1)
        return p // nc, p % nc

    def cond3(g):
        return _cond_index(pair(g, 2)[0] * tq, n_ctx, dec_seq)

    est = (2 * tq * d * 2 + 4 * ec * d * 2 + 4 * nh * ngroups * V7X_SUBLANES * tq * 4 + 4 * nh * nk * tq * 4
           + 4 * tq * d * 4 + d * tq * 4 + 2 * ec * tq * 2 + 2 * ec * tq * 4 + 3 * PEER_GROUP * tq * 4)
    return pl.pallas_call(
        functools.partial(_peer_expert_kernel, nc=nc),
        grid=(npairs + 2,),
        in_specs=[pl.BlockSpec((d, tq), lambda g: (0, pair(g, 0)[0])),
                  pl.BlockSpec((ec, d), lambda g: (pair(g, 0)[1], 0)),
                  pl.BlockSpec((ngroups, d, PEER_GROUP), lambda g: (pair(g, 2)[1], 0, 0)),
                  pl.BlockSpec((nh, ngroups, per, tq), lambda g: (0, pair(g, 1)[1], 0, pair(g, 1)[0])),
                  pl.BlockSpec((nh, ngroups, per, tq), lambda g: (0, pair(g, 1)[1], 0, pair(g, 1)[0])),
                  pl.BlockSpec((nh, nk, tq), lambda g: (0, 0, pair(g, 1)[0])),
                  pl.BlockSpec((nh, nk, tq), lambda g: (0, 0, pair(g, 1)[0])),
                  pl.BlockSpec((tq, d), lambda g: (pair(g, 2)[0], 0)),
                  pl.BlockSpec((None, 6, d), lambda g: (cond3(g), 0, 0))],
        out_specs=pl.BlockSpec((tq, d), lambda g: (pair(g, 2)[0], 0)),
        out_shape=jax.ShapeDtypeStruct((t, d), F32),
        scratch_shapes=[pltpu.VMEM((d, tq), F32), pltpu.VMEM((ec, tq), F32), pltpu.VMEM((ec, tq), F32),
                        pltpu.VMEM((ec, tq), BF16), pltpu.VMEM((ec, tq), BF16)],
        compiler_params=_cparams(est, ("arbitrary",)),
        name="peer_expert",
    )(hb.T, u_tab, vt_tab, th, e0, s1, e1, x, mod)


def _final_kernel(x_ref, g_ref, o_ref):
    x = x_ref[...]
    ms = jnp.mean(x * x, axis=-1, keepdims=True)
    o_ref[...] = (x * lax.rsqrt(ms + EPS)) * g_ref[...]


def _final_norm(x, g, *, tm):
    t, d = x.shape
    return pl.pallas_call(
        _final_kernel,
        grid=(t // tm,),
        in_specs=[pl.BlockSpec((tm, d), lambda i: (i, 0)), pl.BlockSpec((1, d), lambda i: (0, 0))],
        out_specs=pl.BlockSpec((tm, d), lambda i: (i, 0)),
        out_shape=jax.ShapeDtypeStruct((t, d), F32),
        compiler_params=_cparams(6 * tm * d * 4, ("parallel",)),
        name="final_norm",
    )(x, g)


def _block_diag(w):
    h, i, j = w.shape
    return jnp.einsum('hij,hg->higj', w, jnp.eye(h, dtype=w.dtype)).reshape(h * i, h * j)


def _s5_discretize(a_re, a_im, log_dt, b_re, b_im):
    dt = jnp.exp(log_dt)[:, None]
    mag = jnp.exp(a_re * dt)
    abar_r = mag * jnp.cos(a_im * dt)
    abar_i = mag * jnp.sin(a_im * dt)
    den = a_re * a_re + a_im * a_im
    num_r = abar_r - 1.0
    coef_r = ((num_r * a_re + abar_i * a_im) / den)[..., None]
    coef_i = ((abar_i * a_re - num_r * a_im) / den)[..., None]
    return abar_r, abar_i, coef_r * b_re - coef_i * b_im, coef_r * b_im + coef_i * b_re


def _s5_blocks(abar_r, abar_i, bbar_r, bbar_i, c_re, c_im):
    g, p, ch = bbar_r.shape
    gb = g // S5_COLBLK

    def vec(v):
        return v.reshape(S5_COLBLK, 1, gb * p)

    def bmat(b):
        bt = b.transpose(0, 2, 1).reshape(S5_COLBLK, gb, ch, p)
        return jnp.stack([_block_diag(bt[j]) for j in range(S5_COLBLK)])

    def cmat(cc):
        ct = cc.transpose(0, 2, 1).reshape(S5_COLBLK, gb, p, ch)
        return jnp.stack([_block_diag(ct[j]) for j in range(S5_COLBLK)])

    bblk = jnp.concatenate([bmat(bbar_r), bmat(bbar_i)], axis=2).astype(BF16)
    cblk = jnp.concatenate([cmat(c_re), cmat(-c_im)], axis=1).astype(BF16)
    return bblk, cblk, vec(abar_r), vec(abar_i)


def _state_to_blocked(re, im):
    n, g, p = re.shape
    gb = g // S5_COLBLK
    both = jnp.stack([re.reshape(n, S5_COLBLK, gb * p), im.reshape(n, S5_COLBLK, gb * p)], axis=2)
    return both.reshape(n, -1)


def _state_from_blocked(st, g, p):
    n = st.shape[0]
    both = st.reshape(n, S5_COLBLK, 2, (g // S5_COLBLK) * p)
    return both[:, :, 0].reshape(n, g, p), both[:, :, 1].reshape(n, g, p)


def _complex_power(ar, ai, n):
    for _ in range(int(math.log2(n))):
        ar, ai = ar * ar - ai * ai, 2.0 * ar * ai
    return ar, ai


def _mixer_ab(xs, mod, p, le, h0, *, bc, lc, bs, ls, tm):
    n_ctx = bc * lc
    width = p['s5_d'].shape[1]
    g_cnt, p_cnt = p['s5_a_re'].shape[2:]
    w_gates = jnp.concatenate([_block_diag(p['lru_w_a'][le, 0]), _block_diag(p['lru_w_x'][le, 0]),
                               _block_diag(p['lru_w_a'][le, 1]), _block_diag(p['lru_w_x'][le, 1])],
                              axis=1).astype(BF16)
    b_gates = jnp.concatenate([p['lru_b_a'][le, 0], p['lru_b_x'][le, 0],
                               p['lru_b_a'][le, 1], p['lru_b_x'][le, 1]]).reshape(1, 4 * width)
    c8 = LRU_C * jax.nn.softplus(-p['lru_lambda'][le])
    pass_ctx = (n_ctx, 0)
    w_in = p['w_in_ab'][le].astype(BF16)
    s_arrs, xgs = zip(*[_ab_in(x, mod, p['norm1_g'][2 * le].reshape(1, -1), w_in,
                               p['lru_conv_w'][le], p['lru_conv_b'][le].reshape(1, width), w_gates, b_gates, c8,
                               n_ctx=nc_, ctx_row=lc, dec_seq=ls, tm=tm) for x, nc_ in zip(xs, pass_ctx)])

    cols = s_arrs[0].shape[1]
    gc = bc // SCAN_SEQS
    lseg = ls // SAMPLE_SEGS
    s_ctx = (s_arrs[0].reshape(gc, SCAN_SEQS, lc, cols).transpose(0, 2, 1, 3)
             .reshape(n_ctx, cols))
    s_smp = (s_arrs[1].reshape(bs * SAMPLE_SEGS, lseg, cols).transpose(1, 0, 2)
             .reshape(bs * ls, cols))
    chunk_rows = min(1024, lc * SCAN_SEQS, lseg * SCAN_SEQS)

    outs = {}
    finals = {}
    for d in range(2):
        abar_r, abar_i, bbar_r, bbar_i = _s5_discretize(
            p['s5_a_re'][le, d], p['s5_a_im'][le, d], p['s5_log_dt'][le, d],
            p['s5_b_re'][le, d], p['s5_b_im'][le, d])
        bblk, cblk, ar, ai = _s5_blocks(abar_r, abar_i, bbar_r, bbar_i,
                                        p['s5_c_re'][le, d], p['s5_c_im'][le, d])
        nstate = 2 * g_cnt * p_cnt
        z5 = jnp.zeros((gc, SCAN_SEQS, nstate), F32)
        zl = jnp.zeros((gc, SCAN_SEQS, width), F32)
        y, hl, f5, fl, _ = _scan(s_ctx, bblk, cblk, ar, ai, z5, zl, direction=d, store=True,
                                 groups=gc, chunk_rows=chunk_rows)
        outs[('ctx', d)] = (y, hl)
        finals[d] = (f5.reshape(bc, nstate), fl.reshape(bc, width))
        z5 = jnp.zeros((1, SCAN_SEQS, nstate), F32)
        zl = jnp.zeros((1, SCAN_SEQS, width), F32)
        e5, el, ep = _scan(s_smp, bblk, cblk, ar, ai, z5, zl, direction=d, store=False,
                           groups=1, chunk_rows=chunk_rows)
        first = (SAMPLE_SEGS - 1) if d == 1 else 0
        h5 = _state_to_blocked(h0[0][:, d], h0[1][:, d])
        h5_rows = jnp.zeros((bs, SAMPLE_SEGS, nstate), F32).at[:, first].set(h5).reshape(SCAN_SEQS, nstate)
        hl_rows = jnp.zeros((bs, SAMPLE_SEGS, width), F32).at[:, first].set(h0[2][:, d]).reshape(SCAN_SEQS, width)
        arl, ail = _complex_power(ar, ai, lseg)
        i5, il = _chain(e5[0], el[0], ep[0], h5_rows, hl_rows, arl, ail, direction=d)
        y, hl, _, _, _ = _scan(s_smp, bblk, cblk, ar, ai, i5[None], il[None], direction=d, store=True,
                               groups=1, chunk_rows=chunk_rows)
        outs[('smp', d)] = (y, hl)

    def to_tokens(name, arr):
        if name == 'ctx':
            return arr.reshape(gc, lc, SCAN_SEQS, width).transpose(0, 2, 1, 3).reshape(n_ctx, width)
        return arr.reshape(lseg, bs * SAMPLE_SEGS, width).transpose(1, 0, 2).reshape(bs * ls, width)

    w_glu, w_out = p['s5_w_glu'][le].astype(BF16), p['w_out_ab'][le].astype(BF16)
    x1 = []
    for name, x, s_arr, xg, nc_ in zip(('ctx', 'smp'), xs, s_arrs, xgs, pass_ctx):
        yf, hf = (to_tokens(name, a) for a in outs[(name, 0)])
        yb, hb = (to_tokens(name, a) for a in outs[(name, 1)])
        x1.append(_ab_out(x, mod, s_arr, xg, yf, yb, hf, hb, p['s5_d'][le].reshape(1, width),
                          w_glu, p['s5_b_glu'][le].reshape(1, width), w_out, n_ctx=nc_, dec_seq=ls, tm=tm))
    st = [_state_from_blocked(finals[d][0], g_cnt, p_cnt) for d in range(2)]
    new_re = jnp.stack([st[0][0], st[1][0]], axis=1)
    new_im = jnp.stack([st[0][1], st[1][1]], axis=1)
    new_lru = jnp.stack([finals[0][1], finals[1][1]], axis=1)
    return x1, (new_re, new_im, new_lru)


def _fft_split(length):
    n = 2 * length
    n1 = 2
    while n // n1 > 1024:
        n1 *= 2
    return n1


def _mixer_c(xs, mod, p, lo, *, bc, lc, bs, ls, tm):
    width = p['hy_bias'].shape[1]
    w_in, w_out = p['w_in_c'][lo].astype(BF16), p['w_out_c'][lo].astype(BF16)
    out = []
    for x, b, l, nc_ in zip(xs, (bc, bs), (lc, ls), (bc * lc, 0)):
        x0, vg = _c_in(x, mod, p['norm1_g'][2 * lo + 1].reshape(1, -1), w_in,
                       p['b_in_c'][lo].reshape(1, -1), p['hy_conv_w'][lo], p['hy_conv_b'][lo].reshape(1, -1),
                       n_ctx=nc_, ctx_row=lc, dec_seq=ls, tm=tm)
        ct = min(width, 256 if l <= 1024 else 128)
        circ = _hy_filter(l, p['hy_w1'][lo], p['hy_b1'][lo], p['hy_freq1'][lo], p['hy_w2'][lo],
                          p['hy_b2'][lo], p['hy_freq2'][lo], p['hy_w3'][lo], p['hy_decay'][lo], ct=ct)
        yconv = _fftconv(vg.reshape(b, l, width), circ, n1=_fft_split(l), ct=ct).reshape(b * l, width)
        out.append(_c_out(x, mod, x0, vg, yconv, p['hy_bias'][lo].reshape(1, width), w_out,
                          p['b_out_c'][lo].reshape(1, -1), n_ctx=nc_, dec_seq=ls, tm=tm))
    return out


def _peer(xs, mod, p, l, *, pass_ctx, dec_seq, tq, ec):
    keys = p['peer_keys'][l]
    nh, _, nk, dkh = keys.shape
    wq = p['peer_wq'][l].astype(BF16)
    keys = keys.reshape(nh * 2, nk, dkh).astype(BF16)
    ne, d = p['peer_v'][l].shape
    u = p['peer_u'][l].astype(BF16)
    vt = p['peer_v'][l].reshape(ne // PEER_GROUP, PEER_GROUP, d).transpose(0, 2, 1).astype(BF16)
    out = []
    for x, nc_ in zip(xs, pass_ctx):
        hb, th, e0, s1, e1 = _peer_route(x, mod, p['norm2_g'][l].reshape(1, -1), wq, keys,
                                         n_ctx=nc_, dec_seq=dec_seq, tq=tq)
        out.append(_peer_expert(hb, u, vt, th, e0, s1, e1, x, mod, n_ctx=nc_, dec_seq=dec_seq, tq=tq, ec=ec))
    return out


def _forward(x_prompt, x_sample, state_s5_re, state_s5_im, state_lru, c, c_ctx, p, *, tm, tq, ec):
    bc, lc, d = x_prompt.shape
    bs, ls, _ = x_sample.shape
    assert bc % SCAN_SEQS == 0 and bs * SAMPLE_SEGS == SCAN_SEQS
    assert tm % lc == 0 and ls % tm == 0 and (ls // SAMPLE_SEGS) % GRID_W == 0
    n_ctx = bc * lc
    xs = [x_prompt.reshape(n_ctx, d), x_sample.reshape(bs * ls, d)]
    pass_ctx = (n_ctx, 0)
    cond = jnp.zeros((V7X_SUBLANES, d), F32).at[0].set(c_ctx).at[1:1 + bs].set(c)
    mod_all = _modulation(cond, p['w_mod'], p['b_mod'])
    states = None
    for l in range(DEPTH):
        mod = mod_all[l].reshape(V7X_SUBLANES, 6, d)
        if l % 2 == 0:
            le = l // 2
            h0 = (state_s5_re[:, le], state_s5_im[:, le], state_lru[:, le])
            xs, st = _mixer_ab(xs, mod, p, le, h0, bc=bc, lc=lc, bs=bs, ls=ls, tm=tm)
            states = st if states is None else states
        else:
            xs = _mixer_c(xs, mod, p, l // 2, bc=bc, lc=lc, bs=bs, ls=ls, tm=tm)
        xs = _peer(xs, mod, p, l, pass_ctx=pass_ctx, dec_seq=ls, tq=tq, ec=ec)
    g = p['final_g'].reshape(1, d)
    new_re, new_im, new_lru = states
    return (_final_norm(xs[0], g, tm=tm).reshape(bc, lc, d), _final_norm(xs[1], g, tm=tm).reshape(bs, ls, d),
            new_re[:, None], new_im[:, None], new_lru[:, None])


def kernel(x_prompt, x_sample, state_s5_re, state_s5_im, state_lru, c, c_ctx, norm1_g, norm2_g, w_mod, b_mod, w_in_ab, s5_a_re, s5_a_im, s5_log_dt, s5_b_re, s5_b_im, s5_c_re, s5_c_im, s5_d, s5_w_glu, s5_b_glu, lru_conv_w, lru_conv_b, lru_w_a, lru_b_a, lru_w_x, lru_b_x, lru_lambda, w_out_ab, w_in_c, b_in_c, hy_conv_w, hy_conv_b, hy_w1, hy_b1, hy_freq1, hy_w2, hy_b2, hy_freq2, hy_w3, hy_decay, hy_bias, w_out_c, b_out_c, peer_wq, peer_keys, peer_u, peer_v, final_g):
    p = dict(norm1_g=norm1_g, norm2_g=norm2_g, w_mod=w_mod, b_mod=b_mod, w_in_ab=w_in_ab,
             s5_a_re=s5_a_re, s5_a_im=s5_a_im, s5_log_dt=s5_log_dt, s5_b_re=s5_b_re, s5_b_im=s5_b_im,
             s5_c_re=s5_c_re, s5_c_im=s5_c_im, s5_d=s5_d, s5_w_glu=s5_w_glu, s5_b_glu=s5_b_glu,
             lru_conv_w=lru_conv_w, lru_conv_b=lru_conv_b, lru_w_a=lru_w_a, lru_b_a=lru_b_a,
             lru_w_x=lru_w_x, lru_b_x=lru_b_x, lru_lambda=lru_lambda, w_out_ab=w_out_ab,
             w_in_c=w_in_c, b_in_c=b_in_c, hy_conv_w=hy_conv_w, hy_conv_b=hy_conv_b,
             hy_w1=hy_w1, hy_b1=hy_b1, hy_freq1=hy_freq1, hy_w2=hy_w2, hy_b2=hy_b2, hy_freq2=hy_freq2,
             hy_w3=hy_w3, hy_decay=hy_decay, hy_bias=hy_bias, w_out_c=w_out_c, b_out_c=b_out_c,
             peer_wq=peer_wq, peer_keys=peer_keys, peer_u=peer_u, peer_v=peer_v, final_g=final_g)
    return _forward(x_prompt, x_sample, state_s5_re, state_s5_im, state_lru, c, c_ctx, p,
                    tm=512, tq=512, ec=2048)
```

```python
import functools
import math

import jax
import jax.numpy as jnp
import numpy as np
from jax import lax
from jax.experimental import pallas as pl
from jax.experimental.pallas import tpu as pltpu

F32 = jnp.float32
BF16 = jnp.bfloat16
HIGHEST = lax.Precision.HIGHEST

DEPTH = 2
GRID_W = 64
EPS = 1e-6
S5_CH = 16
S5_STATE = 64
LRU_HEADS = 8
LRU_C = 8.0
HY_BANDS = 16
HY_SHIFT = 0.05
PEER_HEADS = 8
PEER_NKEYS = 128
PEER_TOPK = 16
PEER_GROUP = 512
PEER_ACC_ROWS = 256

V7X_LANES = 128
V7X_SUBLANES = 8
V7X_VMEM_BYTES = 64 * 1024 * 1024

SCAN_SEQS = V7X_SUBLANES
SAMPLE_SEGS = 4
S5_COLBLK = 4


def _vmem_limit(est_bytes):
    return int(min(V7X_VMEM_BYTES - (8 << 20), max(32 << 20, est_bytes * 3 // 2)))


def _cparams(est_bytes, sem):
    return pltpu.CompilerParams(dimension_semantics=sem, vmem_limit_bytes=_vmem_limit(est_bytes))


def _gelu(x):
    k = 2.0 * 0.7978845608028654 * math.log2(math.e)
    return x / (1.0 + jnp.exp2(x * (-k - (k * 0.044715) * (x * x))))


def _norm_mod(x, g, sc, sh):
    ms = jnp.mean(x * x, axis=-1, keepdims=True)
    return (x * lax.rsqrt(ms + EPS)) * g * (1.0 + sc) + sh


def _cond_index(row0, n_ctx, dec_seq):
    return jnp.where(row0 < n_ctx, 0, 1 + (row0 - n_ctx) // dec_seq)


def _mod_kernel(c_ref, w_ref, b_ref, o_ref):
    c = c_ref[...]
    s = c * jax.nn.sigmoid(c)
    o_ref[...] = jnp.dot(s, w_ref[...], precision=HIGHEST, preferred_element_type=F32) + b_ref[...]


def _modulation(cond, w_mod, b_mod):
    depth, d, n = w_mod.shape
    rows = cond.shape[0]
    tn = 1536
    est = 2 * d * tn * 4 + 4 * rows * (d + tn) * 4
    return pl.pallas_call(
        _mod_kernel,
        grid=(depth, n // tn),
        in_specs=[pl.BlockSpec((rows, d), lambda l, j: (0, 0)),
                  pl.BlockSpec((None, d, tn), lambda l, j: (l, 0, j)),
                  pl.BlockSpec((None, 1, tn), lambda l, j: (l, 0, j))],
        out_specs=pl.BlockSpec((None, rows, tn), lambda l, j: (l, 0, j)),
        out_shape=jax.ShapeDtypeStruct((depth, rows, n), F32),
        compiler_params=_cparams(est, ("arbitrary", "arbitrary")),
        name="modulation",
    )(cond, w_mod, b_mod.reshape(depth, 1, n))


def _dwconv_rows(x, w, bias, pos, row_len, pad_left):
    tm = x.shape[0]
    acc = None
    for k in range(w.shape[0]):
        d = k - pad_left
        if d == 0:
            term = x
        else:
            shifted = pltpu.roll(x, (-d) % tm, axis=0)
            q = pos + d
            term = jnp.where(q >= 0, jnp.where(q < row_len, shifted, 0.0), 0.0)
        term = term * w[k:k + 1]
        acc = term if acc is None else acc + term
    return acc + bias


def _ab_in_kernel(x_ref, mod_ref, g_ref, w_ref, cw_ref, cb_ref, wg_ref, bg_ref, c8_ref,
                  s_ref, xg_ref, *, n_ctx, ctx_row, width):
    tm = x_ref.shape[0]
    m = mod_ref[...]
    h = _norm_mod(x_ref[...], g_ref[...], m[1:2], m[0:1])
    z = jnp.dot(h.astype(BF16), w_ref[...], preferred_element_type=F32)
    u = z[:, :width]
    xr = z[:, width:2 * width]
    xg_ref[...] = z[:, 2 * width:]
    row_len = jnp.where(pl.program_id(0) * tm < n_ctx, ctx_row, GRID_W)
    pos = lax.broadcasted_iota(jnp.int32, (tm, width), 0) & (row_len - 1)
    xb = _dwconv_rows(xr, cw_ref[...], cb_ref[...], pos, row_len, 2)
    gates = jnp.dot(xb.astype(BF16), wg_ref[...], preferred_element_type=F32) + bg_ref[...]
    s_ref[:, :width] = u
    c8 = c8_ref[...]
    for d in range(2):
        r = jax.nn.sigmoid(gates[:, (2 * d) * width:(2 * d + 1) * width])
        i = jax.nn.sigmoid(gates[:, (2 * d + 1) * width:(2 * d + 2) * width])
        a = jnp.exp(-(c8[d:d + 1] * r))
        b = jnp.sqrt(1.0 - a * a) * (i * xb)
        s_ref[:, (1 + 2 * d) * width:(2 + 2 * d) * width] = a
        s_ref[:, (2 + 2 * d) * width:(3 + 2 * d) * width] = b


def _ab_in(x, mod, g, w_in, conv_w, conv_b, w_gates, b_gates, c8, *, n_ctx, ctx_row, dec_seq, tm):
    t, d = x.shape
    width = w_in.shape[1] // 3
    kern = functools.partial(_ab_in_kernel, n_ctx=n_ctx, ctx_row=ctx_row, width=width)
    const = lambda i: (0, 0)
    est = (2 * tm * d * 4 + d * 3 * width * 2 * 2 + width * 4 * width * 2 * 2
           + 2 * tm * 6 * width * 4 + tm * 12 * width * 4)
    return pl.pallas_call(
        kern,
        grid=(t // tm,),
        in_specs=[pl.BlockSpec((tm, d), lambda i: (i, 0)),
                  pl.BlockSpec((None, 6, d), lambda i: (_cond_index(i * tm, n_ctx, dec_seq), 0, 0)),
                  pl.BlockSpec((1, d), const),
                  pl.BlockSpec(w_in.shape, const),
                  pl.BlockSpec(conv_w.shape, const),
                  pl.BlockSpec((1, width), const),
                  pl.BlockSpec(w_gates.shape, const),
                  pl.BlockSpec((1, 4 * width), const),
                  pl.BlockSpec((2, width), const)],
        out_specs=[pl.BlockSpec((tm, 5 * width), lambda i: (i, 0)),
                   pl.BlockSpec((tm, width), lambda i: (i, 0))],
        out_shape=[jax.ShapeDtypeStruct((t, 5 * width), F32),
                   jax.ShapeDtypeStruct((t, width), F32)],
        compiler_params=_cparams(est, ("parallel",)),
        name="ab_in",
    )(x, mod, g, w_in, conv_w, conv_b, w_gates, b_gates, c8)


def _scan_kernel(u_ref, a_ref, b_ref, bblk_ref, cblk_ref, ar_ref, ai_ref, i5_ref, il_ref, *rest,
                 reverse, store):
    if store:
        y_ref, hl_ref, f5_ref, fl_ref, fp_ref, buf, st5, stl, stp = rest
    else:
        f5_ref, fl_ref, fp_ref, buf, st5, stl, stp = rest
    c = pl.program_id(1)
    rows = u_ref.shape[0]
    nsteps = rows // SCAN_SEQS
    sw = st5.shape[1] // (2 * S5_COLBLK)
    cw = u_ref.shape[1] // S5_COLBLK

    @pl.when(c == 0)
    def _():
        st5[...] = i5_ref[...]
        stl[...] = il_ref[...]
        stp[...] = jnp.ones_like(stp)

    def row0(k):
        t = (nsteps - 1 - k) if reverse else k
        return pl.multiple_of(t * SCAN_SEQS, SCAN_SEQS)

    u = u_ref[...].astype(BF16)
    for j in range(S5_COLBLK):
        buf[...] = jnp.dot(u[:, j * cw:(j + 1) * cw], bblk_ref[j], preferred_element_type=F32)
        arj = jnp.broadcast_to(ar_ref[j], (SCAN_SEQS, sw))
        aij = jnp.broadcast_to(ai_ref[j], (SCAN_SEQS, sw))

        def step(k, carry, arj=arj, aij=aij):
            hr, hi = carry
            r0 = row0(k)
            xr = buf[pl.ds(r0, SCAN_SEQS), 0:sw]
            xi = buf[pl.ds(r0, SCAN_SEQS), sw:2 * sw]
            nr = arj * hr - aij * hi + xr
            ni = arj * hi + aij * hr + xi
            buf[pl.ds(r0, SCAN_SEQS), 0:sw] = nr
            buf[pl.ds(r0, SCAN_SEQS), sw:2 * sw] = ni
            return nr, ni

        base = 2 * sw * j
        hr, hi = lax.fori_loop(0, nsteps, step, (st5[:, base:base + sw], st5[:, base + sw:base + 2 * sw]))
        st5[:, base:base + sw] = hr
        st5[:, base + sw:base + 2 * sw] = hi
        if store:
            y_ref[:, j * cw:(j + 1) * cw] = jnp.dot(buf[...].astype(BF16), cblk_ref[j],
                                                    preferred_element_type=F32)

    def lstep(k, carry):
        h, p = carry
        r0 = row0(k)
        a = a_ref[pl.ds(r0, SCAN_SEQS), :]
        h = a * h + b_ref[pl.ds(r0, SCAN_SEQS), :]
        if store:
            hl_ref[pl.ds(r0, SCAN_SEQS), :] = h
        return h, p * a

    h, p = lax.fori_loop(0, nsteps, lstep, (stl[...], stp[...]))
    stl[...] = h
    stp[...] = p

    @pl.when(c == pl.num_programs(1) - 1)
    def _():
        f5_ref[...] = st5[...]
        fl_ref[...] = stl[...]
        fp_ref[...] = stp[...]


def _scan(s_arr, bblk, cblk, ar, ai, init5, initl, *, direction, store, groups, chunk_rows):
    rows_total, cols = s_arr.shape
    width = cols // 5
    rows_per_group = rows_total // groups
    nc = rows_per_group // chunk_rows
    reverse = direction == 1
    nstate = init5.shape[-1]

    def chunk(g, c):
        return g * nc + ((nc - 1 - c) if reverse else c)

    kern = functools.partial(_scan_kernel, reverse=reverse, store=store)
    const3 = lambda g, c: (0, 0, 0)
    in_specs = [pl.BlockSpec((chunk_rows, width), lambda g, c: (chunk(g, c), 0)),
                pl.BlockSpec((chunk_rows, width), lambda g, c: (chunk(g, c), 1 + 2 * direction)),
                pl.BlockSpec((chunk_rows, width), lambda g, c: (chunk(g, c), 2 + 2 * direction)),
                pl.BlockSpec(bblk.shape, const3),
                pl.BlockSpec(cblk.shape, const3),
                pl.BlockSpec(ar.shape, const3),
                pl.BlockSpec(ai.shape, const3),
                pl.BlockSpec((None, SCAN_SEQS, nstate), lambda g, c: (g, 0, 0)),
                pl.BlockSpec((None, SCAN_SEQS, width), lambda g, c: (g, 0, 0))]
    fin_specs = [pl.BlockSpec((None, SCAN_SEQS, nstate), lambda g, c: (g, 0, 0)),
                 pl.BlockSpec((None, SCAN_SEQS, width), lambda g, c: (g, 0, 0)),
                 pl.BlockSpec((None, SCAN_SEQS, width), lambda g, c: (g, 0, 0))]
    fin_shapes = [jax.ShapeDtypeStruct((groups, SCAN_SEQS, nstate), F32),
                  jax.ShapeDtypeStruct((groups, SCAN_SEQS, width), F32),
                  jax.ShapeDtypeStruct((groups, SCAN_SEQS, width), F32)]
    out_specs, out_shapes = fin_specs, fin_shapes
    if store:
        out_specs = [pl.BlockSpec((chunk_rows, width), lambda g, c: (chunk(g, c), 0))] * 2 + fin_specs
        out_shapes = [jax.ShapeDtypeStruct((rows_total, width), F32)] * 2 + fin_shapes
    bufw = 2 * nstate // (2 * S5_COLBLK)
    est = (10 * chunk_rows * width * 4 + chunk_rows * bufw * 4 * 2
           + 4 * bblk.size * 2 + 8 * SCAN_SEQS * nstate * 4)
    return pl.pallas_call(
        kern,
        grid=(groups, nc),
        in_specs=in_specs,
        out_specs=out_specs,
        out_shape=out_shapes,
        scratch_shapes=[pltpu.VMEM((chunk_rows, bufw), F32),
                        pltpu.VMEM((SCAN_SEQS, nstate), F32),
                        pltpu.VMEM((SCAN_SEQS, width), F32),
                        pltpu.VMEM((SCAN_SEQS, width), F32)],
        compiler_params=_cparams(est, ("arbitrary", "arbitrary")),
        name="scan_store" if store else "scan_ends",
    )(s_arr, s_arr, s_arr, bblk, cblk, ar, ai, init5, initl)


def _chain_kernel(e5_ref, el_ref, ep_ref, h5_ref, hl_ref, arl_ref, ail_ref, i5_ref, il_ref, *, reverse):
    nseg = SAMPLE_SEGS
    sw = arl_ref.shape[-1]
    shift = (SCAN_SEQS - 1) if reverse else 1
    order = range(nseg - 2, -1, -1) if reverse else range(1, nseg)
    seg5 = lax.broadcasted_iota(jnp.int32, (SCAN_SEQS, sw), 0) % nseg
    for j in range(S5_COLBLK):
        base = 2 * sw * j
        re, im = slice(base, base + sw), slice(base + sw, base + 2 * sw)
        ar, ai = arl_ref[j], ail_ref[j]
        er = pltpu.roll(e5_ref[:, re], shift, axis=0)
        ei = pltpu.roll(e5_ref[:, im], shift, axis=0)
        hr, hi = h5_ref[:, re], h5_ref[:, im]
        for k in order:
            pr = pltpu.roll(hr, shift, axis=0)
            pi = pltpu.roll(hi, shift, axis=0)
            hr = jnp.where(seg5 == k, ar * pr - ai * pi + er, hr)
            hi = jnp.where(seg5 == k, ar * pi + ai * pr + ei, hi)
        i5_ref[:, re] = hr
        i5_ref[:, im] = hi
    segl = lax.broadcasted_iota(jnp.int32, el_ref.shape, 0) % nseg
    el = pltpu.roll(el_ref[...], shift, axis=0)
    ep = pltpu.roll(ep_ref[...], shift, axis=0)
    h = hl_ref[...]
    for k in order:
        h = jnp.where(segl == k, ep * pltpu.roll(h, shift, axis=0) + el, h)
    il_ref[...] = h


def _chain(end5, endl, endp, h5_rows, hl_rows, arl, ail, *, direction):
    kern = functools.partial(_chain_kernel, reverse=direction == 1)
    return pl.pallas_call(
        kern,
        out_shape=[jax.ShapeDtypeStruct(end5.shape, F32), jax.ShapeDtypeStruct(endl.shape, F32)],
        name="scan_chain",
    )(end5, endl, endp, h5_rows, hl_rows, arl, ail)


def _ab_out_kernel(x_ref, mod_ref, s_ref, xg_ref, yf_ref, yb_ref, hf_ref, hb_ref, d_ref,
                   wglu_ref, bglu_ref, wout_ref, o_ref):
    width = xg_ref.shape[1]
    m = mod_ref[...]
    y = s_ref[...] * d_ref[...] + yf_ref[...] + yb_ref[...]
    zs = _gelu(y)
    gate = jnp.dot(zs.astype(BF16), wglu_ref[...], preferred_element_type=F32) + bglu_ref[...]
    s5_out = zs * jax.nn.sigmoid(gate)
    lru_out = (hf_ref[...] + hb_ref[...]) * _gelu(xg_ref[...])
    out = (jnp.dot(s5_out.astype(BF16), wout_ref[:width, :], preferred_element_type=F32)
           + jnp.dot(lru_out.astype(BF16), wout_ref[width:, :], preferred_element_type=F32))
    o_ref[...] = x_ref[...] + m[2:3] * out


def _ab_out(x, mod, s_arr, xg, yf, yb, hf, hb, s5_d, w_glu, b_glu, w_out, *, n_ctx, dec_seq, tm):
    t, d = x.shape
    width = xg.shape[1]
    const = lambda i: (0, 0)
    row = lambda i: (i, 0)
    est = 4 * tm * d * 4 + 12 * tm * width * 4 + 4 * (width * width + 2 * width * d) + 8 * tm * width * 4
    return pl.pallas_call(
        _ab_out_kernel,
        grid=(t // tm,),
        in_specs=[pl.BlockSpec((tm, d), row),
                  pl.BlockSpec((None, 6, d), lambda i: (_cond_index(i * tm, n_ctx, dec_seq), 0, 0)),
                  pl.BlockSpec((tm, width), row),
                  pl.BlockSpec((tm, width), row),
                  pl.BlockSpec((tm, width), row),
                  pl.BlockSpec((tm, width), row),
                  pl.BlockSpec((tm, width), row),
                  pl.BlockSpec((tm, width), row),
                  pl.BlockSpec((1, width), const),
                  pl.BlockSpec(w_glu.shape, const),
                  pl.BlockSpec((1, width), const),
                  pl.BlockSpec(w_out.shape, const)],
        out_specs=pl.BlockSpec((tm, d), row),
        out_shape=jax.ShapeDtypeStruct((t, d), F32),
        compiler_params=_cparams(est, ("parallel",)),
        name="ab_out",
    )(x, mod, s_arr, xg, yf, yb, hf, hb, s5_d, w_glu, b_glu, w_out)


def _c_in_kernel(x_ref, mod_ref, g_ref, w_ref, b_ref, cw_ref, cb_ref, x0_ref, vg_ref, *, n_ctx, ctx_row):
    tm = x_ref.shape[0]
    width = x0_ref.shape[1]
    m = mod_ref[...]
    h = _norm_mod(x_ref[...], g_ref[...], m[1:2], m[0:1])
    z = jnp.dot(h.astype(BF16), w_ref[...], preferred_element_type=F32) + b_ref[...]
    row_len = jnp.where(pl.program_id(0) * tm < n_ctx, ctx_row, GRID_W)
    pos = lax.broadcasted_iota(jnp.int32, (tm, width), 0) & (row_len - 1)
    cw = cw_ref[...]
    cb = cb_ref[...]
    parts = [_dwconv_rows(z[:, k * width:(k + 1) * width], cw[:, k * width:(k + 1) * width],
                          cb[:, k * width:(k + 1) * width], pos, row_len, 1) for k in range(3)]
    x0_ref[...] = parts[0]
    vg_ref[...] = parts[2] * parts[1]


def _c_in(x, mod, g, w_in, b_in, conv_w, conv_b, *, n_ctx, ctx_row, dec_seq, tm):
    t, d = x.shape
    width = w_in.shape[1] // 3
    kern = functools.partial(_c_in_kernel, n_ctx=n_ctx, ctx_row=ctx_row)
    const = lambda i: (0, 0)
    est = 2 * tm * d * 4 + 4 * d * 3 * width + 4 * tm * width * 4 + 6 * tm * 3 * width * 4
    return pl.pallas_call(
        kern,
        grid=(t // tm,),
        in_specs=[pl.BlockSpec((tm, d), lambda i: (i, 0)),
                  pl.BlockSpec((None, 6, d), lambda i: (_cond_index(i * tm, n_ctx, dec_seq), 0, 0)),
                  pl.BlockSpec((1, d), const),
                  pl.BlockSpec(w_in.shape, const),
                  pl.BlockSpec((1, 3 * width), const),
                  pl.BlockSpec(conv_w.shape, const),
                  pl.BlockSpec((1, 3 * width), const)],
        out_specs=[pl.BlockSpec((tm, width), lambda i: (i, 0))] * 2,
        out_shape=[jax.ShapeDtypeStruct((t, width), F32)] * 2,
        compiler_params=_cparams(est, ("parallel",)),
        name="c_in",
    )(x, mod, g, w_in, b_in, conv_w, conv_b)


def _c_out_kernel(x_ref, mod_ref, x0_ref, vg_ref, yc_ref, hb_ref, w_ref, b_ref, o_ref):
    m = mod_ref[...]
    vg = vg_ref[...]
    y = (yc_ref[...] + hb_ref[...] * vg) * x0_ref[...]
    out = jnp.dot(y.astype(BF16), w_ref[...], preferred_element_type=F32) + b_ref[...]
    o_ref[...] = x_ref[...] + m[2:3] * out


def _c_out(x, mod, x0, vg, yconv, hy_bias, w_out, b_out, *, n_ctx, dec_seq, tm):
    t, d = x.shape
    width = x0.shape[1]
    const = lambda i: (0, 0)
    row = lambda i: (i, 0)
    est = 4 * tm * d * 4 + 6 * tm * width * 4 + 4 * width * d + 4 * tm * width * 4
    return pl.pallas_call(
        _c_out_kernel,
        grid=(t // tm,),
        in_specs=[pl.BlockSpec((tm, d), row),
                  pl.BlockSpec((None, 6, d), lambda i: (_cond_index(i * tm, n_ctx, dec_seq), 0, 0)),
                  pl.BlockSpec((tm, width), row),
                  pl.BlockSpec((tm, width), row),
                  pl.BlockSpec((tm, width), row),
                  pl.BlockSpec((1, width), const),
                  pl.BlockSpec(w_out.shape, const),
                  pl.BlockSpec((1, d), const)],
        out_specs=pl.BlockSpec((tm, d), row),
        out_shape=jax.ShapeDtypeStruct((t, d), F32),
        compiler_params=_cparams(est, ("parallel",)),
        name="c_out",
    )(x, mod, x0, vg, yconv, hy_bias, w_out, b_out)


def _hy_filter_kernel(w1_ref, b1_ref, f1_ref, w2_ref, b2_ref, f2_ref, w3_ref, dec_ref, o_ref, z_scr, *, length):
    ct = o_ref.shape[1]
    nb = HY_BANDS
    ecols = w1_ref.shape[0]
    row = lax.broadcasted_iota(jnp.int32, (length, 1), 0)
    positions = (row, length - row)

    @pl.when(pl.program_id(0) == 0)
    def _():
        for side in range(2):
            posf = positions[side].astype(F32)
            col = lax.broadcasted_iota(jnp.int32, (length, ecols), 1)
            band_idx = jnp.where(col <= nb, col - 1, col - 1 - nb).astype(F32)
            bands = 1e-4 + band_idx * ((nb - 1 - 1e-4) / (nb - 1))
            ang = (2.0 * math.pi / length) * posf * bands
            emb = jnp.where(col == 0, posf / length,
                            jnp.where(col <= nb, jnp.cos(ang),
                                      jnp.where(col <= 2 * nb, -jnp.sin(ang), 0.0)))
            z = jnp.sin(f1_ref[...] * (jnp.dot(emb, w1_ref[...], precision=HIGHEST,
                                               preferred_element_type=F32) + b1_ref[...]))
            z_scr[side] = jnp.sin(f2_ref[...] * (jnp.dot(z, w2_ref[...], precision=HIGHEST,
                                                         preferred_element_type=F32) + b2_ref[...]))

    def half(side):
        t = positions[side].astype(F32) / length
        filt = jnp.dot(z_scr[side], w3_ref[side], precision=HIGHEST, preferred_element_type=F32)
        return filt * (jnp.exp(-t * jnp.abs(dec_ref[side:side + 1, :])) + HY_SHIFT)

    fwd = half(0)
    bwd = jnp.where(lax.broadcasted_iota(jnp.int32, (length, ct), 0) == 0, 0.0, half(1))
    ss = jnp.sum(fwd * fwd, axis=0, keepdims=True) + jnp.sum(bwd * bwd, axis=0, keepdims=True)
    scale = lax.rsqrt(ss + EPS)
    o_ref[:length, :] = fwd * scale
    o_ref[length:, :] = bwd * scale


def _hy_filter(length, w1, b1, f1, w2, b2, f2, w3, decay, *, ct):
    emb, ff = w1.shape
    width = decay.shape[1]
    ecols = 128
    w1p = jnp.zeros((ecols, ff), F32).at[:emb].set(w1)
    w3s = w3.reshape(ff, 2, width).transpose(1, 0, 2)
    kern = functools.partial(_hy_filter_kernel, length=length)
    const = lambda j: (0, 0)
    est = 10 * length * ct * 4 + 4 * length * ecols * 4
    return pl.pallas_call(
        kern,
        grid=(width // ct,),
        in_specs=[pl.BlockSpec((ecols, ff), const),
                  pl.BlockSpec((1, ff), const),
                  pl.BlockSpec((1, ff), const),
                  pl.BlockSpec((ff, ff), const),
                  pl.BlockSpec((1, ff), const),
                  pl.BlockSpec((1, ff), const),
                  pl.BlockSpec((2, ff, ct), lambda j: (0, 0, j)),
                  pl.BlockSpec((2, ct), lambda j: (0, j))],
        out_specs=pl.BlockSpec((2 * length, ct), lambda j: (0, j)),
        out_shape=jax.ShapeDtypeStruct((2 * length, width), F32),
        scratch_shapes=[pltpu.VMEM((2, length, ff), F32)],
        compiler_params=_cparams(est, ("arbitrary",)),
        name="hy_filter",
    )(w1p, b1.reshape(1, ff), f1.reshape(1, ff), w2, b2.reshape(1, ff), f2.reshape(1, ff), w3s, decay)


def _fftconv_kernel(ct_ref, st_ref, x_ref, k_ref, c_ref, s_ref, twr_ref, twi_ref, o_ref, *scratch, n1):
    kfr, kfi = scratch if scratch else (None, None)
    h1 = n1 // 2
    n2 = c_ref.shape[0]
    n = n1 * n2
    cm = c_ref[...]
    sm = s_ref[...]

    def dft(pairs, inverse):
        w = pairs[0][0].shape[1]
        parts = jnp.concatenate([p.astype(BF16) for pair in pairs for p in pair], axis=1)
        cp = jnp.dot(cm, parts, preferred_element_type=F32)
        sp = jnp.dot(sm, parts, preferred_element_type=F32)
        out = []
        for i in range(len(pairs)):
            ca, cb = cp[:, 2 * i * w:(2 * i + 1) * w], cp[:, (2 * i + 1) * w:(2 * i + 2) * w]
            sa, sb = sp[:, 2 * i * w:(2 * i + 1) * w], sp[:, (2 * i + 1) * w:(2 * i + 2) * w]
            out.append((ca - sb, cb + sa) if inverse else (ca + sb, cb - sa))
        return out

    def slab(t1):
        return slice(t1 * n2, (t1 + 1) * n2)

    twr1, twi1 = twr_ref[...], twi_ref[...]
    tw0 = (jnp.ones_like(twr1), jnp.zeros_like(twi1))

    def next_twiddle(tw):
        return tw[0] * twr1 - tw[1] * twi1, tw[0] * twi1 + tw[1] * twr1

    def filter_stage1(k1, tw):
        ar = ai = None
        for t1 in range(n1):
            kr = k_ref[slab(t1), :]
            tr, ti = ct_ref[k1, t1] * kr, -(st_ref[k1, t1] * kr)
            ar = tr if ar is None else ar + tr
            ai = ti if ai is None else ai + ti
        ar, ai = ar * (1.0 / n), ai * (1.0 / n)
        return ar * tw[0] + ai * tw[1], ai * tw[0] - ar * tw[1]

    if kfr is not None:
        @pl.when(pl.program_id(1) == 0)
        def _():
            def fbody(k1, tw):
                (kfr[k1], kfi[k1]), = dft([filter_stage1(k1, tw)], False)
                return next_twiddle(tw)
            lax.fori_loop(0, n1, fbody, tw0)

    o_ref[...] = jnp.zeros_like(o_ref)

    def body(k1, tw):
        twr, twi = tw
        ar = ai = None
        for t1 in range(h1):
            cc, ss = ct_ref[k1, t1], st_ref[k1, t1]
            xr, xi = x_ref[0, slab(t1), :], x_ref[1, slab(t1), :]
            tr, ti = cc * xr + ss * xi, cc * xi - ss * xr
            ar = tr if ar is None else ar + tr
            ai = ti if ai is None else ai + ti
        data = (ar * twr + ai * twi, ai * twr - ar * twi)
        if kfr is not None:
            (fr, fi), = dft([data], False)
            gr, gi = kfr[k1], kfi[k1]
        else:
            (fr, fi), (gr, gi) = dft([data, filter_stage1(k1, tw)], False)
        (br, bi), = dft([(fr * gr - fi * gi, fr * gi + fi * gr)], True)
        br, bi = br * twr - bi * twi, bi * twr + br * twi
        for t1 in range(h1):
            cc, ss = ct_ref[k1, t1], st_ref[k1, t1]
            o_ref[0, slab(t1), :] += cc * br - ss * bi
            o_ref[1, slab(t1), :] += cc * bi + ss * br
        return next_twiddle(tw)

    lax.fori_loop(0, n1, body, tw0)


def _fftconv(v, circ, *, n1, ct):
    bsz, length, width = v.shape
    n = 2 * length
    n2 = n // n1
    idx = np.arange(n2)
    ang = 2.0 * np.pi * ((idx[:, None] * idx[None, :]) % n2) / n2
    cmat = jnp.asarray(np.cos(ang), F32).astype(BF16)
    smat = jnp.asarray(np.sin(ang), F32).astype(BF16)
    tang = 2.0 * np.pi * idx / n
    twr = jnp.asarray(np.broadcast_to(np.cos(tang)[:, None], (n2, ct)), F32)
    twi = jnp.asarray(np.broadcast_to(np.sin(tang)[:, None], (n2, ct)), F32)
    i1 = np.arange(n1)
    ang1 = 2.0 * np.pi * ((i1[:, None] * i1[None, :]) % n1) / n1
    ctab = jnp.asarray(np.round(np.cos(ang1), 12), F32)
    stab = jnp.asarray(np.round(np.sin(ang1), 12), F32)
    kern = functools.partial(_fftconv_kernel, n1=n1)
    const = lambda j, p: (0, 0)
    once = pl.Buffered(1)
    npairs = bsz // 2
    scratch = [pltpu.VMEM((n1, n2, ct), F32)] * 2 if npairs > 1 else []
    est = (4 * 2 * length * ct * 4 + n * ct * 4 + 2 * n2 * n2 * 2 + 2 * n2 * ct * 4
           + len(scratch) * n * ct * 4 + (8 + 2 * n1) * n2 * ct * 4)
    return pl.pallas_call(
        kern,
        grid=(width // ct, npairs),
        in_specs=[pl.BlockSpec(memory_space=pltpu.SMEM),
                  pl.BlockSpec(memory_space=pltpu.SMEM),
                  pl.BlockSpec((2, length, ct), lambda j, p: (p, 0, j)),
                  pl.BlockSpec((n, ct), lambda j, p: (0, j), pipeline_mode=once),
                  pl.BlockSpec((n2, n2), const, pipeline_mode=once),
                  pl.BlockSpec((n2, n2), const, pipeline_mode=once),
                  pl.BlockSpec((n2, ct), const, pipeline_mode=once),
                  pl.BlockSpec((n2, ct), const, pipeline_mode=once)],
        out_specs=pl.BlockSpec((2, length, ct), lambda j, p: (p, 0, j)),
        out_shape=jax.ShapeDtypeStruct((bsz, length, width), F32),
        scratch_shapes=scratch,
        compiler_params=_cparams(est, ("arbitrary", "arbitrary")),
        name="fftconv",
    )(ctab, stab, v, circ, cmat, smat, twr, twi)


def _sorting_network(n):
    pairs = []
    p = 1
    while p < n:
        k = p
        while k >= 1:
            for j in range(k % p, n - k, 2 * k):
                for i in range(min(k, n - j - k)):
                    if (i + j) // (2 * p) == (i + j + k) // (2 * p):
                        pairs.append((i + j, i + j + k))
            k //= 2
        p *= 2
    return pairs


_SORT16 = _sorting_network(PEER_TOPK)


def _peer_route_kernel(x_ref, mod_ref, g_ref, wq_ref, keys_ref, hb_ref, th_ref, e0_ref, s1_ref, e1_ref,
                       q_scr, sc_scr, top_scr):
    tq = x_ref.shape[0]
    nk = PEER_NKEYS
    k = PEER_TOPK
    nhp = keys_ref.shape[0]
    dkh = keys_ref.shape[2]
    neg = -jnp.inf
    m = mod_ref[...]
    h = _norm_mod(x_ref[...], g_ref[...], m[4:5], m[3:4]).astype(BF16)
    hb_ref[...] = h
    q = jnp.dot(h, wq_ref[...], preferred_element_type=F32)
    for hp in range(nhp):
        q_scr[hp] = q[:, hp * dkh:(hp + 1) * dkh].astype(BF16)

    lw = V7X_LANES
    sub = V7X_SUBLANES
    assert k == 2 * sub
    ncand = k + 7 * sub + (k - sub)
    npad = sum(sub - k // (i + 1) for i in range(2, sub))
    iota_k = lax.broadcasted_iota(jnp.int32, (nk, lw), 0)
    iota_c = lax.broadcasted_iota(jnp.int32, (ncand, lw), 0)
    iota_8 = lax.broadcasted_iota(jnp.int32, (sub, lw), 0)
    iota_r = lax.broadcasted_iota(jnp.int32, (k, lw), 0)

    def topk(vals, iota, exact):
        top = jnp.zeros((k, lw), F32)
        for r in range(k):
            mx = jnp.max(vals, axis=0, keepdims=True)
            if exact:
                first = jnp.min(jnp.where(vals == mx, iota, vals.shape[0]), axis=0, keepdims=True)
                vals = jnp.where(iota == first, neg, vals)
            else:
                vals = jnp.where(vals == mx, neg, vals)
            top = jnp.where(iota_r == r, mx, top)
        return top, vals

    def tied(rest, expected):
        gone = jnp.sum(jnp.where(rest == neg, 1.0, 0.0), axis=0, keepdims=True)
        return jnp.where(gone != expected, 1.0, 0.0)

    def exchange(v, i, j):
        v[i], v[j] = jnp.maximum(v[i], v[j]), jnp.minimum(v[i], v[j])

    def sort_top(tiles):
        v = list(tiles)
        for i, j in _SORT16:
            exchange(v, i, j)
        for shift in (sub // 2, sub // 4, sub // 8):
            other = [pltpu.roll(t, shift, axis=0) for t in v]
            v = [jnp.maximum(v[r], other[k - 1 - r]) for r in range(k)]
            d = k // 2
            while d >= 1:
                for i in range(k):
                    if i & d == 0:
                        exchange(v, i, i + d)
                d //= 2
        top = jnp.zeros((k, lw), F32)
        for r in range(k):
            top = jnp.where(iota_r == r, jnp.concatenate([v[r]] * (k // sub), axis=0), top)
        return top

    def halves(hd):
        assert nk == k * sub
        for p in range(2):
            for c in range(tq // lw):
                lanes = slice(c * lw, (c + 1) * lw)
                top_scr[2 * hd + p, :, lanes] = sort_top(
                    [sc_scr[2 * hd + p, t * sub:(t + 1) * sub, lanes] for t in range(nk // sub)])

    def pairs(hd, exact):
        flag = jnp.zeros((1, lw), F32)
        for c in range(tq // lw):
            lanes = slice(c * lw, (c + 1) * lw)
            a = top_scr[2 * hd, :, lanes]
            b = top_scr[2 * hd + 1, :, lanes]
            pieces = [a[0:1] + b, a[1:2] + b[0:sub]]
            for i in range(2, sub):
                pieces.append(jnp.where(iota_8 < k // (i + 1), a[i:i + 1] + b[0:sub], neg))
            pieces.append(a[sub:k] + b[0:1])
            vmax = a[0:1] + b[0:1]
            top, rest = topk(jnp.concatenate(pieces, axis=0), iota_c, exact)
            z = jnp.sum(jnp.exp(top - vmax), axis=0, keepdims=True)
            last = top[k - 1:k]
            nxt = jnp.max(rest, axis=0, keepdims=True)
            tau = jnp.where(nxt == neg, last, 0.5 * (last + nxt))
            flag = jnp.maximum(flag, tied(rest, float(k + npad)))
            s0 = sc_scr[2 * hd, :, lanes]
            s1 = sc_scr[2 * hd + 1, :, lanes]
            th = tau - s0
            e0 = jnp.exp(s0 - a[0:1]) / z
            per = th_ref.shape[2]
            for r in range(nk // per):
                th_ref[hd, r, :, lanes] = th[r * per:(r + 1) * per]
                e0_ref[hd, r, :, lanes] = e0[r * per:(r + 1) * per]
            s1_ref[hd, :, lanes] = s1
            e1_ref[hd, :, lanes] = jnp.exp(s1 - b[0:1])
        return jnp.max(flag)

    def head_body(hd, carry):
        for p in range(2):
            sc_scr[2 * hd + p] = lax.dot_general(keys_ref[2 * hd + p], q_scr[2 * hd + p], (((1,), (1,)), ((), ())),
                                                 preferred_element_type=F32)
        halves(hd)

        @pl.when(pairs(hd, False) > 0.0)
        def _():
            pairs(hd, True)

        return carry

    lax.fori_loop(0, nhp // 2, head_body, 0)


def _peer_route(x, mod, g, wq, keys, *, n_ctx, dec_seq, tq):
    t, d = x.shape
    nhp, nk, dkh = keys.shape
    nh = nhp // 2
    const = lambda i: (0, 0)
    hshape = jax.ShapeDtypeStruct((nh, nk, t), F32)
    hspec = pl.BlockSpec((nh, nk, tq), lambda i: (0, 0, i))
    per = PEER_GROUP // nk
    gshape = jax.ShapeDtypeStruct((nh, nk // per, per, t), F32)
    gspec = pl.BlockSpec((nh, nk // per, per, tq), lambda i: (0, 0, 0, i))
    pad = V7X_SUBLANES // per
    est = (2 * tq * d * 4 + 4 * d * nhp * dkh + 2 * tq * d * 2 + (4 + 4 * pad) * nh * nk * tq * 4
           + 2 * nhp * nk * tq * 4 + 6 * tq * nhp * dkh * 4)
    return pl.pallas_call(
        _peer_route_kernel,
        grid=(t // tq,),
        in_specs=[pl.BlockSpec((tq, d), lambda i: (i, 0)),
                  pl.BlockSpec((None, 6, d), lambda i: (_cond_index(i * tq, n_ctx, dec_seq), 0, 0)),
                  pl.BlockSpec((1, d), const),
                  pl.BlockSpec(wq.shape, const),
                  pl.BlockSpec(keys.shape, lambda i: (0, 0, 0))],
        out_specs=[pl.BlockSpec((tq, d), lambda i: (i, 0)), gspec, gspec, hspec, hspec],
        out_shape=[jax.ShapeDtypeStruct((t, d), BF16), gshape, gshape, hshape, hshape],
        scratch_shapes=[pltpu.VMEM((nhp, tq, dkh), BF16),
                        pltpu.VMEM((nhp, nk, tq), F32),
                        pltpu.VMEM((nhp, PEER_TOPK, tq), F32)],
        compiler_params=_cparams(est, ("parallel",)),
        name="peer_route",
    )(x, mod, g, wq, keys)


def _peer_expert_kernel(hbt_ref, u_ref, vt_ref, th_ref, e0_ref, s1_ref, e1_ref, x_ref, mod_ref, o_ref,
                        acc, st_a, st_b, hbuf_a, hbuf_b, *, nc):
    g = pl.program_id(0)
    nh = th_ref.shape[0]
    nk = s1_ref.shape[1]
    ngroups, per = th_ref.shape[1], th_ref.shape[2]
    gw = per * nk
    tq = st_a.shape[1]
    kh = PEER_ACC_ROWS // per
    c3 = jnp.maximum(g - 2, 0) % nc

    @pl.when(g == 0)
    def _():
        st_b[...] = jnp.zeros_like(st_b)
        hbuf_a[...] = jnp.zeros_like(hbuf_a)
        hbuf_b[...] = jnp.zeros_like(hbuf_b)

    @pl.when(c3 == 0)
    def _():
        acc[...] = jnp.zeros_like(acc)

    def step(st_w, st_r, hbuf_w, hbuf_r):
        def group(j, carry):
            rows = pl.ds(pl.multiple_of(j * gw, gw), gw)
            acc[...] += jnp.dot(vt_ref[j], hbuf_r[rows, :], preferred_element_type=F32)
            st_w[rows, :] = jnp.dot(u_ref[rows, :], hbt_ref[...], preferred_element_type=F32)
            for q in range(tq // V7X_LANES):
                lanes = slice(q * V7X_LANES, (q + 1) * V7X_LANES)
                for half in range(nk // kh):
                    keys = slice(half * kh, (half + 1) * kh)
                    w = [None] * per
                    for hd in range(nh):
                        s1 = s1_ref[hd, keys, lanes]
                        e1 = e1_ref[hd, keys, lanes]
                        for i in range(per):
                            sel = jnp.where(s1 >= th_ref[hd, j, i:i + 1, lanes], e1, 0.0) * e0_ref[hd, j, i:i + 1, lanes]
                            w[i] = sel if w[i] is None else w[i] + sel
                    for i in range(per):
                        tile = pl.ds(pl.multiple_of(j * gw + i * nk + half * kh, kh), kh)
                        hbuf_w[tile, lanes] = (_gelu(st_r[tile, lanes]) * w[i]).astype(BF16)
            return carry

        lax.fori_loop(0, ngroups, group, 0)

    @pl.when(g % 2 == 0)
    def _():
        step(st_a, st_b, hbuf_b, hbuf_a)

    @pl.when(g % 2 == 1)
    def _():
        step(st_b, st_a, hbuf_a, hbuf_b)

    @pl.when(jnp.logical_and(g >= 2, c3 == nc - 1))
    def _():
        o_ref[...] = x_ref[...] + mod_ref[5:6, :] * acc[...].T


def _peer_expert(hb, u_tab, vt_tab, th, e0, s1, e1, x, mod, *, n_ctx, dec_seq, tq, ec):
    t, d = x.shape
    ne = u_tab.shape[0]
    nh, nk, _ = s1.shape
    per = th.shape[2]
    assert per * nk == PEER_GROUP
    ngroups = ec // PEER_GROUP
    nc = ne // ec
    npairs = (t // tq) * nc

    def pair(g, lag):
        p = jnp.clip(g - lag, 0, npairs - 1)
        return p // nc, p % nc

    def cond3(g):
        return _cond_index(pair(g, 2)[0] * tq, n_ctx, dec_seq)

    est = (2 * tq * d * 2 + 4 * ec * d * 2 + 4 * nh * ngroups * V7X_SUBLANES * tq * 4 + 4 * nh * nk * tq * 4
           + 4 * tq * d * 4 + d * tq * 4 + 2 * ec * tq * 2 + 2 * ec * tq * 4 + 3 * PEER_GROUP * tq * 4)
    return pl.pallas_call(
        functools.partial(_peer_expert_kernel, nc=nc),
        grid=(npairs + 2,),
        in_specs=[pl.BlockSpec((d, tq), lambda g: (0, pair(g, 0)[0])),
                  pl.BlockSpec((ec, d), lambda g: (pair(g, 0)[1], 0)),
                  pl.BlockSpec((ngroups, d, PEER_GROUP), lambda g: (pair(g, 2)[1], 0, 0)),
                  pl.BlockSpec((nh, ngroups, per, tq), lambda g: (0, pair(g, 1)[1], 0, pair(g, 1)[0])),
                  pl.BlockSpec((nh, ngroups, per, tq), lambda g: (0, pair(g, 1)[1], 0, pair(g, 1)[0])),
                  pl.BlockSpec((nh, nk, tq), lambda g: (0, 0, pair(g, 1)[0])),
                  pl.BlockSpec((nh, nk, tq), lambda g: (0, 0, pair(g, 1)[0])),
                  pl.BlockSpec((tq, d), lambda g: (pair(g, 2)[0], 0)),
                  pl.BlockSpec((None, 6, d), lambda g: (cond3(g), 0, 0))],
        out_specs=pl.BlockSpec((tq, d), lambda g: (pair(g, 2)[0], 0)),
        out_shape=jax.ShapeDtypeStruct((t, d), F32),
        scratch_shapes=[pltpu.VMEM((d, tq), F32), pltpu.VMEM((ec, tq), F32), pltpu.VMEM((ec, tq), F32),
                        pltpu.VMEM((ec, tq), BF16), pltpu.VMEM((ec, tq), BF16)],
        compiler_params=_cparams(est, ("arbitrary",)),
        name="peer_expert",
    )(hb.T, u_tab, vt_tab, th, e0, s1, e1, x, mod)


def _final_kernel(x_ref, g_ref, o_ref):
    x = x_ref[...]
    ms = jnp.mean(x * x, axis=-1, keepdims=True)
    o_ref[...] = (x * lax.rsqrt(ms + EPS)) * g_ref[...]


def _final_norm(x, g, *, tm):
    t, d = x.shape
    return pl.pallas_call(
        _final_kernel,
        grid=(t // tm,),
        in_specs=[pl.BlockSpec((tm, d), lambda i: (i, 0)), pl.BlockSpec((1, d), lambda i: (0, 0))],
        out_specs=pl.BlockSpec((tm, d), lambda i: (i, 0)),
        out_shape=jax.ShapeDtypeStruct((t, d), F32),
        compiler_params=_cparams(6 * tm * d * 4, ("parallel",)),
        name="final_norm",
    )(x, g)


def _block_diag(w):
    h, i, j = w.shape
    return jnp.einsum('hij,hg->higj', w, jnp.eye(h, dtype=w.dtype)).reshape(h * i, h * j)


def _s5_discretize(a_re, a_im, log_dt, b_re, b_im):
    dt = jnp.exp(log_dt)[:, None]
    mag = jnp.exp(a_re * dt)
    abar_r = mag * jnp.cos(a_im * dt)
    abar_i = mag * jnp.sin(a_im * dt)
    den = a_re * a_re + a_im * a_im
    num_r = abar_r - 1.0
    coef_r = ((num_r * a_re + abar_i * a_im) / den)[..., None]
    coef_i = ((abar_i * a_re - num_r * a_im) / den)[..., None]
    return abar_r, abar_i, coef_r * b_re - coef_i * b_im, coef_r * b_im + coef_i * b_re


def _s5_blocks(abar_r, abar_i, bbar_r, bbar_i, c_re, c_im):
    g, p, ch = bbar_r.shape
    gb = g // S5_COLBLK

    def vec(v):
        return v.reshape(S5_COLBLK, 1, gb * p)

    def bmat(b):
        bt = b.transpose(0, 2, 1).reshape(S5_COLBLK, gb, ch, p)
        return jnp.stack([_block_diag(bt[j]) for j in range(S5_COLBLK)])

    def cmat(cc):
        ct = cc.transpose(0, 2, 1).reshape(S5_COLBLK, gb, p, ch)
        return jnp.stack([_block_diag(ct[j]) for j in range(S5_COLBLK)])

    bblk = jnp.concatenate([bmat(bbar_r), bmat(bbar_i)], axis=2).astype(BF16)
    cblk = jnp.concatenate([cmat(c_re), cmat(-c_im)], axis=1).astype(BF16)
    return bblk, cblk, vec(abar_r), vec(abar_i)


def _state_to_blocked(re, im):
    n, g, p = re.shape
    gb = g // S5_COLBLK
    both = jnp.stack([re.reshape(n, S5_COLBLK, gb * p), im.reshape(n, S5_COLBLK, gb * p)], axis=2)
    return both.reshape(n, -1)


def _state_from_blocked(st, g, p):
    n = st.shape[0]
    both = st.reshape(n, S5_COLBLK, 2, (g // S5_COLBLK) * p)
    return both[:, :, 0].reshape(n, g, p), both[:, :, 1].reshape(n, g, p)


def _complex_power(ar, ai, n):
    for _ in range(int(math.log2(n))):
        ar, ai = ar * ar - ai * ai, 2.0 * ar * ai
    return ar, ai


def _mixer_ab(xs, mod, p, le, h0, *, bc, lc, bs, ls, tm):
    n_ctx = bc * lc
    width = p['s5_d'].shape[1]
    g_cnt, p_cnt = p['s5_a_re'].shape[2:]
    w_gates = jnp.concatenate([_block_diag(p['lru_w_a'][le, 0]), _block_diag(p['lru_w_x'][le, 0]),
                               _block_diag(p['lru_w_a'][le, 1]), _block_diag(p['lru_w_x'][le, 1])],
                              axis=1).astype(BF16)
    b_gates = jnp.concatenate([p['lru_b_a'][le, 0], p['lru_b_x'][le, 0],
                               p['lru_b_a'][le, 1], p['lru_b_x'][le, 1]]).reshape(1, 4 * width)
    c8 = LRU_C * jax.nn.softplus(-p['lru_lambda'][le])
    pass_ctx = (n_ctx, 0)
    w_in = p['w_in_ab'][le].astype(BF16)
    s_arrs, xgs = zip(*[_ab_in(x, mod, p['norm1_g'][2 * le].reshape(1, -1), w_in,
                               p['lru_conv_w'][le], p['lru_conv_b'][le].reshape(1, width), w_gates, b_gates, c8,
                               n_ctx=nc_, ctx_row=lc, dec_seq=ls, tm=tm) for x, nc_ in zip(xs, pass_ctx)])

    cols = s_arrs[0].shape[1]
    gc = bc // SCAN_SEQS
    lseg = ls // SAMPLE_SEGS
    s_ctx = (s_arrs[0].reshape(gc, SCAN_SEQS, lc, cols).transpose(0, 2, 1, 3)
             .reshape(n_ctx, cols))
    s_smp = (s_arrs[1].reshape(bs * SAMPLE_SEGS, lseg, cols).transpose(1, 0, 2)
             .reshape(bs * ls, cols))
    chunk_rows = min(1024, lc * SCAN_SEQS, lseg * SCAN_SEQS)

    outs = {}
    finals = {}
    for d in range(2):
        abar_r, abar_i, bbar_r, bbar_i = _s5_discretize(
            p['s5_a_re'][le, d], p['s5_a_im'][le, d], p['s5_log_dt'][le, d],
            p['s5_b_re'][le, d], p['s5_b_im'][le, d])
        bblk, cblk, ar, ai = _s5_blocks(abar_r, abar_i, bbar_r, bbar_i,
                                        p['s5_c_re'][le, d], p['s5_c_im'][le, d])
        nstate = 2 * g_cnt * p_cnt
        z5 = jnp.zeros((gc, SCAN_SEQS, nstate), F32)
        zl = jnp.zeros((gc, SCAN_SEQS, width), F32)
        y, hl, f5, fl, _ = _scan(s_ctx, bblk, cblk, ar, ai, z5, zl, direction=d, store=True,
                                 groups=gc, chunk_rows=chunk_rows)
        outs[('ctx', d)] = (y, hl)
        finals[d] = (f5.reshape(bc, nstate), fl.reshape(bc, width))
        z5 = jnp.zeros((1, SCAN_SEQS, nstate), F32)
        zl = jnp.zeros((1, SCAN_SEQS, width), F32)
        e5, el, ep = _scan(s_smp, bblk, cblk, ar, ai, z5, zl, direction=d, store=False,
                           groups=1, chunk_rows=chunk_rows)
        first = (SAMPLE_SEGS - 1) if d == 1 else 0
        h5 = _state_to_blocked(h0[0][:, d], h0[1][:, d])
        h5_rows = jnp.zeros((bs, SAMPLE_SEGS, nstate), F32).at[:, first].set(h5).reshape(SCAN_SEQS, nstate)
        hl_rows = jnp.zeros((bs, SAMPLE_SEGS, width), F32).at[:, first].set(h0[2][:, d]).reshape(SCAN_SEQS, width)
        arl, ail = _complex_power(ar, ai, lseg)
        i5, il = _chain(e5[0], el[0], ep[0], h5_rows, hl_rows, arl, ail, direction=d)
        y, hl, _, _, _ = _scan(s_smp, bblk, cblk, ar, ai, i5[None], il[None], direction=d, store=True,
                               groups=1, chunk_rows=chunk_rows)
        outs[('smp', d)] = (y, hl)

    def to_tokens(name, arr):
        if name == 'ctx':
            return arr.reshape(gc, lc, SCAN_SEQS, width).transpose(0, 2, 1, 3).reshape(n_ctx, width)
        return arr.reshape(lseg, bs * SAMPLE_SEGS, width).transpose(1, 0, 2).reshape(bs * ls, width)

    w_glu, w_out = p['s5_w_glu'][le].astype(BF16), p['w_out_ab'][le].astype(BF16)
    x1 = []
    for name, x, s_arr, xg, nc_ in zip(('ctx', 'smp'), xs, s_arrs, xgs, pass_ctx):
        yf, hf = (to_tokens(name, a) for a in outs[(name, 0)])
        yb, hb = (to_tokens(name, a) for a in outs[(name, 1)])
        x1.append(_ab_out(x, mod, s_arr, xg, yf, yb, hf, hb, p['s5_d'][le].reshape(1, width),
                          w_glu, p['s5_b_glu'][le].reshape(1, width), w_out, n_ctx=nc_, dec_seq=ls, tm=tm))
    st = [_state_from_blocked(finals[d][0], g_cnt, p_cnt) for d in range(2)]
    new_re = jnp.stack([st[0][0], st[1][0]], axis=1)
    new_im = jnp.stack([st[0][1], st[1][1]], axis=1)
    new_lru = jnp.stack([finals[0][1], finals[1][1]], axis=1)
    return x1, (new_re, new_im, new_lru)


def _fft_split(length):
    n = 2 * length
    n1 = 2
    while n // n1 > 1024:
        n1 *= 2
    return n1


def _mixer_c(xs, mod, p, lo, *, bc, lc, bs, ls, tm):
    width = p['hy_bias'].shape[1]
    w_in, w_out = p['w_in_c'][lo].astype(BF16), p['w_out_c'][lo].astype(BF16)
    out = []
    for x, b, l, nc_ in zip(xs, (bc, bs), (lc, ls), (bc * lc, 0)):
        x0, vg = _c_in(x, mod, p['norm1_g'][2 * lo + 1].reshape(1, -1), w_in,
                       p['b_in_c'][lo].reshape(1, -1), p['hy_conv_w'][lo], p['hy_conv_b'][lo].reshape(1, -1),
                       n_ctx=nc_, ctx_row=lc, dec_seq=ls, tm=tm)
        ct = min(width, 256 if l <= 1024 else 128)
        circ = _hy_filter(l, p['hy_w1'][lo], p['hy_b1'][lo], p['hy_freq1'][lo], p['hy_w2'][lo],
                          p['hy_b2'][lo], p['hy_freq2'][lo], p['hy_w3'][lo], p['hy_decay'][lo], ct=ct)
        yconv = _fftconv(vg.reshape(b, l, width), circ, n1=_fft_split(l), ct=ct).reshape(b * l, width)
        out.append(_c_out(x, mod, x0, vg, yconv, p['hy_bias'][lo].reshape(1, width), w_out,
                          p['b_out_c'][lo].reshape(1, -1), n_ctx=nc_, dec_seq=ls, tm=tm))
    return out


def _peer(xs, mod, p, l, *, pass_ctx, dec_seq, tq, ec):
    keys = p['peer_keys'][l]
    nh, _, nk, dkh = keys.shape
    wq = p['peer_wq'][l].astype(BF16)
    keys = keys.reshape(nh * 2, nk, dkh).astype(BF16)
    ne, d = p['peer_v'][l].shape
    u = p['peer_u'][l].astype(BF16)
    vt = p['peer_v'][l].reshape(ne // PEER_GROUP, PEER_GROUP, d).transpose(0, 2, 1).astype(BF16)
    out = []
    for x, nc_ in zip(xs, pass_ctx):
        hb, th, e0, s1, e1 = _peer_route(x, mod, p['norm2_g'][l].reshape(1, -1), wq, keys,
                                         n_ctx=nc_, dec_seq=dec_seq, tq=tq)
        out.append(_peer_expert(hb, u, vt, th, e0, s1, e1, x, mod, n_ctx=nc_, dec_seq=dec_seq, tq=tq, ec=ec))
    return out


def _forward(x_prompt, x_sample, state_s5_re, state_s5_im, state_lru, c, c_ctx, p, *, tm, tq, ec):
    bc, lc, d = x_prompt.shape
    bs, ls, _ = x_sample.shape
    assert bc % SCAN_SEQS == 0 and bs * SAMPLE_SEGS == SCAN_SEQS
    assert tm % lc == 0 and ls % tm == 0 and (ls // SAMPLE_SEGS) % GRID_W == 0
    n_ctx = bc * lc
    xs = [x_prompt.reshape(n_ctx, d), x_sample.reshape(bs * ls, d)]
    pass_ctx = (n_ctx, 0)
    cond = jnp.zeros((V7X_SUBLANES, d), F32).at[0].set(c_ctx).at[1:1 + bs].set(c)
    mod_all = _modulation(cond, p['w_mod'], p['b_mod'])
    states = None
    for l in range(DEPTH):
        mod = mod_all[l].reshape(V7X_SUBLANES, 6, d)
        if l % 2 == 0:
            le = l // 2
            h0 = (state_s5_re[:, le], state_s5_im[:, le], state_lru[:, le])
            xs, st = _mixer_ab(xs, mod, p, le, h0, bc=bc, lc=lc, bs=bs, ls=ls, tm=tm)
            states = st if states is None else states
        else:
            xs = _mixer_c(xs, mod, p, l // 2, bc=bc, lc=lc, bs=bs, ls=ls, tm=tm)
        xs = _peer(xs, mod, p, l, pass_ctx=pass_ctx, dec_seq=ls, tq=tq, ec=ec)
    g = p['final_g'].reshape(1, d)
    new_re, new_im, new_lru = states
    return (_final_norm(xs[0], g, tm=tm).reshape(bc, lc, d), _final_norm(xs[1], g, tm=tm).reshape(bs, ls, d),
            new_re[:, None], new_im[:, None], new_lru[:, None])


def kernel(x_prompt, x_sample, state_s5_re, state_s5_im, state_lru, c, c_ctx, norm1_g, norm2_g, w_mod, b_mod, w_in_ab, s5_a_re, s5_a_im, s5_log_dt, s5_b_re, s5_b_im, s5_c_re, s5_c_im, s5_d, s5_w_glu, s5_b_glu, lru_conv_w, lru_conv_b, lru_w_a, lru_b_a, lru_w_x, lru_b_x, lru_lambda, w_out_ab, w_in_c, b_in_c, hy_conv_w, hy_conv_b, hy_w1, hy_b1, hy_freq1, hy_w2, hy_b2, hy_freq2, hy_w3, hy_decay, hy_bias, w_out_c, b_out_c, peer_wq, peer_keys, peer_u, peer_v, final_g):
    p = dict(norm1_g=norm1_g, norm2_g=norm2_g, w_mod=w_mod, b_mod=b_mod, w_in_ab=w_in_ab,
             s5_a_re=s5_a_re, s5_a_im=s5_a_im, s5_log_dt=s5_log_dt, s5_b_re=s5_b_re, s5_b_im=s5_b_im,
             s5_c_re=s5_c_re, s5_c_im=s5_c_im, s5_d=s5_d, s5_w_glu=s5_w_glu, s5_b_glu=s5_b_glu,
             lru_conv_w=lru_conv_w, lru_conv_b=lru_conv_b, lru_w_a=lru_w_a, lru_b_a=lru_b_a,
             lru_w_x=lru_w_x, lru_b_x=lru_b_x, lru_lambda=lru_lambda, w_out_ab=w_out_ab,
             w_in_c=w_in_c, b_in_c=b_in_c, hy_conv_w=hy_conv_w, hy_conv_b=hy_conv_b,
             hy_w1=hy_w1, hy_b1=hy_b1, hy_freq1=hy_freq1, hy_w2=hy_w2, hy_b2=hy_b2, hy_freq2=hy_freq2,
             hy_w3=hy_w3, hy_decay=hy_decay, hy_bias=hy_bias, w_out_c=w_out_c, b_out_c=b_out_c,
             peer_wq=peer_wq, peer_keys=peer_keys, peer_u=peer_u, peer_v=peer_v, final_g=final_g)
    return _forward(x_prompt, x_sample, state_s5_re, state_s5_im, state_lru, c, c_ctx, p,
                    tm=512, tq=512, ec=2048)
```

```python
import functools
import math

import jax
import jax.numpy as jnp
import numpy as np
from jax import lax
from jax.experimental import pallas as pl
from jax.experimental.pallas import tpu as pltpu

F32 = jnp.float32
BF16 = jnp.bfloat16
HIGHEST = lax.Precision.HIGHEST

DEPTH = 2
GRID_W = 64
EPS = 1e-6
S5_CH = 16
S5_STATE = 64
LRU_HEADS = 8
LRU_C = 8.0
HY_BANDS = 16
HY_SHIFT = 0.05
PEER_HEADS = 8
PEER_NKEYS = 128
PEER_TOPK = 16
PEER_GROUP = 512
PEER_ACC_ROWS = 256

V7X_LANES = 128
V7X_SUBLANES = 8
V7X_VMEM_BYTES = 64 * 1024 * 1024

SCAN_SEQS = V7X_SUBLANES
SAMPLE_SEGS = 4
S5_COLBLK = 4


def _vmem_limit(est_bytes):
    return int(min(V7X_VMEM_BYTES - (8 << 20), max(32 << 20, est_bytes * 3 // 2)))


def _cparams(est_bytes, sem):
    return pltpu.CompilerParams(dimension_semantics=sem, vmem_limit_bytes=_vmem_limit(est_bytes))


def _gelu(x):
    k = 2.0 * 0.7978845608028654 * math.log2(math.e)
    return x / (1.0 + jnp.exp2(x * (-k - (k * 0.044715) * (x * x))))


def _norm_mod(x, g, sc, sh):
    ms = jnp.mean(x * x, axis=-1, keepdims=True)
    return (x * lax.rsqrt(ms + EPS)) * g * (1.0 + sc) + sh


def _cond_index(row0, n_ctx, dec_seq):
    return jnp.where(row0 < n_ctx, 0, 1 + (row0 - n_ctx) // dec_seq)


def _mod_kernel(c_ref, w_ref, b_ref, o_ref):
    c = c_ref[...]
    s = c * jax.nn.sigmoid(c)
    o_ref[...] = jnp.dot(s, w_ref[...], precision=HIGHEST, preferred_element_type=F32) + b_ref[...]


def _modulation(cond, w_mod, b_mod):
    depth, d, n = w_mod.shape
    rows = cond.shape[0]
    tn = 1536
    est = 2 * d * tn * 4 + 4 * rows * (d + tn) * 4
    return pl.pallas_call(
        _mod_kernel,
        grid=(depth, n // tn),
        in_specs=[pl.BlockSpec((rows, d), lambda l, j: (0, 0)),
                  pl.BlockSpec((None, d, tn), lambda l, j: (l, 0, j)),
                  pl.BlockSpec((None, 1, tn), lambda l, j: (l, 0, j))],
        out_specs=pl.BlockSpec((None, rows, tn), lambda l, j: (l, 0, j)),
        out_shape=jax.ShapeDtypeStruct((depth, rows, n), F32),
        compiler_params=_cparams(est, ("arbitrary", "arbitrary")),
        name="modulation",
    )(cond, w_mod, b_mod.reshape(depth, 1, n))


def _dwconv_rows(x, w, bias, pos, row_len, pad_left):
    tm = x.shape[0]
    acc = None
    for k in range(w.shape[0]):
        d = k - pad_left
        if d == 0:
            term = x
        else:
            shifted = pltpu.roll(x, (-d) % tm, axis=0)
            q = pos + d
            term = jnp.where(q >= 0, jnp.where(q < row_len, shifted, 0.0), 0.0)
        term = term * w[k:k + 1]
        acc = term if acc is None else acc + term
    return acc + bias


def _ab_in_kernel(x_ref, mod_ref, g_ref, w_ref, cw_ref, cb_ref, wg_ref, bg_ref, c8_ref,
                  s_ref, xg_ref, *, n_ctx, ctx_row, width):
    tm = x_ref.shape[0]
    m = mod_ref[...]
    h = _norm_mod(x_ref[...], g_ref[...], m[1:2], m[0:1])
    z = jnp.dot(h.astype(BF16), w_ref[...], preferred_element_type=F32)
    u = z[:, :width]
    xr = z[:, width:2 * width]
    xg_ref[...] = z[:, 2 * width:]
    row_len = jnp.where(pl.program_id(0) * tm < n_ctx, ctx_row, GRID_W)
    pos = lax.broadcasted_iota(jnp.int32, (tm, width), 0) & (row_len - 1)
    xb = _dwconv_rows(xr, cw_ref[...], cb_ref[...], pos, row_len, 2)
    gates = jnp.dot(xb.astype(BF16), wg_ref[...], preferred_element_type=F32) + bg_ref[...]
    s_ref[:, :width] = u
    c8 = c8_ref[...]
    for d in range(2):
        r = jax.nn.sigmoid(gates[:, (2 * d) * width:(2 * d + 1) * width])
        i = jax.nn.sigmoid(gates[:, (2 * d + 1) * width:(2 * d + 2) * width])
        a = jnp.exp(-(c8[d:d + 1] * r))
        b = jnp.sqrt(1.0 - a * a) * (i * xb)
        s_ref[:, (1 + 2 * d) * width:(2 + 2 * d) * width] = a
        s_ref[:, (2 + 2 * d) * width:(3 + 2 * d) * width] = b


def _ab_in(x, mod, g, w_in, conv_w, conv_b, w_gates, b_gates, c8, *, n_ctx, ctx_row, dec_seq, tm):
    t, d = x.shape
    width = w_in.shape[1] // 3
    kern = functools.partial(_ab_in_kernel, n_ctx=n_ctx, ctx_row=ctx_row, width=width)
    const = lambda i: (0, 0)
    est = (2 * tm * d * 4 + d * 3 * width * 2 * 2 + width * 4 * width * 2 * 2
           + 2 * tm * 6 * width * 4 + tm * 12 * width * 4)
    return pl.pallas_call(
        kern,
        grid=(t // tm,),
        in_specs=[pl.BlockSpec((tm, d), lambda i: (i, 0)),
                  pl.BlockSpec((None, 6, d), lambda i: (_cond_index(i * tm, n_ctx, dec_seq), 0, 0)),
                  pl.BlockSpec((1, d), const),
                  pl.BlockSpec(w_in.shape, const),
                  pl.BlockSpec(conv_w.shape, const),
                  pl.BlockSpec((1, width), const),
                  pl.BlockSpec(w_gates.shape, const),
                  pl.BlockSpec((1, 4 * width), const),
                  pl.BlockSpec((2, width), const)],
        out_specs=[pl.BlockSpec((tm, 5 * width), lambda i: (i, 0)),
                   pl.BlockSpec((tm, width), lambda i: (i, 0))],
        out_shape=[jax.ShapeDtypeStruct((t, 5 * width), F32),
                   jax.ShapeDtypeStruct((t, width), F32)],
        compiler_params=_cparams(est, ("parallel",)),
        name="ab_in",
    )(x, mod, g, w_in, conv_w, conv_b, w_gates, b_gates, c8)


def _scan_kernel(u_ref, a_ref, b_ref, bblk_ref, cblk_ref, ar_ref, ai_ref, i5_ref, il_ref, *rest,
                 reverse, store):
    if store:
        y_ref, hl_ref, f5_ref, fl_ref, fp_ref, buf, st5, stl, stp = rest
    else:
        f5_ref, fl_ref, fp_ref, buf, st5, stl, stp = rest
    c = pl.program_id(1)
    rows = u_ref.shape[0]
    nsteps = rows // SCAN_SEQS
    sw = st5.shape[1] // (2 * S5_COLBLK)
    cw = u_ref.shape[1] // S5_COLBLK

    @pl.when(c == 0)
    def _():
        st5[...] = i5_ref[...]
        stl[...] = il_ref[...]
        stp[...] = jnp.ones_like(stp)

    def row0(k):
        t = (nsteps - 1 - k) if reverse else k
        return pl.multiple_of(t * SCAN_SEQS, SCAN_SEQS)

    u = u_ref[...].astype(BF16)
    for j in range(S5_COLBLK):
        buf[...] = jnp.dot(u[:, j * cw:(j + 1) * cw], bblk_ref[j], preferred_element_type=F32)
        arj = jnp.broadcast_to(ar_ref[j], (SCAN_SEQS, sw))
        aij = jnp.broadcast_to(ai_ref[j], (SCAN_SEQS, sw))

        def step(k, carry, arj=arj, aij=aij):
            hr, hi = carry
            r0 = row0(k)
            xr = buf[pl.ds(r0, SCAN_SEQS), 0:sw]
            xi = buf[pl.ds(r0, SCAN_SEQS), sw:2 * sw]
            nr = arj * hr - aij * hi + xr
            ni = arj * hi + aij * hr + xi
            buf[pl.ds(r0, SCAN_SEQS), 0:sw] = nr
            buf[pl.ds(r0, SCAN_SEQS), sw:2 * sw] = ni
            return nr, ni

        base = 2 * sw * j
        hr, hi = lax.fori_loop(0, nsteps, step, (st5[:, base:base + sw], st5[:, base + sw:base + 2 * sw]))
        st5[:, base:base + sw] = hr
        st5[:, base + sw:base + 2 * sw] = hi
        if store:
            y_ref[:, j * cw:(j + 1) * cw] = jnp.dot(buf[...].astype(BF16), cblk_ref[j],
                                                    preferred_element_type=F32)

    def lstep(k, carry):
        h, p = carry
        r0 = row0(k)
        a = a_ref[pl.ds(r0, SCAN_SEQS), :]
        h = a * h + b_ref[pl.ds(r0, SCAN_SEQS), :]
        if store:
            hl_ref[pl.ds(r0, SCAN_SEQS), :] = h
        return h, p * a

    h, p = lax.fori_loop(0, nsteps, lstep, (stl[...], stp[...]))
    stl[...] = h
    stp[...] = p

    @pl.when(c == pl.num_programs(1) - 1)
    def _():
        f5_ref[...] = st5[...]
        fl_ref[...] = stl[...]
        fp_ref[...] = stp[...]


def _scan(s_arr, bblk, cblk, ar, ai, init5, initl, *, direction, store, groups, chunk_rows):
    rows_total, cols = s_arr.shape
    width = cols // 5
    rows_per_group = rows_total // groups
    nc = rows_per_group // chunk_rows
    reverse = direction == 1
    nstate = init5.shape[-1]

    def chunk(g, c):
        return g * nc + ((nc - 1 - c) if reverse else c)

    kern = functools.partial(_scan_kernel, reverse=reverse, store=store)
    const3 = lambda g, c: (0, 0, 0)
    in_specs = [pl.BlockSpec((chunk_rows, width), lambda g, c: (chunk(g, c), 0)),
                pl.BlockSpec((chunk_rows, width), lambda g, c: (chunk(g, c), 1 + 2 * direction)),
                pl.BlockSpec((chunk_rows, width), lambda g, c: (chunk(g, c), 2 + 2 * direction)),
                pl.BlockSpec(bblk.shape, const3),
                pl.BlockSpec(cblk.shape, const3),
                pl.BlockSpec(ar.shape, const3),
                pl.BlockSpec(ai.shape, const3),
                pl.BlockSpec((None, SCAN_SEQS, nstate), lambda g, c: (g, 0, 0)),
                pl.BlockSpec((None, SCAN_SEQS, width), lambda g, c: (g, 0, 0))]
    fin_specs = [pl.BlockSpec((None, SCAN_SEQS, nstate), lambda g, c: (g, 0, 0)),
                 pl.BlockSpec((None, SCAN_SEQS, width), lambda g, c: (g, 0, 0)),
                 pl.BlockSpec((None, SCAN_SEQS, width), lambda g, c: (g, 0, 0))]
    fin_shapes = [jax.ShapeDtypeStruct((groups, SCAN_SEQS, nstate), F32),
                  jax.ShapeDtypeStruct((groups, SCAN_SEQS, width), F32),
                  jax.ShapeDtypeStruct((groups, SCAN_SEQS, width), F32)]
    out_specs, out_shapes = fin_specs, fin_shapes
    if store:
        out_specs = [pl.BlockSpec((chunk_rows, width), lambda g, c: (chunk(g, c), 0))] * 2 + fin_specs
        out_shapes = [jax.ShapeDtypeStruct((rows_total, width), F32)] * 2 + fin_shapes
    bufw = 2 * nstate // (2 * S5_COLBLK)
    est = (10 * chunk_rows * width * 4 + chunk_rows * bufw * 4 * 2
           + 4 * bblk.size * 2 + 8 * SCAN_SEQS * nstate * 4)
    return pl.pallas_call(
        kern,
        grid=(groups, nc),
        in_specs=in_specs,
        out_specs=out_specs,
        out_shape=out_shapes,
        scratch_shapes=[pltpu.VMEM((chunk_rows, bufw), F32),
                        pltpu.VMEM((SCAN_SEQS, nstate), F32),
                        pltpu.VMEM((SCAN_SEQS, width), F32),
                        pltpu.VMEM((SCAN_SEQS, width), F32)],
        compiler_params=_cparams(est, ("arbitrary", "arbitrary")),
        name="scan_store" if store else "scan_ends",
    )(s_arr, s_arr, s_arr, bblk, cblk, ar, ai, init5, initl)


def _chain_kernel(e5_ref, el_ref, ep_ref, h5_ref, hl_ref, arl_ref, ail_ref, i5_ref, il_ref, *, reverse):
    nseg = SAMPLE_SEGS
    sw = arl_ref.shape[-1]
    shift = (SCAN_SEQS - 1) if reverse else 1
    order = range(nseg - 2, -1, -1) if reverse else range(1, nseg)
    seg5 = lax.broadcasted_iota(jnp.int32, (SCAN_SEQS, sw), 0) % nseg
    for j in range(S5_COLBLK):
        base = 2 * sw * j
        re, im = slice(base, base + sw), slice(base + sw, base + 2 * sw)
        ar, ai = arl_ref[j], ail_ref[j]
        er = pltpu.roll(e5_ref[:, re], shift, axis=0)
        ei = pltpu.roll(e5_ref[:, im], shift, axis=0)
        hr, hi = h5_ref[:, re], h5_ref[:, im]
        for k in order:
            pr = pltpu.roll(hr, shift, axis=0)
            pi = pltpu.roll(hi, shift, axis=0)
            hr = jnp.where(seg5 == k, ar * pr - ai * pi + er, hr)
            hi = jnp.where(seg5 == k, ar * pi + ai * pr + ei, hi)
        i5_ref[:, re] = hr
        i5_ref[:, im] = hi
    segl = lax.broadcasted_iota(jnp.int32, el_ref.shape, 0) % nseg
    el = pltpu.roll(el_ref[...], shift, axis=0)
    ep = pltpu.roll(ep_ref[...], shift, axis=0)
    h = hl_ref[...]
    for k in order:
        h = jnp.where(segl == k, ep * pltpu.roll(h, shift, axis=0) + el, h)
    il_ref[...] = h


def _chain(end5, endl, endp, h5_rows, hl_rows, arl, ail, *, direction):
    kern = functools.partial(_chain_kernel, reverse=direction == 1)
    return pl.pallas_call(
        kern,
        out_shape=[jax.ShapeDtypeStruct(end5.shape, F32), jax.ShapeDtypeStruct(endl.shape, F32)],
        name="scan_chain",
    )(end5, endl, endp, h5_rows, hl_rows, arl, ail)


def _ab_out_kernel(x_ref, mod_ref, s_ref, xg_ref, yf_ref, yb_ref, hf_ref, hb_ref, d_ref,
                   wglu_ref, bglu_ref, wout_ref, o_ref):
    width = xg_ref.shape[1]
    m = mod_ref[...]
    y = s_ref[...] * d_ref[...] + yf_ref[...] + yb_ref[...]
    zs = _gelu(y)
    gate = jnp.dot(zs.astype(BF16), wglu_ref[...], preferred_element_type=F32) + bglu_ref[...]
    s5_out = zs * jax.nn.sigmoid(gate)
    lru_out = (hf_ref[...] + hb_ref[...]) * _gelu(xg_ref[...])
    out = (jnp.dot(s5_out.astype(BF16), wout_ref[:width, :], preferred_element_type=F32)
           + jnp.dot(lru_out.astype(BF16), wout_ref[width:, :], preferred_element_type=F32))
    o_ref[...] = x_ref[...] + m[2:3] * out


def _ab_out(x, mod, s_arr, xg, yf, yb, hf, hb, s5_d, w_glu, b_glu, w_out, *, n_ctx, dec_seq, tm):
    t, d = x.shape
    width = xg.shape[1]
    const = lambda i: (0, 0)
    row = lambda i: (i, 0)
    est = 4 * tm * d * 4 + 12 * tm * width * 4 + 4 * (width * width + 2 * width * d) + 8 * tm * width * 4
    return pl.pallas_call(
        _ab_out_kernel,
        grid=(t // tm,),
        in_specs=[pl.BlockSpec((tm, d), row),
                  pl.BlockSpec((None, 6, d), lambda i: (_cond_index(i * tm, n_ctx, dec_seq), 0, 0)),
                  pl.BlockSpec((tm, width), row),
                  pl.BlockSpec((tm, width), row),
                  pl.BlockSpec((tm, width), row),
                  pl.BlockSpec((tm, width), row),
                  pl.BlockSpec((tm, width), row),
                  pl.BlockSpec((tm, width), row),
                  pl.BlockSpec((1, width), const),
                  pl.BlockSpec(w_glu.shape, const),
                  pl.BlockSpec((1, width), const),
                  pl.BlockSpec(w_out.shape, const)],
        out_specs=pl.BlockSpec((tm, d), row),
        out_shape=jax.ShapeDtypeStruct((t, d), F32),
        compiler_params=_cparams(est, ("parallel",)),
        name="ab_out",
    )(x, mod, s_arr, xg, yf, yb, hf, hb, s5_d, w_glu, b_glu, w_out)


def _c_in_kernel(x_ref, mod_ref, g_ref, w_ref, b_ref, cw_ref, cb_ref, x0_ref, vg_ref, *, n_ctx, ctx_row):
    tm = x_ref.shape[0]
    width = x0_ref.shape[1]
    m = mod_ref[...]
    h = _norm_mod(x_ref[...], g_ref[...], m[1:2], m[0:1])
    z = jnp.dot(h.astype(BF16), w_ref[...], preferred_element_type=F32) + b_ref[...]
    row_len = jnp.where(pl.program_id(0) * tm < n_ctx, ctx_row, GRID_W)
    pos = lax.broadcasted_iota(jnp.int32, (tm, width), 0) & (row_len - 1)
    cw = cw_ref[...]
    cb = cb_ref[...]
    parts = [_dwconv_rows(z[:, k * width:(k + 1) * width], cw[:, k * width:(k + 1) * width],
                          cb[:, k * width:(k + 1) * width], pos, row_len, 1) for k in range(3)]
    x0_ref[...] = parts[0]
    vg_ref[...] = parts[2] * parts[1]


def _c_in(x, mod, g, w_in, b_in, conv_w, conv_b, *, n_ctx, ctx_row, dec_seq, tm):
    t, d = x.shape
    width = w_in.shape[1] // 3
    kern = functools.partial(_c_in_kernel, n_ctx=n_ctx, ctx_row=ctx_row)
    const = lambda i: (0, 0)
    est = 2 * tm * d * 4 + 4 * d * 3 * width + 4 * tm * width * 4 + 6 * tm * 3 * width * 4
    return pl.pallas_call(
        kern,
        grid=(t // tm,),
        in_specs=[pl.BlockSpec((tm, d), lambda i: (i, 0)),
                  pl.BlockSpec((None, 6, d), lambda i: (_cond_index(i * tm, n_ctx, dec_seq), 0, 0)),
                  pl.BlockSpec((1, d), const),
                  pl.BlockSpec(w_in.shape, const),
                  pl.BlockSpec((1, 3 * width), const),
                  pl.BlockSpec(conv_w.shape, const),
                  pl.BlockSpec((1, 3 * width), const)],
        out_specs=[pl.BlockSpec((tm, width), lambda i: (i, 0))] * 2,
        out_shape=[jax.ShapeDtypeStruct((t, width), F32)] * 2,
        compiler_params=_cparams(est, ("parallel",)),
        name="c_in",
    )(x, mod, g, w_in, b_in, conv_w, conv_b)


def _c_out_kernel(x_ref, mod_ref, x0_ref, vg_ref, yc_ref, hb_ref, w_ref, b_ref, o_ref):
    m = mod_ref[...]
    vg = vg_ref[...]
    y = (yc_ref[...] + hb_ref[...] * vg) * x0_ref[...]
    out = jnp.dot(y.astype(BF16), w_ref[...], preferred_element_type=F32) + b_ref[...]
    o_ref[...] = x_ref[...] + m[2:3] * out


def _c_out(x, mod, x0, vg, yconv, hy_bias, w_out, b_out, *, n_ctx, dec_seq, tm):
    t, d = x.shape
    width = x0.shape[1]
    const = lambda i: (0, 0)
    row = lambda i: (i, 0)
    est = 4 * tm * d * 4 + 6 * tm * width * 4 + 4 * width * d + 4 * tm * width * 4
    return pl.pallas_call(
        _c_out_kernel,
        grid=(t // tm,),
        in_specs=[pl.BlockSpec((tm, d), row),
                  pl.BlockSpec((None, 6, d), lambda i: (_cond_index(i * tm, n_ctx, dec_seq), 0, 0)),
                  pl.BlockSpec((tm, width), row),
                  pl.BlockSpec((tm, width), row),
                  pl.BlockSpec((tm, width), row),
                  pl.BlockSpec((1, width), const),
                  pl.BlockSpec(w_out.shape, const),
                  pl.BlockSpec((1, d), const)],
        out_specs=pl.BlockSpec((tm, d), row),
        out_shape=jax.ShapeDtypeStruct((t, d), F32),
        compiler_params=_cparams(est, ("parallel",)),
        name="c_out",
    )(x, mod, x0, vg, yconv, hy_bias, w_out, b_out)


def _hy_filter_kernel(w1_ref, b1_ref, f1_ref, w2_ref, b2_ref, f2_ref, w3_ref, dec_ref, o_ref, z_scr, *, length):
    ct = o_ref.shape[1]
    nb = HY_BANDS
    ecols = w1_ref.shape[0]
    row = lax.broadcasted_iota(jnp.int32, (length, 1), 0)
    positions = (row, length - row)

    @pl.when(pl.program_id(0) == 0)
    def _():
        for side in range(2):
            posf = positions[side].astype(F32)
            col = lax.broadcasted_iota(jnp.int32, (length, ecols), 1)
            band_idx = jnp.where(col <= nb, col - 1, col - 1 - nb).astype(F32)
            bands = 1e-4 + band_idx * ((nb - 1 - 1e-4) / (nb - 1))
            ang = (2.0 * math.pi / length) * posf * bands
            emb = jnp.where(col == 0, posf / length,
                            jnp.where(col <= nb, jnp.cos(ang),
                                      jnp.where(col <= 2 * nb, -jnp.sin(ang), 0.0)))
            z = jnp.sin(f1_ref[...] * (jnp.dot(emb, w1_ref[...], precision=HIGHEST,
                                               preferred_element_type=F32) + b1_ref[...]))
            z_scr[side] = jnp.sin(f2_ref[...] * (jnp.dot(z, w2_ref[...], precision=HIGHEST,
                                                         preferred_element_type=F32) + b2_ref[...]))

    def half(side):
        wide_row = lax.broadcasted_iota(jnp.int32, (length, ct), 0)
        t = (wide_row if side == 0 else length - wide_row).astype(F32) / length
        filt = jnp.dot(z_scr[side], w3_ref[side], precision=HIGHEST, preferred_element_type=F32)
        return filt * (jnp.exp(-t * jnp.abs(dec_ref[side:side + 1, :])) + HY_SHIFT)

    fwd = half(0)
    bwd = jnp.where(lax.broadcasted_iota(jnp.int32, (length, ct), 0) == 0, 0.0, half(1))
    ss = jnp.sum(fwd * fwd, axis=0, keepdims=True) + jnp.sum(bwd * bwd, axis=0, keepdims=True)
    scale = lax.rsqrt(ss + EPS)
    o_ref[:length, :] = fwd * scale
    o_ref[length:, :] = bwd * scale


def _hy_filter(length, w1, b1, f1, w2, b2, f2, w3, decay, *, ct):
    emb, ff = w1.shape
    width = decay.shape[1]
    ecols = 128
    w1p = jnp.zeros((ecols, ff), F32).at[:emb].set(w1)
    w3s = w3.reshape(ff, 2, width).transpose(1, 0, 2)
    kern = functools.partial(_hy_filter_kernel, length=length)
    const = lambda j: (0, 0)
    est = 10 * length * ct * 4 + 4 * length * ecols * 4
    return pl.pallas_call(
        kern,
        grid=(width // ct,),
        in_specs=[pl.BlockSpec((ecols, ff), const),
                  pl.BlockSpec((1, ff), const),
                  pl.BlockSpec((1, ff), const),
                  pl.BlockSpec((ff, ff), const),
                  pl.BlockSpec((1, ff), const),
                  pl.BlockSpec((1, ff), const),
                  pl.BlockSpec((2, ff, ct), lambda j: (0, 0, j)),
                  pl.BlockSpec((2, ct), lambda j: (0, j))],
        out_specs=pl.BlockSpec((2 * length, ct), lambda j: (0, j)),
        out_shape=jax.ShapeDtypeStruct((2 * length, width), F32),
        scratch_shapes=[pltpu.VMEM((2, length, ff), F32)],
        compiler_params=_cparams(est, ("arbitrary",)),
        name="hy_filter",
    )(w1p, b1.reshape(1, ff), f1.reshape(1, ff), w2, b2.reshape(1, ff), f2.reshape(1, ff), w3s, decay)


def _fftconv_kernel(ct_ref, st_ref, x_ref, k_ref, c_ref, s_ref, twr_ref, twi_ref, o_ref, *scratch, n1):
    kfr, kfi = scratch if scratch else (None, None)
    h1 = n1 // 2
    n2 = c_ref.shape[0]
    n = n1 * n2
    cm = c_ref[...]
    sm = s_ref[...]

    def dft(pairs, inverse):
        w = pairs[0][0].shape[1]
        parts = jnp.concatenate([p.astype(BF16) for pair in pairs for p in pair], axis=1)
        cp = jnp.dot(cm, parts, preferred_element_type=F32)
        sp = jnp.dot(sm, parts, preferred_element_type=F32)
        out = []
        for i in range(len(pairs)):
            ca, cb = cp[:, 2 * i * w:(2 * i + 1) * w], cp[:, (2 * i + 1) * w:(2 * i + 2) * w]
            sa, sb = sp[:, 2 * i * w:(2 * i + 1) * w], sp[:, (2 * i + 1) * w:(2 * i + 2) * w]
            out.append((ca - sb, cb + sa) if inverse else (ca + sb, cb - sa))
        return out

    def slab(t1):
        return slice(t1 * n2, (t1 + 1) * n2)

    twr1, twi1 = twr_ref[...], twi_ref[...]
    tw0 = (jnp.ones_like(twr1), jnp.zeros_like(twi1))

    def next_twiddle(tw):
        return tw[0] * twr1 - tw[1] * twi1, tw[0] * twi1 + tw[1] * twr1

    def filter_stage1(k1, tw):
        ar = ai = None
        for t1 in range(n1):
            kr = k_ref[slab(t1), :]
            tr, ti = ct_ref[k1, t1] * kr, -(st_ref[k1, t1] * kr)
            ar = tr if ar is None else ar + tr
            ai = ti if ai is None else ai + ti
        ar, ai = ar * (1.0 / n), ai * (1.0 / n)
        return ar * tw[0] + ai * tw[1], ai * tw[0] - ar * tw[1]

    if kfr is not None:
        @pl.when(pl.program_id(1) == 0)
        def _():
            def fbody(k1, tw):
                (kfr[k1], kfi[k1]), = dft([filter_stage1(k1, tw)], False)
                return next_twiddle(tw)
            lax.fori_loop(0, n1, fbody, tw0)

    o_ref[...] = jnp.zeros_like(o_ref)

    def forward(k1, tw):
        twr, twi = tw
        ar = ai = None
        for t1 in range(h1):
            cc, ss = ct_ref[k1, t1], st_ref[k1, t1]
            xr, xi = x_ref[0, slab(t1), :], x_ref[1, slab(t1), :]
            tr, ti = cc * xr + ss * xi, cc * xi - ss * xr
            ar = tr if ar is None else ar + tr
            ai = ti if ai is None else ai + ti
        data = (ar * twr + ai * twi, ai * twr - ar * twi)
        if kfr is not None:
            (fr, fi), = dft([data], False)
            gr, gi = kfr[k1], kfi[k1]
        else:
            (fr, fi), (gr, gi) = dft([data, filter_stage1(k1, tw)], False)
        return fr * gr - fi * gi, fr * gi + fi * gr

    def inverse(k1, tw, y):
        twr, twi = tw
        (br, bi), = dft([y], True)
        br, bi = br * twr - bi * twi, bi * twr + br * twi
        for t1 in range(h1):
            cc, ss = ct_ref[k1, t1], st_ref[k1, t1]
            o_ref[0, slab(t1), :] += cc * br - ss * bi
            o_ref[1, slab(t1), :] += cc * bi + ss * br

    def body(k1, carry):
        tw_prev, y_prev = carry
        tw = next_twiddle(tw_prev)
        y = forward(k1, tw)
        inverse(k1 - 1, tw_prev, y_prev)
        return tw, y

    tw_last, y_last = lax.fori_loop(1, n1, body, (tw0, forward(0, tw0)))
    inverse(n1 - 1, tw_last, y_last)


def _fftconv(v, circ, *, n1, ct):
    bsz, length, width = v.shape
    n = 2 * length
    n2 = n // n1
    idx = np.arange(n2)
    ang = 2.0 * np.pi * ((idx[:, None] * idx[None, :]) % n2) / n2
    cmat = jnp.asarray(np.cos(ang), F32).astype(BF16)
    smat = jnp.asarray(np.sin(ang), F32).astype(BF16)
    tang = 2.0 * np.pi * idx / n
    twr = jnp.asarray(np.broadcast_to(np.cos(tang)[:, None], (n2, ct)), F32)
    twi = jnp.asarray(np.broadcast_to(np.sin(tang)[:, None], (n2, ct)), F32)
    i1 = np.arange(n1)
    ang1 = 2.0 * np.pi * ((i1[:, None] * i1[None, :]) % n1) / n1
    ctab = jnp.asarray(np.round(np.cos(ang1), 12), F32)
    stab = jnp.asarray(np.round(np.sin(ang1), 12), F32)
    kern = functools.partial(_fftconv_kernel, n1=n1)
    const = lambda j, p: (0, 0)
    once = pl.Buffered(1)
    npairs = bsz // 2
    scratch = [pltpu.VMEM((n1, n2, ct), F32)] * 2 if npairs > 1 else []
    est = (4 * 2 * length * ct * 4 + n * ct * 4 + 2 * n2 * n2 * 2 + 2 * n2 * ct * 4
           + len(scratch) * n * ct * 4 + (8 + 2 * n1) * n2 * ct * 4)
    return pl.pallas_call(
        kern,
        grid=(width // ct, npairs),
        in_specs=[pl.BlockSpec(memory_space=pltpu.SMEM),
                  pl.BlockSpec(memory_space=pltpu.SMEM),
                  pl.BlockSpec((2, length, ct), lambda j, p: (p, 0, j)),
                  pl.BlockSpec((n, ct), lambda j, p: (0, j), pipeline_mode=once),
                  pl.BlockSpec((n2, n2), const, pipeline_mode=once),
                  pl.BlockSpec((n2, n2), const, pipeline_mode=once),
                  pl.BlockSpec((n2, ct), const, pipeline_mode=once),
                  pl.BlockSpec((n2, ct), const, pipeline_mode=once)],
        out_specs=pl.BlockSpec((2, length, ct), lambda j, p: (p, 0, j)),
        out_shape=jax.ShapeDtypeStruct((bsz, length, width), F32),
        scratch_shapes=scratch,
        compiler_params=_cparams(est, ("arbitrary", "arbitrary")),
        name="fftconv",
    )(ctab, stab, v, circ, cmat, smat, twr, twi)


def _sorting_network(n):
    pairs = []
    p = 1
    while p < n:
        k = p
        while k >= 1:
            for j in range(k % p, n - k, 2 * k):
                for i in range(min(k, n - j - k)):
                    if (i + j) // (2 * p) == (i + j + k) // (2 * p):
                        pairs.append((i + j, i + j + k))
            k //= 2
        p *= 2
    return pairs


_SORT16 = _sorting_network(PEER_TOPK)


def _peer_route_kernel(x_ref, mod_ref, g_ref, wq_ref, keys_ref, hb_ref, th_ref, e0_ref, s1_ref, e1_ref,
                       q_scr, sc_scr, top_scr):
    tq = x_ref.shape[0]
    nk = PEER_NKEYS
    k = PEER_TOPK
    nhp = keys_ref.shape[0]
    dkh = keys_ref.shape[2]
    neg = -jnp.inf
    m = mod_ref[...]
    h = _norm_mod(x_ref[...], g_ref[...], m[4:5], m[3:4]).astype(BF16)
    hb_ref[...] = h
    q = jnp.dot(h, wq_ref[...], preferred_element_type=F32)
    for hp in range(nhp):
        q_scr[hp] = q[:, hp * dkh:(hp + 1) * dkh].astype(BF16)

    lw = V7X_LANES
    sub = V7X_SUBLANES
    assert k == 2 * sub
    ncand = k + 7 * sub + (k - sub)
    npad = sum(sub - k // (i + 1) for i in range(2, sub))
    iota_k = lax.broadcasted_iota(jnp.int32, (nk, lw), 0)
    iota_c = lax.broadcasted_iota(jnp.int32, (ncand, lw), 0)
    iota_8 = lax.broadcasted_iota(jnp.int32, (sub, lw), 0)
    iota_r = lax.broadcasted_iota(jnp.int32, (k, lw), 0)

    def topk(vals, iota, exact):
        top = jnp.zeros((k, lw), F32)
        for r in range(k):
            mx = jnp.max(vals, axis=0, keepdims=True)
            if exact:
                first = jnp.min(jnp.where(vals == mx, iota, vals.shape[0]), axis=0, keepdims=True)
                vals = jnp.where(iota == first, neg, vals)
            else:
                vals = jnp.where(vals == mx, neg, vals)
            top = jnp.where(iota_r == r, mx, top)
        return top, vals

    def tied(rest, expected):
        gone = jnp.sum(jnp.where(rest == neg, 1.0, 0.0), axis=0, keepdims=True)
        return jnp.where(gone != expected, 1.0, 0.0)

    def exchange(v, i, j):
        v[i], v[j] = jnp.maximum(v[i], v[j]), jnp.minimum(v[i], v[j])

    def sort_top(tiles):
        v = list(tiles)
        for i, j in _SORT16:
            exchange(v, i, j)
        for shift in (sub // 2, sub // 4, sub // 8):
            other = [pltpu.roll(t, shift, axis=0) for t in v]
            v = [jnp.maximum(v[r], other[k - 1 - r]) for r in range(k)]
            d = k // 2
            while d >= 1:
                for i in range(k):
                    if i & d == 0:
                        exchange(v, i, i + d)
                d //= 2
        top = jnp.zeros((k, lw), F32)
        for r in range(k):
            top = jnp.where(iota_r == r, jnp.concatenate([v[r]] * (k // sub), axis=0), top)
        return top

    def halves(hd):
        assert nk == k * sub
        for p in range(2):
            for c in range(tq // lw):
                lanes = slice(c * lw, (c + 1) * lw)
                top_scr[2 * hd + p, :, lanes] = sort_top(
                    [sc_scr[2 * hd + p, t * sub:(t + 1) * sub, lanes] for t in range(nk // sub)])

    def pairs(hd, exact):
        flag = jnp.zeros((1, lw), F32)
        for c in range(tq // lw):
            lanes = slice(c * lw, (c + 1) * lw)
            a = top_scr[2 * hd, :, lanes]
            b = top_scr[2 * hd + 1, :, lanes]
            pieces = [a[0:1] + b, a[1:2] + b[0:sub]]
            for i in range(2, sub):
                pieces.append(jnp.where(iota_8 < k // (i + 1), a[i:i + 1] + b[0:sub], neg))
            pieces.append(a[sub:k] + b[0:1])
            vmax = a[0:1] + b[0:1]
            top, rest = topk(jnp.concatenate(pieces, axis=0), iota_c, exact)
            z = jnp.sum(jnp.exp(top - vmax), axis=0, keepdims=True)
            last = top[k - 1:k]
            nxt = jnp.max(rest, axis=0, keepdims=True)
            tau = jnp.where(nxt == neg, last, 0.5 * (last + nxt))
            flag = jnp.maximum(flag, tied(rest, float(k + npad)))
            s0 = sc_scr[2 * hd, :, lanes]
            s1 = sc_scr[2 * hd + 1, :, lanes]
            th = jnp.where(s0 >= a[k - 1:k], tau - s0, jnp.inf)
            s1 = jnp.where(s1 >= b[k - 1:k], s1, neg)
            e0 = jnp.exp(s0 - a[0:1]) / z
            per = th_ref.shape[2]
            for r in range(nk // per):
                th_ref[hd, r, :, lanes] = th[r * per:(r + 1) * per]
                e0_ref[hd, r, :, lanes] = e0[r * per:(r + 1) * per]
            s1_ref[hd, :, lanes] = s1
            e1_ref[hd, :, lanes] = jnp.exp(s1 - b[0:1])
        return jnp.max(flag)

    def head_body(hd, carry):
        for p in range(2):
            sc_scr[2 * hd + p] = lax.dot_general(keys_ref[2 * hd + p], q_scr[2 * hd + p], (((1,), (1,)), ((), ())),
                                                 preferred_element_type=F32)
        halves(hd)

        @pl.when(pairs(hd, False) > 0.0)
        def _():
            pairs(hd, True)

        return carry

    lax.fori_loop(0, nhp // 2, head_body, 0)


def _peer_route(x, mod, g, wq, keys, *, n_ctx, dec_seq, tq):
    t, d = x.shape
    nhp, nk, dkh = keys.shape
    nh = nhp // 2
    const = lambda i: (0, 0)
    hshape = jax.ShapeDtypeStruct((nh, nk, t), F32)
    hspec = pl.BlockSpec((nh, nk, tq), lambda i: (0, 0, i))
    per = PEER_GROUP // nk
    gshape = jax.ShapeDtypeStruct((nh, nk // per, per, t), F32)
    gspec = pl.BlockSpec((nh, nk // per, per, tq), lambda i: (0, 0, 0, i))
    pad = V7X_SUBLANES // per
    est = (2 * tq * d * 4 + 4 * d * nhp * dkh + 2 * tq * d * 2 + (4 + 4 * pad) * nh * nk * tq * 4
           + 2 * nhp * nk * tq * 4 + 6 * tq * nhp * dkh * 4)
    return pl.pallas_call(
        _peer_route_kernel,
        grid=(t // tq,),
        in_specs=[pl.BlockSpec((tq, d), lambda i: (i, 0)),
                  pl.BlockSpec((None, 6, d), lambda i: (_cond_index(i * tq, n_ctx, dec_seq), 0, 0)),
                  pl.BlockSpec((1, d), const),
                  pl.BlockSpec(wq.shape, const),
                  pl.BlockSpec(keys.shape, lambda i: (0, 0, 0))],
        out_specs=[pl.BlockSpec((tq, d), lambda i: (i, 0)), gspec, gspec, hspec, hspec],
        out_shape=[jax.ShapeDtypeStruct((t, d), BF16), gshape, gshape, hshape, hshape],
        scratch_shapes=[pltpu.VMEM((nhp, tq, dkh), BF16),
                        pltpu.VMEM((nhp, nk, tq), F32),
                        pltpu.VMEM((nhp, PEER_TOPK, tq), F32)],
        compiler_params=_cparams(est, ("parallel",)),
        name="peer_route",
    )(x, mod, g, wq, keys)


def _peer_expert_kernel(hbt_ref, u_ref, vt_ref, th_ref, e0_ref, s1_ref, e1_ref, x_ref, mod_ref, o_ref,
                        acc, st_a, st_b, hbuf_a, hbuf_b, *, nc):
    g = pl.program_id(0)
    nh = th_ref.shape[0]
    nk = s1_ref.shape[1]
    ngroups, per = th_ref.shape[1], th_ref.shape[2]
    gw = per * nk
    tq = st_a.shape[1]
    kh = PEER_ACC_ROWS // per
    c3 = jnp.maximum(g - 2, 0) % nc

    @pl.when(g == 0)
    def _():
        st_b[...] = jnp.zeros_like(st_b)
        hbuf_a[...] = jnp.zeros_like(hbuf_a)
        hbuf_b[...] = jnp.zeros_like(hbuf_b)

    @pl.when(c3 == 0)
    def _():
        acc[...] = jnp.zeros_like(acc)

    def step(st_w, st_r, hbuf_w, hbuf_r):
        def group(j, carry):
            rows = pl.ds(pl.multiple_of(j * gw, gw), gw)
            acc[...] += jnp.dot(vt_ref[j], hbuf_r[rows, :], preferred_element_type=F32)
            st_w[rows, :] = jnp.dot(u_ref[rows, :], hbt_ref[...], preferred_element_type=F32)
            for q in range(tq // V7X_LANES):
                lanes = slice(q * V7X_LANES, (q + 1) * V7X_LANES)
                for half in range(nk // kh):
                    keys = slice(half * kh, (half + 1) * kh)
                    w = [None] * per
                    for hd in range(nh):
                        s1 = s1_ref[hd, keys, lanes]
                        e1 = e1_ref[hd, keys, lanes]
                        for i in range(per):
                            sel = jnp.where(s1 >= th_ref[hd, j, i:i + 1, lanes], e1, 0.0) * e0_ref[hd, j, i:i + 1, lanes]
                            w[i] = sel if w[i] is None else w[i] + sel
                    for i in range(per):
                        tile = pl.ds(pl.multiple_of(j * gw + i * nk + half * kh, kh), kh)
                        hbuf_w[tile, lanes] = (_gelu(st_r[tile, lanes]) * w[i]).astype(BF16)
            return carry

        lax.fori_loop(0, ngroups, group, 0)

    @pl.when(g % 2 == 0)
    def _():
        step(st_a, st_b, hbuf_b, hbuf_a)

    @pl.when(g % 2 == 1)
    def _():
        step(st_b, st_a, hbuf_a, hbuf_b)

    @pl.when(jnp.logical_and(g >= 2, c3 == nc - 1))
    def _():
        o_ref[...] = x_ref[...] + mod_ref[5:6, :] * acc[...].T


def _peer_expert(hb, u_tabs, layer, vt_tab, th, e0, s1, e1, x, mod, *, n_ctx, dec_seq, tq, ec):
    t, d = x.shape
    ne = u_tabs.shape[1]
    nh, nk, _ = s1.shape
    per = th.shape[2]
    assert per * nk == PEER_GROUP
    ngroups = ec // PEER_GROUP
    nc = ne // ec
    npairs = (t // tq) * nc

    def pair(g, lag):
        p = jnp.clip(g - lag, 0, npairs - 1)
        return p // nc, p % nc

    def cond3(g):
        return _cond_index(pair(g, 2)[0] * tq, n_ctx, dec_seq)

    est = (2 * tq * d * 2 + 4 * ec * d * 2 + 4 * nh * ngroups * V7X_SUBLANES * tq * 4 + 4 * nh * nk * tq * 4
           + 4 * tq * d * 4 + d * tq * 4 + 2 * ec * tq * 2 + 2 * ec * tq * 4 + 3 * PEER_GROUP * tq * 4)
    return pl.pallas_call(
        functools.partial(_peer_expert_kernel, nc=nc),
        grid=(npairs + 2,),
        in_specs=[pl.BlockSpec((d, tq), lambda g: (0, pair(g, 0)[0])),
                  pl.BlockSpec((None, ec, d), lambda g: (layer, pair(g, 0)[1], 0)),
                  pl.BlockSpec((ngroups, d, PEER_GROUP), lambda g: (pair(g, 2)[1], 0, 0)),
                  pl.BlockSpec((nh, ngroups, per, tq), lambda g: (0, pair(g, 1)[1], 0, pair(g, 1)[0])),
                  pl.BlockSpec((nh, ngroups, per, tq), lambda g: (0, pair(g, 1)[1], 0, pair(g, 1)[0])),
                  pl.BlockSpec((nh, nk, tq), lambda g: (0, 0, pair(g, 1)[0])),
                  pl.BlockSpec((nh, nk, tq), lambda g: (0, 0, pair(g, 1)[0])),
                  pl.BlockSpec((tq, d), lambda g: (pair(g, 2)[0], 0)),
                  pl.BlockSpec((None, 6, d), lambda g: (cond3(g), 0, 0))],
        out_specs=pl.BlockSpec((tq, d), lambda g: (pair(g, 2)[0], 0)),
        out_shape=jax.ShapeDtypeStruct((t, d), F32),
        scratch_shapes=[pltpu.VMEM((d, tq), F32), pltpu.VMEM((ec, tq), F32), pltpu.VMEM((ec, tq), F32),
                        pltpu.VMEM((ec, tq), BF16), pltpu.VMEM((ec, tq), BF16)],
        compiler_params=_cparams(est, ("arbitrary",)),
        name="peer_expert",
    )(hb.T, u_tabs, vt_tab, th, e0, s1, e1, x, mod)


def _final_kernel(x_ref, g_ref, o_ref):
    x = x_ref[...]
    ms = jnp.mean(x * x, axis=-1, keepdims=True)
    o_ref[...] = (x * lax.rsqrt(ms + EPS)) * g_ref[...]


def _final_norm(x, g, *, tm):
    t, d = x.shape
    return pl.pallas_call(
        _final_kernel,
        grid=(t // tm,),
        in_specs=[pl.BlockSpec((tm, d), lambda i: (i, 0)), pl.BlockSpec((1, d), lambda i: (0, 0))],
        out_specs=pl.BlockSpec((tm, d), lambda i: (i, 0)),
        out_shape=jax.ShapeDtypeStruct((t, d), F32),
        compiler_params=_cparams(6 * tm * d * 4, ("parallel",)),
        name="final_norm",
    )(x, g)


def _block_diag(w):
    h, i, j = w.shape
    return jnp.einsum('hij,hg->higj', w, jnp.eye(h, dtype=w.dtype)).reshape(h * i, h * j)


def _s5_discretize(a_re, a_im, log_dt, b_re, b_im):
    dt = jnp.exp(log_dt)[:, None]
    mag = jnp.exp(a_re * dt)
    abar_r = mag * jnp.cos(a_im * dt)
    abar_i = mag * jnp.sin(a_im * dt)
    den = a_re * a_re + a_im * a_im
    num_r = abar_r - 1.0
    coef_r = ((num_r * a_re + abar_i * a_im) / den)[..., None]
    coef_i = ((abar_i * a_re - num_r * a_im) / den)[..., None]
    return abar_r, abar_i, coef_r * b_re - coef_i * b_im, coef_r * b_im + coef_i * b_re


def _s5_blocks(abar_r, abar_i, bbar_r, bbar_i, c_re, c_im):
    g, p, ch = bbar_r.shape
    gb = g // S5_COLBLK

    def vec(v):
        return v.reshape(S5_COLBLK, 1, gb * p)

    def bmat(b):
        bt = b.transpose(0, 2, 1).reshape(S5_COLBLK, gb, ch, p)
        return jnp.stack([_block_diag(bt[j]) for j in range(S5_COLBLK)])

    def cmat(cc):
        ct = cc.transpose(0, 2, 1).reshape(S5_COLBLK, gb, p, ch)
        return jnp.stack([_block_diag(ct[j]) for j in range(S5_COLBLK)])

    bblk = jnp.concatenate([bmat(bbar_r), bmat(bbar_i)], axis=2).astype(BF16)
    cblk = jnp.concatenate([cmat(c_re), cmat(-c_im)], axis=1).astype(BF16)
    return bblk, cblk, vec(abar_r), vec(abar_i)


def _state_to_blocked(re, im):
    n, g, p = re.shape
    gb = g // S5_COLBLK
    both = jnp.stack([re.reshape(n, S5_COLBLK, gb * p), im.reshape(n, S5_COLBLK, gb * p)], axis=2)
    return both.reshape(n, -1)


def _state_from_blocked(st, g, p):
    n = st.shape[0]
    both = st.reshape(n, S5_COLBLK, 2, (g // S5_COLBLK) * p)
    return both[:, :, 0].reshape(n, g, p), both[:, :, 1].reshape(n, g, p)


def _complex_power(ar, ai, n):
    for _ in range(int(math.log2(n))):
        ar, ai = ar * ar - ai * ai, 2.0 * ar * ai
    return ar, ai


def _mixer_ab(xs, mod, p, le, h0, *, bc, lc, bs, ls, tm):
    n_ctx = bc * lc
    width = p['s5_d'].shape[1]
    g_cnt, p_cnt = p['s5_a_re'].shape[2:]
    w_gates = jnp.concatenate([_block_diag(p['lru_w_a'][le, 0]), _block_diag(p['lru_w_x'][le, 0]),
                               _block_diag(p['lru_w_a'][le, 1]), _block_diag(p['lru_w_x'][le, 1])],
                              axis=1).astype(BF16)
    b_gates = jnp.concatenate([p['lru_b_a'][le, 0], p['lru_b_x'][le, 0],
                               p['lru_b_a'][le, 1], p['lru_b_x'][le, 1]]).reshape(1, 4 * width)
    c8 = LRU_C * jax.nn.softplus(-p['lru_lambda'][le])
    pass_ctx = (n_ctx, 0)
    w_in = p['w_in_ab'][le].astype(BF16)
    s_arrs, xgs = zip(*[_ab_in(x, mod, p['norm1_g'][2 * le].reshape(1, -1), w_in,
                               p['lru_conv_w'][le], p['lru_conv_b'][le].reshape(1, width), w_gates, b_gates, c8,
                               n_ctx=nc_, ctx_row=lc, dec_seq=ls, tm=tm) for x, nc_ in zip(xs, pass_ctx)])

    cols = s_arrs[0].shape[1]
    gc = bc // SCAN_SEQS
    lseg = ls // SAMPLE_SEGS
    s_ctx = (s_arrs[0].reshape(gc, SCAN_SEQS, lc, cols).transpose(0, 2, 1, 3)
             .reshape(n_ctx, cols))
    s_smp = (s_arrs[1].reshape(bs * SAMPLE_SEGS, lseg, cols).transpose(1, 0, 2)
             .reshape(bs * ls, cols))
    chunk_rows = min(1024, lc * SCAN_SEQS, lseg * SCAN_SEQS)

    outs = {}
    finals = {}
    for d in range(2):
        abar_r, abar_i, bbar_r, bbar_i = _s5_discretize(
            p['s5_a_re'][le, d], p['s5_a_im'][le, d], p['s5_log_dt'][le, d],
            p['s5_b_re'][le, d], p['s5_b_im'][le, d])
        bblk, cblk, ar, ai = _s5_blocks(abar_r, abar_i, bbar_r, bbar_i,
                                        p['s5_c_re'][le, d], p['s5_c_im'][le, d])
        nstate = 2 * g_cnt * p_cnt
        z5 = jnp.zeros((gc, SCAN_SEQS, nstate), F32)
        zl = jnp.zeros((gc, SCAN_SEQS, width), F32)
        y, hl, f5, fl, _ = _scan(s_ctx, bblk, cblk, ar, ai, z5, zl, direction=d, store=True,
                                 groups=gc, chunk_rows=chunk_rows)
        outs[('ctx', d)] = (y, hl)
        finals[d] = (f5.reshape(bc, nstate), fl.reshape(bc, width))
        z5 = jnp.zeros((1, SCAN_SEQS, nstate), F32)
        zl = jnp.zeros((1, SCAN_SEQS, width), F32)
        e5, el, ep = _scan(s_smp, bblk, cblk, ar, ai, z5, zl, direction=d, store=False,
                           groups=1, chunk_rows=chunk_rows)
        first = (SAMPLE_SEGS - 1) if d == 1 else 0
        h5 = _state_to_blocked(h0[0][:, d], h0[1][:, d])
        h5_rows = jnp.zeros((bs, SAMPLE_SEGS, nstate), F32).at[:, first].set(h5).reshape(SCAN_SEQS, nstate)
        hl_rows = jnp.zeros((bs, SAMPLE_SEGS, width), F32).at[:, first].set(h0[2][:, d]).reshape(SCAN_SEQS, width)
        arl, ail = _complex_power(ar, ai, lseg)
        i5, il = _chain(e5[0], el[0], ep[0], h5_rows, hl_rows, arl, ail, direction=d)
        y, hl, _, _, _ = _scan(s_smp, bblk, cblk, ar, ai, i5[None], il[None], direction=d, store=True,
                               groups=1, chunk_rows=chunk_rows)
        outs[('smp', d)] = (y, hl)

    def to_tokens(name, arr):
        if name == 'ctx':
            return arr.reshape(gc, lc, SCAN_SEQS, width).transpose(0, 2, 1, 3).reshape(n_ctx, width)
        return arr.reshape(lseg, bs * SAMPLE_SEGS, width).transpose(1, 0, 2).reshape(bs * ls, width)

    w_glu, w_out = p['s5_w_glu'][le].astype(BF16), p['w_out_ab'][le].astype(BF16)
    x1 = []
    for name, x, s_arr, xg, nc_ in zip(('ctx', 'smp'), xs, s_arrs, xgs, pass_ctx):
        yf, hf = (to_tokens(name, a) for a in outs[(name, 0)])
        yb, hb = (to_tokens(name, a) for a in outs[(name, 1)])
        x1.append(_ab_out(x, mod, s_arr, xg, yf, yb, hf, hb, p['s5_d'][le].reshape(1, width),
                          w_glu, p['s5_b_glu'][le].reshape(1, width), w_out, n_ctx=nc_, dec_seq=ls, tm=tm))
    st = [_state_from_blocked(finals[d][0], g_cnt, p_cnt) for d in range(2)]
    new_re = jnp.stack([st[0][0], st[1][0]], axis=1)
    new_im = jnp.stack([st[0][1], st[1][1]], axis=1)
    new_lru = jnp.stack([finals[0][1], finals[1][1]], axis=1)
    return x1, (new_re, new_im, new_lru)


def _fft_split(length):
    n = 2 * length
    n1 = 2
    while n // n1 > 1024:
        n1 *= 2
    return n1


def _mixer_c(xs, mod, p, lo, *, bc, lc, bs, ls, tm):
    width = p['hy_bias'].shape[1]
    w_in, w_out = p['w_in_c'][lo].astype(BF16), p['w_out_c'][lo].astype(BF16)
    out = []
    for x, b, l, nc_ in zip(xs, (bc, bs), (lc, ls), (bc * lc, 0)):
        x0, vg = _c_in(x, mod, p['norm1_g'][2 * lo + 1].reshape(1, -1), w_in,
                       p['b_in_c'][lo].reshape(1, -1), p['hy_conv_w'][lo], p['hy_conv_b'][lo].reshape(1, -1),
                       n_ctx=nc_, ctx_row=lc, dec_seq=ls, tm=tm)
        ct = min(width, 256 if l <= 1024 else 128)
        circ = _hy_filter(l, p['hy_w1'][lo], p['hy_b1'][lo], p['hy_freq1'][lo], p['hy_w2'][lo],
                          p['hy_b2'][lo], p['hy_freq2'][lo], p['hy_w3'][lo], p['hy_decay'][lo], ct=ct)
        yconv = _fftconv(vg.reshape(b, l, width), circ, n1=_fft_split(l), ct=ct).reshape(b * l, width)
        out.append(_c_out(x, mod, x0, vg, yconv, p['hy_bias'][lo].reshape(1, width), w_out,
                          p['b_out_c'][lo].reshape(1, -1), n_ctx=nc_, dec_seq=ls, tm=tm))
    return out


def _peer(xs, mod, p, l, *, pass_ctx, dec_seq, tq, ec):
    keys = p['peer_keys'][l]
    nh, _, nk, dkh = keys.shape
    wq = p['peer_wq'][l].astype(BF16)
    keys = keys.reshape(nh * 2, nk, dkh).astype(BF16)
    ne, d = p['peer_v'][l].shape
    vt = p['peer_v'][l].reshape(ne // PEER_GROUP, PEER_GROUP, d).transpose(0, 2, 1).astype(BF16)
    out = []
    for x, nc_ in zip(xs, pass_ctx):
        hb, th, e0, s1, e1 = _peer_route(x, mod, p['norm2_g'][l].reshape(1, -1), wq, keys,
                                         n_ctx=nc_, dec_seq=dec_seq, tq=tq)
        out.append(_peer_expert(hb, p['peer_u_bf16'], l, vt, th, e0, s1, e1, x, mod,
                                n_ctx=nc_, dec_seq=dec_seq, tq=tq, ec=ec))
    return out


def _forward(x_prompt, x_sample, state_s5_re, state_s5_im, state_lru, c, c_ctx, p, *, tm, tq, ec):
    bc, lc, d = x_prompt.shape
    bs, ls, _ = x_sample.shape
    assert bc % SCAN_SEQS == 0 and bs * SAMPLE_SEGS == SCAN_SEQS
    assert tm % lc == 0 and ls % tm == 0 and (ls // SAMPLE_SEGS) % GRID_W == 0
    n_ctx = bc * lc
    xs = [x_prompt.reshape(n_ctx, d), x_sample.reshape(bs * ls, d)]
    pass_ctx = (n_ctx, 0)
    p = dict(p, peer_u_bf16=p['peer_u'].astype(BF16))
    cond = jnp.zeros((V7X_SUBLANES, d), F32).at[0].set(c_ctx).at[1:1 + bs].set(c)
    mod_all = _modulation(cond, p['w_mod'], p['b_mod'])
    states = None
    for l in range(DEPTH):
        mod = mod_all[l].reshape(V7X_SUBLANES, 6, d)
        if l % 2 == 0:
            le = l // 2
            h0 = (state_s5_re[:, le], state_s5_im[:, le], state_lru[:, le])
            xs, st = _mixer_ab(xs, mod, p, le, h0, bc=bc, lc=lc, bs=bs, ls=ls, tm=tm)
            states = st if states is None else states
        else:
            xs = _mixer_c(xs, mod, p, l // 2, bc=bc, lc=lc, bs=bs, ls=ls, tm=tm)
        xs = _peer(xs, mod, p, l, pass_ctx=pass_ctx, dec_seq=ls, tq=tq, ec=ec)
    g = p['final_g'].reshape(1, d)
    new_re, new_im, new_lru = states
    return (_final_norm(xs[0], g, tm=tm).reshape(bc, lc, d), _final_norm(xs[1], g, tm=tm).reshape(bs, ls, d),
            new_re[:, None], new_im[:, None], new_lru[:, None])


def kernel(x_prompt, x_sample, state_s5_re, state_s5_im, state_lru, c, c_ctx, norm1_g, norm2_g, w_mod, b_mod, w_in_ab, s5_a_re, s5_a_im, s5_log_dt, s5_b_re, s5_b_im, s5_c_re, s5_c_im, s5_d, s5_w_glu, s5_b_glu, lru_conv_w, lru_conv_b, lru_w_a, lru_b_a, lru_w_x, lru_b_x, lru_lambda, w_out_ab, w_in_c, b_in_c, hy_conv_w, hy_conv_b, hy_w1, hy_b1, hy_freq1, hy_w2, hy_b2, hy_freq2, hy_w3, hy_decay, hy_bias, w_out_c, b_out_c, peer_wq, peer_keys, peer_u, peer_v, final_g):
    p = dict(norm1_g=norm1_g, norm2_g=norm2_g, w_mod=w_mod, b_mod=b_mod, w_in_ab=w_in_ab,
             s5_a_re=s5_a_re, s5_a_im=s5_a_im, s5_log_dt=s5_log_dt, s5_b_re=s5_b_re, s5_b_im=s5_b_im,
             s5_c_re=s5_c_re, s5_c_im=s5_c_im, s5_d=s5_d, s5_w_glu=s5_w_glu, s5_b_glu=s5_b_glu,
             lru_conv_w=lru_conv_w, lru_conv_b=lru_conv_b, lru_w_a=lru_w_a, lru_b_a=lru_b_a,
             lru_w_x=lru_w_x, lru_b_x=lru_b_x, lru_lambda=lru_lambda, w_out_ab=w_out_ab,
             w_in_c=w_in_c, b_in_c=b_in_c, hy_conv_w=hy_conv_w, hy_conv_b=hy_conv_b,
             hy_w1=hy_w1, hy_b1=hy_b1, hy_freq1=hy_freq1, hy_w2=hy_w2, hy_b2=hy_b2, hy_freq2=hy_freq2,
             hy_w3=hy_w3, hy_decay=hy_decay, hy_bias=hy_bias, w_out_c=w_out_c, b_out_c=b_out_c,
             peer_wq=peer_wq, peer_keys=peer_keys, peer_u=peer_u, peer_v=peer_v, final_g=final_g)
    return _forward(x_prompt, x_sample, state_s5_re, state_s5_im, state_lru, c, c_ctx, p,
                    tm=512, tq=512, ec=2048)
```

```python
import functools
import math

import jax
import jax.numpy as jnp
import numpy as np
from jax import lax
from jax.experimental import pallas as pl
from jax.experimental.pallas import tpu as pltpu

F32 = jnp.float32
BF16 = jnp.bfloat16
HIGHEST = lax.Precision.HIGHEST

DEPTH = 2
GRID_W = 64
EPS = 1e-6
S5_CH = 16
S5_STATE = 64
LRU_HEADS = 8
LRU_C = 8.0
HY_BANDS = 16
HY_SHIFT = 0.05
PEER_HEADS = 8
PEER_NKEYS = 128
PEER_TOPK = 16
PEER_GROUP = 512
PEER_ACC_ROWS = 128

V7X_LANES = 128
V7X_SUBLANES = 8
V7X_VMEM_BYTES = 64 * 1024 * 1024

SCAN_SEQS = V7X_SUBLANES
SAMPLE_SEGS = 4
S5_COLBLK = 4


def _vmem_limit(est_bytes):
    return int(min(V7X_VMEM_BYTES - (8 << 20), max(32 << 20, est_bytes * 3 // 2)))


def _cparams(est_bytes, sem):
    return pltpu.CompilerParams(dimension_semantics=sem, vmem_limit_bytes=_vmem_limit(est_bytes))


def _gelu(x):
    k = 2.0 * 0.7978845608028654 * math.log2(math.e)
    return x / (1.0 + jnp.exp2(x * (-k - (k * 0.044715) * (x * x))))


def _norm_mod(x, g, sc, sh):
    ms = jnp.mean(x * x, axis=-1, keepdims=True)
    return (x * lax.rsqrt(ms + EPS)) * g * (1.0 + sc) + sh


def _cond_index(row0, n_ctx, dec_seq):
    return jnp.where(row0 < n_ctx, 0, 1 + (row0 - n_ctx) // dec_seq)


def _mod_kernel(c_ref, w_ref, b_ref, o_ref):
    c = c_ref[...]
    s = c * jax.nn.sigmoid(c)
    o_ref[...] = jnp.dot(s, w_ref[...], precision=HIGHEST, preferred_element_type=F32) + b_ref[...]


def _modulation(cond, w_mod, b_mod):
    depth, d, n = w_mod.shape
    rows = cond.shape[0]
    tn = 1536
    est = 2 * d * tn * 4 + 4 * rows * (d + tn) * 4
    return pl.pallas_call(
        _mod_kernel,
        grid=(depth, n // tn),
        in_specs=[pl.BlockSpec((rows, d), lambda l, j: (0, 0)),
                  pl.BlockSpec((None, d, tn), lambda l, j: (l, 0, j)),
                  pl.BlockSpec((None, 1, tn), lambda l, j: (l, 0, j))],
        out_specs=pl.BlockSpec((None, rows, tn), lambda l, j: (l, 0, j)),
        out_shape=jax.ShapeDtypeStruct((depth, rows, n), F32),
        compiler_params=_cparams(est, ("arbitrary", "arbitrary")),
        name="modulation",
    )(cond, w_mod, b_mod.reshape(depth, 1, n))


def _dwconv_rows(x, w, bias, pos, row_len, pad_left):
    tm = x.shape[0]
    acc = None
    for k in range(w.shape[0]):
        d = k - pad_left
        if d == 0:
            term = x
        else:
            shifted = pltpu.roll(x, (-d) % tm, axis=0)
            q = pos + d
            term = jnp.where(q >= 0, jnp.where(q < row_len, shifted, 0.0), 0.0)
        term = term * w[k:k + 1]
        acc = term if acc is None else acc + term
    return acc + bias


def _ab_in_kernel(x_ref, mod_ref, g_ref, w_ref, cw_ref, cb_ref, wg_ref, bg_ref, c8_ref,
                  s_ref, xg_ref, *, n_ctx, ctx_row, width):
    tm = x_ref.shape[0]
    m = mod_ref[...]
    h = _norm_mod(x_ref[...], g_ref[...], m[1:2], m[0:1])
    z = jnp.dot(h.astype(BF16), w_ref[...], preferred_element_type=F32)
    u = z[:, :width]
    xr = z[:, width:2 * width]
    xg_ref[...] = z[:, 2 * width:]
    row_len = jnp.where(pl.program_id(0) * tm < n_ctx, ctx_row, GRID_W)
    pos = lax.broadcasted_iota(jnp.int32, (tm, width), 0) & (row_len - 1)
    xb = _dwconv_rows(xr, cw_ref[...], cb_ref[...], pos, row_len, 2)
    gates = jnp.dot(xb.astype(BF16), wg_ref[...], preferred_element_type=F32) + bg_ref[...]
    s_ref[:, :width] = u
    c8 = c8_ref[...]
    for d in range(2):
        r = jax.nn.sigmoid(gates[:, (2 * d) * width:(2 * d + 1) * width])
        i = jax.nn.sigmoid(gates[:, (2 * d + 1) * width:(2 * d + 2) * width])
        a = jnp.exp(-(c8[d:d + 1] * r))
        b = jnp.sqrt(1.0 - a * a) * (i * xb)
        s_ref[:, (1 + 2 * d) * width:(2 + 2 * d) * width] = a
        s_ref[:, (2 + 2 * d) * width:(3 + 2 * d) * width] = b


def _ab_in(x, mod, g, w_in, conv_w, conv_b, w_gates, b_gates, c8, *, n_ctx, ctx_row, dec_seq, tm):
    t, d = x.shape
    width = w_in.shape[1] // 3
    kern = functools.partial(_ab_in_kernel, n_ctx=n_ctx, ctx_row=ctx_row, width=width)
    const = lambda i: (0, 0)
    est = (2 * tm * d * 4 + d * 3 * width * 2 * 2 + width * 4 * width * 2 * 2
           + 2 * tm * 6 * width * 4 + tm * 12 * width * 4)
    return pl.pallas_call(
        kern,
        grid=(t // tm,),
        in_specs=[pl.BlockSpec((tm, d), lambda i: (i, 0)),
                  pl.BlockSpec((None, 6, d), lambda i: (_cond_index(i * tm, n_ctx, dec_seq), 0, 0)),
                  pl.BlockSpec((1, d), const),
                  pl.BlockSpec(w_in.shape, const),
                  pl.BlockSpec(conv_w.shape, const),
                  pl.BlockSpec((1, width), const),
                  pl.BlockSpec(w_gates.shape, const),
                  pl.BlockSpec((1, 4 * width), const),
                  pl.BlockSpec((2, width), const)],
        out_specs=[pl.BlockSpec((tm, 5 * width), lambda i: (i, 0)),
                   pl.BlockSpec((tm, width), lambda i: (i, 0))],
        out_shape=[jax.ShapeDtypeStruct((t, 5 * width), F32),
                   jax.ShapeDtypeStruct((t, width), F32)],
        compiler_params=_cparams(est, ("parallel",)),
        name="ab_in",
    )(x, mod, g, w_in, conv_w, conv_b, w_gates, b_gates, c8)


def _scan_kernel(u_ref, a_ref, b_ref, bblk_ref, cblk_ref, ar_ref, ai_ref, i5_ref, il_ref, *rest,
                 reverse, store):
    if store:
        y_ref, hl_ref, f5_ref, fl_ref, fp_ref, buf, st5, stl, stp = rest
    else:
        f5_ref, fl_ref, fp_ref, buf, st5, stl, stp = rest
    c = pl.program_id(1)
    rows = u_ref.shape[0]
    nsteps = rows // SCAN_SEQS
    sw = st5.shape[1] // (2 * S5_COLBLK)
    cw = u_ref.shape[1] // S5_COLBLK

    @pl.when(c == 0)
    def _():
        st5[...] = i5_ref[...]
        stl[...] = il_ref[...]
        stp[...] = jnp.ones_like(stp)

    def row0(k):
        t = (nsteps - 1 - k) if reverse else k
        return pl.multiple_of(t * SCAN_SEQS, SCAN_SEQS)

    u = u_ref[...].astype(BF16)
    for j in range(S5_COLBLK):
        buf[...] = jnp.dot(u[:, j * cw:(j + 1) * cw], bblk_ref[j], preferred_element_type=F32)
        arj = jnp.broadcast_to(ar_ref[j], (SCAN_SEQS, sw))
        aij = jnp.broadcast_to(ai_ref[j], (SCAN_SEQS, sw))

        def step(k, carry, arj=arj, aij=aij):
            hr, hi = carry
            r0 = row0(k)
            xr = buf[pl.ds(r0, SCAN_SEQS), 0:sw]
            xi = buf[pl.ds(r0, SCAN_SEQS), sw:2 * sw]
            nr = arj * hr - aij * hi + xr
            ni = arj * hi + aij * hr + xi
            buf[pl.ds(r0, SCAN_SEQS), 0:sw] = nr
            buf[pl.ds(r0, SCAN_SEQS), sw:2 * sw] = ni
            return nr, ni

        base = 2 * sw * j
        hr, hi = lax.fori_loop(0, nsteps, step, (st5[:, base:base + sw], st5[:, base + sw:base + 2 * sw]))
        st5[:, base:base + sw] = hr
        st5[:, base + sw:base + 2 * sw] = hi
        if store:
            y_ref[:, j * cw:(j + 1) * cw] = jnp.dot(buf[...].astype(BF16), cblk_ref[j],
                                                    preferred_element_type=F32)

    def lstep(k, carry):
        h, p = carry
        r0 = row0(k)
        a = a_ref[pl.ds(r0, SCAN_SEQS), :]
        h = a * h + b_ref[pl.ds(r0, SCAN_SEQS), :]
        if store:
            hl_ref[pl.ds(r0, SCAN_SEQS), :] = h
        return h, p * a

    h, p = lax.fori_loop(0, nsteps, lstep, (stl[...], stp[...]))
    stl[...] = h
    stp[...] = p

    @pl.when(c == pl.num_programs(1) - 1)
    def _():
        f5_ref[...] = st5[...]
        fl_ref[...] = stl[...]
        fp_ref[...] = stp[...]


def _scan(s_arr, bblk, cblk, ar, ai, init5, initl, *, direction, store, groups, chunk_rows):
    rows_total, cols = s_arr.shape
    width = cols // 5
    rows_per_group = rows_total // groups
    nc = rows_per_group // chunk_rows
    reverse = direction == 1
    nstate = init5.shape[-1]

    def chunk(g, c):
        return g * nc + ((nc - 1 - c) if reverse else c)

    kern = functools.partial(_scan_kernel, reverse=reverse, store=store)
    const3 = lambda g, c: (0, 0, 0)
    in_specs = [pl.BlockSpec((chunk_rows, width), lambda g, c: (chunk(g, c), 0)),
                pl.BlockSpec((chunk_rows, width), lambda g, c: (chunk(g, c), 1 + 2 * direction)),
                pl.BlockSpec((chunk_rows, width), lambda g, c: (chunk(g, c), 2 + 2 * direction)),
                pl.BlockSpec(bblk.shape, const3),
                pl.BlockSpec(cblk.shape, const3),
                pl.BlockSpec(ar.shape, const3),
                pl.BlockSpec(ai.shape, const3),
                pl.BlockSpec((None, SCAN_SEQS, nstate), lambda g, c: (g, 0, 0)),
                pl.BlockSpec((None, SCAN_SEQS, width), lambda g, c: (g, 0, 0))]
    fin_specs = [pl.BlockSpec((None, SCAN_SEQS, nstate), lambda g, c: (g, 0, 0)),
                 pl.BlockSpec((None, SCAN_SEQS, width), lambda g, c: (g, 0, 0)),
                 pl.BlockSpec((None, SCAN_SEQS, width), lambda g, c: (g, 0, 0))]
    fin_shapes = [jax.ShapeDtypeStruct((groups, SCAN_SEQS, nstate), F32),
                  jax.ShapeDtypeStruct((groups, SCAN_SEQS, width), F32),
                  jax.ShapeDtypeStruct((groups, SCAN_SEQS, width), F32)]
    out_specs, out_shapes = fin_specs, fin_shapes
    if store:
        out_specs = [pl.BlockSpec((chunk_rows, width), lambda g, c: (chunk(g, c), 0))] * 2 + fin_specs
        out_shapes = [jax.ShapeDtypeStruct((rows_total, width), F32)] * 2 + fin_shapes
    bufw = 2 * nstate // (2 * S5_COLBLK)
    est = (10 * chunk_rows * width * 4 + chunk_rows * bufw * 4 * 2
           + 4 * bblk.size * 2 + 8 * SCAN_SEQS * nstate * 4)
    return pl.pallas_call(
        kern,
        grid=(groups, nc),
        in_specs=in_specs,
        out_specs=out_specs,
        out_shape=out_shapes,
        scratch_shapes=[pltpu.VMEM((chunk_rows, bufw), F32),
                        pltpu.VMEM((SCAN_SEQS, nstate), F32),
                        pltpu.VMEM((SCAN_SEQS, width), F32),
                        pltpu.VMEM((SCAN_SEQS, width), F32)],
        compiler_params=_cparams(est, ("arbitrary", "arbitrary")),
        name="scan_store" if store else "scan_ends",
    )(s_arr, s_arr, s_arr, bblk, cblk, ar, ai, init5, initl)


def _chain_kernel(e5_ref, el_ref, ep_ref, h5_ref, hl_ref, arl_ref, ail_ref, i5_ref, il_ref, *, reverse):
    nseg = SAMPLE_SEGS
    sw = arl_ref.shape[-1]
    shift = (SCAN_SEQS - 1) if reverse else 1
    order = range(nseg - 2, -1, -1) if reverse else range(1, nseg)
    seg5 = lax.broadcasted_iota(jnp.int32, (SCAN_SEQS, sw), 0) % nseg
    for j in range(S5_COLBLK):
        base = 2 * sw * j
        re, im = slice(base, base + sw), slice(base + sw, base + 2 * sw)
        ar, ai = arl_ref[j], ail_ref[j]
        er = pltpu.roll(e5_ref[:, re], shift, axis=0)
        ei = pltpu.roll(e5_ref[:, im], shift, axis=0)
        hr, hi = h5_ref[:, re], h5_ref[:, im]
        for k in order:
            pr = pltpu.roll(hr, shift, axis=0)
            pi = pltpu.roll(hi, shift, axis=0)
            hr = jnp.where(seg5 == k, ar * pr - ai * pi + er, hr)
            hi = jnp.where(seg5 == k, ar * pi + ai * pr + ei, hi)
        i5_ref[:, re] = hr
        i5_ref[:, im] = hi
    segl = lax.broadcasted_iota(jnp.int32, el_ref.shape, 0) % nseg
    el = pltpu.roll(el_ref[...], shift, axis=0)
    ep = pltpu.roll(ep_ref[...], shift, axis=0)
    h = hl_ref[...]
    for k in order:
        h = jnp.where(segl == k, ep * pltpu.roll(h, shift, axis=0) + el, h)
    il_ref[...] = h


def _chain(end5, endl, endp, h5_rows, hl_rows, arl, ail, *, direction):
    kern = functools.partial(_chain_kernel, reverse=direction == 1)
    return pl.pallas_call(
        kern,
        out_shape=[jax.ShapeDtypeStruct(end5.shape, F32), jax.ShapeDtypeStruct(endl.shape, F32)],
        name="scan_chain",
    )(end5, endl, endp, h5_rows, hl_rows, arl, ail)


def _ab_out_kernel(x_ref, mod_ref, s_ref, xg_ref, yf_ref, yb_ref, hf_ref, hb_ref, d_ref,
                   wglu_ref, bglu_ref, wout_ref, o_ref):
    width = xg_ref.shape[1]
    m = mod_ref[...]
    y = s_ref[...] * d_ref[...] + yf_ref[...] + yb_ref[...]
    zs = _gelu(y)
    gate = jnp.dot(zs.astype(BF16), wglu_ref[...], preferred_element_type=F32) + bglu_ref[...]
    s5_out = zs * jax.nn.sigmoid(gate)
    lru_out = (hf_ref[...] + hb_ref[...]) * _gelu(xg_ref[...])
    out = (jnp.dot(s5_out.astype(BF16), wout_ref[:width, :], preferred_element_type=F32)
           + jnp.dot(lru_out.astype(BF16), wout_ref[width:, :], preferred_element_type=F32))
    o_ref[...] = x_ref[...] + m[2:3] * out


def _ab_out(x, mod, s_arr, xg, yf, yb, hf, hb, s5_d, w_glu, b_glu, w_out, *, n_ctx, dec_seq, tm):
    t, d = x.shape
    width = xg.shape[1]
    const = lambda i: (0, 0)
    row = lambda i: (i, 0)
    est = 4 * tm * d * 4 + 12 * tm * width * 4 + 4 * (width * width + 2 * width * d) + 8 * tm * width * 4
    return pl.pallas_call(
        _ab_out_kernel,
        grid=(t // tm,),
        in_specs=[pl.BlockSpec((tm, d), row),
                  pl.BlockSpec((None, 6, d), lambda i: (_cond_index(i * tm, n_ctx, dec_seq), 0, 0)),
                  pl.BlockSpec((tm, width), row),
                  pl.BlockSpec((tm, width), row),
                  pl.BlockSpec((tm, width), row),
                  pl.BlockSpec((tm, width), row),
                  pl.BlockSpec((tm, width), row),
                  pl.BlockSpec((tm, width), row),
                  pl.BlockSpec((1, width), const),
                  pl.BlockSpec(w_glu.shape, const),
                  pl.BlockSpec((1, width), const),
                  pl.BlockSpec(w_out.shape, const)],
        out_specs=pl.BlockSpec((tm, d), row),
        out_shape=jax.ShapeDtypeStruct((t, d), F32),
        compiler_params=_cparams(est, ("parallel",)),
        name="ab_out",
    )(x, mod, s_arr, xg, yf, yb, hf, hb, s5_d, w_glu, b_glu, w_out)


def _c_in_kernel(x_ref, mod_ref, g_ref, w_ref, b_ref, cw_ref, cb_ref, x0_ref, vg_ref, *, n_ctx, ctx_row):
    tm = x_ref.shape[0]
    width = x0_ref.shape[1]
    m = mod_ref[...]
    h = _norm_mod(x_ref[...], g_ref[...], m[1:2], m[0:1])
    z = jnp.dot(h.astype(BF16), w_ref[...], preferred_element_type=F32) + b_ref[...]
    row_len = jnp.where(pl.program_id(0) * tm < n_ctx, ctx_row, GRID_W)
    pos = lax.broadcasted_iota(jnp.int32, (tm, width), 0) & (row_len - 1)
    cw = cw_ref[...]
    cb = cb_ref[...]
    parts = [_dwconv_rows(z[:, k * width:(k + 1) * width], cw[:, k * width:(k + 1) * width],
                          cb[:, k * width:(k + 1) * width], pos, row_len, 1) for k in range(3)]
    x0_ref[...] = parts[0]
    vg_ref[...] = parts[2] * parts[1]


def _c_in(x, mod, g, w_in, b_in, conv_w, conv_b, *, n_ctx, ctx_row, dec_seq, tm):
    t, d = x.shape
    width = w_in.shape[1] // 3
    kern = functools.partial(_c_in_kernel, n_ctx=n_ctx, ctx_row=ctx_row)
    const = lambda i: (0, 0)
    est = 2 * tm * d * 4 + 4 * d * 3 * width + 4 * tm * width * 4 + 6 * tm * 3 * width * 4
    return pl.pallas_call(
        kern,
        grid=(t // tm,),
        in_specs=[pl.BlockSpec((tm, d), lambda i: (i, 0)),
                  pl.BlockSpec((None, 6, d), lambda i: (_cond_index(i * tm, n_ctx, dec_seq), 0, 0)),
                  pl.BlockSpec((1, d), const),
                  pl.BlockSpec(w_in.shape, const),
                  pl.BlockSpec((1, 3 * width), const),
                  pl.BlockSpec(conv_w.shape, const),
                  pl.BlockSpec((1, 3 * width), const)],
        out_specs=[pl.BlockSpec((tm, width), lambda i: (i, 0))] * 2,
        out_shape=[jax.ShapeDtypeStruct((t, width), F32)] * 2,
        compiler_params=_cparams(est, ("parallel",)),
        name="c_in",
    )(x, mod, g, w_in, b_in, conv_w, conv_b)


def _c_out_kernel(x_ref, mod_ref, x0_ref, vg_ref, yc_ref, hb_ref, w_ref, b_ref, o_ref):
    m = mod_ref[...]
    vg = vg_ref[...]
    y = (yc_ref[...] + hb_ref[...] * vg) * x0_ref[...]
    out = jnp.dot(y.astype(BF16), w_ref[...], preferred_element_type=F32) + b_ref[...]
    o_ref[...] = x_ref[...] + m[2:3] * out


def _c_out(x, mod, x0, vg, yconv, hy_bias, w_out, b_out, *, n_ctx, dec_seq, tm):
    t, d = x.shape
    width = x0.shape[1]
    const = lambda i: (0, 0)
    row = lambda i: (i, 0)
    est = 4 * tm * d * 4 + 6 * tm * width * 4 + 4 * width * d + 4 * tm * width * 4
    return pl.pallas_call(
        _c_out_kernel,
        grid=(t // tm,),
        in_specs=[pl.BlockSpec((tm, d), row),
                  pl.BlockSpec((None, 6, d), lambda i: (_cond_index(i * tm, n_ctx, dec_seq), 0, 0)),
                  pl.BlockSpec((tm, width), row),
                  pl.BlockSpec((tm, width), row),
                  pl.BlockSpec((tm, width), row),
                  pl.BlockSpec((1, width), const),
                  pl.BlockSpec(w_out.shape, const),
                  pl.BlockSpec((1, d), const)],
        out_specs=pl.BlockSpec((tm, d), row),
        out_shape=jax.ShapeDtypeStruct((t, d), F32),
        compiler_params=_cparams(est, ("parallel",)),
        name="c_out",
    )(x, mod, x0, vg, yconv, hy_bias, w_out, b_out)


def _hy_filter_kernel(w1_ref, b1_ref, f1_ref, w2_ref, b2_ref, f2_ref, w3_ref, dec_ref, o_ref, z_scr, *, length):
    ct = o_ref.shape[1]
    nb = HY_BANDS
    ecols = w1_ref.shape[0]
    row = lax.broadcasted_iota(jnp.int32, (length, 1), 0)
    positions = (row, length - row)

    @pl.when(pl.program_id(0) == 0)
    def _():
        for side in range(2):
            posf = positions[side].astype(F32)
            col = lax.broadcasted_iota(jnp.int32, (length, ecols), 1)
            band_idx = jnp.where(col <= nb, col - 1, col - 1 - nb).astype(F32)
            bands = 1e-4 + band_idx * ((nb - 1 - 1e-4) / (nb - 1))
            ang = (2.0 * math.pi / length) * posf * bands
            emb = jnp.where(col == 0, posf / length,
                            jnp.where(col <= nb, jnp.cos(ang),
                                      jnp.where(col <= 2 * nb, -jnp.sin(ang), 0.0)))
            z = jnp.sin(f1_ref[...] * (jnp.dot(emb, w1_ref[...], precision=HIGHEST,
                                               preferred_element_type=F32) + b1_ref[...]))
            z_scr[side] = jnp.sin(f2_ref[...] * (jnp.dot(z, w2_ref[...], precision=HIGHEST,
                                                         preferred_element_type=F32) + b2_ref[...]))

    def half(side):
        wide_row = lax.broadcasted_iota(jnp.int32, (length, ct), 0)
        t = (wide_row if side == 0 else length - wide_row).astype(F32) / length
        filt = jnp.dot(z_scr[side], w3_ref[side], precision=HIGHEST, preferred_element_type=F32)
        return filt * (jnp.exp(-t * jnp.abs(dec_ref[side:side + 1, :])) + HY_SHIFT)

    fwd = half(0)
    bwd = jnp.where(lax.broadcasted_iota(jnp.int32, (length, ct), 0) == 0, 0.0, half(1))
    ss = jnp.sum(fwd * fwd, axis=0, keepdims=True) + jnp.sum(bwd * bwd, axis=0, keepdims=True)
    scale = lax.rsqrt(ss + EPS)
    o_ref[:length, :] = fwd * scale
    o_ref[length:, :] = bwd * scale


def _hy_filter(length, w1, b1, f1, w2, b2, f2, w3, decay, *, ct):
    emb, ff = w1.shape
    width = decay.shape[1]
    ecols = 128
    w1p = jnp.zeros((ecols, ff), F32).at[:emb].set(w1)
    w3s = w3.reshape(ff, 2, width).transpose(1, 0, 2)
    kern = functools.partial(_hy_filter_kernel, length=length)
    const = lambda j: (0, 0)
    est = 10 * length * ct * 4 + 4 * length * ecols * 4
    return pl.pallas_call(
        kern,
        grid=(width // ct,),
        in_specs=[pl.BlockSpec((ecols, ff), const),
                  pl.BlockSpec((1, ff), const),
                  pl.BlockSpec((1, ff), const),
                  pl.BlockSpec((ff, ff), const),
                  pl.BlockSpec((1, ff), const),
                  pl.BlockSpec((1, ff), const),
                  pl.BlockSpec((2, ff, ct), lambda j: (0, 0, j)),
                  pl.BlockSpec((2, ct), lambda j: (0, j))],
        out_specs=pl.BlockSpec((2 * length, ct), lambda j: (0, j)),
        out_shape=jax.ShapeDtypeStruct((2 * length, width), F32),
        scratch_shapes=[pltpu.VMEM((2, length, ff), F32)],
        compiler_params=_cparams(est, ("arbitrary",)),
        name="hy_filter",
    )(w1p, b1.reshape(1, ff), f1.reshape(1, ff), w2, b2.reshape(1, ff), f2.reshape(1, ff), w3s, decay)


def _fftconv_kernel(ct_ref, st_ref, x_ref, k_ref, c_ref, s_ref, twr_ref, twi_ref, o_ref, *scratch, n1):
    kfr, kfi = scratch if scratch else (None, None)
    h1 = n1 // 2
    n2 = c_ref.shape[0]
    n = n1 * n2
    cm = c_ref[...]
    sm = s_ref[...]

    def dft(pairs, inverse):
        w = pairs[0][0].shape[1]
        parts = jnp.concatenate([p.astype(BF16) for pair in pairs for p in pair], axis=1)
        cp = jnp.dot(cm, parts, preferred_element_type=F32)
        sp = jnp.dot(sm, parts, preferred_element_type=F32)
        out = []
        for i in range(len(pairs)):
            ca, cb = cp[:, 2 * i * w:(2 * i + 1) * w], cp[:, (2 * i + 1) * w:(2 * i + 2) * w]
            sa, sb = sp[:, 2 * i * w:(2 * i + 1) * w], sp[:, (2 * i + 1) * w:(2 * i + 2) * w]
            out.append((ca - sb, cb + sa) if inverse else (ca + sb, cb - sa))
        return out

    def slab(t1):
        return slice(t1 * n2, (t1 + 1) * n2)

    twr1, twi1 = twr_ref[...], twi_ref[...]
    tw0 = (jnp.ones_like(twr1), jnp.zeros_like(twi1))

    def next_twiddle(tw):
        return tw[0] * twr1 - tw[1] * twi1, tw[0] * twi1 + tw[1] * twr1

    def filter_stage1(k1, tw):
        ar = ai = None
        for t1 in range(n1):
            kr = k_ref[slab(t1), :]
            tr, ti = ct_ref[k1, t1] * kr, -(st_ref[k1, t1] * kr)
            ar = tr if ar is None else ar + tr
            ai = ti if ai is None else ai + ti
        ar, ai = ar * (1.0 / n), ai * (1.0 / n)
        return ar * tw[0] + ai * tw[1], ai * tw[0] - ar * tw[1]

    if kfr is not None:
        @pl.when(pl.program_id(1) == 0)
        def _():
            def fbody(k1, tw):
                (kfr[k1], kfi[k1]), = dft([filter_stage1(k1, tw)], False)
                return next_twiddle(tw)
            lax.fori_loop(0, n1, fbody, tw0)

    o_ref[...] = jnp.zeros_like(o_ref)

    def forward(k1, tw):
        twr, twi = tw
        ar = ai = None
        for t1 in range(h1):
            cc, ss = ct_ref[k1, t1], st_ref[k1, t1]
            xr, xi = x_ref[0, slab(t1), :], x_ref[1, slab(t1), :]
            tr, ti = cc * xr + ss * xi, cc * xi - ss * xr
            ar = tr if ar is None else ar + tr
            ai = ti if ai is None else ai + ti
        data = (ar * twr + ai * twi, ai * twr - ar * twi)
        if kfr is not None:
            (fr, fi), = dft([data], False)
            gr, gi = kfr[k1], kfi[k1]
        else:
            (fr, fi), (gr, gi) = dft([data, filter_stage1(k1, tw)], False)
        return fr * gr - fi * gi, fr * gi + fi * gr

    def inverse(k1, tw, y):
        twr, twi = tw
        (br, bi), = dft([y], True)
        br, bi = br * twr - bi * twi, bi * twr + br * twi
        for t1 in range(h1):
            cc, ss = ct_ref[k1, t1], st_ref[k1, t1]
            o_ref[0, slab(t1), :] += cc * br - ss * bi
            o_ref[1, slab(t1), :] += cc * bi + ss * br

    def body(k1, carry):
        tw_prev, y_prev = carry
        tw = next_twiddle(tw_prev)
        y = forward(k1, tw)
        inverse(k1 - 1, tw_prev, y_prev)
        return tw, y

    tw_last, y_last = lax.fori_loop(1, n1, body, (tw0, forward(0, tw0)))
    inverse(n1 - 1, tw_last, y_last)


def _fftconv(v, circ, *, n1, ct):
    bsz, length, width = v.shape
    n = 2 * length
    n2 = n // n1
    idx = np.arange(n2)
    ang = 2.0 * np.pi * ((idx[:, None] * idx[None, :]) % n2) / n2
    cmat = jnp.asarray(np.cos(ang), F32).astype(BF16)
    smat = jnp.asarray(np.sin(ang), F32).astype(BF16)
    tang = 2.0 * np.pi * idx / n
    twr = jnp.asarray(np.broadcast_to(np.cos(tang)[:, None], (n2, ct)), F32)
    twi = jnp.asarray(np.broadcast_to(np.sin(tang)[:, None], (n2, ct)), F32)
    i1 = np.arange(n1)
    ang1 = 2.0 * np.pi * ((i1[:, None] * i1[None, :]) % n1) / n1
    ctab = jnp.asarray(np.round(np.cos(ang1), 12), F32)
    stab = jnp.asarray(np.round(np.sin(ang1), 12), F32)
    kern = functools.partial(_fftconv_kernel, n1=n1)
    const = lambda j, p: (0, 0)
    once = pl.Buffered(1)
    npairs = bsz // 2
    scratch = [pltpu.VMEM((n1, n2, ct), F32)] * 2 if npairs > 1 else []
    est = (4 * 2 * length * ct * 4 + n * ct * 4 + 2 * n2 * n2 * 2 + 2 * n2 * ct * 4
           + len(scratch) * n * ct * 4 + (8 + 2 * n1) * n2 * ct * 4)
    return pl.pallas_call(
        kern,
        grid=(width // ct, npairs),
        in_specs=[pl.BlockSpec(memory_space=pltpu.SMEM),
                  pl.BlockSpec(memory_space=pltpu.SMEM),
                  pl.BlockSpec((2, length, ct), lambda j, p: (p, 0, j)),
                  pl.BlockSpec((n, ct), lambda j, p: (0, j), pipeline_mode=once),
                  pl.BlockSpec((n2, n2), const, pipeline_mode=once),
                  pl.BlockSpec((n2, n2), const, pipeline_mode=once),
                  pl.BlockSpec((n2, ct), const, pipeline_mode=once),
                  pl.BlockSpec((n2, ct), const, pipeline_mode=once)],
        out_specs=pl.BlockSpec((2, length, ct), lambda j, p: (p, 0, j)),
        out_shape=jax.ShapeDtypeStruct((bsz, length, width), F32),
        scratch_shapes=scratch,
        compiler_params=_cparams(est, ("arbitrary", "arbitrary")),
        name="fftconv",
    )(ctab, stab, v, circ, cmat, smat, twr, twi)


def _sorting_network(n):
    pairs = []
    p = 1
    while p < n:
        k = p
        while k >= 1:
            for j in range(k % p, n - k, 2 * k):
                for i in range(min(k, n - j - k)):
                    if (i + j) // (2 * p) == (i + j + k) // (2 * p):
                        pairs.append((i + j, i + j + k))
            k //= 2
        p *= 2
    return pairs


_SORT16 = _sorting_network(PEER_TOPK)


def _peer_route_kernel(x_ref, mod_ref, g_ref, wq_ref, keys_ref, hb_ref, th_ref, e0_ref, s1_ref, e1_ref,
                       q_scr, sc_scr, top_scr):
    tq = x_ref.shape[0]
    nk = PEER_NKEYS
    k = PEER_TOPK
    nhp = keys_ref.shape[0]
    dkh = keys_ref.shape[2]
    neg = -jnp.inf
    m = mod_ref[...]
    h = _norm_mod(x_ref[...], g_ref[...], m[4:5], m[3:4]).astype(BF16)
    hb_ref[...] = h
    q = jnp.dot(h, wq_ref[...], preferred_element_type=F32)
    for hp in range(nhp):
        q_scr[hp] = q[:, hp * dkh:(hp + 1) * dkh].astype(BF16)

    lw = V7X_LANES
    sub = V7X_SUBLANES
    assert k == 2 * sub
    ncand = k + 7 * sub + (k - sub)
    npad = sum(sub - k // (i + 1) for i in range(2, sub))
    iota_k = lax.broadcasted_iota(jnp.int32, (nk, lw), 0)
    iota_c = lax.broadcasted_iota(jnp.int32, (ncand, lw), 0)
    iota_8 = lax.broadcasted_iota(jnp.int32, (sub, lw), 0)
    iota_r = lax.broadcasted_iota(jnp.int32, (k, lw), 0)

    def topk(vals, iota, exact):
        top = jnp.zeros((k, lw), F32)
        for r in range(k):
            mx = jnp.max(vals, axis=0, keepdims=True)
            if exact:
                first = jnp.min(jnp.where(vals == mx, iota, vals.shape[0]), axis=0, keepdims=True)
                vals = jnp.where(iota == first, neg, vals)
            else:
                vals = jnp.where(vals == mx, neg, vals)
            top = jnp.where(iota_r == r, mx, top)
        return top, vals

    def tied(rest, expected):
        gone = jnp.sum(jnp.where(rest == neg, 1.0, 0.0), axis=0, keepdims=True)
        return jnp.where(gone != expected, 1.0, 0.0)

    def exchange(v, i, j):
        v[i], v[j] = jnp.maximum(v[i], v[j]), jnp.minimum(v[i], v[j])

    def sort_top(tiles):
        v = list(tiles)
        for i, j in _SORT16:
            exchange(v, i, j)
        for shift in (sub // 2, sub // 4, sub // 8):
            other = [pltpu.roll(t, shift, axis=0) for t in v]
            v = [jnp.maximum(v[r], other[k - 1 - r]) for r in range(k)]
            d = k // 2
            while d >= 1:
                for i in range(k):
                    if i & d == 0:
                        exchange(v, i, i + d)
                d //= 2
        top = jnp.zeros((k, lw), F32)
        for r in range(k):
            top = jnp.where(iota_r == r, jnp.concatenate([v[r]] * (k // sub), axis=0), top)
        return top

    def halves(hd):
        assert nk == k * sub
        for p in range(2):
            for c in range(tq // lw):
                lanes = slice(c * lw, (c + 1) * lw)
                top_scr[2 * hd + p, :, lanes] = sort_top(
                    [sc_scr[2 * hd + p, t * sub:(t + 1) * sub, lanes] for t in range(nk // sub)])

    def pairs(hd, exact):
        flag = jnp.zeros((1, lw), F32)
        for c in range(tq // lw):
            lanes = slice(c * lw, (c + 1) * lw)
            a = top_scr[2 * hd, :, lanes]
            b = top_scr[2 * hd + 1, :, lanes]
            pieces = [a[0:1] + b, a[1:2] + b[0:sub]]
            for i in range(2, sub):
                pieces.append(jnp.where(iota_8 < k // (i + 1), a[i:i + 1] + b[0:sub], neg))
            pieces.append(a[sub:k] + b[0:1])
            vmax = a[0:1] + b[0:1]
            top, rest = topk(jnp.concatenate(pieces, axis=0), iota_c, exact)
            z = jnp.sum(jnp.exp(top - vmax), axis=0, keepdims=True)
            last = top[k - 1:k]
            nxt = jnp.max(rest, axis=0, keepdims=True)
            tau = jnp.where(nxt == neg, last, 0.5 * (last + nxt))
            flag = jnp.maximum(flag, tied(rest, float(k + npad)))
            s0 = sc_scr[2 * hd, :, lanes]
            s1 = sc_scr[2 * hd + 1, :, lanes]
            th = jnp.where(s0 >= a[k - 1:k], tau - s0, jnp.inf)
            s1 = jnp.where(s1 >= b[k - 1:k], s1, neg)
            e0 = jnp.exp(s0 - a[0:1]) / z
            per = th_ref.shape[2]
            for r in range(nk // per):
                th_ref[hd, r, :, lanes] = th[r * per:(r + 1) * per]
                e0_ref[hd, r, :, lanes] = e0[r * per:(r + 1) * per]
            s1_ref[hd, :, lanes] = s1
            e1_ref[hd, :, lanes] = jnp.exp(s1 - b[0:1])
        return jnp.max(flag)

    def head_body(hd, carry):
        for p in range(2):
            sc_scr[2 * hd + p] = lax.dot_general(keys_ref[2 * hd + p], q_scr[2 * hd + p], (((1,), (1,)), ((), ())),
                                                 preferred_element_type=F32)
        halves(hd)

        @pl.when(pairs(hd, False) > 0.0)
        def _():
            pairs(hd, True)

        return carry

    lax.fori_loop(0, nhp // 2, head_body, 0)


def _peer_route(x, mod, g, wq, keys, *, n_ctx, dec_seq, tq):
    t, d = x.shape
    nhp, nk, dkh = keys.shape
    nh = nhp // 2
    const = lambda i: (0, 0)
    hshape = jax.ShapeDtypeStruct((nh, nk, t), F32)
    hspec = pl.BlockSpec((nh, nk, tq), lambda i: (0, 0, i))
    per = PEER_GROUP // nk
    gshape = jax.ShapeDtypeStruct((nh, nk // per, per, t), F32)
    gspec = pl.BlockSpec((nh, nk // per, per, tq), lambda i: (0, 0, 0, i))
    pad = V7X_SUBLANES // per
    est = (2 * tq * d * 4 + 4 * d * nhp * dkh + 2 * tq * d * 2 + (4 + 4 * pad) * nh * nk * tq * 4
           + 2 * nhp * nk * tq * 4 + 6 * tq * nhp * dkh * 4)
    return pl.pallas_call(
        _peer_route_kernel,
        grid=(t // tq,),
        in_specs=[pl.BlockSpec((tq, d), lambda i: (i, 0)),
                  pl.BlockSpec((None, 6, d), lambda i: (_cond_index(i * tq, n_ctx, dec_seq), 0, 0)),
                  pl.BlockSpec((1, d), const),
                  pl.BlockSpec(wq.shape, const),
                  pl.BlockSpec(keys.shape, lambda i: (0, 0, 0))],
        out_specs=[pl.BlockSpec((tq, d), lambda i: (i, 0)), gspec, gspec, hspec, hspec],
        out_shape=[jax.ShapeDtypeStruct((t, d), BF16), gshape, gshape, hshape, hshape],
        scratch_shapes=[pltpu.VMEM((nhp, tq, dkh), BF16),
                        pltpu.VMEM((nhp, nk, tq), F32),
                        pltpu.VMEM((nhp, PEER_TOPK, tq), F32)],
        compiler_params=_cparams(est, ("parallel",)),
        name="peer_route",
    )(x, mod, g, wq, keys)


def _peer_expert_kernel(hbt_ref, u_ref, vt_ref, th_ref, e0_ref, s1_ref, e1_ref, x_ref, mod_ref, o_ref,
                        acc, st_a, st_b, hbuf_a, hbuf_b, *, nc):
    g = pl.program_id(0)
    nh = th_ref.shape[0]
    nk = s1_ref.shape[1]
    ngroups, per = th_ref.shape[1], th_ref.shape[2]
    gw = per * nk
    tq = st_a.shape[1]
    kh = PEER_ACC_ROWS // per
    c3 = jnp.maximum(g - 2, 0) % nc

    @pl.when(g == 0)
    def _():
        st_b[...] = jnp.zeros_like(st_b)
        hbuf_a[...] = jnp.zeros_like(hbuf_a)
        hbuf_b[...] = jnp.zeros_like(hbuf_b)

    @pl.when(c3 == 0)
    def _():
        acc[...] = jnp.zeros_like(acc)

    def step(st_w, st_r, hbuf_w, hbuf_r):
        def group(j, carry):
            rows = pl.ds(pl.multiple_of(j * gw, gw), gw)
            acc[...] += jnp.dot(vt_ref[j], hbuf_r[rows, :], preferred_element_type=F32)
            st_w[rows, :] = jnp.dot(u_ref[rows, :], hbt_ref[...], preferred_element_type=F32)
            for q in range(tq // V7X_LANES):
                lanes = slice(q * V7X_LANES, (q + 1) * V7X_LANES)
                for half in range(nk // kh):
                    keys = slice(half * kh, (half + 1) * kh)
                    w = [None] * per
                    for hd in range(nh):
                        s1 = s1_ref[hd, keys, lanes]
                        e1 = e1_ref[hd, keys, lanes]
                        for i in range(per):
                            sel = jnp.where(s1 >= th_ref[hd, j, i:i + 1, lanes], e1, 0.0) * e0_ref[hd, j, i:i + 1, lanes]
                            w[i] = sel if w[i] is None else w[i] + sel
                    for i in range(per):
                        tile = pl.ds(pl.multiple_of(j * gw + i * nk + half * kh, kh), kh)
                        hbuf_w[tile, lanes] = (_gelu(st_r[tile, lanes]) * w[i]).astype(BF16)
            return carry

        lax.fori_loop(0, ngroups, group, 0)

    @pl.when(g % 2 == 0)
    def _():
        step(st_a, st_b, hbuf_b, hbuf_a)

    @pl.when(g % 2 == 1)
    def _():
        step(st_b, st_a, hbuf_a, hbuf_b)

    @pl.when(jnp.logical_and(g >= 2, c3 == nc - 1))
    def _():
        o_ref[...] = x_ref[...] + mod_ref[5:6, :] * acc[...].T


def _peer_expert(hb, u_tabs, layer, vt_tab, th, e0, s1, e1, x, mod, *, n_ctx, dec_seq, tq, ec):
    t, d = x.shape
    ne = u_tabs.shape[1]
    nh, nk, _ = s1.shape
    per = th.shape[2]
    assert per * nk == PEER_GROUP
    ngroups = ec // PEER_GROUP
    nc = ne // ec
    npairs = (t // tq) * nc

    def pair(g, lag):
        p = jnp.clip(g - lag, 0, npairs - 1)
        return p // nc, p % nc

    def cond3(g):
        return _cond_index(pair(g, 2)[0] * tq, n_ctx, dec_seq)

    est = (2 * tq * d * 2 + 4 * ec * d * 2 + 4 * nh * ngroups * V7X_SUBLANES * tq * 4 + 4 * nh * nk * tq * 4
           + 4 * tq * d * 4 + d * tq * 4 + 2 * ec * tq * 2 + 2 * ec * tq * 4 + 3 * PEER_GROUP * tq * 4)
    return pl.pallas_call(
        functools.partial(_peer_expert_kernel, nc=nc),
        grid=(npairs + 2,),
        in_specs=[pl.BlockSpec((d, tq), lambda g: (0, pair(g, 0)[0])),
                  pl.BlockSpec((None, ec, d), lambda g: (layer, pair(g, 0)[1], 0)),
                  pl.BlockSpec((ngroups, d, PEER_GROUP), lambda g: (pair(g, 2)[1], 0, 0)),
                  pl.BlockSpec((nh, ngroups, per, tq), lambda g: (0, pair(g, 1)[1], 0, pair(g, 1)[0])),
                  pl.BlockSpec((nh, ngroups, per, tq), lambda g: (0, pair(g, 1)[1], 0, pair(g, 1)[0])),
                  pl.BlockSpec((nh, nk, tq), lambda g: (0, 0, pair(g, 1)[0])),
                  pl.BlockSpec((nh, nk, tq), lambda g: (0, 0, pair(g, 1)[0])),
                  pl.BlockSpec((tq, d), lambda g: (pair(g, 2)[0], 0)),
                  pl.BlockSpec((None, 6, d), lambda g: (cond3(g), 0, 0))],
        out_specs=pl.BlockSpec((tq, d), lambda g: (pair(g, 2)[0], 0)),
        out_shape=jax.ShapeDtypeStruct((t, d), F32),
        scratch_shapes=[pltpu.VMEM((d, tq), F32), pltpu.VMEM((ec, tq), F32), pltpu.VMEM((ec, tq), F32),
                        pltpu.VMEM((ec, tq), BF16), pltpu.VMEM((ec, tq), BF16)],
        compiler_params=_cparams(est, ("arbitrary",)),
        name="peer_expert",
    )(hb.T, u_tabs, vt_tab, th, e0, s1, e1, x, mod)


def _final_kernel(x_ref, g_ref, o_ref):
    x = x_ref[...]
    ms = jnp.mean(x * x, axis=-1, keepdims=True)
    o_ref[...] = (x * lax.rsqrt(ms + EPS)) * g_ref[...]


def _final_norm(x, g, *, tm):
    t, d = x.shape
    return pl.pallas_call(
        _final_kernel,
        grid=(t // tm,),
        in_specs=[pl.BlockSpec((tm, d), lambda i: (i, 0)), pl.BlockSpec((1, d), lambda i: (0, 0))],
        out_specs=pl.BlockSpec((tm, d), lambda i: (i, 0)),
        out_shape=jax.ShapeDtypeStruct((t, d), F32),
        compiler_params=_cparams(6 * tm * d * 4, ("parallel",)),
        name="final_norm",
    )(x, g)


def _block_diag(w):
    h, i, j = w.shape
    return jnp.einsum('hij,hg->higj', w, jnp.eye(h, dtype=w.dtype)).reshape(h * i, h * j)


def _s5_discretize(a_re, a_im, log_dt, b_re, b_im):
    dt = jnp.exp(log_dt)[:, None]
    mag = jnp.exp(a_re * dt)
    abar_r = mag * jnp.cos(a_im * dt)
    abar_i = mag * jnp.sin(a_im * dt)
    den = a_re * a_re + a_im * a_im
    num_r = abar_r - 1.0
    coef_r = ((num_r * a_re + abar_i * a_im) / den)[..., None]
    coef_i = ((abar_i * a_re - num_r * a_im) / den)[..., None]
    return abar_r, abar_i, coef_r * b_re - coef_i * b_im, coef_r * b_im + coef_i * b_re


def _s5_blocks(abar_r, abar_i, bbar_r, bbar_i, c_re, c_im):
    g, p, ch = bbar_r.shape
    gb = g // S5_COLBLK

    def vec(v):
        return v.reshape(S5_COLBLK, 1, gb * p)

    def bmat(b):
        bt = b.transpose(0, 2, 1).reshape(S5_COLBLK, gb, ch, p)
        return jnp.stack([_block_diag(bt[j]) for j in range(S5_COLBLK)])

    def cmat(cc):
        ct = cc.transpose(0, 2, 1).reshape(S5_COLBLK, gb, p, ch)
        return jnp.stack([_block_diag(ct[j]) for j in range(S5_COLBLK)])

    bblk = jnp.concatenate([bmat(bbar_r), bmat(bbar_i)], axis=2).astype(BF16)
    cblk = jnp.concatenate([cmat(c_re), cmat(-c_im)], axis=1).astype(BF16)
    return bblk, cblk, vec(abar_r), vec(abar_i)


def _state_to_blocked(re, im):
    n, g, p = re.shape
    gb = g // S5_COLBLK
    both = jnp.stack([re.reshape(n, S5_COLBLK, gb * p), im.reshape(n, S5_COLBLK, gb * p)], axis=2)
    return both.reshape(n, -1)


def _state_from_blocked(st, g, p):
    n = st.shape[0]
    both = st.reshape(n, S5_COLBLK, 2, (g // S5_COLBLK) * p)
    return both[:, :, 0].reshape(n, g, p), both[:, :, 1].reshape(n, g, p)


def _complex_power(ar, ai, n):
    for _ in range(int(math.log2(n))):
        ar, ai = ar * ar - ai * ai, 2.0 * ar * ai
    return ar, ai


def _mixer_ab(xs, mod, p, le, h0, *, bc, lc, bs, ls, tm):
    n_ctx = bc * lc
    width = p['s5_d'].shape[1]
    g_cnt, p_cnt = p['s5_a_re'].shape[2:]
    w_gates = jnp.concatenate([_block_diag(p['lru_w_a'][le, 0]), _block_diag(p['lru_w_x'][le, 0]),
                               _block_diag(p['lru_w_a'][le, 1]), _block_diag(p['lru_w_x'][le, 1])],
                              axis=1).astype(BF16)
    b_gates = jnp.concatenate([p['lru_b_a'][le, 0], p['lru_b_x'][le, 0],
                               p['lru_b_a'][le, 1], p['lru_b_x'][le, 1]]).reshape(1, 4 * width)
    c8 = LRU_C * jax.nn.softplus(-p['lru_lambda'][le])
    pass_ctx = (n_ctx, 0)
    w_in = p['w_in_ab'][le].astype(BF16)
    s_arrs, xgs = zip(*[_ab_in(x, mod, p['norm1_g'][2 * le].reshape(1, -1), w_in,
                               p['lru_conv_w'][le], p['lru_conv_b'][le].reshape(1, width), w_gates, b_gates, c8,
                               n_ctx=nc_, ctx_row=lc, dec_seq=ls, tm=tm) for x, nc_ in zip(xs, pass_ctx)])

    cols = s_arrs[0].shape[1]
    gc = bc // SCAN_SEQS
    lseg = ls // SAMPLE_SEGS
    s_ctx = (s_arrs[0].reshape(gc, SCAN_SEQS, lc, cols).transpose(0, 2, 1, 3)
             .reshape(n_ctx, cols))
    s_smp = (s_arrs[1].reshape(bs * SAMPLE_SEGS, lseg, cols).transpose(1, 0, 2)
             .reshape(bs * ls, cols))
    chunk_rows = min(1024, lc * SCAN_SEQS, lseg * SCAN_SEQS)

    outs = {}
    finals = {}
    for d in range(2):
        abar_r, abar_i, bbar_r, bbar_i = _s5_discretize(
            p['s5_a_re'][le, d], p['s5_a_im'][le, d], p['s5_log_dt'][le, d],
            p['s5_b_re'][le, d], p['s5_b_im'][le, d])
        bblk, cblk, ar, ai = _s5_blocks(abar_r, abar_i, bbar_r, bbar_i,
                                        p['s5_c_re'][le, d], p['s5_c_im'][le, d])
        nstate = 2 * g_cnt * p_cnt
        z5 = jnp.zeros((gc, SCAN_SEQS, nstate), F32)
        zl = jnp.zeros((gc, SCAN_SEQS, width), F32)
        y, hl, f5, fl, _ = _scan(s_ctx, bblk, cblk, ar, ai, z5, zl, direction=d, store=True,
                                 groups=gc, chunk_rows=chunk_rows)
        outs[('ctx', d)] = (y, hl)
        finals[d] = (f5.reshape(bc, nstate), fl.reshape(bc, width))
        z5 = jnp.zeros((1, SCAN_SEQS, nstate), F32)
        zl = jnp.zeros((1, SCAN_SEQS, width), F32)
        e5, el, ep = _scan(s_smp, bblk, cblk, ar, ai, z5, zl, direction=d, store=False,
                           groups=1, chunk_rows=chunk_rows)
        first = (SAMPLE_SEGS - 1) if d == 1 else 0
        h5 = _state_to_blocked(h0[0][:, d], h0[1][:, d])
        h5_rows = jnp.zeros((bs, SAMPLE_SEGS, nstate), F32).at[:, first].set(h5).reshape(SCAN_SEQS, nstate)
        hl_rows = jnp.zeros((bs, SAMPLE_SEGS, width), F32).at[:, first].set(h0[2][:, d]).reshape(SCAN_SEQS, width)
        arl, ail = _complex_power(ar, ai, lseg)
        i5, il = _chain(e5[0], el[0], ep[0], h5_rows, hl_rows, arl, ail, direction=d)
        y, hl, _, _, _ = _scan(s_smp, bblk, cblk, ar, ai, i5[None], il[None], direction=d, store=True,
                               groups=1, chunk_rows=chunk_rows)
        outs[('smp', d)] = (y, hl)

    def to_tokens(name, arr):
        if name == 'ctx':
            return arr.reshape(gc, lc, SCAN_SEQS, width).transpose(0, 2, 1, 3).reshape(n_ctx, width)
        return arr.reshape(lseg, bs * SAMPLE_SEGS, width).transpose(1, 0, 2).reshape(bs * ls, width)

    w_glu, w_out = p['s5_w_glu'][le].astype(BF16), p['w_out_ab'][le].astype(BF16)
    x1 = []
    for name, x, s_arr, xg, nc_ in zip(('ctx', 'smp'), xs, s_arrs, xgs, pass_ctx):
        yf, hf = (to_tokens(name, a) for a in outs[(name, 0)])
        yb, hb = (to_tokens(name, a) for a in outs[(name, 1)])
        x1.append(_ab_out(x, mod, s_arr, xg, yf, yb, hf, hb, p['s5_d'][le].reshape(1, width),
                          w_glu, p['s5_b_glu'][le].reshape(1, width), w_out, n_ctx=nc_, dec_seq=ls, tm=tm))
    st = [_state_from_blocked(finals[d][0], g_cnt, p_cnt) for d in range(2)]
    new_re = jnp.stack([st[0][0], st[1][0]], axis=1)
    new_im = jnp.stack([st[0][1], st[1][1]], axis=1)
    new_lru = jnp.stack([finals[0][1], finals[1][1]], axis=1)
    return x1, (new_re, new_im, new_lru)


def _fft_split(length):
    n = 2 * length
    n1 = 2
    while n // n1 > 1024:
        n1 *= 2
    return n1


def _mixer_c(xs, mod, p, lo, *, bc, lc, bs, ls, tm):
    width = p['hy_bias'].shape[1]
    w_in, w_out = p['w_in_c'][lo].astype(BF16), p['w_out_c'][lo].astype(BF16)
    out = []
    for x, b, l, nc_ in zip(xs, (bc, bs), (lc, ls), (bc * lc, 0)):
        x0, vg = _c_in(x, mod, p['norm1_g'][2 * lo + 1].reshape(1, -1), w_in,
                       p['b_in_c'][lo].reshape(1, -1), p['hy_conv_w'][lo], p['hy_conv_b'][lo].reshape(1, -1),
                       n_ctx=nc_, ctx_row=lc, dec_seq=ls, tm=tm)
        ct = min(width, 256 if l <= 1024 else 128)
        circ = _hy_filter(l, p['hy_w1'][lo], p['hy_b1'][lo], p['hy_freq1'][lo], p['hy_w2'][lo],
                          p['hy_b2'][lo], p['hy_freq2'][lo], p['hy_w3'][lo], p['hy_decay'][lo], ct=ct)
        yconv = _fftconv(vg.reshape(b, l, width), circ, n1=_fft_split(l), ct=ct).reshape(b * l, width)
        out.append(_c_out(x, mod, x0, vg, yconv, p['hy_bias'][lo].reshape(1, width), w_out,
                          p['b_out_c'][lo].reshape(1, -1), n_ctx=nc_, dec_seq=ls, tm=tm))
    return out


def _peer(xs, mod, p, l, *, pass_ctx, dec_seq, tq, ec):
    keys = p['peer_keys'][l]
    nh, _, nk, dkh = keys.shape
    wq = p['peer_wq'][l].astype(BF16)
    keys = keys.reshape(nh * 2, nk, dkh).astype(BF16)
    ne, d = p['peer_v'][l].shape
    vt = p['peer_v'][l].reshape(ne // PEER_GROUP, PEER_GROUP, d).transpose(0, 2, 1).astype(BF16)
    out = []
    for x, nc_ in zip(xs, pass_ctx):
        hb, th, e0, s1, e1 = _peer_route(x, mod, p['norm2_g'][l].reshape(1, -1), wq, keys,
                                         n_ctx=nc_, dec_seq=dec_seq, tq=tq)
        out.append(_peer_expert(hb, p['peer_u_bf16'], l, vt, th, e0, s1, e1, x, mod,
                                n_ctx=nc_, dec_seq=dec_seq, tq=tq, ec=ec))
    return out


def _forward(x_prompt, x_sample, state_s5_re, state_s5_im, state_lru, c, c_ctx, p, *, tm, tq, ec):
    bc, lc, d = x_prompt.shape
    bs, ls, _ = x_sample.shape
    assert bc % SCAN_SEQS == 0 and bs * SAMPLE_SEGS == SCAN_SEQS
    assert tm % lc == 0 and ls % tm == 0 and (ls // SAMPLE_SEGS) % GRID_W == 0
    n_ctx = bc * lc
    xs = [x_prompt.reshape(n_ctx, d), x_sample.reshape(bs * ls, d)]
    pass_ctx = (n_ctx, 0)
    p = dict(p, peer_u_bf16=p['peer_u'].astype(BF16))
    cond = jnp.zeros((V7X_SUBLANES, d), F32).at[0].set(c_ctx).at[1:1 + bs].set(c)
    mod_all = _modulation(cond, p['w_mod'], p['b_mod'])
    states = None
    for l in range(DEPTH):
        mod = mod_all[l].reshape(V7X_SUBLANES, 6, d)
        if l % 2 == 0:
            le = l // 2
            h0 = (state_s5_re[:, le], state_s5_im[:, le], state_lru[:, le])
            xs, st = _mixer_ab(xs, mod, p, le, h0, bc=bc, lc=lc, bs=bs, ls=ls, tm=tm)
            states = st if states is None else states
        else:
            xs = _mixer_c(xs, mod, p, l // 2, bc=bc, lc=lc, bs=bs, ls=ls, tm=tm)
        xs = _peer(xs, mod, p, l, pass_ctx=pass_ctx, dec_seq=ls, tq=tq, ec=ec)
    g = p['final_g'].reshape(1, d)
    new_re, new_im, new_lru = states
    return (_final_norm(xs[0], g, tm=tm).reshape(bc, lc, d), _final_norm(xs[1], g, tm=tm).reshape(bs, ls, d),
            new_re[:, None], new_im[:, None], new_lru[:, None])


def kernel(x_prompt, x_sample, state_s5_re, state_s5_im, state_lru, c, c_ctx, norm1_g, norm2_g, w_mod, b_mod, w_in_ab, s5_a_re, s5_a_im, s5_log_dt, s5_b_re, s5_b_im, s5_c_re, s5_c_im, s5_d, s5_w_glu, s5_b_glu, lru_conv_w, lru_conv_b, lru_w_a, lru_b_a, lru_w_x, lru_b_x, lru_lambda, w_out_ab, w_in_c, b_in_c, hy_conv_w, hy_conv_b, hy_w1, hy_b1, hy_freq1, hy_w2, hy_b2, hy_freq2, hy_w3, hy_decay, hy_bias, w_out_c, b_out_c, peer_wq, peer_keys, peer_u, peer_v, final_g):
    p = dict(norm1_g=norm1_g, norm2_g=norm2_g, w_mod=w_mod, b_mod=b_mod, w_in_ab=w_in_ab,
             s5_a_re=s5_a_re, s5_a_im=s5_a_im, s5_log_dt=s5_log_dt, s5_b_re=s5_b_re, s5_b_im=s5_b_im,
             s5_c_re=s5_c_re, s5_c_im=s5_c_im, s5_d=s5_d, s5_w_glu=s5_w_glu, s5_b_glu=s5_b_glu,
             lru_conv_w=lru_conv_w, lru_conv_b=lru_conv_b, lru_w_a=lru_w_a, lru_b_a=lru_b_a,
             lru_w_x=lru_w_x, lru_b_x=lru_b_x, lru_lambda=lru_lambda, w_out_ab=w_out_ab,
             w_in_c=w_in_c, b_in_c=b_in_c, hy_conv_w=hy_conv_w, hy_conv_b=hy_conv_b,
             hy_w1=hy_w1, hy_b1=hy_b1, hy_freq1=hy_freq1, hy_w2=hy_w2, hy_b2=hy_b2, hy_freq2=hy_freq2,
             hy_w3=hy_w3, hy_decay=hy_decay, hy_bias=hy_bias, w_out_c=w_out_c, b_out_c=b_out_c,
             peer_wq=peer_wq, peer_keys=peer_keys, peer_u=peer_u, peer_v=peer_v, final_g=final_g)
    return _forward(x_prompt, x_sample, state_s5_re, state_s5_im, state_lru, c, c_ctx, p,
                    tm=512, tq=512, ec=2048)
```

```python
import functools
import math

import jax
import jax.numpy as jnp
import numpy as np
from jax import lax
from jax.experimental import pallas as pl
from jax.experimental.pallas import tpu as pltpu

F32 = jnp.float32
BF16 = jnp.bfloat16
HIGHEST = lax.Precision.HIGHEST

DEPTH = 2
GRID_W = 64
EPS = 1e-6
LRU_C = 8.0
HY_BANDS = 16
HY_SHIFT = 0.05
PEER_NKEYS = 128
PEER_TOPK = 16
PEER_GROUP = 512
PEER_ACC_ROWS = 128

V7X_LANES = 128
V7X_SUBLANES = 8
V7X_VMEM_BYTES = 64 * 1024 * 1024

SCAN_SEQS = V7X_SUBLANES
SAMPLE_SEGS = 4
S5_COLBLK = 4


def _vmem_limit(est_bytes):
    return int(min(V7X_VMEM_BYTES - (8 << 20), max(32 << 20, est_bytes * 3 // 2)))


def _cparams(est_bytes, sem):
    return pltpu.CompilerParams(dimension_semantics=sem, vmem_limit_bytes=_vmem_limit(est_bytes))


def _gelu(x):
    k = 2.0 * 0.7978845608028654 * math.log2(math.e)
    return x / (1.0 + jnp.exp2(x * (-k - (k * 0.044715) * (x * x))))


def _norm_mod(x, g, sc, sh):
    ms = jnp.mean(x * x, axis=-1, keepdims=True)
    return (x * lax.rsqrt(ms + EPS)) * g * (1.0 + sc) + sh


def _cond_index(row0, n_ctx, dec_seq):
    return jnp.where(row0 < n_ctx, 0, 1 + (row0 - n_ctx) // dec_seq)


def _mod_kernel(c_ref, w_ref, b_ref, o_ref):
    c = c_ref[...]
    s = c * jax.nn.sigmoid(c)
    o_ref[...] = jnp.dot(s, w_ref[...], precision=HIGHEST, preferred_element_type=F32) + b_ref[...]


def _modulation(cond, w_mod, b_mod):
    depth, d, n = w_mod.shape
    rows = cond.shape[0]
    tn = 1536
    est = 2 * d * tn * 4 + 4 * rows * (d + tn) * 4
    return pl.pallas_call(
        _mod_kernel,
        grid=(depth, n // tn),
        in_specs=[pl.BlockSpec((rows, d), lambda l, j: (0, 0)),
                  pl.BlockSpec((None, d, tn), lambda l, j: (l, 0, j)),
                  pl.BlockSpec((None, 1, tn), lambda l, j: (l, 0, j))],
        out_specs=pl.BlockSpec((None, rows, tn), lambda l, j: (l, 0, j)),
        out_shape=jax.ShapeDtypeStruct((depth, rows, n), F32),
        compiler_params=_cparams(est, ("arbitrary", "arbitrary")),
        name="modulation",
    )(cond, w_mod, b_mod.reshape(depth, 1, n))


def _dwconv_rows(x, w, bias, pos, row_len, pad_left):
    tm = x.shape[0]
    acc = None
    for k in range(w.shape[0]):
        d = k - pad_left
        if d == 0:
            term = x
        else:
            shifted = pltpu.roll(x, (-d) % tm, axis=0)
            q = pos + d
            term = jnp.where(q >= 0, jnp.where(q < row_len, shifted, 0.0), 0.0)
        term = term * w[k:k + 1]
        acc = term if acc is None else acc + term
    return acc + bias


def _ab_in_kernel(x_ref, mod_ref, g_ref, w_ref, cw_ref, cb_ref, wg_ref, bg_ref, c8_ref,
                  s_ref, xg_ref, *, n_ctx, ctx_row, width):
    tm = x_ref.shape[0]
    m = mod_ref[...]
    h = _norm_mod(x_ref[...], g_ref[...], m[1:2], m[0:1])
    z = jnp.dot(h.astype(BF16), w_ref[...], preferred_element_type=F32)
    u = z[:, :width]
    xr = z[:, width:2 * width]
    xg_ref[...] = z[:, 2 * width:]
    row_len = jnp.where(pl.program_id(0) * tm < n_ctx, ctx_row, GRID_W)
    pos = lax.broadcasted_iota(jnp.int32, (tm, width), 0) & (row_len - 1)
    xb = _dwconv_rows(xr, cw_ref[...], cb_ref[...], pos, row_len, 2)
    gates = jnp.dot(xb.astype(BF16), wg_ref[...], preferred_element_type=F32) + bg_ref[...]
    s_ref[:, :width] = u
    c8 = c8_ref[...]
    for d in range(2):
        r = jax.nn.sigmoid(gates[:, (2 * d) * width:(2 * d + 1) * width])
        i = jax.nn.sigmoid(gates[:, (2 * d + 1) * width:(2 * d + 2) * width])
        a = jnp.exp(-(c8[d:d + 1] * r))
        b = jnp.sqrt(1.0 - a * a) * (i * xb)
        s_ref[:, (1 + 2 * d) * width:(2 + 2 * d) * width] = a
        s_ref[:, (2 + 2 * d) * width:(3 + 2 * d) * width] = b


def _ab_in(x, mod, g, w_in, conv_w, conv_b, w_gates, b_gates, c8, *, n_ctx, ctx_row, dec_seq, tm):
    t, d = x.shape
    width = w_in.shape[1] // 3
    kern = functools.partial(_ab_in_kernel, n_ctx=n_ctx, ctx_row=ctx_row, width=width)
    const = lambda i: (0, 0)
    est = (2 * tm * d * 4 + d * 3 * width * 2 * 2 + width * 4 * width * 2 * 2
           + 2 * tm * 6 * width * 4 + tm * 12 * width * 4)
    return pl.pallas_call(
        kern,
        grid=(t // tm,),
        in_specs=[pl.BlockSpec((tm, d), lambda i: (i, 0)),
                  pl.BlockSpec((None, 6, d), lambda i: (_cond_index(i * tm, n_ctx, dec_seq), 0, 0)),
                  pl.BlockSpec((1, d), const),
                  pl.BlockSpec(w_in.shape, const),
                  pl.BlockSpec(conv_w.shape, const),
                  pl.BlockSpec((1, width), const),
                  pl.BlockSpec(w_gates.shape, const),
                  pl.BlockSpec((1, 4 * width), const),
                  pl.BlockSpec((2, width), const)],
        out_specs=[pl.BlockSpec((tm, 5 * width), lambda i: (i, 0)),
                   pl.BlockSpec((tm, width), lambda i: (i, 0))],
        out_shape=[jax.ShapeDtypeStruct((t, 5 * width), F32),
                   jax.ShapeDtypeStruct((t, width), F32)],
        compiler_params=_cparams(est, ("parallel",)),
        name="ab_in",
    )(x, mod, g, w_in, conv_w, conv_b, w_gates, b_gates, c8)


def _scan_kernel(u_ref, a_ref, b_ref, bblk_ref, cblk_ref, ar_ref, ai_ref, i5_ref, il_ref, *rest,
                 reverse, store):
    if store:
        y_ref, hl_ref, f5_ref, fl_ref, fp_ref, buf, st5, stl, stp = rest
    else:
        f5_ref, fl_ref, fp_ref, buf, st5, stl, stp = rest
    c = pl.program_id(1)
    rows = u_ref.shape[0]
    nsteps = rows // SCAN_SEQS
    sw = st5.shape[1] // (2 * S5_COLBLK)
    cw = u_ref.shape[1] // S5_COLBLK

    @pl.when(c == 0)
    def _():
        st5[...] = i5_ref[...]
        stl[...] = il_ref[...]
        stp[...] = jnp.ones_like(stp)

    def row0(k):
        t = (nsteps - 1 - k) if reverse else k
        return pl.multiple_of(t * SCAN_SEQS, SCAN_SEQS)

    u = u_ref[...].astype(BF16)
    for j in range(S5_COLBLK):
        buf[...] = jnp.dot(u[:, j * cw:(j + 1) * cw], bblk_ref[j], preferred_element_type=F32)
        arj = jnp.broadcast_to(ar_ref[j], (SCAN_SEQS, sw))
        aij = jnp.broadcast_to(ai_ref[j], (SCAN_SEQS, sw))

        def step(k, carry, arj=arj, aij=aij):
            hr, hi = carry
            r0 = row0(k)
            xr = buf[pl.ds(r0, SCAN_SEQS), 0:sw]
            xi = buf[pl.ds(r0, SCAN_SEQS), sw:2 * sw]
            nr = arj * hr - aij * hi + xr
            ni = arj * hi + aij * hr + xi
            buf[pl.ds(r0, SCAN_SEQS), 0:sw] = nr
            buf[pl.ds(r0, SCAN_SEQS), sw:2 * sw] = ni
            return nr, ni

        base = 2 * sw * j
        hr, hi = lax.fori_loop(0, nsteps, step, (st5[:, base:base + sw], st5[:, base + sw:base + 2 * sw]))
        st5[:, base:base + sw] = hr
        st5[:, base + sw:base + 2 * sw] = hi
        if store:
            y_ref[:, j * cw:(j + 1) * cw] = jnp.dot(buf[...].astype(BF16), cblk_ref[j],
                                                    preferred_element_type=F32)

    def lstep(k, carry):
        h, p = carry
        r0 = row0(k)
        a = a_ref[pl.ds(r0, SCAN_SEQS), :]
        h = a * h + b_ref[pl.ds(r0, SCAN_SEQS), :]
        if store:
            hl_ref[pl.ds(r0, SCAN_SEQS), :] = h
        return h, p * a

    h, p = lax.fori_loop(0, nsteps, lstep, (stl[...], stp[...]))
    stl[...] = h
    stp[...] = p

    @pl.when(c == pl.num_programs(1) - 1)
    def _():
        f5_ref[...] = st5[...]
        fl_ref[...] = stl[...]
        fp_ref[...] = stp[...]


def _scan(s_arr, bblk, cblk, ar, ai, init5, initl, *, direction, store, groups, chunk_rows):
    rows_total, cols = s_arr.shape
    width = cols // 5
    rows_per_group = rows_total // groups
    nc = rows_per_group // chunk_rows
    reverse = direction == 1
    nstate = init5.shape[-1]

    def chunk(g, c):
        return g * nc + ((nc - 1 - c) if reverse else c)

    kern = functools.partial(_scan_kernel, reverse=reverse, store=store)
    const3 = lambda g, c: (0, 0, 0)
    in_specs = [pl.BlockSpec((chunk_rows, width), lambda g, c: (chunk(g, c), 0)),
                pl.BlockSpec((chunk_rows, width), lambda g, c: (chunk(g, c), 1 + 2 * direction)),
                pl.BlockSpec((chunk_rows, width), lambda g, c: (chunk(g, c), 2 + 2 * direction)),
                pl.BlockSpec(bblk.shape, const3),
                pl.BlockSpec(cblk.shape, const3),
                pl.BlockSpec(ar.shape, const3),
                pl.BlockSpec(ai.shape, const3),
                pl.BlockSpec((None, SCAN_SEQS, nstate), lambda g, c: (g, 0, 0)),
                pl.BlockSpec((None, SCAN_SEQS, width), lambda g, c: (g, 0, 0))]
    fin_specs = [pl.BlockSpec((None, SCAN_SEQS, nstate), lambda g, c: (g, 0, 0)),
                 pl.BlockSpec((None, SCAN_SEQS, width), lambda g, c: (g, 0, 0)),
                 pl.BlockSpec((None, SCAN_SEQS, width), lambda g, c: (g, 0, 0))]
    fin_shapes = [jax.ShapeDtypeStruct((groups, SCAN_SEQS, nstate), F32),
                  jax.ShapeDtypeStruct((groups, SCAN_SEQS, width), F32),
                  jax.ShapeDtypeStruct((groups, SCAN_SEQS, width), F32)]
    out_specs, out_shapes = fin_specs, fin_shapes
    if store:
        out_specs = [pl.BlockSpec((chunk_rows, width), lambda g, c: (chunk(g, c), 0))] * 2 + fin_specs
        out_shapes = [jax.ShapeDtypeStruct((rows_total, width), F32)] * 2 + fin_shapes
    bufw = 2 * nstate // (2 * S5_COLBLK)
    est = (10 * chunk_rows * width * 4 + chunk_rows * bufw * 4 * 2
           + 4 * bblk.size * 2 + 8 * SCAN_SEQS * nstate * 4)
    return pl.pallas_call(
        kern,
        grid=(groups, nc),
        in_specs=in_specs,
        out_specs=out_specs,
        out_shape=out_shapes,
        scratch_shapes=[pltpu.VMEM((chunk_rows, bufw), F32),
                        pltpu.VMEM((SCAN_SEQS, nstate), F32),
                        pltpu.VMEM((SCAN_SEQS, width), F32),
                        pltpu.VMEM((SCAN_SEQS, width), F32)],
        compiler_params=_cparams(est, ("arbitrary", "arbitrary")),
        name="scan_store" if store else "scan_ends",
    )(s_arr, s_arr, s_arr, bblk, cblk, ar, ai, init5, initl)


def _chain_kernel(e5_ref, el_ref, ep_ref, h5_ref, hl_ref, arl_ref, ail_ref, i5_ref, il_ref, *, reverse):
    nseg = SAMPLE_SEGS
    sw = arl_ref.shape[-1]
    shift = (SCAN_SEQS - 1) if reverse else 1
    order = range(nseg - 2, -1, -1) if reverse else range(1, nseg)
    seg5 = lax.broadcasted_iota(jnp.int32, (SCAN_SEQS, sw), 0) % nseg
    for j in range(S5_COLBLK):
        base = 2 * sw * j
        re, im = slice(base, base + sw), slice(base + sw, base + 2 * sw)
        ar, ai = arl_ref[j], ail_ref[j]
        er = pltpu.roll(e5_ref[:, re], shift, axis=0)
        ei = pltpu.roll(e5_ref[:, im], shift, axis=0)
        hr, hi = h5_ref[:, re], h5_ref[:, im]
        for k in order:
            pr = pltpu.roll(hr, shift, axis=0)
            pi = pltpu.roll(hi, shift, axis=0)
            hr = jnp.where(seg5 == k, ar * pr - ai * pi + er, hr)
            hi = jnp.where(seg5 == k, ar * pi + ai * pr + ei, hi)
        i5_ref[:, re] = hr
        i5_ref[:, im] = hi
    segl = lax.broadcasted_iota(jnp.int32, el_ref.shape, 0) % nseg
    el = pltpu.roll(el_ref[...], shift, axis=0)
    ep = pltpu.roll(ep_ref[...], shift, axis=0)
    h = hl_ref[...]
    for k in order:
        h = jnp.where(segl == k, ep * pltpu.roll(h, shift, axis=0) + el, h)
    il_ref[...] = h


def _chain(end5, endl, endp, h5_rows, hl_rows, arl, ail, *, direction):
    kern = functools.partial(_chain_kernel, reverse=direction == 1)
    return pl.pallas_call(
        kern,
        out_shape=[jax.ShapeDtypeStruct(end5.shape, F32), jax.ShapeDtypeStruct(endl.shape, F32)],
        name="scan_chain",
    )(end5, endl, endp, h5_rows, hl_rows, arl, ail)


def _ab_out_kernel(x_ref, mod_ref, s_ref, xg_ref, yf_ref, yb_ref, hf_ref, hb_ref, d_ref,
                   wglu_ref, bglu_ref, wout_ref, o_ref):
    width = xg_ref.shape[1]
    m = mod_ref[...]
    y = s_ref[...] * d_ref[...] + yf_ref[...] + yb_ref[...]
    zs = _gelu(y)
    gate = jnp.dot(zs.astype(BF16), wglu_ref[...], preferred_element_type=F32) + bglu_ref[...]
    s5_out = zs * jax.nn.sigmoid(gate)
    lru_out = (hf_ref[...] + hb_ref[...]) * _gelu(xg_ref[...])
    out = (jnp.dot(s5_out.astype(BF16), wout_ref[:width, :], preferred_element_type=F32)
           + jnp.dot(lru_out.astype(BF16), wout_ref[width:, :], preferred_element_type=F32))
    o_ref[...] = x_ref[...] + m[2:3] * out


def _ab_out(x, mod, s_arr, xg, yf, yb, hf, hb, s5_d, w_glu, b_glu, w_out, *, n_ctx, dec_seq, tm):
    t, d = x.shape
    width = xg.shape[1]
    const = lambda i: (0, 0)
    row = lambda i: (i, 0)
    est = 4 * tm * d * 4 + 12 * tm * width * 4 + 4 * (width * width + 2 * width * d) + 8 * tm * width * 4
    return pl.pallas_call(
        _ab_out_kernel,
        grid=(t // tm,),
        in_specs=[pl.BlockSpec((tm, d), row),
                  pl.BlockSpec((None, 6, d), lambda i: (_cond_index(i * tm, n_ctx, dec_seq), 0, 0)),
                  pl.BlockSpec((tm, width), row),
                  pl.BlockSpec((tm, width), row),
                  pl.BlockSpec((tm, width), row),
                  pl.BlockSpec((tm, width), row),
                  pl.BlockSpec((tm, width), row),
                  pl.BlockSpec((tm, width), row),
                  pl.BlockSpec((1, width), const),
                  pl.BlockSpec(w_glu.shape, const),
                  pl.BlockSpec((1, width), const),
                  pl.BlockSpec(w_out.shape, const)],
        out_specs=pl.BlockSpec((tm, d), row),
        out_shape=jax.ShapeDtypeStruct((t, d), F32),
        compiler_params=_cparams(est, ("parallel",)),
        name="ab_out",
    )(x, mod, s_arr, xg, yf, yb, hf, hb, s5_d, w_glu, b_glu, w_out)


def _c_in_kernel(x_ref, mod_ref, g_ref, w_ref, b_ref, cw_ref, cb_ref, x0_ref, vg_ref, *, n_ctx, ctx_row):
    tm = x_ref.shape[0]
    width = x0_ref.shape[1]
    m = mod_ref[...]
    h = _norm_mod(x_ref[...], g_ref[...], m[1:2], m[0:1])
    z = jnp.dot(h.astype(BF16), w_ref[...], preferred_element_type=F32) + b_ref[...]
    row_len = jnp.where(pl.program_id(0) * tm < n_ctx, ctx_row, GRID_W)
    pos = lax.broadcasted_iota(jnp.int32, (tm, width), 0) & (row_len - 1)
    cw = cw_ref[...]
    cb = cb_ref[...]
    parts = [_dwconv_rows(z[:, k * width:(k + 1) * width], cw[:, k * width:(k + 1) * width],
                          cb[:, k * width:(k + 1) * width], pos, row_len, 1) for k in range(3)]
    x0_ref[...] = parts[0]
    vg_ref[...] = parts[2] * parts[1]


def _c_in(x, mod, g, w_in, b_in, conv_w, conv_b, *, n_ctx, ctx_row, dec_seq, tm):
    t, d = x.shape
    width = w_in.shape[1] // 3
    kern = functools.partial(_c_in_kernel, n_ctx=n_ctx, ctx_row=ctx_row)
    const = lambda i: (0, 0)
    est = 2 * tm * d * 4 + 4 * d * 3 * width + 4 * tm * width * 4 + 6 * tm * 3 * width * 4
    return pl.pallas_call(
        kern,
        grid=(t // tm,),
        in_specs=[pl.BlockSpec((tm, d), lambda i: (i, 0)),
                  pl.BlockSpec((None, 6, d), lambda i: (_cond_index(i * tm, n_ctx, dec_seq), 0, 0)),
                  pl.BlockSpec((1, d), const),
                  pl.BlockSpec(w_in.shape, const),
                  pl.BlockSpec((1, 3 * width), const),
                  pl.BlockSpec(conv_w.shape, const),
                  pl.BlockSpec((1, 3 * width), const)],
        out_specs=[pl.BlockSpec((tm, width), lambda i: (i, 0))] * 2,
        out_shape=[jax.ShapeDtypeStruct((t, width), F32)] * 2,
        compiler_params=_cparams(est, ("parallel",)),
        name="c_in",
    )(x, mod, g, w_in, b_in, conv_w, conv_b)


def _c_out_kernel(x_ref, mod_ref, x0_ref, vg_ref, yc_ref, hb_ref, w_ref, b_ref, o_ref):
    m = mod_ref[...]
    vg = vg_ref[...]
    y = (yc_ref[...] + hb_ref[...] * vg) * x0_ref[...]
    out = jnp.dot(y.astype(BF16), w_ref[...], preferred_element_type=F32) + b_ref[...]
    o_ref[...] = x_ref[...] + m[2:3] * out


def _c_out(x, mod, x0, vg, yconv, hy_bias, w_out, b_out, *, n_ctx, dec_seq, tm):
    t, d = x.shape
    width = x0.shape[1]
    const = lambda i: (0, 0)
    row = lambda i: (i, 0)
    est = 4 * tm * d * 4 + 6 * tm * width * 4 + 4 * width * d + 4 * tm * width * 4
    return pl.pallas_call(
        _c_out_kernel,
        grid=(t // tm,),
        in_specs=[pl.BlockSpec((tm, d), row),
                  pl.BlockSpec((None, 6, d), lambda i: (_cond_index(i * tm, n_ctx, dec_seq), 0, 0)),
                  pl.BlockSpec((tm, width), row),
                  pl.BlockSpec((tm, width), row),
                  pl.BlockSpec((tm, width), row),
                  pl.BlockSpec((1, width), const),
                  pl.BlockSpec(w_out.shape, const),
                  pl.BlockSpec((1, d), const)],
        out_specs=pl.BlockSpec((tm, d), row),
        out_shape=jax.ShapeDtypeStruct((t, d), F32),
        compiler_params=_cparams(est, ("parallel",)),
        name="c_out",
    )(x, mod, x0, vg, yconv, hy_bias, w_out, b_out)


def _hy_filter_kernel(w1_ref, b1_ref, f1_ref, w2_ref, b2_ref, f2_ref, w3_ref, dec_ref, o_ref, z_scr, *, length):
    ct = o_ref.shape[1]
    nb = HY_BANDS
    ecols = w1_ref.shape[0]
    row = lax.broadcasted_iota(jnp.int32, (length, 1), 0)
    positions = (row, length - row)

    @pl.when(pl.program_id(0) == 0)
    def _():
        for side in range(2):
            posf = positions[side].astype(F32)
            col = lax.broadcasted_iota(jnp.int32, (length, ecols), 1)
            band_idx = jnp.where(col <= nb, col - 1, col - 1 - nb).astype(F32)
            bands = 1e-4 + band_idx * ((nb - 1 - 1e-4) / (nb - 1))
            ang = (2.0 * math.pi / length) * posf * bands
            emb = jnp.where(col == 0, posf / length,
                            jnp.where(col <= nb, jnp.cos(ang),
                                      jnp.where(col <= 2 * nb, -jnp.sin(ang), 0.0)))
            z = jnp.sin(f1_ref[...] * (jnp.dot(emb, w1_ref[...], precision=HIGHEST,
                                               preferred_element_type=F32) + b1_ref[...]))
            z_scr[side] = jnp.sin(f2_ref[...] * (jnp.dot(z, w2_ref[...], precision=HIGHEST,
                                                         preferred_element_type=F32) + b2_ref[...]))

    def half(side):
        wide_row = lax.broadcasted_iota(jnp.int32, (length, ct), 0)
        t = (wide_row if side == 0 else length - wide_row).astype(F32) / length
        filt = jnp.dot(z_scr[side], w3_ref[side], precision=HIGHEST, preferred_element_type=F32)
        return filt * (jnp.exp(-t * jnp.abs(dec_ref[side:side + 1, :])) + HY_SHIFT)

    fwd = half(0)
    bwd = jnp.where(lax.broadcasted_iota(jnp.int32, (length, ct), 0) == 0, 0.0, half(1))
    ss = jnp.sum(fwd * fwd, axis=0, keepdims=True) + jnp.sum(bwd * bwd, axis=0, keepdims=True)
    scale = lax.rsqrt(ss + EPS)
    o_ref[:length, :] = fwd * scale
    o_ref[length:, :] = bwd * scale


def _hy_filter(length, w1, b1, f1, w2, b2, f2, w3, decay, *, ct):
    emb, ff = w1.shape
    width = decay.shape[1]
    ecols = 128
    w1p = jnp.zeros((ecols, ff), F32).at[:emb].set(w1)
    w3s = w3.reshape(ff, 2, width).transpose(1, 0, 2)
    kern = functools.partial(_hy_filter_kernel, length=length)
    const = lambda j: (0, 0)
    est = 10 * length * ct * 4 + 4 * length * ecols * 4
    return pl.pallas_call(
        kern,
        grid=(width // ct,),
        in_specs=[pl.BlockSpec((ecols, ff), const),
                  pl.BlockSpec((1, ff), const),
                  pl.BlockSpec((1, ff), const),
                  pl.BlockSpec((ff, ff), const),
                  pl.BlockSpec((1, ff), const),
                  pl.BlockSpec((1, ff), const),
                  pl.BlockSpec((2, ff, ct), lambda j: (0, 0, j)),
                  pl.BlockSpec((2, ct), lambda j: (0, j))],
        out_specs=pl.BlockSpec((2 * length, ct), lambda j: (0, j)),
        out_shape=jax.ShapeDtypeStruct((2 * length, width), F32),
        scratch_shapes=[pltpu.VMEM((2, length, ff), F32)],
        compiler_params=_cparams(est, ("arbitrary",)),
        name="hy_filter",
    )(w1p, b1.reshape(1, ff), f1.reshape(1, ff), w2, b2.reshape(1, ff), f2.reshape(1, ff), w3s, decay)


def _fftconv_kernel(ct_ref, st_ref, x_ref, k_ref, c_ref, s_ref, twr_ref, twi_ref, o_ref, *scratch, n1):
    kfr, kfi = scratch if scratch else (None, None)
    h1 = n1 // 2
    n2 = c_ref.shape[0]
    n = n1 * n2
    cm = c_ref[...]
    sm = s_ref[...]

    def dft(pairs, inverse):
        w = pairs[0][0].shape[1]
        parts = jnp.concatenate([p.astype(BF16) for pair in pairs for p in pair], axis=1)
        cp = jnp.dot(cm, parts, preferred_element_type=F32)
        sp = jnp.dot(sm, parts, preferred_element_type=F32)
        out = []
        for i in range(len(pairs)):
            ca, cb = cp[:, 2 * i * w:(2 * i + 1) * w], cp[:, (2 * i + 1) * w:(2 * i + 2) * w]
            sa, sb = sp[:, 2 * i * w:(2 * i + 1) * w], sp[:, (2 * i + 1) * w:(2 * i + 2) * w]
            out.append((ca - sb, cb + sa) if inverse else (ca + sb, cb - sa))
        return out

    def slab(t1):
        return slice(t1 * n2, (t1 + 1) * n2)

    twr1, twi1 = twr_ref[...], twi_ref[...]
    tw0 = (jnp.ones_like(twr1), jnp.zeros_like(twi1))

    def next_twiddle(tw):
        return tw[0] * twr1 - tw[1] * twi1, tw[0] * twi1 + tw[1] * twr1

    def filter_stage1(k1, tw):
        ar = ai = None
        for t1 in range(n1):
            kr = k_ref[slab(t1), :]
            tr, ti = ct_ref[k1, t1] * kr, -(st_ref[k1, t1] * kr)
            ar = tr if ar is None else ar + tr
            ai = ti if ai is None else ai + ti
        ar, ai = ar * (1.0 / n), ai * (1.0 / n)
        return ar * tw[0] + ai * tw[1], ai * tw[0] - ar * tw[1]

    if kfr is not None:
        @pl.when(pl.program_id(1) == 0)
        def _():
            def fbody(k1, tw):
                (kfr[k1], kfi[k1]), = dft([filter_stage1(k1, tw)], False)
                return next_twiddle(tw)
            lax.fori_loop(0, n1, fbody, tw0)

    o_ref[...] = jnp.zeros_like(o_ref)

    def forward(k1, tw):
        twr, twi = tw
        ar = ai = None
        for t1 in range(h1):
            cc, ss = ct_ref[k1, t1], st_ref[k1, t1]
            xr, xi = x_ref[0, slab(t1), :], x_ref[1, slab(t1), :]
            tr, ti = cc * xr + ss * xi, cc * xi - ss * xr
            ar = tr if ar is None else ar + tr
            ai = ti if ai is None else ai + ti
        data = (ar * twr + ai * twi, ai * twr - ar * twi)
        if kfr is not None:
            (fr, fi), = dft([data], False)
            gr, gi = kfr[k1], kfi[k1]
        else:
            (fr, fi), (gr, gi) = dft([data, filter_stage1(k1, tw)], False)
        return fr * gr - fi * gi, fr * gi + fi * gr

    def inverse(k1, tw, y):
        twr, twi = tw
        (br, bi), = dft([y], True)
        br, bi = br * twr - bi * twi, bi * twr + br * twi
        for t1 in range(h1):
            cc, ss = ct_ref[k1, t1], st_ref[k1, t1]
            o_ref[0, slab(t1), :] += cc * br - ss * bi
            o_ref[1, slab(t1), :] += cc * bi + ss * br

    def body(k1, carry):
        tw_prev, y_prev = carry
        tw = next_twiddle(tw_prev)
        y = forward(k1, tw)
        inverse(k1 - 1, tw_prev, y_prev)
        return tw, y

    tw_last, y_last = lax.fori_loop(1, n1, body, (tw0, forward(0, tw0)))
    inverse(n1 - 1, tw_last, y_last)


def _fftconv(v, circ, *, n1, ct):
    bsz, length, width = v.shape
    n = 2 * length
    n2 = n // n1
    idx = np.arange(n2)
    ang = 2.0 * np.pi * ((idx[:, None] * idx[None, :]) % n2) / n2
    cmat = jnp.asarray(np.cos(ang), F32).astype(BF16)
    smat = jnp.asarray(np.sin(ang), F32).astype(BF16)
    tang = 2.0 * np.pi * idx / n
    twr = jnp.asarray(np.broadcast_to(np.cos(tang)[:, None], (n2, ct)), F32)
    twi = jnp.asarray(np.broadcast_to(np.sin(tang)[:, None], (n2, ct)), F32)
    i1 = np.arange(n1)
    ang1 = 2.0 * np.pi * ((i1[:, None] * i1[None, :]) % n1) / n1
    ctab = jnp.asarray(np.round(np.cos(ang1), 12), F32)
    stab = jnp.asarray(np.round(np.sin(ang1), 12), F32)
    kern = functools.partial(_fftconv_kernel, n1=n1)
    const = lambda j, p: (0, 0)
    once = pl.Buffered(1)
    npairs = bsz // 2
    scratch = [pltpu.VMEM((n1, n2, ct), F32)] * 2 if npairs > 1 else []
    est = (4 * 2 * length * ct * 4 + n * ct * 4 + 2 * n2 * n2 * 2 + 2 * n2 * ct * 4
           + len(scratch) * n * ct * 4 + (8 + 2 * n1) * n2 * ct * 4)
    return pl.pallas_call(
        kern,
        grid=(width // ct, npairs),
        in_specs=[pl.BlockSpec(memory_space=pltpu.SMEM),
                  pl.BlockSpec(memory_space=pltpu.SMEM),
                  pl.BlockSpec((2, length, ct), lambda j, p: (p, 0, j)),
                  pl.BlockSpec((n, ct), lambda j, p: (0, j), pipeline_mode=once),
                  pl.BlockSpec((n2, n2), const, pipeline_mode=once),
                  pl.BlockSpec((n2, n2), const, pipeline_mode=once),
                  pl.BlockSpec((n2, ct), const, pipeline_mode=once),
                  pl.BlockSpec((n2, ct), const, pipeline_mode=once)],
        out_specs=pl.BlockSpec((2, length, ct), lambda j, p: (p, 0, j)),
        out_shape=jax.ShapeDtypeStruct((bsz, length, width), F32),
        scratch_shapes=scratch,
        compiler_params=_cparams(est, ("arbitrary", "arbitrary")),
        name="fftconv",
    )(ctab, stab, v, circ, cmat, smat, twr, twi)


def _sorting_network(n):
    pairs = []
    p = 1
    while p < n:
        k = p
        while k >= 1:
            for j in range(k % p, n - k, 2 * k):
                for i in range(min(k, n - j - k)):
                    if (i + j) // (2 * p) == (i + j + k) // (2 * p):
                        pairs.append((i + j, i + j + k))
            k //= 2
        p *= 2
    return pairs


_SORT16 = _sorting_network(PEER_TOPK)


def _peer_route_kernel(x_ref, mod_ref, g_ref, wq_ref, keys_ref, hb_ref, th_ref, e0_ref, s1_ref, e1_ref,
                       q_scr, sc_scr, top_scr):
    tq = x_ref.shape[0]
    nk = PEER_NKEYS
    k = PEER_TOPK
    nhp = keys_ref.shape[0]
    dkh = keys_ref.shape[2]
    neg = -jnp.inf
    m = mod_ref[...]
    h = _norm_mod(x_ref[...], g_ref[...], m[4:5], m[3:4]).astype(BF16)
    hb_ref[...] = h
    q = jnp.dot(h, wq_ref[...], preferred_element_type=F32)
    for hp in range(nhp):
        q_scr[hp] = q[:, hp * dkh:(hp + 1) * dkh].astype(BF16)

    lw = V7X_LANES
    sub = V7X_SUBLANES
    assert k == 2 * sub
    ncand = k + 7 * sub + (k - sub)
    npad = sum(sub - k // (i + 1) for i in range(2, sub))
    iota_c = lax.broadcasted_iota(jnp.int32, (ncand, lw), 0)
    iota_8 = lax.broadcasted_iota(jnp.int32, (sub, lw), 0)
    iota_r = lax.broadcasted_iota(jnp.int32, (k, lw), 0)

    def topk(vals, iota, exact):
        top = jnp.zeros((k, lw), F32)
        for r in range(k):
            mx = jnp.max(vals, axis=0, keepdims=True)
            if exact:
                first = jnp.min(jnp.where(vals == mx, iota, vals.shape[0]), axis=0, keepdims=True)
                vals = jnp.where(iota == first, neg, vals)
            else:
                vals = jnp.where(vals == mx, neg, vals)
            top = jnp.where(iota_r == r, mx, top)
        return top, vals

    def tied(rest, expected):
        gone = jnp.sum(jnp.where(rest == neg, 1.0, 0.0), axis=0, keepdims=True)
        return jnp.where(gone != expected, 1.0, 0.0)

    def exchange(v, i, j):
        v[i], v[j] = jnp.maximum(v[i], v[j]), jnp.minimum(v[i], v[j])

    def sort_top(tiles):
        v = list(tiles)
        for i, j in _SORT16:
            exchange(v, i, j)
        for shift in (sub // 2, sub // 4, sub // 8):
            other = [pltpu.roll(t, shift, axis=0) for t in v]
            v = [jnp.maximum(v[r], other[k - 1 - r]) for r in range(k)]
            d = k // 2
            while d >= 1:
                for i in range(k):
                    if i & d == 0:
                        exchange(v, i, i + d)
                d //= 2
        top = jnp.zeros((k, lw), F32)
        for r in range(k):
            top = jnp.where(iota_r == r, jnp.concatenate([v[r]] * (k // sub), axis=0), top)
        return top

    def halves(hd):
        assert nk == k * sub
        for p in range(2):
            for c in range(tq // lw):
                lanes = slice(c * lw, (c + 1) * lw)
                top_scr[2 * hd + p, :, lanes] = sort_top(
                    [sc_scr[2 * hd + p, t * sub:(t + 1) * sub, lanes] for t in range(nk // sub)])

    def pairs(hd, exact):
        flag = jnp.zeros((1, lw), F32)
        for c in range(tq // lw):
            lanes = slice(c * lw, (c + 1) * lw)
            a = top_scr[2 * hd, :, lanes]
            b = top_scr[2 * hd + 1, :, lanes]
            pieces = [a[0:1] + b, a[1:2] + b[0:sub]]
            for i in range(2, sub):
                pieces.append(jnp.where(iota_8 < k // (i + 1), a[i:i + 1] + b[0:sub], neg))
            pieces.append(a[sub:k] + b[0:1])
            vmax = a[0:1] + b[0:1]
            top, rest = topk(jnp.concatenate(pieces, axis=0), iota_c, exact)
            z = jnp.sum(jnp.exp(top - vmax), axis=0, keepdims=True)
            last = top[k - 1:k]
            nxt = jnp.max(rest, axis=0, keepdims=True)
            tau = jnp.where(nxt == neg, last, 0.5 * (last + nxt))
            flag = jnp.maximum(flag, tied(rest, float(k + npad)))
            s0 = sc_scr[2 * hd, :, lanes]
            s1 = sc_scr[2 * hd + 1, :, lanes]
            th = jnp.where(s0 >= a[k - 1:k], tau - s0, jnp.inf)
            s1 = jnp.where(s1 >= b[k - 1:k], s1, neg)
            e0 = jnp.exp(s0 - a[0:1]) / z
            per = th_ref.shape[2]
            for r in range(nk // per):
                th_ref[hd, r, :, lanes] = th[r * per:(r + 1) * per]
                e0_ref[hd, r, :, lanes] = e0[r * per:(r + 1) * per]
            s1_ref[hd, :, lanes] = s1
            e1_ref[hd, :, lanes] = jnp.exp(s1 - b[0:1])
        return jnp.max(flag)

    def head_body(hd, carry):
        for p in range(2):
            sc_scr[2 * hd + p] = lax.dot_general(keys_ref[2 * hd + p], q_scr[2 * hd + p], (((1,), (1,)), ((), ())),
                                                 preferred_element_type=F32)
        halves(hd)

        @pl.when(pairs(hd, False) > 0.0)
        def _():
            pairs(hd, True)

        return carry

    lax.fori_loop(0, nhp // 2, head_body, 0)


def _peer_route(x, mod, g, wq, keys, *, n_ctx, dec_seq, tq):
    t, d = x.shape
    nhp, nk, dkh = keys.shape
    nh = nhp // 2
    const = lambda i: (0, 0)
    hshape = jax.ShapeDtypeStruct((nh, nk, t), F32)
    hspec = pl.BlockSpec((nh, nk, tq), lambda i: (0, 0, i))
    per = PEER_GROUP // nk
    gshape = jax.ShapeDtypeStruct((nh, nk // per, per, t), F32)
    gspec = pl.BlockSpec((nh, nk // per, per, tq), lambda i: (0, 0, 0, i))
    pad = V7X_SUBLANES // per
    est = (2 * tq * d * 4 + 4 * d * nhp * dkh + 2 * tq * d * 2 + (4 + 4 * pad) * nh * nk * tq * 4
           + 2 * nhp * nk * tq * 4 + 6 * tq * nhp * dkh * 4)
    return pl.pallas_call(
        _peer_route_kernel,
        grid=(t // tq,),
        in_specs=[pl.BlockSpec((tq, d), lambda i: (i, 0)),
                  pl.BlockSpec((None, 6, d), lambda i: (_cond_index(i * tq, n_ctx, dec_seq), 0, 0)),
                  pl.BlockSpec((1, d), const),
                  pl.BlockSpec(wq.shape, const),
                  pl.BlockSpec(keys.shape, lambda i: (0, 0, 0))],
        out_specs=[pl.BlockSpec((tq, d), lambda i: (i, 0)), gspec, gspec, hspec, hspec],
        out_shape=[jax.ShapeDtypeStruct((t, d), BF16), gshape, gshape, hshape, hshape],
        scratch_shapes=[pltpu.VMEM((nhp, tq, dkh), BF16),
                        pltpu.VMEM((nhp, nk, tq), F32),
                        pltpu.VMEM((nhp, PEER_TOPK, tq), F32)],
        compiler_params=_cparams(est, ("parallel",)),
        name="peer_route",
    )(x, mod, g, wq, keys)


def _peer_expert_kernel(hbt_ref, u_ref, vt_ref, th_ref, e0_ref, s1_ref, e1_ref, x_ref, mod_ref, o_ref,
                        acc, st_a, st_b, hbuf_a, hbuf_b, *, nc):
    g = pl.program_id(0)
    nh = th_ref.shape[0]
    nk = s1_ref.shape[1]
    ngroups, per = th_ref.shape[1], th_ref.shape[2]
    gw = per * nk
    tq = st_a.shape[1]
    kh = PEER_ACC_ROWS // per
    c3 = jnp.maximum(g - 2, 0) % nc

    @pl.when(g == 0)
    def _():
        st_b[...] = jnp.zeros_like(st_b)
        hbuf_a[...] = jnp.zeros_like(hbuf_a)
        hbuf_b[...] = jnp.zeros_like(hbuf_b)

    @pl.when(c3 == 0)
    def _():
        acc[...] = jnp.zeros_like(acc)

    def step(st_w, st_r, hbuf_w, hbuf_r):
        def group(j, carry):
            rows = pl.ds(pl.multiple_of(j * gw, gw), gw)
            acc[...] += jnp.dot(vt_ref[j], hbuf_r[rows, :], preferred_element_type=F32)
            st_w[rows, :] = jnp.dot(u_ref[rows, :], hbt_ref[...], preferred_element_type=F32)
            for q in range(tq // V7X_LANES):
                lanes = slice(q * V7X_LANES, (q + 1) * V7X_LANES)
                for half in range(nk // kh):
                    keys = slice(half * kh, (half + 1) * kh)
                    w = [None] * per
                    for hd in range(nh):
                        s1 = s1_ref[hd, keys, lanes]
                        e1 = e1_ref[hd, keys, lanes]
                        for i in range(per):
                            sel = jnp.where(s1 >= th_ref[hd, j, i:i + 1, lanes], e1, 0.0) * e0_ref[hd, j, i:i + 1, lanes]
                            w[i] = sel if w[i] is None else w[i] + sel
                    for i in range(per):
                        tile = pl.ds(pl.multiple_of(j * gw + i * nk + half * kh, kh), kh)
                        hbuf_w[tile, lanes] = (_gelu(st_r[tile, lanes]) * w[i]).astype(BF16)
            return carry

        lax.fori_loop(0, ngroups, group, 0)

    @pl.when(g % 2 == 0)
    def _():
        step(st_a, st_b, hbuf_b, hbuf_a)

    @pl.when(g % 2 == 1)
    def _():
        step(st_b, st_a, hbuf_a, hbuf_b)

    @pl.when(jnp.logical_and(g >= 2, c3 == nc - 1))
    def _():
        o_ref[...] = x_ref[...] + mod_ref[5:6, :] * acc[...].T


def _peer_expert(hb, u_tabs, layer, vt_tab, th, e0, s1, e1, x, mod, *, n_ctx, dec_seq, tq, ec):
    t, d = x.shape
    ne = u_tabs.shape[1]
    nh, nk, _ = s1.shape
    per = th.shape[2]
    assert per * nk == PEER_GROUP
    ngroups = ec // PEER_GROUP
    nc = ne // ec
    npairs = (t // tq) * nc

    def pair(g, lag):
        p = jnp.clip(g - lag, 0, npairs - 1)
        return p // nc, p % nc

    def cond3(g):
        return _cond_index(pair(g, 2)[0] * tq, n_ctx, dec_seq)

    est = (2 * tq * d * 2 + 4 * ec * d * 2 + 4 * nh * ngroups * V7X_SUBLANES * tq * 4 + 4 * nh * nk * tq * 4
           + 4 * tq * d * 4 + d * tq * 4 + 2 * ec * tq * 2 + 2 * ec * tq * 4 + 3 * PEER_GROUP * tq * 4)
    return pl.pallas_call(
        functools.partial(_peer_expert_kernel, nc=nc),
        grid=(npairs + 2,),
        in_specs=[pl.BlockSpec((d, tq), lambda g: (0, pair(g, 0)[0])),
                  pl.BlockSpec((None, ec, d), lambda g: (layer, pair(g, 0)[1], 0)),
                  pl.BlockSpec((ngroups, d, PEER_GROUP), lambda g: (pair(g, 2)[1], 0, 0)),
                  pl.BlockSpec((nh, ngroups, per, tq), lambda g: (0, pair(g, 1)[1], 0, pair(g, 1)[0])),
                  pl.BlockSpec((nh, ngroups, per, tq), lambda g: (0, pair(g, 1)[1], 0, pair(g, 1)[0])),
                  pl.BlockSpec((nh, nk, tq), lambda g: (0, 0, pair(g, 1)[0])),
                  pl.BlockSpec((nh, nk, tq), lambda g: (0, 0, pair(g, 1)[0])),
                  pl.BlockSpec((tq, d), lambda g: (pair(g, 2)[0], 0)),
                  pl.BlockSpec((None, 6, d), lambda g: (cond3(g), 0, 0))],
        out_specs=pl.BlockSpec((tq, d), lambda g: (pair(g, 2)[0], 0)),
        out_shape=jax.ShapeDtypeStruct((t, d), F32),
        scratch_shapes=[pltpu.VMEM((d, tq), F32), pltpu.VMEM((ec, tq), F32), pltpu.VMEM((ec, tq), F32),
                        pltpu.VMEM((ec, tq), BF16), pltpu.VMEM((ec, tq), BF16)],
        compiler_params=_cparams(est, ("arbitrary",)),
        name="peer_expert",
    )(hb.T, u_tabs, vt_tab, th, e0, s1, e1, x, mod)


def _final_kernel(x_ref, g_ref, o_ref):
    x = x_ref[...]
    ms = jnp.mean(x * x, axis=-1, keepdims=True)
    o_ref[...] = (x * lax.rsqrt(ms + EPS)) * g_ref[...]


def _final_norm(x, g, *, tm):
    t, d = x.shape
    return pl.pallas_call(
        _final_kernel,
        grid=(t // tm,),
        in_specs=[pl.BlockSpec((tm, d), lambda i: (i, 0)), pl.BlockSpec((1, d), lambda i: (0, 0))],
        out_specs=pl.BlockSpec((tm, d), lambda i: (i, 0)),
        out_shape=jax.ShapeDtypeStruct((t, d), F32),
        compiler_params=_cparams(6 * tm * d * 4, ("parallel",)),
        name="final_norm",
    )(x, g)


def _block_diag(w):
    h, i, j = w.shape
    return jnp.einsum('hij,hg->higj', w, jnp.eye(h, dtype=w.dtype)).reshape(h * i, h * j)


def _s5_discretize(a_re, a_im, log_dt, b_re, b_im):
    dt = jnp.exp(log_dt)[:, None]
    mag = jnp.exp(a_re * dt)
    abar_r = mag * jnp.cos(a_im * dt)
    abar_i = mag * jnp.sin(a_im * dt)
    den = a_re * a_re + a_im * a_im
    num_r = abar_r - 1.0
    coef_r = ((num_r * a_re + abar_i * a_im) / den)[..., None]
    coef_i = ((abar_i * a_re - num_r * a_im) / den)[..., None]
    return abar_r, abar_i, coef_r * b_re - coef_i * b_im, coef_r * b_im + coef_i * b_re


def _s5_blocks(abar_r, abar_i, bbar_r, bbar_i, c_re, c_im):
    g, p, ch = bbar_r.shape
    gb = g // S5_COLBLK

    def vec(v):
        return v.reshape(S5_COLBLK, 1, gb * p)

    def bmat(b):
        bt = b.transpose(0, 2, 1).reshape(S5_COLBLK, gb, ch, p)
        return jnp.stack([_block_diag(bt[j]) for j in range(S5_COLBLK)])

    def cmat(cc):
        ct = cc.transpose(0, 2, 1).reshape(S5_COLBLK, gb, p, ch)
        return jnp.stack([_block_diag(ct[j]) for j in range(S5_COLBLK)])

    bblk = jnp.concatenate([bmat(bbar_r), bmat(bbar_i)], axis=2).astype(BF16)
    cblk = jnp.concatenate([cmat(c_re), cmat(-c_im)], axis=1).astype(BF16)
    return bblk, cblk, vec(abar_r), vec(abar_i)


def _state_to_blocked(re, im):
    n, g, p = re.shape
    gb = g // S5_COLBLK
    both = jnp.stack([re.reshape(n, S5_COLBLK, gb * p), im.reshape(n, S5_COLBLK, gb * p)], axis=2)
    return both.reshape(n, -1)


def _state_from_blocked(st, g, p):
    n = st.shape[0]
    both = st.reshape(n, S5_COLBLK, 2, (g // S5_COLBLK) * p)
    return both[:, :, 0].reshape(n, g, p), both[:, :, 1].reshape(n, g, p)


def _complex_power(ar, ai, n):
    for _ in range(int(math.log2(n))):
        ar, ai = ar * ar - ai * ai, 2.0 * ar * ai
    return ar, ai


def _mixer_ab(xs, mod, p, le, h0, *, bc, lc, bs, ls, tm):
    n_ctx = bc * lc
    width = p['s5_d'].shape[1]
    g_cnt, p_cnt = p['s5_a_re'].shape[2:]
    w_gates = jnp.concatenate([_block_diag(p['lru_w_a'][le, 0]), _block_diag(p['lru_w_x'][le, 0]),
                               _block_diag(p['lru_w_a'][le, 1]), _block_diag(p['lru_w_x'][le, 1])],
                              axis=1).astype(BF16)
    b_gates = jnp.concatenate([p['lru_b_a'][le, 0], p['lru_b_x'][le, 0],
                               p['lru_b_a'][le, 1], p['lru_b_x'][le, 1]]).reshape(1, 4 * width)
    c8 = LRU_C * jax.nn.softplus(-p['lru_lambda'][le])
    pass_ctx = (n_ctx, 0)
    w_in = p['w_in_ab'][le].astype(BF16)
    s_arrs, xgs = zip(*[_ab_in(x, mod, p['norm1_g'][2 * le].reshape(1, -1), w_in,
                               p['lru_conv_w'][le], p['lru_conv_b'][le].reshape(1, width), w_gates, b_gates, c8,
                               n_ctx=nc_, ctx_row=lc, dec_seq=ls, tm=tm) for x, nc_ in zip(xs, pass_ctx)])

    cols = s_arrs[0].shape[1]
    gc = bc // SCAN_SEQS
    lseg = ls // SAMPLE_SEGS
    s_ctx = (s_arrs[0].reshape(gc, SCAN_SEQS, lc, cols).transpose(0, 2, 1, 3)
             .reshape(n_ctx, cols))
    s_smp = (s_arrs[1].reshape(bs * SAMPLE_SEGS, lseg, cols).transpose(1, 0, 2)
             .reshape(bs * ls, cols))
    chunk_rows = min(1024, lc * SCAN_SEQS, lseg * SCAN_SEQS)

    outs = {}
    finals = {}
    for d in range(2):
        abar_r, abar_i, bbar_r, bbar_i = _s5_discretize(
            p['s5_a_re'][le, d], p['s5_a_im'][le, d], p['s5_log_dt'][le, d],
            p['s5_b_re'][le, d], p['s5_b_im'][le, d])
        bblk, cblk, ar, ai = _s5_blocks(abar_r, abar_i, bbar_r, bbar_i,
                                        p['s5_c_re'][le, d], p['s5_c_im'][le, d])
        nstate = 2 * g_cnt * p_cnt
        z5 = jnp.zeros((gc, SCAN_SEQS, nstate), F32)
        zl = jnp.zeros((gc, SCAN_SEQS, width), F32)
        y, hl, f5, fl, _ = _scan(s_ctx, bblk, cblk, ar, ai, z5, zl, direction=d, store=True,
                                 groups=gc, chunk_rows=chunk_rows)
        outs[('ctx', d)] = (y, hl)
        finals[d] = (f5.reshape(bc, nstate), fl.reshape(bc, width))
        z5 = jnp.zeros((1, SCAN_SEQS, nstate), F32)
        zl = jnp.zeros((1, SCAN_SEQS, width), F32)
        e5, el, ep = _scan(s_smp, bblk, cblk, ar, ai, z5, zl, direction=d, store=False,
                           groups=1, chunk_rows=chunk_rows)
        first = (SAMPLE_SEGS - 1) if d == 1 else 0
        h5 = _state_to_blocked(h0[0][:, d], h0[1][:, d])
        h5_rows = jnp.zeros((bs, SAMPLE_SEGS, nstate), F32).at[:, first].set(h5).reshape(SCAN_SEQS, nstate)
        hl_rows = jnp.zeros((bs, SAMPLE_SEGS, width), F32).at[:, first].set(h0[2][:, d]).reshape(SCAN_SEQS, width)
        arl, ail = _complex_power(ar, ai, lseg)
        i5, il = _chain(e5[0], el[0], ep[0], h5_rows, hl_rows, arl, ail, direction=d)
        y, hl, _, _, _ = _scan(s_smp, bblk, cblk, ar, ai, i5[None], il[None], direction=d, store=True,
                               groups=1, chunk_rows=chunk_rows)
        outs[('smp', d)] = (y, hl)

    def to_tokens(name, arr):
        if name == 'ctx':
            return arr.reshape(gc, lc, SCAN_SEQS, width).transpose(0, 2, 1, 3).reshape(n_ctx, width)
        return arr.reshape(lseg, bs * SAMPLE_SEGS, width).transpose(1, 0, 2).reshape(bs * ls, width)

    w_glu, w_out = p['s5_w_glu'][le].astype(BF16), p['w_out_ab'][le].astype(BF16)
    x1 = []
    for name, x, s_arr, xg, nc_ in zip(('ctx', 'smp'), xs, s_arrs, xgs, pass_ctx):
        yf, hf = (to_tokens(name, a) for a in outs[(name, 0)])
        yb, hb = (to_tokens(name, a) for a in outs[(name, 1)])
        x1.append(_ab_out(x, mod, s_arr, xg, yf, yb, hf, hb, p['s5_d'][le].reshape(1, width),
                          w_glu, p['s5_b_glu'][le].reshape(1, width), w_out, n_ctx=nc_, dec_seq=ls, tm=tm))
    st = [_state_from_blocked(finals[d][0], g_cnt, p_cnt) for d in range(2)]
    new_re = jnp.stack([st[0][0], st[1][0]], axis=1)
    new_im = jnp.stack([st[0][1], st[1][1]], axis=1)
    new_lru = jnp.stack([finals[0][1], finals[1][1]], axis=1)
    return x1, (new_re, new_im, new_lru)


def _fft_split(length):
    n = 2 * length
    n1 = 2
    while n // n1 > 1024:
        n1 *= 2
    return n1


def _mixer_c(xs, mod, p, lo, *, bc, lc, bs, ls, tm):
    width = p['hy_bias'].shape[1]
    w_in, w_out = p['w_in_c'][lo].astype(BF16), p['w_out_c'][lo].astype(BF16)
    out = []
    for x, b, l, nc_ in zip(xs, (bc, bs), (lc, ls), (bc * lc, 0)):
        x0, vg = _c_in(x, mod, p['norm1_g'][2 * lo + 1].reshape(1, -1), w_in,
                       p['b_in_c'][lo].reshape(1, -1), p['hy_conv_w'][lo], p['hy_conv_b'][lo].reshape(1, -1),
                       n_ctx=nc_, ctx_row=lc, dec_seq=ls, tm=tm)
        ct = min(width, 256 if l <= 1024 else 128)
        circ = _hy_filter(l, p['hy_w1'][lo], p['hy_b1'][lo], p['hy_freq1'][lo], p['hy_w2'][lo],
                          p['hy_b2'][lo], p['hy_freq2'][lo], p['hy_w3'][lo], p['hy_decay'][lo], ct=ct)
        yconv = _fftconv(vg.reshape(b, l, width), circ, n1=_fft_split(l), ct=ct).reshape(b * l, width)
        out.append(_c_out(x, mod, x0, vg, yconv, p['hy_bias'][lo].reshape(1, width), w_out,
                          p['b_out_c'][lo].reshape(1, -1), n_ctx=nc_, dec_seq=ls, tm=tm))
    return out


def _peer(xs, mod, p, l, *, pass_ctx, dec_seq, tq, ec):
    keys = p['peer_keys'][l]
    nh, _, nk, dkh = keys.shape
    wq = p['peer_wq'][l].astype(BF16)
    keys = keys.reshape(nh * 2, nk, dkh).astype(BF16)
    ne, d = p['peer_v'][l].shape
    vt = p['peer_v'][l].reshape(ne // PEER_GROUP, PEER_GROUP, d).transpose(0, 2, 1).astype(BF16)
    out = []
    for x, nc_ in zip(xs, pass_ctx):
        hb, th, e0, s1, e1 = _peer_route(x, mod, p['norm2_g'][l].reshape(1, -1), wq, keys,
                                         n_ctx=nc_, dec_seq=dec_seq, tq=tq)
        out.append(_peer_expert(hb, p['peer_u_bf16'], l, vt, th, e0, s1, e1, x, mod,
                                n_ctx=nc_, dec_seq=dec_seq, tq=tq, ec=ec))
    return out


def _forward(x_prompt, x_sample, state_s5_re, state_s5_im, state_lru, c, c_ctx, p, *, tm, tq, ec):
    bc, lc, d = x_prompt.shape
    bs, ls, _ = x_sample.shape
    assert bc % SCAN_SEQS == 0 and bs * SAMPLE_SEGS == SCAN_SEQS
    assert tm % lc == 0 and ls % tm == 0 and (ls // SAMPLE_SEGS) % GRID_W == 0
    n_ctx = bc * lc
    xs = [x_prompt.reshape(n_ctx, d), x_sample.reshape(bs * ls, d)]
    pass_ctx = (n_ctx, 0)
    p = dict(p, peer_u_bf16=p['peer_u'].astype(BF16))
    cond = jnp.zeros((V7X_SUBLANES, d), F32).at[0].set(c_ctx).at[1:1 + bs].set(c)
    mod_all = _modulation(cond, p['w_mod'], p['b_mod'])
    states = None
    for l in range(DEPTH):
        mod = mod_all[l].reshape(V7X_SUBLANES, 6, d)
        if l % 2 == 0:
            le = l // 2
            h0 = (state_s5_re[:, le], state_s5_im[:, le], state_lru[:, le])
            xs, st = _mixer_ab(xs, mod, p, le, h0, bc=bc, lc=lc, bs=bs, ls=ls, tm=tm)
            states = st if states is None else states
        else:
            xs = _mixer_c(xs, mod, p, l // 2, bc=bc, lc=lc, bs=bs, ls=ls, tm=tm)
        xs = _peer(xs, mod, p, l, pass_ctx=pass_ctx, dec_seq=ls, tq=tq, ec=ec)
    g = p['final_g'].reshape(1, d)
    new_re, new_im, new_lru = states
    return (_final_norm(xs[0], g, tm=tm).reshape(bc, lc, d), _final_norm(xs[1], g, tm=tm).reshape(bs, ls, d),
            new_re[:, None], new_im[:, None], new_lru[:, None])


def kernel(x_prompt, x_sample, state_s5_re, state_s5_im, state_lru, c, c_ctx, norm1_g, norm2_g, w_mod, b_mod, w_in_ab, s5_a_re, s5_a_im, s5_log_dt, s5_b_re, s5_b_im, s5_c_re, s5_c_im, s5_d, s5_w_glu, s5_b_glu, lru_conv_w, lru_conv_b, lru_w_a, lru_b_a, lru_w_x, lru_b_x, lru_lambda, w_out_ab, w_in_c, b_in_c, hy_conv_w, hy_conv_b, hy_w1, hy_b1, hy_freq1, hy_w2, hy_b2, hy_freq2, hy_w3, hy_decay, hy_bias, w_out_c, b_out_c, peer_wq, peer_keys, peer_u, peer_v, final_g):
    p = dict(norm1_g=norm1_g, norm2_g=norm2_g, w_mod=w_mod, b_mod=b_mod, w_in_ab=w_in_ab,
             s5_a_re=s5_a_re, s5_a_im=s5_a_im, s5_log_dt=s5_log_dt, s5_b_re=s5_b_re, s5_b_im=s5_b_im,
             s5_c_re=s5_c_re, s5_c_im=s5_c_im, s5_d=s5_d, s5_w_glu=s5_w_glu, s5_b_glu=s5_b_glu,
             lru_conv_w=lru_conv_w, lru_conv_b=lru_conv_b, lru_w_a=lru_w_a, lru_b_a=lru_b_a,
             lru_w_x=lru_w_x, lru_b_x=lru_b_x, lru_lambda=lru_lambda, w_out_ab=w_out_ab,
             w_in_c=w_in_c, b_in_c=b_in_c, hy_conv_w=hy_conv_w, hy_conv_b=hy_conv_b,
             hy_w1=hy_w1, hy_b1=hy_b1, hy_freq1=hy_freq1, hy_w2=hy_w2, hy_b2=hy_b2, hy_freq2=hy_freq2,
             hy_w3=hy_w3, hy_decay=hy_decay, hy_bias=hy_bias, w_out_c=w_out_c, b_out_c=b_out_c,
             peer_wq=peer_wq, peer_keys=peer_keys, peer_u=peer_u, peer_v=peer_v, final_g=final_g)
    return _forward(x_prompt, x_sample, state_s5_re, state_s5_im, state_lru, c, c_ctx, p,
                    tm=512, tq=512, ec=2048)
```

```python
import functools
import math

import jax
import jax.numpy as jnp
import numpy as np
from jax import lax
from jax.experimental import pallas as pl
from jax.experimental.pallas import tpu as pltpu

F32 = jnp.float32
BF16 = jnp.bfloat16
HIGHEST = lax.Precision.HIGHEST

DEPTH = 2
GRID_W = 64
EPS = 1e-6
LRU_C = 8.0
HY_BANDS = 16
HY_SHIFT = 0.05
PEER_NKEYS = 128
PEER_TOPK = 16
PEER_GROUP = 512
PEER_ACC_ROWS = 128

V7X_LANES = 128
V7X_SUBLANES = 8
V7X_VMEM_BYTES = 64 * 1024 * 1024

SCAN_SEQS = V7X_SUBLANES
SAMPLE_SEGS = 4
S5_COLBLK = 4


def _vmem_limit(est_bytes):
    return int(min(V7X_VMEM_BYTES - (8 << 20), max(32 << 20, est_bytes * 3 // 2)))


def _cparams(est_bytes, sem):
    return pltpu.CompilerParams(dimension_semantics=sem, vmem_limit_bytes=_vmem_limit(est_bytes))


def _gelu(x):
    k = 2.0 * 0.7978845608028654 * math.log2(math.e)
    return x / (1.0 + jnp.exp2(x * (-k - (k * 0.044715) * (x * x))))


def _norm_mod(x, g, sc, sh):
    ms = jnp.mean(x * x, axis=-1, keepdims=True)
    return (x * lax.rsqrt(ms + EPS)) * g * (1.0 + sc) + sh


def _cond_index(row0, n_ctx, dec_seq):
    return jnp.where(row0 < n_ctx, 0, 1 + (row0 - n_ctx) // dec_seq)


def _mod_kernel(c_ref, w_ref, b_ref, o_ref):
    c = c_ref[...]
    s = c * jax.nn.sigmoid(c)
    o_ref[...] = jnp.dot(s, w_ref[...], precision=HIGHEST, preferred_element_type=F32) + b_ref[...]


def _modulation(cond, w_mod, b_mod):
    depth, d, n = w_mod.shape
    rows = cond.shape[0]
    tn = 1536
    est = 2 * d * tn * 4 + 4 * rows * (d + tn) * 4
    return pl.pallas_call(
        _mod_kernel,
        grid=(depth, n // tn),
        in_specs=[pl.BlockSpec((rows, d), lambda l, j: (0, 0)),
                  pl.BlockSpec((None, d, tn), lambda l, j: (l, 0, j)),
                  pl.BlockSpec((None, 1, tn), lambda l, j: (l, 0, j))],
        out_specs=pl.BlockSpec((None, rows, tn), lambda l, j: (l, 0, j)),
        out_shape=jax.ShapeDtypeStruct((depth, rows, n), F32),
        compiler_params=_cparams(est, ("arbitrary", "arbitrary")),
        name="modulation",
    )(cond, w_mod, b_mod.reshape(depth, 1, n))


def _dwconv_rows(x, w, bias, pos, row_len, pad_left):
    tm = x.shape[0]
    acc = None
    for k in range(w.shape[0]):
        d = k - pad_left
        if d == 0:
            term = x
        else:
            shifted = pltpu.roll(x, (-d) % tm, axis=0)
            q = pos + d
            term = jnp.where(q >= 0, jnp.where(q < row_len, shifted, 0.0), 0.0)
        term = term * w[k:k + 1]
        acc = term if acc is None else acc + term
    return acc + bias


def _ab_in_kernel(x_ref, mod_ref, g_ref, w_ref, cw_ref, cb_ref, wg_ref, bg_ref, c8_ref,
                  s_ref, xg_ref, *, n_ctx, ctx_row, width):
    tm = x_ref.shape[0]
    m = mod_ref[...]
    h = _norm_mod(x_ref[...], g_ref[...], m[1:2], m[0:1])
    z = jnp.dot(h.astype(BF16), w_ref[...], preferred_element_type=F32)
    u = z[:, :width]
    xr = z[:, width:2 * width]
    xg_ref[...] = z[:, 2 * width:]
    row_len = jnp.where(pl.program_id(0) * tm < n_ctx, ctx_row, GRID_W)
    pos = lax.broadcasted_iota(jnp.int32, (tm, width), 0) & (row_len - 1)
    xb = _dwconv_rows(xr, cw_ref[...], cb_ref[...], pos, row_len, 2)
    gates = jnp.dot(xb.astype(BF16), wg_ref[...], preferred_element_type=F32) + bg_ref[...]
    s_ref[:, :width] = u
    c8 = c8_ref[...]
    for d in range(2):
        r = jax.nn.sigmoid(gates[:, (2 * d) * width:(2 * d + 1) * width])
        i = jax.nn.sigmoid(gates[:, (2 * d + 1) * width:(2 * d + 2) * width])
        a = jnp.exp(-(c8[d:d + 1] * r))
        b = jnp.sqrt(1.0 - a * a) * (i * xb)
        s_ref[:, (1 + 2 * d) * width:(2 + 2 * d) * width] = a
        s_ref[:, (2 + 2 * d) * width:(3 + 2 * d) * width] = b


def _ab_in(x, mod, g, w_in, conv_w, conv_b, w_gates, b_gates, c8, *, n_ctx, ctx_row, dec_seq, tm):
    t, d = x.shape
    width = w_in.shape[1] // 3
    kern = functools.partial(_ab_in_kernel, n_ctx=n_ctx, ctx_row=ctx_row, width=width)
    const = lambda i: (0, 0)
    est = (2 * tm * d * 4 + d * 3 * width * 2 * 2 + width * 4 * width * 2 * 2
           + 2 * tm * 6 * width * 4 + tm * 12 * width * 4)
    return pl.pallas_call(
        kern,
        grid=(t // tm,),
        in_specs=[pl.BlockSpec((tm, d), lambda i: (i, 0)),
                  pl.BlockSpec((None, 6, d), lambda i: (_cond_index(i * tm, n_ctx, dec_seq), 0, 0)),
                  pl.BlockSpec((1, d), const),
                  pl.BlockSpec(w_in.shape, const),
                  pl.BlockSpec(conv_w.shape, const),
                  pl.BlockSpec((1, width), const),
                  pl.BlockSpec(w_gates.shape, const),
                  pl.BlockSpec((1, 4 * width), const),
                  pl.BlockSpec((2, width), const)],
        out_specs=[pl.BlockSpec((tm, 5 * width), lambda i: (i, 0)),
                   pl.BlockSpec((tm, width), lambda i: (i, 0))],
        out_shape=[jax.ShapeDtypeStruct((t, 5 * width), F32),
                   jax.ShapeDtypeStruct((t, width), F32)],
        compiler_params=_cparams(est, ("parallel",)),
        name="ab_in",
    )(x, mod, g, w_in, conv_w, conv_b, w_gates, b_gates, c8)


def _scan_kernel(u_ref, a_ref, b_ref, bblk_ref, cblk_ref, ar_ref, ai_ref, i5_ref, il_ref, *rest,
                 reverse, store):
    if store:
        y_ref, hl_ref, f5_ref, fl_ref, fp_ref, buf, st5, stl, stp = rest
    else:
        f5_ref, fl_ref, fp_ref, buf, st5, stl, stp = rest
    c = pl.program_id(1)
    rows = u_ref.shape[0]
    nsteps = rows // SCAN_SEQS
    sw = st5.shape[1] // (2 * S5_COLBLK)
    cw = u_ref.shape[1] // S5_COLBLK

    @pl.when(c == 0)
    def _():
        st5[...] = i5_ref[...]
        stl[...] = il_ref[...]
        stp[...] = jnp.ones_like(stp)

    def row0(k):
        t = (nsteps - 1 - k) if reverse else k
        return pl.multiple_of(t * SCAN_SEQS, SCAN_SEQS)

    u = u_ref[...].astype(BF16)
    for j in range(S5_COLBLK):
        buf[...] = jnp.dot(u[:, j * cw:(j + 1) * cw], bblk_ref[j], preferred_element_type=F32)
        arj = jnp.broadcast_to(ar_ref[j], (SCAN_SEQS, sw))
        aij = jnp.broadcast_to(ai_ref[j], (SCAN_SEQS, sw))

        def step(k, carry, arj=arj, aij=aij):
            hr, hi = carry
            r0 = row0(k)
            xr = buf[pl.ds(r0, SCAN_SEQS), 0:sw]
            xi = buf[pl.ds(r0, SCAN_SEQS), sw:2 * sw]
            nr = arj * hr - aij * hi + xr
            ni = arj * hi + aij * hr + xi
            buf[pl.ds(r0, SCAN_SEQS), 0:sw] = nr
            buf[pl.ds(r0, SCAN_SEQS), sw:2 * sw] = ni
            return nr, ni

        base = 2 * sw * j
        hr, hi = lax.fori_loop(0, nsteps, step, (st5[:, base:base + sw], st5[:, base + sw:base + 2 * sw]))
        st5[:, base:base + sw] = hr
        st5[:, base + sw:base + 2 * sw] = hi
        if store:
            y_ref[:, j * cw:(j + 1) * cw] = jnp.dot(buf[...].astype(BF16), cblk_ref[j],
                                                    preferred_element_type=F32)

    def lstep(k, carry):
        h, p = carry
        r0 = row0(k)
        a = a_ref[pl.ds(r0, SCAN_SEQS), :]
        h = a * h + b_ref[pl.ds(r0, SCAN_SEQS), :]
        if store:
            hl_ref[pl.ds(r0, SCAN_SEQS), :] = h
        return h, p * a

    h, p = lax.fori_loop(0, nsteps, lstep, (stl[...], stp[...]))
    stl[...] = h
    stp[...] = p

    @pl.when(c == pl.num_programs(1) - 1)
    def _():
        f5_ref[...] = st5[...]
        fl_ref[...] = stl[...]
        fp_ref[...] = stp[...]


def _scan(s_arr, bblk, cblk, ar, ai, init5, initl, *, direction, store, groups, chunk_rows):
    rows_total, cols = s_arr.shape
    width = cols // 5
    rows_per_group = rows_total // groups
    nc = rows_per_group // chunk_rows
    reverse = direction == 1
    nstate = init5.shape[-1]

    def chunk(g, c):
        return g * nc + ((nc - 1 - c) if reverse else c)

    kern = functools.partial(_scan_kernel, reverse=reverse, store=store)
    const3 = lambda g, c: (0, 0, 0)
    in_specs = [pl.BlockSpec((chunk_rows, width), lambda g, c: (chunk(g, c), 0)),
                pl.BlockSpec((chunk_rows, width), lambda g, c: (chunk(g, c), 1 + 2 * direction)),
                pl.BlockSpec((chunk_rows, width), lambda g, c: (chunk(g, c), 2 + 2 * direction)),
                pl.BlockSpec(bblk.shape, const3),
                pl.BlockSpec(cblk.shape, const3),
                pl.BlockSpec(ar.shape, const3),
                pl.BlockSpec(ai.shape, const3),
                pl.BlockSpec((None, SCAN_SEQS, nstate), lambda g, c: (g, 0, 0)),
                pl.BlockSpec((None, SCAN_SEQS, width), lambda g, c: (g, 0, 0))]
    fin_specs = [pl.BlockSpec((None, SCAN_SEQS, nstate), lambda g, c: (g, 0, 0)),
                 pl.BlockSpec((None, SCAN_SEQS, width), lambda g, c: (g, 0, 0)),
                 pl.BlockSpec((None, SCAN_SEQS, width), lambda g, c: (g, 0, 0))]
    fin_shapes = [jax.ShapeDtypeStruct((groups, SCAN_SEQS, nstate), F32),
                  jax.ShapeDtypeStruct((groups, SCAN_SEQS, width), F32),
                  jax.ShapeDtypeStruct((groups, SCAN_SEQS, width), F32)]
    out_specs, out_shapes = fin_specs, fin_shapes
    if store:
        out_specs = [pl.BlockSpec((chunk_rows, width), lambda g, c: (chunk(g, c), 0))] * 2 + fin_specs
        out_shapes = [jax.ShapeDtypeStruct((rows_total, width), F32)] * 2 + fin_shapes
    bufw = 2 * nstate // (2 * S5_COLBLK)
    est = (10 * chunk_rows * width * 4 + chunk_rows * bufw * 4 * 2
           + 4 * bblk.size * 2 + 8 * SCAN_SEQS * nstate * 4)
    return pl.pallas_call(
        kern,
        grid=(groups, nc),
        in_specs=in_specs,
        out_specs=out_specs,
        out_shape=out_shapes,
        scratch_shapes=[pltpu.VMEM((chunk_rows, bufw), F32),
                        pltpu.VMEM((SCAN_SEQS, nstate), F32),
                        pltpu.VMEM((SCAN_SEQS, width), F32),
                        pltpu.VMEM((SCAN_SEQS, width), F32)],
        compiler_params=_cparams(est, ("arbitrary", "arbitrary")),
        name="scan_store" if store else "scan_ends",
    )(s_arr, s_arr, s_arr, bblk, cblk, ar, ai, init5, initl)


def _chain_kernel(e5_ref, el_ref, ep_ref, h5_ref, hl_ref, arl_ref, ail_ref, i5_ref, il_ref, *, reverse):
    nseg = SAMPLE_SEGS
    sw = arl_ref.shape[-1]
    shift = (SCAN_SEQS - 1) if reverse else 1
    order = range(nseg - 2, -1, -1) if reverse else range(1, nseg)
    seg5 = lax.broadcasted_iota(jnp.int32, (SCAN_SEQS, sw), 0) % nseg
    for j in range(S5_COLBLK):
        base = 2 * sw * j
        re, im = slice(base, base + sw), slice(base + sw, base + 2 * sw)
        ar, ai = arl_ref[j], ail_ref[j]
        er = pltpu.roll(e5_ref[:, re], shift, axis=0)
        ei = pltpu.roll(e5_ref[:, im], shift, axis=0)
        hr, hi = h5_ref[:, re], h5_ref[:, im]
        for k in order:
            pr = pltpu.roll(hr, shift, axis=0)
            pi = pltpu.roll(hi, shift, axis=0)
            hr = jnp.where(seg5 == k, ar * pr - ai * pi + er, hr)
            hi = jnp.where(seg5 == k, ar * pi + ai * pr + ei, hi)
        i5_ref[:, re] = hr
        i5_ref[:, im] = hi
    segl = lax.broadcasted_iota(jnp.int32, el_ref.shape, 0) % nseg
    el = pltpu.roll(el_ref[...], shift, axis=0)
    ep = pltpu.roll(ep_ref[...], shift, axis=0)
    h = hl_ref[...]
    for k in order:
        h = jnp.where(segl == k, ep * pltpu.roll(h, shift, axis=0) + el, h)
    il_ref[...] = h


def _chain(end5, endl, endp, h5_rows, hl_rows, arl, ail, *, direction):
    kern = functools.partial(_chain_kernel, reverse=direction == 1)
    return pl.pallas_call(
        kern,
        out_shape=[jax.ShapeDtypeStruct(end5.shape, F32), jax.ShapeDtypeStruct(endl.shape, F32)],
        name="scan_chain",
    )(end5, endl, endp, h5_rows, hl_rows, arl, ail)


def _ab_out_kernel(x_ref, mod_ref, s_ref, xg_ref, yf_ref, yb_ref, hf_ref, hb_ref, d_ref,
                   wglu_ref, bglu_ref, wout_ref, o_ref):
    width = xg_ref.shape[1]
    m = mod_ref[...]
    y = s_ref[...] * d_ref[...] + yf_ref[...] + yb_ref[...]
    zs = _gelu(y)
    gate = jnp.dot(zs.astype(BF16), wglu_ref[...], preferred_element_type=F32) + bglu_ref[...]
    s5_out = zs * jax.nn.sigmoid(gate)
    lru_out = (hf_ref[...] + hb_ref[...]) * _gelu(xg_ref[...])
    out = (jnp.dot(s5_out.astype(BF16), wout_ref[:width, :], preferred_element_type=F32)
           + jnp.dot(lru_out.astype(BF16), wout_ref[width:, :], preferred_element_type=F32))
    o_ref[...] = x_ref[...] + m[2:3] * out


def _ab_out(x, mod, s_arr, xg, yf, yb, hf, hb, s5_d, w_glu, b_glu, w_out, *, n_ctx, dec_seq, tm):
    t, d = x.shape
    width = xg.shape[1]
    const = lambda i: (0, 0)
    row = lambda i: (i, 0)
    est = 4 * tm * d * 4 + 12 * tm * width * 4 + 4 * (width * width + 2 * width * d) + 8 * tm * width * 4
    return pl.pallas_call(
        _ab_out_kernel,
        grid=(t // tm,),
        in_specs=[pl.BlockSpec((tm, d), row),
                  pl.BlockSpec((None, 6, d), lambda i: (_cond_index(i * tm, n_ctx, dec_seq), 0, 0)),
                  pl.BlockSpec((tm, width), row),
                  pl.BlockSpec((tm, width), row),
                  pl.BlockSpec((tm, width), row),
                  pl.BlockSpec((tm, width), row),
                  pl.BlockSpec((tm, width), row),
                  pl.BlockSpec((tm, width), row),
                  pl.BlockSpec((1, width), const),
                  pl.BlockSpec(w_glu.shape, const),
                  pl.BlockSpec((1, width), const),
                  pl.BlockSpec(w_out.shape, const)],
        out_specs=pl.BlockSpec((tm, d), row),
        out_shape=jax.ShapeDtypeStruct((t, d), F32),
        compiler_params=_cparams(est, ("parallel",)),
        name="ab_out",
    )(x, mod, s_arr, xg, yf, yb, hf, hb, s5_d, w_glu, b_glu, w_out)


def _c_in_kernel(x_ref, mod_ref, g_ref, w_ref, b_ref, cw_ref, cb_ref, x0_ref, vg_ref, *, n_ctx, ctx_row):
    tm = x_ref.shape[0]
    width = x0_ref.shape[1]
    m = mod_ref[...]
    h = _norm_mod(x_ref[...], g_ref[...], m[1:2], m[0:1])
    z = jnp.dot(h.astype(BF16), w_ref[...], preferred_element_type=F32) + b_ref[...]
    row_len = jnp.where(pl.program_id(0) * tm < n_ctx, ctx_row, GRID_W)
    pos = lax.broadcasted_iota(jnp.int32, (tm, width), 0) & (row_len - 1)
    cw = cw_ref[...]
    cb = cb_ref[...]
    parts = [_dwconv_rows(z[:, k * width:(k + 1) * width], cw[:, k * width:(k + 1) * width],
                          cb[:, k * width:(k + 1) * width], pos, row_len, 1) for k in range(3)]
    x0_ref[...] = parts[0]
    vg_ref[...] = parts[2] * parts[1]


def _c_in(x, mod, g, w_in, b_in, conv_w, conv_b, *, n_ctx, ctx_row, dec_seq, tm):
    t, d = x.shape
    width = w_in.shape[1] // 3
    kern = functools.partial(_c_in_kernel, n_ctx=n_ctx, ctx_row=ctx_row)
    const = lambda i: (0, 0)
    est = 2 * tm * d * 4 + 4 * d * 3 * width + 4 * tm * width * 4 + 6 * tm * 3 * width * 4
    return pl.pallas_call(
        kern,
        grid=(t // tm,),
        in_specs=[pl.BlockSpec((tm, d), lambda i: (i, 0)),
                  pl.BlockSpec((None, 6, d), lambda i: (_cond_index(i * tm, n_ctx, dec_seq), 0, 0)),
                  pl.BlockSpec((1, d), const),
                  pl.BlockSpec(w_in.shape, const),
                  pl.BlockSpec((1, 3 * width), const),
                  pl.BlockSpec(conv_w.shape, const),
                  pl.BlockSpec((1, 3 * width), const)],
        out_specs=[pl.BlockSpec((tm, width), lambda i: (i, 0))] * 2,
        out_shape=[jax.ShapeDtypeStruct((t, width), F32)] * 2,
        compiler_params=_cparams(est, ("parallel",)),
        name="c_in",
    )(x, mod, g, w_in, b_in, conv_w, conv_b)


def _c_out_kernel(x_ref, mod_ref, x0_ref, vg_ref, yc_ref, hb_ref, w_ref, b_ref, o_ref):
    m = mod_ref[...]
    vg = vg_ref[...]
    y = (yc_ref[...] + hb_ref[...] * vg) * x0_ref[...]
    out = jnp.dot(y.astype(BF16), w_ref[...], preferred_element_type=F32) + b_ref[...]
    o_ref[...] = x_ref[...] + m[2:3] * out


def _c_out(x, mod, x0, vg, yconv, hy_bias, w_out, b_out, *, n_ctx, dec_seq, tm):
    t, d = x.shape
    width = x0.shape[1]
    const = lambda i: (0, 0)
    row = lambda i: (i, 0)
    est = 4 * tm * d * 4 + 6 * tm * width * 4 + 4 * width * d + 4 * tm * width * 4
    return pl.pallas_call(
        _c_out_kernel,
        grid=(t // tm,),
        in_specs=[pl.BlockSpec((tm, d), row),
                  pl.BlockSpec((None, 6, d), lambda i: (_cond_index(i * tm, n_ctx, dec_seq), 0, 0)),
                  pl.BlockSpec((tm, width), row),
                  pl.BlockSpec((tm, width), row),
                  pl.BlockSpec((tm, width), row),
                  pl.BlockSpec((1, width), const),
                  pl.BlockSpec(w_out.shape, const),
                  pl.BlockSpec((1, d), const)],
        out_specs=pl.BlockSpec((tm, d), row),
        out_shape=jax.ShapeDtypeStruct((t, d), F32),
        compiler_params=_cparams(est, ("parallel",)),
        name="c_out",
    )(x, mod, x0, vg, yconv, hy_bias, w_out, b_out)


def _hy_filter_kernel(w1_ref, b1_ref, f1_ref, w2_ref, b2_ref, f2_ref, w3_ref, dec_ref, o_ref, z_scr, *, length):
    ct = o_ref.shape[1]
    nb = HY_BANDS
    ecols = w1_ref.shape[0]
    row = lax.broadcasted_iota(jnp.int32, (length, 1), 0)
    positions = (row, length - row)

    @pl.when(pl.program_id(0) == 0)
    def _():
        for side in range(2):
            posf = positions[side].astype(F32)
            col = lax.broadcasted_iota(jnp.int32, (length, ecols), 1)
            band_idx = jnp.where(col <= nb, col - 1, col - 1 - nb).astype(F32)
            bands = 1e-4 + band_idx * ((nb - 1 - 1e-4) / (nb - 1))
            ang = (2.0 * math.pi / length) * posf * bands
            emb = jnp.where(col == 0, posf / length,
                            jnp.where(col <= nb, jnp.cos(ang),
                                      jnp.where(col <= 2 * nb, -jnp.sin(ang), 0.0)))
            z = jnp.sin(f1_ref[...] * (jnp.dot(emb, w1_ref[...], precision=HIGHEST,
                                               preferred_element_type=F32) + b1_ref[...]))
            z_scr[side] = jnp.sin(f2_ref[...] * (jnp.dot(z, w2_ref[...], precision=HIGHEST,
                                                         preferred_element_type=F32) + b2_ref[...]))

    def half(side):
        wide_row = lax.broadcasted_iota(jnp.int32, (length, ct), 0)
        t = (wide_row if side == 0 else length - wide_row).astype(F32) / length
        filt = jnp.dot(z_scr[side], w3_ref[side], precision=HIGHEST, preferred_element_type=F32)
        return filt * (jnp.exp(-t * jnp.abs(dec_ref[side:side + 1, :])) + HY_SHIFT)

    fwd = half(0)
    bwd = jnp.where(lax.broadcasted_iota(jnp.int32, (length, ct), 0) == 0, 0.0, half(1))
    ss = jnp.sum(fwd * fwd, axis=0, keepdims=True) + jnp.sum(bwd * bwd, axis=0, keepdims=True)
    scale = lax.rsqrt(ss + EPS)
    o_ref[:length, :] = fwd * scale
    o_ref[length:, :] = bwd * scale


def _hy_filter(length, w1, b1, f1, w2, b2, f2, w3, decay, *, ct):
    emb, ff = w1.shape
    width = decay.shape[1]
    ecols = 128
    w1p = jnp.zeros((ecols, ff), F32).at[:emb].set(w1)
    w3s = w3.reshape(ff, 2, width).transpose(1, 0, 2)
    kern = functools.partial(_hy_filter_kernel, length=length)
    const = lambda j: (0, 0)
    est = 10 * length * ct * 4 + 4 * length * ecols * 4
    return pl.pallas_call(
        kern,
        grid=(width // ct,),
        in_specs=[pl.BlockSpec((ecols, ff), const),
                  pl.BlockSpec((1, ff), const),
                  pl.BlockSpec((1, ff), const),
                  pl.BlockSpec((ff, ff), const),
                  pl.BlockSpec((1, ff), const),
                  pl.BlockSpec((1, ff), const),
                  pl.BlockSpec((2, ff, ct), lambda j: (0, 0, j)),
                  pl.BlockSpec((2, ct), lambda j: (0, j))],
        out_specs=pl.BlockSpec((2 * length, ct), lambda j: (0, j)),
        out_shape=jax.ShapeDtypeStruct((2 * length, width), F32),
        scratch_shapes=[pltpu.VMEM((2, length, ff), F32)],
        compiler_params=_cparams(est, ("arbitrary",)),
        name="hy_filter",
    )(w1p, b1.reshape(1, ff), f1.reshape(1, ff), w2, b2.reshape(1, ff), f2.reshape(1, ff), w3s, decay)


def _fftconv_kernel(ct_ref, st_ref, x_ref, k_ref, c_ref, s_ref, twr_ref, twi_ref, o_ref, *scratch, n1):
    kfr, kfi = scratch if scratch else (None, None)
    h1 = n1 // 2
    n2 = c_ref.shape[0]
    n = n1 * n2
    cm = c_ref[...]
    sm = s_ref[...]

    def dft(pairs, inverse):
        w = pairs[0][0].shape[1]
        parts = jnp.concatenate([p.astype(BF16) for pair in pairs for p in pair], axis=1)
        cp = jnp.dot(cm, parts, preferred_element_type=F32)
        sp = jnp.dot(sm, parts, preferred_element_type=F32)
        out = []
        for i in range(len(pairs)):
            ca, cb = cp[:, 2 * i * w:(2 * i + 1) * w], cp[:, (2 * i + 1) * w:(2 * i + 2) * w]
            sa, sb = sp[:, 2 * i * w:(2 * i + 1) * w], sp[:, (2 * i + 1) * w:(2 * i + 2) * w]
            out.append((ca - sb, cb + sa) if inverse else (ca + sb, cb - sa))
        return out

    def slab(t1):
        return slice(t1 * n2, (t1 + 1) * n2)

    twr1, twi1 = twr_ref[...], twi_ref[...]
    tw0 = (jnp.ones_like(twr1), jnp.zeros_like(twi1))

    def next_twiddle(tw):
        return tw[0] * twr1 - tw[1] * twi1, tw[0] * twi1 + tw[1] * twr1

    def filter_stage1(k1, tw):
        ar = ai = None
        for t1 in range(n1):
            kr = k_ref[slab(t1), :]
            tr, ti = ct_ref[k1, t1] * kr, -(st_ref[k1, t1] * kr)
            ar = tr if ar is None else ar + tr
            ai = ti if ai is None else ai + ti
        ar, ai = ar * (1.0 / n), ai * (1.0 / n)
        return ar * tw[0] + ai * tw[1], ai * tw[0] - ar * tw[1]

    if kfr is not None:
        @pl.when(pl.program_id(1) == 0)
        def _():
            def fbody(k1, tw):
                (kfr[k1], kfi[k1]), = dft([filter_stage1(k1, tw)], False)
                return next_twiddle(tw)
            lax.fori_loop(0, n1, fbody, tw0)

    o_ref[...] = jnp.zeros_like(o_ref)

    def forward(k1, tw):
        twr, twi = tw
        ar = ai = None
        for t1 in range(h1):
            cc, ss = ct_ref[k1, t1], st_ref[k1, t1]
            xr, xi = x_ref[0, slab(t1), :], x_ref[1, slab(t1), :]
            tr, ti = cc * xr + ss * xi, cc * xi - ss * xr
            ar = tr if ar is None else ar + tr
            ai = ti if ai is None else ai + ti
        data = (ar * twr + ai * twi, ai * twr - ar * twi)
        if kfr is not None:
            (fr, fi), = dft([data], False)
            gr, gi = kfr[k1], kfi[k1]
        else:
            (fr, fi), (gr, gi) = dft([data, filter_stage1(k1, tw)], False)
        return fr * gr - fi * gi, fr * gi + fi * gr

    def inverse(k1, tw, y):
        twr, twi = tw
        (br, bi), = dft([y], True)
        br, bi = br * twr - bi * twi, bi * twr + br * twi
        for t1 in range(h1):
            cc, ss = ct_ref[k1, t1], st_ref[k1, t1]
            o_ref[0, slab(t1), :] += cc * br - ss * bi
            o_ref[1, slab(t1), :] += cc * bi + ss * br

    def body(k1, carry):
        tw_prev, y_prev = carry
        tw = next_twiddle(tw_prev)
        y = forward(k1, tw)
        inverse(k1 - 1, tw_prev, y_prev)
        return tw, y

    tw_last, y_last = lax.fori_loop(1, n1, body, (tw0, forward(0, tw0)))
    inverse(n1 - 1, tw_last, y_last)


def _fftconv(v, circ, *, n1, ct):
    bsz, length, width = v.shape
    n = 2 * length
    n2 = n // n1
    idx = np.arange(n2)
    ang = 2.0 * np.pi * ((idx[:, None] * idx[None, :]) % n2) / n2
    cmat = jnp.asarray(np.cos(ang), F32).astype(BF16)
    smat = jnp.asarray(np.sin(ang), F32).astype(BF16)
    tang = 2.0 * np.pi * idx / n
    twr = jnp.asarray(np.broadcast_to(np.cos(tang)[:, None], (n2, ct)), F32)
    twi = jnp.asarray(np.broadcast_to(np.sin(tang)[:, None], (n2, ct)), F32)
    i1 = np.arange(n1)
    ang1 = 2.0 * np.pi * ((i1[:, None] * i1[None, :]) % n1) / n1
    ctab = jnp.asarray(np.round(np.cos(ang1), 12), F32)
    stab = jnp.asarray(np.round(np.sin(ang1), 12), F32)
    kern = functools.partial(_fftconv_kernel, n1=n1)
    const = lambda j, p: (0, 0)
    once = pl.Buffered(1)
    npairs = bsz // 2
    scratch = [pltpu.VMEM((n1, n2, ct), F32)] * 2 if npairs > 1 else []
    est = (4 * 2 * length * ct * 4 + n * ct * 4 + 2 * n2 * n2 * 2 + 2 * n2 * ct * 4
           + len(scratch) * n * ct * 4 + (8 + 2 * n1) * n2 * ct * 4)
    return pl.pallas_call(
        kern,
        grid=(width // ct, npairs),
        in_specs=[pl.BlockSpec(memory_space=pltpu.SMEM),
                  pl.BlockSpec(memory_space=pltpu.SMEM),
                  pl.BlockSpec((2, length, ct), lambda j, p: (p, 0, j)),
                  pl.BlockSpec((n, ct), lambda j, p: (0, j), pipeline_mode=once),
                  pl.BlockSpec((n2, n2), const, pipeline_mode=once),
                  pl.BlockSpec((n2, n2), const, pipeline_mode=once),
                  pl.BlockSpec((n2, ct), const, pipeline_mode=once),
                  pl.BlockSpec((n2, ct), const, pipeline_mode=once)],
        out_specs=pl.BlockSpec((2, length, ct), lambda j, p: (p, 0, j)),
        out_shape=jax.ShapeDtypeStruct((bsz, length, width), F32),
        scratch_shapes=scratch,
        compiler_params=_cparams(est, ("arbitrary", "arbitrary")),
        name="fftconv",
    )(ctab, stab, v, circ, cmat, smat, twr, twi)


def _sorting_network(n):
    pairs = []
    p = 1
    while p < n:
        k = p
        while k >= 1:
            for j in range(k % p, n - k, 2 * k):
                for i in range(min(k, n - j - k)):
                    if (i + j) // (2 * p) == (i + j + k) // (2 * p):
                        pairs.append((i + j, i + j + k))
            k //= 2
        p *= 2
    return pairs


_SORT16 = _sorting_network(PEER_TOPK)


def _peer_route_kernel(x_ref, mod_ref, g_ref, wq_ref, keys_ref, hb_ref, th_ref, e0_ref, s1_ref, e1_ref,
                       q_scr, sc_scr, top_scr):
    tq = x_ref.shape[0]
    nk = PEER_NKEYS
    k = PEER_TOPK
    nhp = keys_ref.shape[0]
    dkh = keys_ref.shape[2]
    neg = -jnp.inf
    m = mod_ref[...]
    h = _norm_mod(x_ref[...], g_ref[...], m[4:5], m[3:4]).astype(BF16)
    hb_ref[...] = h
    q = jnp.dot(h, wq_ref[...], preferred_element_type=F32)
    for hp in range(nhp):
        q_scr[hp] = q[:, hp * dkh:(hp + 1) * dkh].astype(BF16)

    lw = V7X_LANES
    sub = V7X_SUBLANES
    assert k == 2 * sub
    ncand = k + 7 * sub + (k - sub)
    npad = sum(sub - k // (i + 1) for i in range(2, sub))
    iota_c = lax.broadcasted_iota(jnp.int32, (ncand, lw), 0)
    iota_8 = lax.broadcasted_iota(jnp.int32, (sub, lw), 0)
    iota_r = lax.broadcasted_iota(jnp.int32, (k, lw), 0)

    def topk(vals, iota, exact):
        top = jnp.zeros((k, lw), F32)
        for r in range(k):
            mx = jnp.max(vals, axis=0, keepdims=True)
            if exact:
                first = jnp.min(jnp.where(vals == mx, iota, vals.shape[0]), axis=0, keepdims=True)
                vals = jnp.where(iota == first, neg, vals)
            else:
                vals = jnp.where(vals == mx, neg, vals)
            top = jnp.where(iota_r == r, mx, top)
        return top, vals

    def tied(rest, expected):
        gone = jnp.sum(jnp.where(rest == neg, 1.0, 0.0), axis=0, keepdims=True)
        return jnp.where(gone != expected, 1.0, 0.0)

    def exchange(v, i, j):
        v[i], v[j] = jnp.maximum(v[i], v[j]), jnp.minimum(v[i], v[j])

    def sort_top(tiles):
        v = list(tiles)
        for i, j in _SORT16:
            exchange(v, i, j)
        for shift in (sub // 2, sub // 4, sub // 8):
            other = [pltpu.roll(t, shift, axis=0) for t in v]
            v = [jnp.maximum(v[r], other[k - 1 - r]) for r in range(k)]
            d = k // 2
            while d >= 1:
                for i in range(k):
                    if i & d == 0:
                        exchange(v, i, i + d)
                d //= 2
        top = jnp.zeros((k, lw), F32)
        for r in range(k):
            top = jnp.where(iota_r == r, jnp.concatenate([v[r]] * (k // sub), axis=0), top)
        return top

    def halves(hd):
        assert nk == k * sub
        for p in range(2):
            for c in range(tq // lw):
                lanes = slice(c * lw, (c + 1) * lw)
                top_scr[2 * hd + p, :, lanes] = sort_top(
                    [sc_scr[2 * hd + p, t * sub:(t + 1) * sub, lanes] for t in range(nk // sub)])

    def pairs(hd, exact):
        flag = jnp.zeros((1, lw), F32)
        for c in range(tq // lw):
            lanes = slice(c * lw, (c + 1) * lw)
            a = top_scr[2 * hd, :, lanes]
            b = top_scr[2 * hd + 1, :, lanes]
            pieces = [a[0:1] + b, a[1:2] + b[0:sub]]
            for i in range(2, sub):
                pieces.append(jnp.where(iota_8 < k // (i + 1), a[i:i + 1] + b[0:sub], neg))
            pieces.append(a[sub:k] + b[0:1])
            vmax = a[0:1] + b[0:1]
            top, rest = topk(jnp.concatenate(pieces, axis=0), iota_c, exact)
            z = jnp.sum(jnp.exp(top - vmax), axis=0, keepdims=True)
            last = top[k - 1:k]
            nxt = jnp.max(rest, axis=0, keepdims=True)
            tau = jnp.where(nxt == neg, last, 0.5 * (last + nxt))
            flag = jnp.maximum(flag, tied(rest, float(k + npad)))
            s0 = sc_scr[2 * hd, :, lanes]
            s1 = sc_scr[2 * hd + 1, :, lanes]
            th = jnp.where(s0 >= a[k - 1:k], tau - s0, jnp.inf)
            s1 = jnp.where(s1 >= b[k - 1:k], s1, neg)
            e0 = jnp.exp(s0 - a[0:1]) / z
            per = th_ref.shape[2]
            for r in range(nk // per):
                th_ref[hd, r, :, lanes] = th[r * per:(r + 1) * per]
                e0_ref[hd, r, :, lanes] = e0[r * per:(r + 1) * per]
            s1_ref[hd, :, lanes] = s1
            e1_ref[hd, :, lanes] = jnp.exp(s1 - b[0:1])
        return jnp.max(flag)

    def head_body(hd, carry):
        for p in range(2):
            sc_scr[2 * hd + p] = lax.dot_general(keys_ref[2 * hd + p], q_scr[2 * hd + p], (((1,), (1,)), ((), ())),
                                                 preferred_element_type=F32)
        halves(hd)

        @pl.when(pairs(hd, False) > 0.0)
        def _():
            pairs(hd, True)

        return carry

    lax.fori_loop(0, nhp // 2, head_body, 0)


def _peer_route(x, mod, g, wq, keys, *, n_ctx, dec_seq, tq):
    t, d = x.shape
    nhp, nk, dkh = keys.shape
    nh = nhp // 2
    const = lambda i: (0, 0)
    hshape = jax.ShapeDtypeStruct((nh, nk, t), F32)
    hspec = pl.BlockSpec((nh, nk, tq), lambda i: (0, 0, i))
    per = PEER_GROUP // nk
    gshape = jax.ShapeDtypeStruct((nh, nk // per, per, t), F32)
    gspec = pl.BlockSpec((nh, nk // per, per, tq), lambda i: (0, 0, 0, i))
    pad = V7X_SUBLANES // per
    est = (2 * tq * d * 4 + 4 * d * nhp * dkh + 2 * tq * d * 2 + (4 + 4 * pad) * nh * nk * tq * 4
           + 2 * nhp * nk * tq * 4 + 6 * tq * nhp * dkh * 4)
    return pl.pallas_call(
        _peer_route_kernel,
        grid=(t // tq,),
        in_specs=[pl.BlockSpec((tq, d), lambda i: (i, 0)),
                  pl.BlockSpec((None, 6, d), lambda i: (_cond_index(i * tq, n_ctx, dec_seq), 0, 0)),
                  pl.BlockSpec((1, d), const),
                  pl.BlockSpec(wq.shape, const),
                  pl.BlockSpec(keys.shape, lambda i: (0, 0, 0))],
        out_specs=[pl.BlockSpec((tq, d), lambda i: (i, 0)), gspec, gspec, hspec, hspec],
        out_shape=[jax.ShapeDtypeStruct((t, d), BF16), gshape, gshape, hshape, hshape],
        scratch_shapes=[pltpu.VMEM((nhp, tq, dkh), BF16),
                        pltpu.VMEM((nhp, nk, tq), F32),
                        pltpu.VMEM((nhp, PEER_TOPK, tq), F32)],
        compiler_params=_cparams(est, ("parallel",)),
        name="peer_route",
    )(x, mod, g, wq, keys)


def _peer_expert_kernel(hbt_ref, u_ref, vt_ref, th_ref, e0_ref, s1_ref, e1_ref, x_ref, mod_ref, o_ref,
                        acc, st_a, st_b, hbuf_a, hbuf_b, *, nc):
    g = pl.program_id(0)
    nh = th_ref.shape[0]
    nk = s1_ref.shape[1]
    ngroups, per = th_ref.shape[1], th_ref.shape[2]
    gw = per * nk
    tq = st_a.shape[1]
    kh = PEER_ACC_ROWS // per
    nq = tq // V7X_LANES
    c3 = jnp.maximum(g - 2, 0) % nc

    @pl.when(g == 0)
    def _():
        st_b[...] = jnp.zeros_like(st_b)
        hbuf_a[...] = jnp.zeros_like(hbuf_a)
        hbuf_b[...] = jnp.zeros_like(hbuf_b)

    @pl.when(c3 == 0)
    def _():
        acc[...] = jnp.zeros_like(acc)

    def step(st_w, st_r, hbuf_w, hbuf_r):
        def group(j, carry):
            rows = pl.ds(pl.multiple_of(j * gw, gw), gw)
            res_c = jnp.dot(vt_ref[j], hbuf_r[rows, :], preferred_element_type=F32)
            acc[...] += res_c
            res_a = jnp.dot(u_ref[rows, :], hbt_ref[...], preferred_element_type=F32)
            st_w[rows, :] = res_a

            def zero_of(x):
                u = pltpu.bitcast(x, jnp.uint32)
                return jnp.concatenate([pltpu.bitcast((u >> 16) >> 16, F32)] * (kh // V7X_SUBLANES), axis=0)

            anchors = {nq // 2 - 1: zero_of(res_a[-V7X_SUBLANES:, -V7X_LANES:]),
                       nq - 1: zero_of(res_c[-V7X_SUBLANES:, -V7X_LANES:])}
            for q in range(tq // V7X_LANES):
                lanes = slice(q * V7X_LANES, (q + 1) * V7X_LANES)
                for half in range(nk // kh):
                    keys = slice(half * kh, (half + 1) * kh)
                    w = [None] * per
                    for hd in range(nh):
                        s1 = s1_ref[hd, keys, lanes]
                        e1 = e1_ref[hd, keys, lanes]
                        for i in range(per):
                            sel = jnp.where(s1 >= th_ref[hd, j, i:i + 1, lanes], e1, 0.0) * e0_ref[hd, j, i:i + 1, lanes]
                            w[i] = sel if w[i] is None else w[i] + sel
                    for i in range(per):
                        tile = pl.ds(pl.multiple_of(j * gw + i * nk + half * kh, kh), kh)
                        anchored = q in anchors and half == nk // kh - 1 and i == per - 1
                        wi = w[i] + anchors[q] if anchored else w[i]
                        hbuf_w[tile, lanes] = (_gelu(st_r[tile, lanes]) * wi).astype(BF16)
            return carry

        lax.fori_loop(0, ngroups, group, 0)

    @pl.when(g % 2 == 0)
    def _():
        step(st_a, st_b, hbuf_b, hbuf_a)

    @pl.when(g % 2 == 1)
    def _():
        step(st_b, st_a, hbuf_a, hbuf_b)

    @pl.when(jnp.logical_and(g >= 2, c3 == nc - 1))
    def _():
        o_ref[...] = x_ref[...] + mod_ref[5:6, :] * acc[...].T


def _peer_expert(hb, u_tabs, layer, vt_tab, th, e0, s1, e1, x, mod, *, n_ctx, dec_seq, tq, ec):
    t, d = x.shape
    ne = u_tabs.shape[1]
    nh, nk, _ = s1.shape
    per = th.shape[2]
    assert per * nk == PEER_GROUP
    ngroups = ec // PEER_GROUP
    nc = ne // ec
    npairs = (t // tq) * nc

    def pair(g, lag):
        p = jnp.clip(g - lag, 0, npairs - 1)
        return p // nc, p % nc

    def cond3(g):
        return _cond_index(pair(g, 2)[0] * tq, n_ctx, dec_seq)

    est = (2 * tq * d * 2 + 4 * ec * d * 2 + 4 * nh * ngroups * V7X_SUBLANES * tq * 4 + 4 * nh * nk * tq * 4
           + 4 * tq * d * 4 + d * tq * 4 + 2 * ec * tq * 2 + 2 * ec * tq * 4 + 3 * PEER_GROUP * tq * 4)
    return pl.pallas_call(
        functools.partial(_peer_expert_kernel, nc=nc),
        grid=(npairs + 2,),
        in_specs=[pl.BlockSpec((d, tq), lambda g: (0, pair(g, 0)[0])),
                  pl.BlockSpec((None, ec, d), lambda g: (layer, pair(g, 0)[1], 0)),
                  pl.BlockSpec((ngroups, d, PEER_GROUP), lambda g: (pair(g, 2)[1], 0, 0)),
                  pl.BlockSpec((nh, ngroups, per, tq), lambda g: (0, pair(g, 1)[1], 0, pair(g, 1)[0])),
                  pl.BlockSpec((nh, ngroups, per, tq), lambda g: (0, pair(g, 1)[1], 0, pair(g, 1)[0])),
                  pl.BlockSpec((nh, nk, tq), lambda g: (0, 0, pair(g, 1)[0])),
                  pl.BlockSpec((nh, nk, tq), lambda g: (0, 0, pair(g, 1)[0])),
                  pl.BlockSpec((tq, d), lambda g: (pair(g, 2)[0], 0)),
                  pl.BlockSpec((None, 6, d), lambda g: (cond3(g), 0, 0))],
        out_specs=pl.BlockSpec((tq, d), lambda g: (pair(g, 2)[0], 0)),
        out_shape=jax.ShapeDtypeStruct((t, d), F32),
        scratch_shapes=[pltpu.VMEM((d, tq), F32), pltpu.VMEM((ec, tq), F32), pltpu.VMEM((ec, tq), F32),
                        pltpu.VMEM((ec, tq), BF16), pltpu.VMEM((ec, tq), BF16)],
        compiler_params=_cparams(est, ("arbitrary",)),
        name="peer_expert",
    )(hb.T, u_tabs, vt_tab, th, e0, s1, e1, x, mod)


def _final_kernel(x_ref, g_ref, o_ref):
    x = x_ref[...]
    ms = jnp.mean(x * x, axis=-1, keepdims=True)
    o_ref[...] = (x * lax.rsqrt(ms + EPS)) * g_ref[...]


def _final_norm(x, g, *, tm):
    t, d = x.shape
    return pl.pallas_call(
        _final_kernel,
        grid=(t // tm,),
        in_specs=[pl.BlockSpec((tm, d), lambda i: (i, 0)), pl.BlockSpec((1, d), lambda i: (0, 0))],
        out_specs=pl.BlockSpec((tm, d), lambda i: (i, 0)),
        out_shape=jax.ShapeDtypeStruct((t, d), F32),
        compiler_params=_cparams(6 * tm * d * 4, ("parallel",)),
        name="final_norm",
    )(x, g)


def _block_diag(w):
    h, i, j = w.shape
    return jnp.einsum('hij,hg->higj', w, jnp.eye(h, dtype=w.dtype)).reshape(h * i, h * j)


def _s5_discretize(a_re, a_im, log_dt, b_re, b_im):
    dt = jnp.exp(log_dt)[:, None]
    mag = jnp.exp(a_re * dt)
    abar_r = mag * jnp.cos(a_im * dt)
    abar_i = mag * jnp.sin(a_im * dt)
    den = a_re * a_re + a_im * a_im
    num_r = abar_r - 1.0
    coef_r = ((num_r * a_re + abar_i * a_im) / den)[..., None]
    coef_i = ((abar_i * a_re - num_r * a_im) / den)[..., None]
    return abar_r, abar_i, coef_r * b_re - coef_i * b_im, coef_r * b_im + coef_i * b_re


def _s5_blocks(abar_r, abar_i, bbar_r, bbar_i, c_re, c_im):
    g, p, ch = bbar_r.shape
    gb = g // S5_COLBLK

    def vec(v):
        return v.reshape(S5_COLBLK, 1, gb * p)

    def bmat(b):
        bt = b.transpose(0, 2, 1).reshape(S5_COLBLK, gb, ch, p)
        return jnp.stack([_block_diag(bt[j]) for j in range(S5_COLBLK)])

    def cmat(cc):
        ct = cc.transpose(0, 2, 1).reshape(S5_COLBLK, gb, p, ch)
        return jnp.stack([_block_diag(ct[j]) for j in range(S5_COLBLK)])

    bblk = jnp.concatenate([bmat(bbar_r), bmat(bbar_i)], axis=2).astype(BF16)
    cblk = jnp.concatenate([cmat(c_re), cmat(-c_im)], axis=1).astype(BF16)
    return bblk, cblk, vec(abar_r), vec(abar_i)


def _state_to_blocked(re, im):
    n, g, p = re.shape
    gb = g // S5_COLBLK
    both = jnp.stack([re.reshape(n, S5_COLBLK, gb * p), im.reshape(n, S5_COLBLK, gb * p)], axis=2)
    return both.reshape(n, -1)


def _state_from_blocked(st, g, p):
    n = st.shape[0]
    both = st.reshape(n, S5_COLBLK, 2, (g // S5_COLBLK) * p)
    return both[:, :, 0].reshape(n, g, p), both[:, :, 1].reshape(n, g, p)


def _complex_power(ar, ai, n):
    for _ in range(int(math.log2(n))):
        ar, ai = ar * ar - ai * ai, 2.0 * ar * ai
    return ar, ai


def _mixer_ab(xs, mod, p, le, h0, *, bc, lc, bs, ls, tm):
    n_ctx = bc * lc
    width = p['s5_d'].shape[1]
    g_cnt, p_cnt = p['s5_a_re'].shape[2:]
    w_gates = jnp.concatenate([_block_diag(p['lru_w_a'][le, 0]), _block_diag(p['lru_w_x'][le, 0]),
                               _block_diag(p['lru_w_a'][le, 1]), _block_diag(p['lru_w_x'][le, 1])],
                              axis=1).astype(BF16)
    b_gates = jnp.concatenate([p['lru_b_a'][le, 0], p['lru_b_x'][le, 0],
                               p['lru_b_a'][le, 1], p['lru_b_x'][le, 1]]).reshape(1, 4 * width)
    c8 = LRU_C * jax.nn.softplus(-p['lru_lambda'][le])
    pass_ctx = (n_ctx, 0)
    w_in = p['w_in_ab'][le].astype(BF16)
    s_arrs, xgs = zip(*[_ab_in(x, mod, p['norm1_g'][2 * le].reshape(1, -1), w_in,
                               p['lru_conv_w'][le], p['lru_conv_b'][le].reshape(1, width), w_gates, b_gates, c8,
                               n_ctx=nc_, ctx_row=lc, dec_seq=ls, tm=tm) for x, nc_ in zip(xs, pass_ctx)])

    cols = s_arrs[0].shape[1]
    gc = bc // SCAN_SEQS
    lseg = ls // SAMPLE_SEGS
    s_ctx = (s_arrs[0].reshape(gc, SCAN_SEQS, lc, cols).transpose(0, 2, 1, 3)
             .reshape(n_ctx, cols))
    s_smp = (s_arrs[1].reshape(bs * SAMPLE_SEGS, lseg, cols).transpose(1, 0, 2)
             .reshape(bs * ls, cols))
    chunk_rows = min(1024, lc * SCAN_SEQS, lseg * SCAN_SEQS)

    outs = {}
    finals = {}
    for d in range(2):
        abar_r, abar_i, bbar_r, bbar_i = _s5_discretize(
            p['s5_a_re'][le, d], p['s5_a_im'][le, d], p['s5_log_dt'][le, d],
            p['s5_b_re'][le, d], p['s5_b_im'][le, d])
        bblk, cblk, ar, ai = _s5_blocks(abar_r, abar_i, bbar_r, bbar_i,
                                        p['s5_c_re'][le, d], p['s5_c_im'][le, d])
        nstate = 2 * g_cnt * p_cnt
        z5 = jnp.zeros((gc, SCAN_SEQS, nstate), F32)
        zl = jnp.zeros((gc, SCAN_SEQS, width), F32)
        y, hl, f5, fl, _ = _scan(s_ctx, bblk, cblk, ar, ai, z5, zl, direction=d, store=True,
                                 groups=gc, chunk_rows=chunk_rows)
        outs[('ctx', d)] = (y, hl)
        finals[d] = (f5.reshape(bc, nstate), fl.reshape(bc, width))
        z5 = jnp.zeros((1, SCAN_SEQS, nstate), F32)
        zl = jnp.zeros((1, SCAN_SEQS, width), F32)
        e5, el, ep = _scan(s_smp, bblk, cblk, ar, ai, z5, zl, direction=d, store=False,
                           groups=1, chunk_rows=chunk_rows)
        first = (SAMPLE_SEGS - 1) if d == 1 else 0
        h5 = _state_to_blocked(h0[0][:, d], h0[1][:, d])
        h5_rows = jnp.zeros((bs, SAMPLE_SEGS, nstate), F32).at[:, first].set(h5).reshape(SCAN_SEQS, nstate)
        hl_rows = jnp.zeros((bs, SAMPLE_SEGS, width), F32).at[:, first].set(h0[2][:, d]).reshape(SCAN_SEQS, width)
        arl, ail = _complex_power(ar, ai, lseg)
        i5, il = _chain(e5[0], el[0], ep[0], h5_rows, hl_rows, arl, ail, direction=d)
        y, hl, _, _, _ = _scan(s_smp, bblk, cblk, ar, ai, i5[None], il[None], direction=d, store=True,
                               groups=1, chunk_rows=chunk_rows)
        outs[('smp', d)] = (y, hl)

    def to_tokens(name, arr):
        if name == 'ctx':
            return arr.reshape(gc, lc, SCAN_SEQS, width).transpose(0, 2, 1, 3).reshape(n_ctx, width)
        return arr.reshape(lseg, bs * SAMPLE_SEGS, width).transpose(1, 0, 2).reshape(bs * ls, width)

    w_glu, w_out = p['s5_w_glu'][le].astype(BF16), p['w_out_ab'][le].astype(BF16)
    x1 = []
    for name, x, s_arr, xg, nc_ in zip(('ctx', 'smp'), xs, s_arrs, xgs, pass_ctx):
        yf, hf = (to_tokens(name, a) for a in outs[(name, 0)])
        yb, hb = (to_tokens(name, a) for a in outs[(name, 1)])
        x1.append(_ab_out(x, mod, s_arr, xg, yf, yb, hf, hb, p['s5_d'][le].reshape(1, width),
                          w_glu, p['s5_b_glu'][le].reshape(1, width), w_out, n_ctx=nc_, dec_seq=ls, tm=tm))
    st = [_state_from_blocked(finals[d][0], g_cnt, p_cnt) for d in range(2)]
    new_re = jnp.stack([st[0][0], st[1][0]], axis=1)
    new_im = jnp.stack([st[0][1], st[1][1]], axis=1)
    new_lru = jnp.stack([finals[0][1], finals[1][1]], axis=1)
    return x1, (new_re, new_im, new_lru)


def _fft_split(length):
    n = 2 * length
    n1 = 2
    while n // n1 > 1024:
        n1 *= 2
    return n1


def _mixer_c(xs, mod, p, lo, *, bc, lc, bs, ls, tm):
    width = p['hy_bias'].shape[1]
    w_in, w_out = p['w_in_c'][lo].astype(BF16), p['w_out_c'][lo].astype(BF16)
    out = []
    for x, b, l, nc_ in zip(xs, (bc, bs), (lc, ls), (bc * lc, 0)):
        x0, vg = _c_in(x, mod, p['norm1_g'][2 * lo + 1].reshape(1, -1), w_in,
                       p['b_in_c'][lo].reshape(1, -1), p['hy_conv_w'][lo], p['hy_conv_b'][lo].reshape(1, -1),
                       n_ctx=nc_, ctx_row=lc, dec_seq=ls, tm=tm)
        ct = min(width, 256 if l <= 1024 else 128)
        circ = _hy_filter(l, p['hy_w1'][lo], p['hy_b1'][lo], p['hy_freq1'][lo], p['hy_w2'][lo],
                          p['hy_b2'][lo], p['hy_freq2'][lo], p['hy_w3'][lo], p['hy_decay'][lo], ct=ct)
        yconv = _fftconv(vg.reshape(b, l, width), circ, n1=_fft_split(l), ct=ct).reshape(b * l, width)
        out.append(_c_out(x, mod, x0, vg, yconv, p['hy_bias'][lo].reshape(1, width), w_out,
                          p['b_out_c'][lo].reshape(1, -1), n_ctx=nc_, dec_seq=ls, tm=tm))
    return out


def _peer(xs, mod, p, l, *, pass_ctx, dec_seq, tq, ec):
    keys = p['peer_keys'][l]
    nh, _, nk, dkh = keys.shape
    wq = p['peer_wq'][l].astype(BF16)
    keys = keys.reshape(nh * 2, nk, dkh).astype(BF16)
    ne, d = p['peer_v'][l].shape
    vt = p['peer_v'][l].reshape(ne // PEER_GROUP, PEER_GROUP, d).transpose(0, 2, 1).astype(BF16)
    out = []
    for x, nc_ in zip(xs, pass_ctx):
        hb, th, e0, s1, e1 = _peer_route(x, mod, p['norm2_g'][l].reshape(1, -1), wq, keys,
                                         n_ctx=nc_, dec_seq=dec_seq, tq=tq)
        out.append(_peer_expert(hb, p['peer_u_bf16'], l, vt, th, e0, s1, e1, x, mod,
                                n_ctx=nc_, dec_seq=dec_seq, tq=tq, ec=ec))
    return out


def _forward(x_prompt, x_sample, state_s5_re, state_s5_im, state_lru, c, c_ctx, p, *, tm, tq, ec):
    bc, lc, d = x_prompt.shape
    bs, ls, _ = x_sample.shape
    assert bc % SCAN_SEQS == 0 and bs * SAMPLE_SEGS == SCAN_SEQS
    assert tm % lc == 0 and ls % tm == 0 and (ls // SAMPLE_SEGS) % GRID_W == 0
    n_ctx = bc * lc
    xs = [x_prompt.reshape(n_ctx, d), x_sample.reshape(bs * ls, d)]
    pass_ctx = (n_ctx, 0)
    p = dict(p, peer_u_bf16=p['peer_u'].astype(BF16))
    cond = jnp.zeros((V7X_SUBLANES, d), F32).at[0].set(c_ctx).at[1:1 + bs].set(c)
    mod_all = _modulation(cond, p['w_mod'], p['b_mod'])
    states = None
    for l in range(DEPTH):
        mod = mod_all[l].reshape(V7X_SUBLANES, 6, d)
        if l % 2 == 0:
            le = l // 2
            h0 = (state_s5_re[:, le], state_s5_im[:, le], state_lru[:, le])
            xs, st = _mixer_ab(xs, mod, p, le, h0, bc=bc, lc=lc, bs=bs, ls=ls, tm=tm)
            states = st if states is None else states
        else:
            xs = _mixer_c(xs, mod, p, l // 2, bc=bc, lc=lc, bs=bs, ls=ls, tm=tm)
        xs = _peer(xs, mod, p, l, pass_ctx=pass_ctx, dec_seq=ls, tq=tq, ec=ec)
    g = p['final_g'].reshape(1, d)
    new_re, new_im, new_lru = states
    return (_final_norm(xs[0], g, tm=tm).reshape(bc, lc, d), _final_norm(xs[1], g, tm=tm).reshape(bs, ls, d),
            new_re[:, None], new_im[:, None], new_lru[:, None])


def kernel(x_prompt, x_sample, state_s5_re, state_s5_im, state_lru, c, c_ctx, norm1_g, norm2_g, w_mod, b_mod, w_in_ab, s5_a_re, s5_a_im, s5_log_dt, s5_b_re, s5_b_im, s5_c_re, s5_c_im, s5_d, s5_w_glu, s5_b_glu, lru_conv_w, lru_conv_b, lru_w_a, lru_b_a, lru_w_x, lru_b_x, lru_lambda, w_out_ab, w_in_c, b_in_c, hy_conv_w, hy_conv_b, hy_w1, hy_b1, hy_freq1, hy_w2, hy_b2, hy_freq2, hy_w3, hy_decay, hy_bias, w_out_c, b_out_c, peer_wq, peer_keys, peer_u, peer_v, final_g):
    p = dict(norm1_g=norm1_g, norm2_g=norm2_g, w_mod=w_mod, b_mod=b_mod, w_in_ab=w_in_ab,
             s5_a_re=s5_a_re, s5_a_im=s5_a_im, s5_log_dt=s5_log_dt, s5_b_re=s5_b_re, s5_b_im=s5_b_im,
             s5_c_re=s5_c_re, s5_c_im=s5_c_im, s5_d=s5_d, s5_w_glu=s5_w_glu, s5_b_glu=s5_b_glu,
             lru_conv_w=lru_conv_w, lru_conv_b=lru_conv_b, lru_w_a=lru_w_a, lru_b_a=lru_b_a,
             lru_w_x=lru_w_x, lru_b_x=lru_b_x, lru_lambda=lru_lambda, w_out_ab=w_out_ab,
             w_in_c=w_in_c, b_in_c=b_in_c, hy_conv_w=hy_conv_w, hy_conv_b=hy_conv_b,
             hy_w1=hy_w1, hy_b1=hy_b1, hy_freq1=hy_freq1, hy_w2=hy_w2, hy_b2=hy_b2, hy_freq2=hy_freq2,
             hy_w3=hy_w3, hy_decay=hy_decay, hy_bias=hy_bias, w_out_c=w_out_c, b_out_c=b_out_c,
             peer_wq=peer_wq, peer_keys=peer_keys, peer_u=peer_u, peer_v=peer_v, final_g=final_g)
    return _forward(x_prompt, x_sample, state_s5_re, state_s5_im, state_lru, c, c_ctx, p,
                    tm=512, tq=512, ec=2048)
```
